```python
import jax
import jax.numpy as jnp
from jax import lax
import numpy as np

D_MODEL = 1024
BATCH = 4
SEQ = 4096
DEPTH = 2

GRID_W = 64
CTX_LEN = 256
N_HEADS = 4
HEAD_DIM = 64
MIX_W = N_HEADS * HEAD_DIM
N_BRANCH = 4
CONV_W = 4
LRU_C = 8.0
RET_CHUNK = 128
MLSTM_CHUNK = 128
HGRN_CHUNK = 16
ROPE_BASE = 10000.0
N_GROUPS = 4
EXP_PER_GROUP = 4
N_EXPERTS = N_GROUPS * EXP_PER_GROUP
TOP_K = 2
D_EXPERT = 512
N_MOD = 6
EPS = 1e-6
M_INIT = -1e30

IN_COLS = (
    ('a_x', MIX_W), ('a_g', MIX_W),
    ('b_q', MIX_W), ('b_k', MIX_W), ('b_v', MIX_W), ('b_g', MIX_W),
    ('c_q', MIX_W), ('c_k', MIX_W), ('c_v', MIX_W), ('c_o', MIX_W), ('c_gates', 4 * N_HEADS),
    ('d_q', MIX_W), ('d_ff', MIX_W), ('d_fb', MIX_W), ('d_i', MIX_W), ('d_g', MIX_W),
    ('merge', N_BRANCH * D_MODEL),
)
IN_W = 15 * MIX_W + 4 * N_HEADS + N_BRANCH * D_MODEL

kernel_name = 'hybrid_lru_retention_mlstm_hgrn2_hmoe_dit'


def rmsnorm(t):
    tf = t.astype(jnp.float32)
    return (tf * lax.rsqrt(jnp.mean(tf * tf, -1, keepdims=True) + EPS)).astype(t.dtype)


def modulate(t, shift, scale):
    return rmsnorm(t) * (1.0 + scale) + shift


def split_cols(p):
    out, off = {}, 0
    for name, width in IN_COLS:
        out[name] = p[..., off:off + width]
        off += width
    return out


def heads(t):
    b, n, _ = t.shape
    return t.astype(jnp.float32).reshape(b, n, N_HEADS, HEAD_DIM).transpose(0, 2, 1, 3)


def head_norm(o, center):
    if center:
        o = o - jnp.mean(o, -1, keepdims=True)
    o = o * lax.rsqrt(jnp.mean(o * o, -1, keepdims=True) + EPS)
    b, h, n, dh = o.shape
    return o.transpose(0, 2, 1, 3).reshape(b, n, h * dh)


def tflip(t, rev):
    return jnp.flip(t, 2) if rev else t


def centred_dwconv(t, w, b):
    y = lax.conv_general_dilated(
        t, w[:, None, :].astype(t.dtype), window_strides=(1,), padding=((2, 1),),
        dimension_numbers=('NWC', 'WIO', 'NWC'), feature_group_count=t.shape[-1])
    return y + b


def rope_tables(row, col):
    n = HEAD_DIM // 4
    inv = jnp.power(ROPE_BASE, -jnp.arange(n, dtype=jnp.float32) / n)
    ang = jnp.concatenate([row[:, None] * inv, col[:, None] * inv], -1)
    return jnp.cos(ang), jnp.sin(ang)


def apply_rope(t, cos, sin):
    half = HEAD_DIM // 2
    t1, t2 = t[..., :half], t[..., half:]
    return jnp.concatenate([t1 * cos - t2 * sin, t1 * sin + t2 * cos], -1)


def rglru_scan(u, w_r, b_r, w_i, b_i, lam, h0, reverse):
    b, n, _ = u.shape
    uf = u.astype(jnp.float32)
    if reverse:
        uf = jnp.flip(uf, 1)
    uh = uf.reshape(b, n, N_HEADS, HEAD_DIM)
    r = jax.nn.sigmoid(jnp.einsum('bthi,hij->bthj', uh, w_r).reshape(b, n, MIX_W) + b_r)
    gi = jax.nn.sigmoid(jnp.einsum('bthi,hij->bthj', uh, w_i).reshape(b, n, MIX_W) + b_i)
    log_a = LRU_C * r * jax.nn.log_sigmoid(lam.astype(jnp.float32))
    a = jnp.exp(log_a)
    inp = jnp.sqrt(-jnp.expm1(2.0 * log_a)) * (gi * uf)

    def combine(left, right):
        a_l, h_l = left
        a_r, h_r = right
        return a_r * a_l, a_r * h_l + h_r

    a_cum, h = lax.associative_scan(combine, (a, inp), axis=1)
    h = h + a_cum * h0[:, None, :]
    h_final = h[:, -1]
    if reverse:
        h = jnp.flip(h, 1)
    return h, h_final


def mixer_rglru(pc, px, conv_w, conv_b, gate_w, gate_b, lam):
    uc = centred_dwconv(pc['a_x'], conv_w, conv_b)
    ux = centred_dwconv(px['a_x'], conv_w, conv_b)
    h0 = jnp.zeros((uc.shape[0], MIX_W), jnp.float32)
    hcs, hxs = [], []
    for d in range(2):
        prm = (gate_w[d, 0], gate_b[d, 0], gate_w[d, 1], gate_b[d, 1], lam[d])
        hc, state = rglru_scan(uc, *prm, h0, d == 1)
        hx, _ = rglru_scan(ux, *prm, state, d == 1)
        hcs.append(hc)
        hxs.append(hx)
    yc = jax.nn.gelu(pc['a_g'].astype(jnp.float32)) * (hcs[0] + hcs[1])
    yx = jax.nn.gelu(px['a_g'].astype(jnp.float32)) * (hxs[0] + hxs[1])
    return yc, yx


def retention_chunked(q, k, v, log_gamma, s0, include_diag):
    b, h, n, dh = q.shape
    L = RET_CHUNK
    nc = n // L
    q, k, v = (t.reshape(b, h, nc, L, dh) for t in (q, k, v))
    pos = jnp.arange(L, dtype=jnp.float32)
    diff = pos[:, None] - pos[None, :]
    mask = (diff >= 0) if include_diag else (diff > 0)
    decay = jnp.where(mask, jnp.exp(log_gamma[:, None, None] * jnp.maximum(diff, 0.0)), 0.0)
    scores = jnp.einsum('bhnid,bhnjd->bhnij', q, k) * decay[:, None]
    o_intra = jnp.einsum('bhnij,bhnje->bhnie', scores, v)
    k_w = jnp.exp(log_gamma[:, None] * (L - 1.0 - pos))
    kv = jnp.einsum('bhnjd,bhnje->bhnde', k * k_w[:, None, :, None], v)
    g_chunk = jnp.exp(log_gamma * L)[:, None, None]

    def step(s, kv_n):
        return g_chunk * s + kv_n, s

    s_final, s_prev = lax.scan(step, s0, jnp.moveaxis(kv, 2, 0))
    s_prev = jnp.moveaxis(s_prev, 0, 2)
    q_w = jnp.exp(log_gamma[:, None] * (pos + 1.0))
    o_inter = jnp.einsum('bhnid,bhnde->bhnie', q * q_w[:, None, :, None], s_prev)
    return (o_intra + o_inter).reshape(b, h, n, dh), s_final


def mixer_retention(pc, px, theta, cos, sin):
    scale = HEAD_DIM ** -0.5
    qc, kc, vc = heads(pc['b_q']), heads(pc['b_k']) * scale, heads(pc['b_v'])
    qx = apply_rope(heads(px['b_q']), cos, sin)
    kx = apply_rope(heads(px['b_k']), cos, sin) * scale
    vx = heads(px['b_v'])
    s0 = jnp.zeros(qc.shape[:2] + (HEAD_DIM, HEAD_DIM), jnp.float32)
    ocs, oxs = [], []
    for d in range(2):
        rev = d == 1
        lg = jax.nn.log_sigmoid(theta[d].astype(jnp.float32))
        oc, state = retention_chunked(tflip(qc, rev), tflip(kc, rev), tflip(vc, rev), lg, s0, not rev)
        ox, _ = retention_chunked(tflip(qx, rev), tflip(kx, rev), tflip(vx, rev), lg, state, not rev)
        ocs.append(tflip(oc, rev))
        oxs.append(tflip(ox, rev))

    def out(p, o):
        return head_norm(o[0] + o[1], True) * jax.nn.silu(p['b_g'].astype(jnp.float32))

    return out(pc, ocs), out(px, oxs)


def mlstm_chunked(q, k, v, log_i, log_f, state):
    b, h, n, dh = q.shape
    L = MLSTM_CHUNK
    nc = n // L
    q, k, v = (t.reshape(b, h, nc, L, dh) for t in (q, k, v))
    log_i = log_i.reshape(b, h, nc, L)
    cum = jnp.cumsum(log_f.reshape(b, h, nc, L), -1)
    causal = jnp.tril(jnp.ones((L, L), bool))
    log_d = jnp.where(causal, cum[..., :, None] - cum[..., None, :] + log_i[..., None, :], -jnp.inf)
    m_intra = jnp.max(log_d, -1)
    cum_end = cum[..., -1]
    log_w = cum_end[..., None] - cum + log_i
    m_loc = jnp.max(log_w, -1)
    w = jnp.exp(log_w - m_loc[..., None])
    c_loc = jnp.einsum('bhnj,bhnjd,bhnje->bhnde', w, k, v)
    n_loc = jnp.einsum('bhnj,bhnjd->bhnd', w, k)

    def step(carry, xs):
        c_s, n_s, m_s = carry
        c_l, n_l, m_l, ce = xs
        m_new = jnp.maximum(ce + m_s, m_l)
        s_old = jnp.exp(ce + m_s - m_new)
        s_loc = jnp.exp(m_l - m_new)
        c_new = s_old[..., None, None] * c_s + s_loc[..., None, None] * c_l
        n_new = s_old[..., None] * n_s + s_loc[..., None] * n_l
        return (c_new, n_new, m_new), (c_s, n_s, m_s)

    xs = tuple(jnp.moveaxis(t, 2, 0) for t in (c_loc, n_loc, m_loc, cum_end))
    final, prev = lax.scan(step, state, xs)
    c_prev, n_prev, m_prev = (jnp.moveaxis(t, 0, 2) for t in prev)
    m_inter = cum + m_prev[..., None]
    m_q = jnp.maximum(m_intra, m_inter)
    s = jnp.einsum('bhnid,bhnjd->bhnij', q, k) * jnp.exp(log_d - m_q[..., None])
    s_inter = jnp.exp(m_inter - m_q)
    num = jnp.einsum('bhnij,bhnje->bhnie', s, v) + s_inter[..., None] * jnp.einsum('bhnid,bhnde->bhnie', q, c_prev)
    den = jnp.sum(s, -1) + s_inter * jnp.einsum('bhnid,bhnd->bhni', q, n_prev)
    hid = num / jnp.maximum(jnp.abs(den), jnp.exp(-m_q))[..., None]
    return hid.reshape(b, h, n, dh), final


def mixer_mlstm(pc, px, conv_w, conv_b, gate_b):
    def prep(p):
        b, n, _ = p['c_q'].shape
        qk = jax.nn.silu(centred_dwconv(jnp.concatenate([p['c_q'], p['c_k']], -1), conv_w, conv_b))
        q = heads(qk[..., :MIX_W])
        k = heads(qk[..., MIX_W:]) * HEAD_DIM ** -0.5
        v = heads(p['c_v'])
        g = (p['c_gates'].astype(jnp.float32) + gate_b.reshape(-1)).reshape(b, n, 4, N_HEADS)
        return q, k, v, g.transpose(2, 0, 3, 1)

    qc, kc, vc, gc = prep(pc)
    qx, kx, vx, gx = prep(px)
    b = qc.shape[0]
    init = (jnp.zeros((b, N_HEADS, HEAD_DIM, HEAD_DIM), jnp.float32),
            jnp.zeros((b, N_HEADS, HEAD_DIM), jnp.float32),
            jnp.full((b, N_HEADS), M_INIT, jnp.float32))
    hcs, hxs = [], []
    for d in range(2):
        rev = d == 1
        hc, state = mlstm_chunked(tflip(qc, rev), tflip(kc, rev), tflip(vc, rev), tflip(gc[2 * d], rev),
                                  tflip(jax.nn.log_sigmoid(gc[2 * d + 1]), rev), init)
        hx, _ = mlstm_chunked(tflip(qx, rev), tflip(kx, rev), tflip(vx, rev), tflip(gx[2 * d], rev),
                              tflip(jax.nn.log_sigmoid(gx[2 * d + 1]), rev), state)
        hcs.append(tflip(hc, rev))
        hxs.append(tflip(hx, rev))

    def out(p, o):
        return jax.nn.sigmoid(p['c_o'].astype(jnp.float32)) * head_norm(o[0] + o[1], True)

    return out(pc, hcs), out(px, hxs)


def gla_chunked(q, k, v, log_f, s0):
    b, h, n, dk = q.shape
    dv = v.shape[-1]
    L = HGRN_CHUNK
    nc = n // L
    q, k, log_f = (t.reshape(b, h, nc, L, dk) for t in (q, k, log_f))
    v = v.reshape(b, h, nc, L, dv)
    cum = jnp.cumsum(log_f, 3)
    causal = jnp.tril(jnp.ones((L, L), bool))[:, :, None]
    decay = jnp.exp(jnp.where(causal, cum[..., :, None, :] - cum[..., None, :, :], -jnp.inf))
    scores = jnp.einsum('bhnic,bhnjc,bhnijc->bhnij', q, k, decay)
    o_intra = jnp.einsum('bhnij,bhnje->bhnie', scores, v)
    cum_end = cum[..., -1, :]
    kv = jnp.einsum('bhnjc,bhnje->bhnce', k * jnp.exp(cum_end[..., None, :] - cum), v)

    def step(s, xs):
        kv_n, ce = xs
        return jnp.exp(ce)[..., None] * s + kv_n, s

    s_final, s_prev = lax.scan(step, s0, (jnp.moveaxis(kv, 2, 0), jnp.moveaxis(cum_end, 2, 0)))
    s_prev = jnp.moveaxis(s_prev, 0, 2)
    o_inter = jnp.einsum('bhnic,bhnce->bhnie', q * jnp.exp(cum), s_prev)
    return (o_intra + o_inter).reshape(b, h, n, dv), s_final


def mixer_hgrn2(pc, px, lb):
    def prep(p):
        q = heads(jax.nn.silu(p['d_q'].astype(jnp.float32)))
        v = heads(p['d_i'])
        f = [heads(lb + (1.0 - lb) * jax.nn.sigmoid(p[name].astype(jnp.float32))) for name in ('d_ff', 'd_fb')]
        return q, v, f

    qc, vc, fc = prep(pc)
    qx, vx, fx = prep(px)
    s0 = jnp.zeros(qc.shape[:2] + (HEAD_DIM, HEAD_DIM), jnp.float32)
    ocs, oxs = [], []
    for d in range(2):
        rev = d == 1
        f_c, f_x = tflip(fc[d], rev), tflip(fx[d], rev)
        oc, state = gla_chunked(tflip(qc, rev), 1.0 - f_c, tflip(vc, rev), jnp.log(f_c), s0)
        ox, _ = gla_chunked(tflip(qx, rev), 1.0 - f_x, tflip(vx, rev), jnp.log(f_x), state)
        ocs.append(tflip(oc, rev))
        oxs.append(tflip(ox, rev))

    def out(p, o):
        return head_norm(o[0] + o[1], False) * jax.nn.silu(p['d_g'].astype(jnp.float32))

    return out(pc, ocs), out(px, oxs)


def token_mixers(hc, hx, w_in, a_conv_w, a_conv_b, a_gate_w, a_gate_b, a_lambda, b_theta,
                 c_conv_w, c_conv_b, c_gate_b, lb, w_branch, w_out, cos, sin, need_ctx):
    pc = split_cols(hc @ w_in)
    px = split_cols(hx @ w_in)
    ya = mixer_rglru(pc, px, a_conv_w, a_conv_b, a_gate_w, a_gate_b, a_lambda)
    yb = mixer_retention(pc, px, b_theta, cos, sin)
    yc = mixer_mlstm(pc, px, c_conv_w, c_conv_b, c_gate_b)
    yd = mixer_hgrn2(pc, px, lb)

    def merge(p, ys, dtype):
        y = jnp.stack(ys, 2).astype(dtype)
        gate = jax.nn.sigmoid(p['merge'].reshape(y.shape[:2] + (N_BRANCH, D_MODEL)))
        z = jnp.sum(gate * jnp.einsum('btnw,nwd->btnd', y, w_branch), 2)
        return z @ w_out

    mix_x = merge(px, [ya[1], yb[1], yc[1], yd[1]], hx.dtype)
    mix_c = merge(pc, [ya[0], yb[0], yc[0], yd[0]], hc.dtype) if need_ctx else None
    return mix_c, mix_x


def hier_moe(h, w_group, b_group, w_router, b_router, w1, w3, w2):
    lead = h.shape[:-1]
    g_logit = (h @ w_group).astype(jnp.float32) + b_group
    g_idx = jnp.argmax(g_logit, -1)
    g_prob = jnp.max(jax.nn.softmax(g_logit, -1), -1, keepdims=True)
    e_logit = ((h @ w_router).astype(jnp.float32) + b_router).reshape(lead + (N_GROUPS, EXP_PER_GROUP))
    e_logit = jnp.sum(e_logit * jax.nn.one_hot(g_idx, N_GROUPS, dtype=jnp.float32)[..., None], -2)
    top_v, top_i = lax.top_k(e_logit, TOP_K)
    top_w = jax.nn.softmax(top_v, -1) * g_prob
    expert_id = g_idx[..., None] * EXP_PER_GROUP + top_i
    gates = jnp.sum(jax.nn.one_hot(expert_id, N_EXPERTS, dtype=jnp.float32) * top_w[..., None], -2)
    out = jnp.zeros_like(h)
    for e in range(N_EXPERTS):
        y = (jax.nn.silu(h @ w1[e]) * (h @ w3[e])) @ w2[e]
        out = out + gates[..., e:e + 1].astype(h.dtype) * y
    return out


def setup_inputs(seed: int = 0) -> dict:
    key = jax.random.key(seed)
    ks = jax.random.split(key, 28)
    f32 = jnp.float32

    def nrm(k, shape, scale):
        return jax.random.normal(k, shape, f32) * scale

    lru_a8 = jax.random.uniform(ks[11], (DEPTH, 2, MIX_W), f32, 0.9, 0.999)
    lru_a = lru_a8 ** (1.0 / LRU_C)
    gamma = 1.0 - 2.0 ** (-5.0 - jnp.arange(N_HEADS, dtype=f32))
    theta0 = jnp.log(gamma) - jnp.log1p(-gamma)
    zero_h = jnp.zeros((N_HEADS,), f32)
    f_bias = jnp.linspace(3.0, 6.0, N_HEADS, dtype=f32)
    gate_bias0 = jnp.stack([zero_h, f_bias, zero_h, f_bias])
    return {
        'x': nrm(ks[0], (BATCH, SEQ, D_MODEL), 1.0),
        'c': nrm(ks[1], (BATCH, D_MODEL), 1.0),
        'ctx': nrm(ks[2], (BATCH, CTX_LEN, D_MODEL), 1.0),
        'c_ctx': nrm(ks[3], (D_MODEL,), 1.0),
        'w_mod': nrm(ks[4], (DEPTH, D_MODEL, N_MOD * D_MODEL), 0.5 * D_MODEL ** -0.5),
        'b_mod': nrm(ks[5], (DEPTH, N_MOD * D_MODEL), 0.02),
        'w_in': nrm(ks[6], (DEPTH, D_MODEL, IN_W), D_MODEL ** -0.5),
        'a_conv_w': nrm(ks[7], (DEPTH, CONV_W, MIX_W), CONV_W ** -0.5),
        'a_conv_b': nrm(ks[8], (DEPTH, MIX_W), 0.02),
        'a_gate_w': nrm(ks[9], (DEPTH, 2, 2, N_HEADS, HEAD_DIM, HEAD_DIM), HEAD_DIM ** -0.5),
        'a_gate_b': nrm(ks[10], (DEPTH, 2, 2, MIX_W), 0.02),
        'a_lambda': jnp.log(lru_a) - jnp.log1p(-lru_a),
        'b_theta': theta0 + nrm(ks[12], (DEPTH, 2, N_HEADS), 0.01),
        'c_conv_w': nrm(ks[13], (DEPTH, CONV_W, 2 * MIX_W), CONV_W ** -0.5),
        'c_conv_b': nrm(ks[14], (DEPTH, 2 * MIX_W), 0.02),
        'c_gate_b': gate_bias0 + nrm(ks[15], (DEPTH, 4, N_HEADS), 0.1),
        'd_lb': nrm(ks[16], (DEPTH, MIX_W), 1.0),
        'w_branch': nrm(ks[17], (DEPTH, N_BRANCH, MIX_W, D_MODEL), MIX_W ** -0.5),
        'w_out': nrm(ks[18], (DEPTH, D_MODEL, D_MODEL), D_MODEL ** -0.5),
        'moe_w_group': nrm(ks[19], (DEPTH, D_MODEL, N_GROUPS), D_MODEL ** -0.5),
        'moe_b_group': nrm(ks[20], (DEPTH, N_GROUPS), 0.01),
        'moe_w_router': nrm(ks[21], (DEPTH, D_MODEL, N_EXPERTS), D_MODEL ** -0.5),
        'moe_b_router': nrm(ks[22], (DEPTH, N_EXPERTS), 0.01),
        'moe_w1': nrm(ks[23], (DEPTH, N_EXPERTS, D_MODEL, D_EXPERT), D_MODEL ** -0.5),
        'moe_w3': nrm(ks[24], (DEPTH, N_EXPERTS, D_MODEL, D_EXPERT), D_MODEL ** -0.5),
        'moe_w2': nrm(ks[25], (DEPTH, N_EXPERTS, D_EXPERT, D_MODEL), D_EXPERT ** -0.5),
        'final_norm_w': 1.0 + nrm(ks[26], (D_MODEL,), 0.02),
    }


def reference(x, c, ctx, c_ctx, w_mod, b_mod, w_in, a_conv_w, a_conv_b, a_gate_w, a_gate_b,
              a_lambda, b_theta, c_conv_w, c_conv_b, c_gate_b, d_lb, w_branch, w_out,
              moe_w_group, moe_b_group, moe_w_router, moe_b_router, moe_w1, moe_w3, moe_w2,
              final_norm_w):
    n_lat = x.shape[1]
    rows = n_lat // GRID_W
    row = jnp.repeat(jnp.arange(rows), GRID_W)
    col = jnp.tile(jnp.arange(GRID_W), rows)
    cos, sin = rope_tables(row, col)
    lbs = jnp.cumsum(jax.nn.softmax(d_lb.astype(jnp.float32), axis=0), axis=0)
    lbs = lbs - lbs[0]
    s_c = jax.nn.silu(c)
    s_cc = jax.nn.silu(c_ctx)
    for l in range(DEPTH):
        need_ctx = l < DEPTH - 1
        mx = (s_c @ w_mod[l] + b_mod[l]).reshape(-1, N_MOD, 1, D_MODEL)
        mc = (s_cc @ w_mod[l] + b_mod[l]).reshape(N_MOD, D_MODEL)
        hx = modulate(x, mx[:, 0], mx[:, 1])
        hc = modulate(ctx, mc[0], mc[1])
        mix_c, mix_x = token_mixers(hc, hx, w_in[l], a_conv_w[l], a_conv_b[l], a_gate_w[l], a_gate_b[l],
                                    a_lambda[l], b_theta[l], c_conv_w[l], c_conv_b[l], c_gate_b[l],
                                    lbs[l], w_branch[l], w_out[l], cos, sin, need_ctx)
        moe_prm = (moe_w_group[l], moe_b_group[l], moe_w_router[l], moe_b_router[l],
                   moe_w1[l], moe_w3[l], moe_w2[l])
        x = x + mx[:, 2] * mix_x
        x = x + mx[:, 5] * hier_moe(modulate(x, mx[:, 3], mx[:, 4]), *moe_prm)
        if need_ctx:
            ctx = ctx + mc[2] * mix_c
            ctx = ctx + mc[5] * hier_moe(modulate(ctx, mc[3], mc[4]), *moe_prm)
    return rmsnorm(x) * final_norm_w
```

```python
import functools

import jax
import jax.numpy as jnp
from jax import lax
from jax.experimental import pallas as pl
from jax.experimental.pallas import tpu as pltpu

F32 = jnp.float32
BF16 = jnp.bfloat16

EPS = 1e-6
N_HEADS = 4
HEAD_DIM = 64
MIX_W = N_HEADS * HEAD_DIM
N_BRANCH = 4
CONV_W = 4
LRU_C = 8.0
GRID_W = 64
ROPE_BASE = 10000.0
N_GROUPS = 4
EXP_PER_GROUP = 4
N_EXPERTS = N_GROUPS * EXP_PER_GROUP
D_EXPERT = 512
N_MOD = 6
M_INIT = -1e30

CHUNK = 128
SUB = 16
HALO = 16
LANES = 128
MOD_ROWS = 16
VMEM_LIMIT_BYTES = 56 * 1024 * 1024
NEG_INF = float("-inf")


def _cparams(*sem):
    return pltpu.CompilerParams(dimension_semantics=sem, vmem_limit_bytes=VMEM_LIMIT_BYTES)


def _token_tile(n, cap):
    best = 16
    for t in range(16, cap + 1, 16):
        if n % t == 0:
            best = t
    return best


def _modulate(x, shift, scale):
    ms = jnp.mean(x * x, axis=-1, keepdims=True)
    return x * lax.rsqrt(ms + EPS) * (1.0 + scale) + shift


def _pick(m, is_ctx, k):
    return jnp.where(is_ctx, m[8 + k:9 + k], m[k:k + 1])


def _sigmoid(x):
    return 1.0 / (1.0 + jnp.exp(-x))


def _silu(x):
    return x * _sigmoid(x)


def _log_sigmoid(x):
    return jnp.minimum(x, 0.0) - jnp.log(1.0 + jnp.exp(-jnp.abs(x)))


def _split3(x):
    hi = x.astype(BF16)
    r = x - hi.astype(F32)
    mid = r.astype(BF16)
    lo = (r - mid.astype(F32)).astype(BF16)
    return hi, mid, lo


def _sel_dot(sel, x):
    hi, mid, lo = _split3(x)
    d = functools.partial(jnp.dot, preferred_element_type=F32)
    return d(sel, hi) + d(sel, mid) + d(sel, lo)


def _dot_sel(x, sel):
    hi, mid, lo = _split3(x)
    d = functools.partial(jnp.dot, preferred_element_type=F32)
    return d(hi, sel) + d(mid, sel) + d(lo, sel)


def _dot_nt(a, b):
    return lax.dot_general(a, b, (((1,), (1,)), ((), ())), preferred_element_type=F32)


def _dot_tn(a, b):
    return lax.dot_general(a, b, (((0,), (0,)), ((), ())), preferred_element_type=F32)


def _head_of_lane(width):
    return lax.broadcasted_iota(jnp.int32, (1, width), 1) // HEAD_DIM


def _block_ones(width):
    r = lax.broadcasted_iota(jnp.int32, (width, width), 0) // HEAD_DIM
    c = lax.broadcasted_iota(jnp.int32, (width, width), 1) // HEAD_DIM
    return r == c


def _head_norm(o, center):
    ones = jnp.where(_block_ones(MIX_W), 1.0, 0.0).astype(BF16)
    inv = 1.0 / HEAD_DIM
    if center:
        o = o - _dot_sel(o, ones) * inv
    var = _dot_sel(o * o, ones) * inv
    return o * lax.rsqrt(var + EPS)


def _bwd_chunk(s, n_ctx_chunks, n_chunks):
    return jnp.where(s < n_ctx_chunks, n_ctx_chunks - 1 - s, n_chunks - 1 + n_ctx_chunks - s)


def _conv_chunk(ref, c, w, b, *, n, n_ctx, width0=0, width=None):
    L = CHUNK
    wl = L + 2 * HALO
    r0 = c * L
    start = pl.multiple_of(jnp.clip(r0 - HALO, 0, n - wl), HALO)
    off = r0 - start
    if width is None:
        win = ref[0, pl.ds(start, wl), :]
    else:
        win = ref[0, pl.ds(start, wl), width0:width0 + width]
    ri = lax.broadcasted_iota(jnp.int32, (L, wl), 0)
    mi = lax.broadcasted_iota(jnp.int32, (L, wl), 1)
    g = r0 + ri
    y = b
    seg_g = jnp.where(g < n_ctx, 0, 1)
    for k in range(CONV_W):
        src = g + (k - 2)
        hit = (mi == ri + off + (k - 2)) & (jnp.where(src < n_ctx, 0, 1) == seg_g) & (src >= 0) & (src < n)
        sel = jnp.where(hit, 1.0, 0.0).astype(BF16)
        y = y + w[k:k + 1] * jnp.dot(sel, win, preferred_element_type=F32)
    return y


def _mod_kernel(cc_ref, w_ref, b_ref, o_ref):
    s = _silu(cc_ref[...])
    o_ref[0] = jnp.dot(s, w_ref[0], precision=lax.Precision.HIGHEST, preferred_element_type=F32) + b_ref[0]


def _mod_vectors(cc, w_mod, b_mod):
    depth, d, dm = w_mod.shape
    tn = 1536
    return pl.pallas_call(
        _mod_kernel,
        grid=(depth, dm // tn),
        in_specs=[pl.BlockSpec((8, d), lambda l, j: (0, 0)),
                  pl.BlockSpec((1, d, tn), lambda l, j: (l, 0, j)),
                  pl.BlockSpec((1, 1, tn), lambda l, j: (l, 0, j))],
        out_specs=pl.BlockSpec((1, 8, tn), lambda l, j: (l, 0, j)),
        out_shape=jax.ShapeDtypeStruct((depth, 8, dm), F32),
        compiler_params=_cparams("parallel", "parallel"),
        name="mod_vectors",
    )(cc, w_mod, b_mod.reshape(depth, 1, dm))


def _in_proj_kernel(x_ref, m_ref, w_ref, wg_ref, p_ref, g_ref, *, tm, n_ctx, col_chunk):
    i = pl.program_id(1)
    row = i * tm + lax.broadcasted_iota(jnp.int32, (tm, 1), 0)
    is_ctx = row < n_ctx
    m = m_ref[0]
    h = _modulate(x_ref[0], _pick(m, is_ctx, 0), _pick(m, is_ctx, 1)).astype(BF16)
    for j in range(w_ref.shape[1] // col_chunk):
        sl = slice(j * col_chunk, (j + 1) * col_chunk)
        p_ref[0, :, sl] = jnp.dot(h, w_ref[:, sl], preferred_element_type=F32).astype(BF16)
    g_ref[0] = jnp.dot(h, wg_ref[...], preferred_element_type=F32)


def _in_proj(xc, mvec, w_mix, w_gate, n_ctx):
    b, n, d = xc.shape
    tm = _token_tile(n, 640)
    wc = w_mix.shape[1]
    kern = functools.partial(_in_proj_kernel, tm=tm, n_ctx=n_ctx, col_chunk=768)
    return pl.pallas_call(
        kern,
        grid=(b, n // tm),
        in_specs=[pl.BlockSpec((1, tm, d), lambda bi, i: (bi, i, 0)),
                  pl.BlockSpec((1, MOD_ROWS, d), lambda bi, i: (bi, 0, 0)),
                  pl.BlockSpec((d, wc), lambda bi, i: (0, 0)),
                  pl.BlockSpec((d, LANES), lambda bi, i: (0, 0))],
        out_specs=[pl.BlockSpec((1, tm, wc), lambda bi, i: (bi, i, 0)),
                   pl.BlockSpec((1, tm, LANES), lambda bi, i: (bi, i, 0))],
        out_shape=[jax.ShapeDtypeStruct((b, n, wc), BF16),
                   jax.ShapeDtypeStruct((b, n, LANES), F32)],
        compiler_params=_cparams("parallel", "parallel"),
        name="in_proj",
    )(xc, mvec, w_mix, w_gate)


def _lin_scan(a, x, reverse):
    L = a.shape[0]
    row = lax.broadcasted_iota(jnp.int32, (L, 1), 0)
    k = 1
    while k < L:
        if reverse:
            a_s = pltpu.roll(a, L - k, 0)
            x_s = pltpu.roll(x, L - k, 0)
            valid = row < L - k
        else:
            a_s = pltpu.roll(a, k, 0)
            x_s = pltpu.roll(x, k, 0)
            valid = row >= k
        x = jnp.where(valid, a * x_s + x, x)
        a = jnp.where(valid, a * a_s, a)
        k *= 2
    return a, x


def _mixer_a_kernel(ax_ref, ag_ref, cw_ref, cb_ref, gw_ref, gb_ref, lam_ref, out_ref, u_s, hf_s,
                    *, n, n_ctx):
    L = CHUNK
    n_chunks = n // L
    n_ctx_chunks = n_ctx // L
    cw = cw_ref[...]
    cb = cb_ref[...]
    log_lam = _log_sigmoid(lam_ref[...])

    def direction(d, c, carry):
        r0 = pl.multiple_of(c * L, L)
        u = u_s[pl.ds(r0, L), :]
        pre = jnp.dot(u.astype(BF16), gw_ref[:, d * 2 * MIX_W:(d + 1) * 2 * MIX_W],
                      preferred_element_type=F32) + gb_ref[:, d * 2 * MIX_W:(d + 1) * 2 * MIX_W]
        r = _sigmoid(pre[:, :MIX_W])
        gi = _sigmoid(pre[:, MIX_W:])
        log_a = LRU_C * r * log_lam[d:d + 1]
        a = jnp.exp(log_a)
        inp = jnp.sqrt(1.0 - jnp.exp(2.0 * log_a)) * (gi * u)
        a_cum, h = _lin_scan(a, inp, reverse=(d == 1))
        h = h + a_cum * carry
        new_carry = h[0:1] if d == 1 else h[L - 1:L]
        return r0, h, new_carry

    def conv_body(c, _):
        r0 = pl.multiple_of(c * L, L)
        u_s[pl.ds(r0, L), :] = _conv_chunk(ax_ref, c, cw, cb, n=n, n_ctx=n_ctx)
        return 0

    lax.fori_loop(0, n_chunks, conv_body, 0)

    def fwd_body(c, carry):
        r0, h, carry = direction(0, c, carry)
        hf_s[pl.ds(r0, L), :] = h
        return carry

    lax.fori_loop(0, n_chunks, fwd_body, jnp.zeros((1, MIX_W), F32))

    def bwd_body(s, carry):
        c = _bwd_chunk(s, n_ctx_chunks, n_chunks)
        r0, h, carry = direction(1, c, carry)
        gate = jax.nn.gelu(ag_ref[0, pl.ds(r0, L), :].astype(F32), approximate=True)
        out_ref[0, pl.ds(r0, L), :] = (gate * (hf_s[pl.ds(r0, L), :] + h)).astype(out_ref.dtype)
        return carry

    lax.fori_loop(0, n_chunks, bwd_body, jnp.zeros((1, MIX_W), F32))


def _seq_spec(n, col):
    return pl.BlockSpec((1, n, MIX_W), lambda b: (b, 0, col))


def _full_spec(shape):
    return pl.BlockSpec(shape, lambda b: (0,) * len(shape))


def _mixer_a(p, conv_w, conv_b, gate_w, gate_b, lam, n_ctx):
    b, n, _ = p.shape
    kern = functools.partial(_mixer_a_kernel, n=n, n_ctx=n_ctx)
    return pl.pallas_call(
        kern,
        grid=(b,),
        in_specs=[_seq_spec(n, 0), _seq_spec(n, 1),
                  _full_spec(conv_w.shape), _full_spec(conv_b.shape),
                  _full_spec(gate_w.shape), _full_spec(gate_b.shape), _full_spec(lam.shape)],
        out_specs=pl.BlockSpec((1, n, MIX_W), lambda bi: (bi, 0, 0)),
        out_shape=jax.ShapeDtypeStruct((b, n, MIX_W), BF16),
        scratch_shapes=[pltpu.VMEM((n, MIX_W), F32), pltpu.VMEM((n, MIX_W), F32)],
        compiler_params=_cparams("parallel"),
        name="mixer_rglru",
    )(p, p, conv_w, conv_b, gate_w, gate_b, lam)


def _rope(t, cos, sin):
    lane = lax.broadcasted_iota(jnp.int32, (1, MIX_W), 1) % HEAD_DIM
    half = HEAD_DIM // 2
    swapped = jnp.where(lane < half, pltpu.roll(t, MIX_W - half, 1), pltpu.roll(t, half, 1))
    return t * cos + swapped * sin


def _mixer_b_kernel(q_ref, k_ref, v_ref, g_ref, cos_ref, sin_ref, lg_ref, out_ref,
                    ob_s, dec_s, st_s, *, n, n_ctx):
    L = CHUNK
    n_chunks = n // L
    n_ctx_chunks = n_ctx // L
    scale = HEAD_DIM ** -0.5
    head = _head_of_lane(MIX_W)
    bd = _block_ones(MIX_W)
    lgf = lg_ref[0:1, :]
    lgb = lg_ref[1:2, :]
    pos = lax.broadcasted_iota(jnp.int32, (L, 1), 0).astype(F32)
    diff = (lax.broadcasted_iota(jnp.int32, (L, L), 0) - lax.broadcasted_iota(jnp.int32, (L, L), 1)).astype(F32)
    for h in range(N_HEADS):
        lf = lgf[:, h * HEAD_DIM:h * HEAD_DIM + 1]
        lb = lgb[:, h * HEAD_DIM:h * HEAD_DIM + 1]
        dec_s[h] = jnp.where(diff >= 0, jnp.exp(lf * jnp.maximum(diff, 0.0)), jnp.exp(lb * jnp.maximum(-diff, 0.0)))

    def load(c):
        r0 = pl.multiple_of(c * L, L)
        rows = pl.ds(r0, L)
        cos = cos_ref[rows, :]
        sin = sin_ref[rows, :]
        q = _rope(q_ref[0, rows, :].astype(F32), cos, sin)
        k = _rope(k_ref[0, rows, :].astype(F32), cos, sin) * scale
        v = v_ref[0, rows, :]
        return rows, q, k, v

    def state_step(d, q, k, v, lg, q_exp, k_exp):
        o_inter = jnp.dot((q * jnp.exp(lg * q_exp)).astype(BF16), st_s[d].astype(BF16), preferred_element_type=F32)
        kv = _dot_tn((k * jnp.exp(lg * k_exp)).astype(BF16), v)
        st_s[d] = jnp.exp(lg * float(L)) * st_s[d] + jnp.where(bd, kv, 0.0)
        return o_inter

    st_s[...] = jnp.zeros_like(st_s)

    def bwd_body(s, _):
        c = _bwd_chunk(s, n_ctx_chunks, n_chunks)
        rows, q, k, v = load(c)
        ob_s[rows, :] = state_step(1, q, k, v, lgb, float(L) - pos, pos)
        return 0

    lax.fori_loop(0, n_chunks, bwd_body, 0)

    def fwd_body(c, _):
        rows, q, k, v = load(c)
        o = ob_s[rows, :] + state_step(0, q, k, v, lgf, pos + 1.0, float(L) - 1.0 - pos)
        kb = k.astype(BF16)
        for h in range(N_HEADS):
            hm = head == h
            sc = _dot_nt(jnp.where(hm, q, 0.0).astype(BF16), kb) * dec_s[h]
            o = o + jnp.where(hm, jnp.dot(sc.astype(BF16), v, preferred_element_type=F32), 0.0)
        gate = _silu(g_ref[0, rows, :].astype(F32))
        out_ref[0, rows, :] = (_head_norm(o, True) * gate).astype(out_ref.dtype)
        return 0

    lax.fori_loop(0, n_chunks, fwd_body, 0)


def _mixer_b(p, cos, sin, log_gamma, n_ctx):
    b, n, _ = p.shape
    kern = functools.partial(_mixer_b_kernel, n=n, n_ctx=n_ctx)
    return pl.pallas_call(
        kern,
        grid=(b,),
        in_specs=[_seq_spec(n, 2), _seq_spec(n, 3), _seq_spec(n, 4), _seq_spec(n, 5),
                  _full_spec(cos.shape), _full_spec(sin.shape), _full_spec(log_gamma.shape)],
        out_specs=pl.BlockSpec((1, n, MIX_W), lambda bi: (bi, 0, 0)),
        out_shape=jax.ShapeDtypeStruct((b, n, MIX_W), BF16),
        scratch_shapes=[pltpu.VMEM((n, MIX_W), F32),
                        pltpu.VMEM((N_HEADS, CHUNK, CHUNK), F32),
                        pltpu.VMEM((2, MIX_W, MIX_W), F32)],
        compiler_params=_cparams("parallel"),
        name="mixer_retention",
    )(p, p, p, p, cos, sin, log_gamma)


def _mixer_c_kernel(q_ref, k_ref, v_ref, o_ref, g_ref, cw_ref, cb_ref, gb_ref, out_ref,
                    qs_s, ks_s, hf_s, c_s, n_s, m_s, *, n, n_ctx):
    L = CHUNK
    n_chunks = n // L
    n_ctx_chunks = n_ctx // L
    head = _head_of_lane(MIX_W)
    bd = _block_ones(MIX_W)
    cw = cw_ref[...]
    cb = cb_ref[...]
    ri = lax.broadcasted_iota(jnp.int32, (L, L), 0)
    ci = lax.broadcasted_iota(jnp.int32, (L, L), 1)
    gl = lax.broadcasted_iota(jnp.int32, (LANES, MIX_W), 0)
    hl = lax.broadcasted_iota(jnp.int32, (LANES, MIX_W), 1) // HEAD_DIM

    def expand(kind):
        return jnp.where(gl == kind * N_HEADS + hl, 1.0, 0.0).astype(BF16)

    def conv_body(c, _):
        rows = pl.ds(pl.multiple_of(c * L, L), L)
        qs_s[rows, :] = _silu(_conv_chunk(q_ref, c, cw[:, :MIX_W], cb[:, :MIX_W], n=n, n_ctx=n_ctx)).astype(BF16)
        ks_s[rows, :] = (_silu(_conv_chunk(k_ref, c, cw[:, MIX_W:], cb[:, MIX_W:], n=n, n_ctx=n_ctx))
                         * HEAD_DIM ** -0.5).astype(BF16)
        return 0

    lax.fori_loop(0, n_chunks, conv_body, 0)

    def chunk(d, c):
        rev = d == 1
        rows = pl.ds(pl.multiple_of(c * L, L), L)
        q = qs_s[rows, :]
        k = ks_s[rows, :]
        v = v_ref[0, rows, :]
        g = g_ref[0, rows, :] + gb_ref[...]
        log_i = _dot_sel(g, expand(2 * d))
        log_f = _dot_sel(_log_sigmoid(g), expand(2 * d + 1))
        causal = (ci >= ri) if rev else (ci <= ri)
        tri = jnp.where(causal, 1.0, 0.0).astype(BF16)
        cum = _sel_dot(tri, log_f)
        cum_end = cum[0:1] if rev else cum[L - 1:L]
        m_prev = m_s[d]
        c_prev = c_s[d]
        n_prev = n_s[d]
        row_src = (log_i - cum).T
        m_inter = cum + m_prev
        num_inter = jnp.dot(q, c_prev.astype(BF16), preferred_element_type=F32)
        qn = q.astype(F32) * n_prev
        kf = k
        hid = jnp.zeros((L, MIX_W), F32)
        for h in range(N_HEADS):
            hm = head == h
            lane0 = h * HEAD_DIM
            log_d = jnp.where(causal, cum[:, lane0:lane0 + 1] + row_src[lane0:lane0 + 1, :], NEG_INF)
            m_q = jnp.maximum(jnp.max(log_d, axis=1, keepdims=True), m_inter[:, lane0:lane0 + 1])
            sc = _dot_nt(jnp.where(hm, q, jnp.zeros_like(q)), kf) * jnp.exp(log_d - m_q)
            s_inter = jnp.exp(m_inter[:, lane0:lane0 + 1] - m_q)
            num = jnp.dot(sc.astype(BF16), v, preferred_element_type=F32) + s_inter * num_inter
            den = (jnp.sum(sc, axis=1, keepdims=True)
                   + s_inter * jnp.sum(jnp.where(hm, qn, 0.0), axis=1, keepdims=True))
            hid = hid + jnp.where(hm, num / jnp.maximum(jnp.abs(den), jnp.exp(-m_q)), 0.0)
        log_w = cum_end - cum + log_i
        m_loc = jnp.max(log_w, axis=0, keepdims=True)
        kw = k.astype(F32) * jnp.exp(log_w - m_loc)
        c_loc = jnp.where(bd, _dot_tn(kw.astype(BF16), v), 0.0)
        n_loc = jnp.sum(kw, axis=0, keepdims=True)
        m_new = jnp.maximum(cum_end + m_prev, m_loc)
        s_old = jnp.exp(cum_end + m_prev - m_new)
        s_loc = jnp.exp(m_loc - m_new)
        c_s[d] = s_old * c_prev + s_loc * c_loc
        n_s[d] = s_old * n_prev + s_loc * n_loc
        m_s[d] = m_new
        return rows, hid

    c_s[...] = jnp.zeros_like(c_s)
    n_s[...] = jnp.zeros_like(n_s)
    m_s[...] = jnp.full(m_s.shape, M_INIT, F32)

    def fwd_body(c, _):
        rows, hid = chunk(0, c)
        hf_s[rows, :] = hid
        return 0

    lax.fori_loop(0, n_chunks, fwd_body, 0)

    def bwd_body(s, _):
        rows, hid = chunk(1, _bwd_chunk(s, n_ctx_chunks, n_chunks))
        gate = _sigmoid(o_ref[0, rows, :].astype(F32))
        out_ref[0, rows, :] = (gate * _head_norm(hf_s[rows, :] + hid, True)).astype(out_ref.dtype)
        return 0

    lax.fori_loop(0, n_chunks, bwd_body, 0)


def _mixer_c(p, gates, conv_w, conv_b, gate_b, n_ctx):
    b, n, _ = p.shape
    kern = functools.partial(_mixer_c_kernel, n=n, n_ctx=n_ctx)
    return pl.pallas_call(
        kern,
        grid=(b,),
        in_specs=[_seq_spec(n, 6), _seq_spec(n, 7), _seq_spec(n, 8), _seq_spec(n, 9),
                  pl.BlockSpec((1, n, LANES), lambda bi: (bi, 0, 0)),
                  _full_spec(conv_w.shape), _full_spec(conv_b.shape), _full_spec(gate_b.shape)],
        out_specs=pl.BlockSpec((1, n, MIX_W), lambda bi: (bi, 0, 0)),
        out_shape=jax.ShapeDtypeStruct((b, n, MIX_W), BF16),
        scratch_shapes=[pltpu.VMEM((n, MIX_W), BF16), pltpu.VMEM((n, MIX_W), BF16),
                        pltpu.VMEM((n, MIX_W), F32),
                        pltpu.VMEM((2, MIX_W, MIX_W), F32),
                        pltpu.VMEM((2, 1, MIX_W), F32),
                        pltpu.VMEM((2, 1, MIX_W), F32)],
        compiler_params=_cparams("parallel"),
        name="mixer_mlstm",
    )(p, p, p, p, gates, conv_w, conv_b, gate_b)


def _mixer_d_kernel(q_ref, ff_ref, fb_ref, v_ref, g_ref, lb_ref, out_ref, of_s, st_s, *, n, n_ctx):
    S = SUB
    n_blocks = n // S
    n_ctx_blocks = n_ctx // S
    lb = lb_ref[...]
    ones_bd = jnp.where(_block_ones(MIX_W), 1.0, 0.0).astype(BF16)
    bd = _block_ones(MIX_W)
    ri = lax.broadcasted_iota(jnp.int32, (S, S), 0)
    ci = lax.broadcasted_iota(jnp.int32, (S, S), 1)
    pair = lax.broadcasted_iota(jnp.int32, (S * S, 1), 0)
    pi = pair // S
    pj = pair % S
    fold = jnp.where(lax.broadcasted_iota(jnp.int32, (S, S * S), 1) // S
                     == lax.broadcasted_iota(jnp.int32, (S, S * S), 0), 1.0, 0.0).astype(BF16)

    def block(d, blk):
        rev = d == 1
        rows = pl.ds(pl.multiple_of(blk * S, S), S)
        q = _silu(q_ref[0, rows, :].astype(F32))
        raw = (fb_ref if rev else ff_ref)[0, rows, :].astype(F32)
        v = v_ref[0, rows, :]
        f = lb + (1.0 - lb) * _sigmoid(raw)
        k = 1.0 - f
        log_f = jnp.log(f)
        tri = jnp.where((ci >= ri) if rev else (ci <= ri), 1.0, 0.0).astype(BF16)
        cum = _sel_dot(tri, log_f)
        tot = cum[0:1] if rev else cum[S - 1:S]
        st = st_s[d]
        o = _dot_nt((q * jnp.exp(cum)).astype(BF16), st.astype(BF16))
        cum_i = jnp.concatenate([jnp.broadcast_to(cum[i:i + 1], (S, MIX_W)) for i in range(S)], axis=0)
        q_i = jnp.concatenate([jnp.broadcast_to(q[i:i + 1], (S, MIX_W)) for i in range(S)], axis=0)
        cum_j = jnp.concatenate([cum] * S, axis=0)
        k_j = jnp.concatenate([k] * S, axis=0)
        v_j = jnp.concatenate([v.astype(F32)] * S, axis=0)
        vis = (pj >= pi) if rev else (pj <= pi)
        dec = jnp.exp(jnp.where(vis, cum_i - cum_j, NEG_INF))
        scores = jnp.dot((q_i * k_j * dec).astype(BF16), ones_bd, preferred_element_type=F32)
        o = o + jnp.dot(fold, (scores * v_j).astype(BF16), preferred_element_type=F32)
        kv_t = _dot_tn(v, (k * jnp.exp(tot - cum)).astype(BF16))
        st_s[d] = st * jnp.exp(tot) + jnp.where(bd, kv_t, 0.0)
        return rows, o

    st_s[...] = jnp.zeros_like(st_s)

    def fwd_body(blk, _):
        rows, o = block(0, blk)
        of_s[rows, :] = o
        return 0

    lax.fori_loop(0, n_blocks, fwd_body, 0)

    def bwd_body(s, _):
        rows, o = block(1, _bwd_chunk(s, n_ctx_blocks, n_blocks))
        gate = _silu(g_ref[0, rows, :].astype(F32))
        out_ref[0, rows, :] = (_head_norm(of_s[rows, :] + o, False) * gate).astype(out_ref.dtype)
        return 0

    lax.fori_loop(0, n_blocks, bwd_body, 0)


def _mixer_d(p, lb, n_ctx):
    b, n, _ = p.shape
    kern = functools.partial(_mixer_d_kernel, n=n, n_ctx=n_ctx)
    return pl.pallas_call(
        kern,
        grid=(b,),
        in_specs=[_seq_spec(n, 10), _seq_spec(n, 11), _seq_spec(n, 12), _seq_spec(n, 13), _seq_spec(n, 14),
                  _full_spec(lb.shape)],
        out_specs=pl.BlockSpec((1, n, MIX_W), lambda bi: (bi, 0, 0)),
        out_shape=jax.ShapeDtypeStruct((b, n, MIX_W), BF16),
        scratch_shapes=[pltpu.VMEM((n, MIX_W), F32), pltpu.VMEM((2, MIX_W, MIX_W), F32)],
        compiler_params=_cparams("parallel"),
        name="mixer_hgrn2",
    )(p, p, p, p, p, lb)


def _merge_kernel(x_ref, m_ref, ya_ref, yb_ref, yc_ref, yd_ref, wm_ref, wb_ref, wo_ref, wr_ref, br_ref,
                  xo_ref, h2_ref, gates_ref, *, tm, n_ctx):
    i = pl.program_id(1)
    row = i * tm + lax.broadcasted_iota(jnp.int32, (tm, 1), 0)
    is_ctx = row < n_ctx
    m = m_ref[0]
    x = x_ref[0]
    d = x.shape[-1]
    h = _modulate(x, _pick(m, is_ctx, 0), _pick(m, is_ctx, 1)).astype(BF16)
    z = jnp.zeros((tm, d), F32)
    for nb, y_ref in enumerate((ya_ref, yb_ref, yc_ref, yd_ref)):
        gate = _sigmoid(jnp.dot(h, wm_ref[:, nb * d:(nb + 1) * d], preferred_element_type=F32))
        z = z + gate * jnp.dot(y_ref[0], wb_ref[nb], preferred_element_type=F32)
    mix = jnp.dot(z.astype(BF16), wo_ref[...], preferred_element_type=F32)
    x1 = x + _pick(m, is_ctx, 2) * mix
    xo_ref[0] = x1
    h2 = _modulate(x1, _pick(m, is_ctx, 3), _pick(m, is_ctx, 4))
    h2_ref[0] = h2.astype(BF16)

    logit = jnp.dot(h2, wr_ref[...], precision=lax.Precision.HIGHEST, preferred_element_type=F32) + br_ref[...]
    lane = lax.broadcasted_iota(jnp.int32, (tm, LANES), 1)
    big = jnp.int32(LANES)
    is_group = (lane >= N_EXPERTS) & (lane < N_EXPERTS + N_GROUPS)
    gl = jnp.where(is_group, logit, NEG_INF)
    g_max = jnp.max(gl, axis=1, keepdims=True)
    g_idx = jnp.min(jnp.where(gl == g_max, lane, big), axis=1, keepdims=True) - N_EXPERTS
    g_prob = 1.0 / jnp.sum(jnp.where(is_group, jnp.exp(logit - g_max), 0.0), axis=1, keepdims=True)
    in_group = (lane < N_EXPERTS) & (lane // EXP_PER_GROUP == g_idx)
    el = jnp.where(in_group, logit, NEG_INF)
    v1 = jnp.max(el, axis=1, keepdims=True)
    i1 = jnp.min(jnp.where(el == v1, lane, big), axis=1, keepdims=True)
    el2 = jnp.where(lane == i1, NEG_INF, el)
    v2 = jnp.max(el2, axis=1, keepdims=True)
    i2 = jnp.min(jnp.where(el2 == v2, lane, big), axis=1, keepdims=True)
    e2 = jnp.exp(v2 - v1)
    w1 = g_prob / (1.0 + e2)
    w2 = g_prob * e2 / (1.0 + e2)
    gates_ref[0] = jnp.where(lane == i1, w1, 0.0) + jnp.where(lane == i2, w2, 0.0)


def _merge(xc, mvec, ys, w_merge, w_branch, w_out, w_route, b_route, n_ctx):
    b, n, d = xc.shape
    tm = _token_tile(n, 320)
    kern = functools.partial(_merge_kernel, tm=tm, n_ctx=n_ctx)
    tok = lambda w: pl.BlockSpec((1, tm, w), lambda bi, i: (bi, i, 0))
    const = lambda shape: pl.BlockSpec(shape, lambda bi, i: (0,) * len(shape))
    return pl.pallas_call(
        kern,
        grid=(b, n // tm),
        in_specs=[tok(d), pl.BlockSpec((1, MOD_ROWS, d), lambda bi, i: (bi, 0, 0)),
                  tok(MIX_W), tok(MIX_W), tok(MIX_W), tok(MIX_W),
                  const(w_merge.shape), const(w_branch.shape), const(w_out.shape),
                  const(w_route.shape), const(b_route.shape)],
        out_specs=[tok(d), tok(d), tok(LANES)],
        out_shape=[jax.ShapeDtypeStruct((b, n, d), F32),
                   jax.ShapeDtypeStruct((b, n, d), BF16),
                   jax.ShapeDtypeStruct((b, n, LANES), F32)],
        compiler_params=_cparams("parallel", "parallel"),
        name="merge_route",
    )(xc, mvec, *ys, w_merge, w_branch, w_out, w_route, b_route)


def _moe_kernel(x_ref, m_ref, h_ref, gates_ref, w1_ref, w3_ref, w2_ref, o_ref, acc_s, *, tm, n_ctx):
    i = pl.program_id(1)
    e = pl.program_id(2)

    @pl.when(e == 0)
    def _():
        acc_s[...] = jnp.zeros_like(acc_s)

    h = h_ref[0]
    a = jnp.dot(h, w1_ref[0, 0], preferred_element_type=F32)
    g = jnp.dot(h, w3_ref[0, 0], preferred_element_type=F32)
    y = jnp.dot((_silu(a) * g).astype(BF16), w2_ref[0, 0], preferred_element_type=F32)
    lane = lax.broadcasted_iota(jnp.int32, (tm, LANES), 1)
    gate = jnp.sum(jnp.where(lane == e, gates_ref[0], 0.0), axis=1, keepdims=True)
    acc_s[...] += gate * y

    @pl.when(e == N_EXPERTS - 1)
    def _():
        row = i * tm + lax.broadcasted_iota(jnp.int32, (tm, 1), 0)
        o_ref[0] = x_ref[0] + _pick(m_ref[0], row < n_ctx, 5) * acc_s[...]


def _moe(x1, mvec, h2, gates, w1, w3, w2, layer, n_ctx):
    b, n, d = x1.shape
    tm = _token_tile(n, 1100)
    kern = functools.partial(_moe_kernel, tm=tm, n_ctx=n_ctx)
    tok = lambda w: pl.BlockSpec((1, tm, w), lambda bi, i, e: (bi, i, 0))
    return pl.pallas_call(
        kern,
        grid=(b, n // tm, N_EXPERTS),
        in_specs=[tok(d), pl.BlockSpec((1, MOD_ROWS, d), lambda bi, i, e: (bi, 0, 0)), tok(d), tok(LANES),
                  pl.BlockSpec((1, 1, d, D_EXPERT), lambda bi, i, e: (layer, e, 0, 0)),
                  pl.BlockSpec((1, 1, d, D_EXPERT), lambda bi, i, e: (layer, e, 0, 0)),
                  pl.BlockSpec((1, 1, D_EXPERT, d), lambda bi, i, e: (layer, e, 0, 0))],
        out_specs=tok(d),
        out_shape=jax.ShapeDtypeStruct((b, n, d), F32),
        scratch_shapes=[pltpu.VMEM((tm, d), F32)],
        compiler_params=_cparams("parallel", "parallel", "arbitrary"),
        name="moe_experts",
    )(x1, mvec, h2, gates, w1, w3, w2)


def _final_kernel(x_ref, w_ref, o_ref):
    x = x_ref[0]
    ms = jnp.mean(x * x, axis=-1, keepdims=True)
    o_ref[0] = x * lax.rsqrt(ms + EPS) * w_ref[...]


def _final_norm(xc, w, n_ctx):
    b, n, d = xc.shape
    tm = 256
    skip = n_ctx // tm
    return pl.pallas_call(
        _final_kernel,
        grid=(b, (n - n_ctx) // tm),
        in_specs=[pl.BlockSpec((1, tm, d), lambda bi, i: (bi, i + skip, 0)),
                  pl.BlockSpec((1, d), lambda bi, i: (0, 0))],
        out_specs=pl.BlockSpec((1, tm, d), lambda bi, i: (bi, i, 0)),
        out_shape=jax.ShapeDtypeStruct((b, n - n_ctx, d), F32),
        compiler_params=_cparams("parallel", "parallel"),
        name="final_norm",
    )(xc, w.reshape(1, d))


def _rope_tables(n_lat, n_ctx):
    rows = n_lat // GRID_W
    row = jnp.repeat(jnp.arange(rows), GRID_W).astype(F32)
    col = jnp.tile(jnp.arange(GRID_W), rows).astype(F32)
    nq = HEAD_DIM // 4
    inv = jnp.power(ROPE_BASE, -jnp.arange(nq, dtype=F32) / nq)
    ang = jnp.concatenate([row[:, None] * inv, col[:, None] * inv], -1)
    cos = jnp.cos(ang)
    sin = jnp.sin(ang)
    cos_h = jnp.concatenate([cos, cos], -1)
    sin_h = jnp.concatenate([-sin, sin], -1)
    cos_full = jnp.concatenate([jnp.ones((n_ctx, HEAD_DIM), F32), cos_h], 0)
    sin_full = jnp.concatenate([jnp.zeros((n_ctx, HEAD_DIM), F32), sin_h], 0)
    return jnp.tile(cos_full, (1, N_HEADS)), jnp.tile(sin_full, (1, N_HEADS))


def _block_diag_heads(w):
    out = jnp.zeros((MIX_W, MIX_W), w.dtype)
    for h in range(N_HEADS):
        out = out.at[h * HEAD_DIM:(h + 1) * HEAD_DIM, h * HEAD_DIM:(h + 1) * HEAD_DIM].set(w[h])
    return out


def kernel(x, c, ctx, c_ctx, w_mod, b_mod, w_in, a_conv_w, a_conv_b, a_gate_w, a_gate_b, a_lambda, b_theta,
           c_conv_w, c_conv_b, c_gate_b, d_lb, w_branch, w_out, moe_w_group, moe_b_group, moe_w_router,
           moe_b_router, moe_w1, moe_w3, moe_w2, final_norm_w):
    bsz, n_lat, d = x.shape
    n_ctx = ctx.shape[1]
    depth = w_mod.shape[0]
    assert n_ctx % CHUNK == 0 and n_lat % CHUNK == 0 and n_ctx % 256 == 0 and n_lat % 256 == 0

    xc = jnp.concatenate([ctx, x], axis=1)
    cos, sin = _rope_tables(n_lat, n_ctx)

    cc = jnp.zeros((8, d), F32).at[:bsz].set(c).at[bsz].set(c_ctx)
    mod = _mod_vectors(cc, w_mod, b_mod)
    mx = mod[:, :bsz].reshape(depth, bsz, N_MOD, d)
    mc = jnp.broadcast_to(mod[:, bsz].reshape(depth, 1, N_MOD, d), (depth, bsz, N_MOD, d))
    pad = jnp.zeros((depth, bsz, 8 - N_MOD, d), F32)
    mvec = jnp.concatenate([mx, pad, mc, pad], axis=2)

    lbs = jnp.cumsum(jax.nn.softmax(d_lb.astype(F32), axis=0), axis=0)
    lbs = lbs - lbs[0]

    n_mix_cols = 15 * MIX_W
    gate0 = 10 * MIX_W
    n_gate = 4 * N_HEADS
    w1b = moe_w1.astype(BF16)
    w3b = moe_w3.astype(BF16)
    w2b = moe_w2.astype(BF16)

    for l in range(depth):
        wl = w_in[l]
        w_mix = jnp.concatenate([wl[:, :gate0], wl[:, gate0 + n_gate:n_mix_cols + n_gate]], axis=1).astype(BF16)
        w_gate = jnp.pad(wl[:, gate0:gate0 + n_gate], ((0, 0), (0, LANES - n_gate))).astype(BF16)
        w_merge = wl[:, n_mix_cols + n_gate:].astype(BF16)
        p, gates_c = _in_proj(xc, mvec[l], w_mix, w_gate, n_ctx)

        gw = jnp.concatenate([_block_diag_heads(a_gate_w[l, dd, j]) for dd in range(2) for j in range(2)],
                             axis=1).astype(BF16)
        gb = a_gate_b[l].reshape(1, 4 * MIX_W)
        ya = _mixer_a(p, a_conv_w[l], a_conv_b[l].reshape(1, MIX_W), gw, gb, a_lambda[l], n_ctx)

        log_gamma = jnp.repeat(jax.nn.log_sigmoid(b_theta[l].astype(F32)), HEAD_DIM, axis=1)
        yb = _mixer_b(p, cos, sin, log_gamma, n_ctx)

        gate_b = jnp.pad(c_gate_b[l].reshape(1, n_gate), ((0, 0), (0, LANES - n_gate)))
        yc = _mixer_c(p, gates_c, c_conv_w[l], c_conv_b[l].reshape(1, 2 * MIX_W), gate_b, n_ctx)

        yd = _mixer_d(p, lbs[l].reshape(1, MIX_W), n_ctx)

        w_route = jnp.pad(jnp.concatenate([moe_w_router[l], moe_w_group[l]], axis=1),
                          ((0, 0), (0, LANES - N_EXPERTS - N_GROUPS)))
        b_route = jnp.pad(jnp.concatenate([moe_b_router[l], moe_b_group[l]]),
                          (0, LANES - N_EXPERTS - N_GROUPS)).reshape(1, LANES)
        x1, h2, gates = _merge(xc, mvec[l], (ya, yb, yc, yd), w_merge, w_branch[l].astype(BF16),
                               w_out[l].astype(BF16), w_route, b_route, n_ctx)
        xc = _moe(x1, mvec[l], h2, gates, w1b, w3b, w2b, l, n_ctx)

    return _final_norm(xc, final_norm_w, n_ctx)
```

```python
import functools

import jax
import jax.numpy as jnp
import numpy as np
from jax import lax
from jax.experimental import pallas as pl
from jax.experimental.pallas import tpu as pltpu

F32 = jnp.float32
BF16 = jnp.bfloat16

EPS = 1e-6
N_HEADS = 4
HEAD_DIM = 64
MIX_W = N_HEADS * HEAD_DIM
N_BRANCH = 4
CONV_W = 4
LRU_C = 8.0
GRID_W = 64
ROPE_BASE = 10000.0
N_GROUPS = 4
EXP_PER_GROUP = 4
N_EXPERTS = N_GROUPS * EXP_PER_GROUP
D_EXPERT = 512
N_MOD = 6
M_INIT = -1e30

CHUNK = 128
SUB = 16
HALO = 16
LANES = 128
MOD_ROWS = 16
VMEM_LIMIT_BYTES = 56 * 1024 * 1024
NEG_INF = float("-inf")


def _cparams(*sem):
    return pltpu.CompilerParams(dimension_semantics=sem, vmem_limit_bytes=VMEM_LIMIT_BYTES)


def _token_tile(n, cap):
    best = 16
    for t in range(16, cap + 1, 16):
        if n % t == 0:
            best = t
    return best


def _modulate(x, shift, scale):
    ms = jnp.mean(x * x, axis=-1, keepdims=True)
    return x * lax.rsqrt(ms + EPS) * (1.0 + scale) + shift


def _pick(m, is_ctx, k):
    return jnp.where(is_ctx, m[8 + k:9 + k], m[k:k + 1])


def _sigmoid(x):
    return 1.0 / (1.0 + jnp.exp(-x))


def _silu(x):
    return x * _sigmoid(x)


def _log_sigmoid(x):
    return jnp.minimum(x, 0.0) - jnp.log(1.0 + jnp.exp(-jnp.abs(x)))


def _split3(x):
    hi = x.astype(BF16)
    r = x - hi.astype(F32)
    mid = r.astype(BF16)
    lo = (r - mid.astype(F32)).astype(BF16)
    return hi, mid, lo


def _sel_dot(sel, x):
    hi, mid, lo = _split3(x)
    d = functools.partial(jnp.dot, preferred_element_type=F32)
    return d(sel, hi) + d(sel, mid) + d(sel, lo)


def _dot_sel(x, sel):
    hi, mid, lo = _split3(x)
    d = functools.partial(jnp.dot, preferred_element_type=F32)
    return d(hi, sel) + d(mid, sel) + d(lo, sel)


def _dot_nt(a, b):
    return lax.dot_general(a, b, (((1,), (1,)), ((), ())), preferred_element_type=F32)


def _dot_tn(a, b):
    return lax.dot_general(a, b, (((0,), (0,)), ((), ())), preferred_element_type=F32)


def _head_of_lane(width):
    return lax.broadcasted_iota(jnp.int32, (1, width), 1) // HEAD_DIM


def _block_ones(width):
    r = lax.broadcasted_iota(jnp.int32, (width, width), 0) // HEAD_DIM
    c = lax.broadcasted_iota(jnp.int32, (width, width), 1) // HEAD_DIM
    return r == c


def _head_norm(o, center):
    ones = jnp.where(_block_ones(MIX_W), 1.0, 0.0).astype(BF16)
    inv = 1.0 / HEAD_DIM
    if center:
        o = o - _dot_sel(o, ones) * inv
    var = _dot_sel(o * o, ones) * inv
    return o * lax.rsqrt(var + EPS)


def _bwd_chunk(s, n_ctx_chunks, n_chunks):
    return jnp.where(s < n_ctx_chunks, n_ctx_chunks - 1 - s, n_chunks - 1 + n_ctx_chunks - s)


def _conv_chunk(ref, c, w, b, *, n, n_ctx, width0=0, width=None):
    L = CHUNK
    wl = L + 2 * HALO
    r0 = c * L
    start = pl.multiple_of(jnp.clip(r0 - HALO, 0, n - wl), HALO)
    off = r0 - start
    if width is None:
        win = ref[0, pl.ds(start, wl), :]
    else:
        win = ref[0, pl.ds(start, wl), width0:width0 + width]
    ri = lax.broadcasted_iota(jnp.int32, (L, wl), 0)
    mi = lax.broadcasted_iota(jnp.int32, (L, wl), 1)
    g = r0 + ri
    y = b
    seg_g = jnp.where(g < n_ctx, 0, 1)
    for k in range(CONV_W):
        src = g + (k - 2)
        hit = (mi == ri + off + (k - 2)) & (jnp.where(src < n_ctx, 0, 1) == seg_g) & (src >= 0) & (src < n)
        sel = jnp.where(hit, 1.0, 0.0).astype(BF16)
        y = y + w[k:k + 1] * jnp.dot(sel, win, preferred_element_type=F32)
    return y


def _mod_kernel(cc_ref, w_ref, b_ref, o_ref):
    s = _silu(cc_ref[...])
    o_ref[0] = jnp.dot(s, w_ref[0], precision=lax.Precision.HIGHEST, preferred_element_type=F32) + b_ref[0]


def _mod_vectors(cc, w_mod, b_mod):
    depth, d, dm = w_mod.shape
    tn = 1536
    return pl.pallas_call(
        _mod_kernel,
        grid=(depth, dm // tn),
        in_specs=[pl.BlockSpec((8, d), lambda l, j: (0, 0)),
                  pl.BlockSpec((1, d, tn), lambda l, j: (l, 0, j)),
                  pl.BlockSpec((1, 1, tn), lambda l, j: (l, 0, j))],
        out_specs=pl.BlockSpec((1, 8, tn), lambda l, j: (l, 0, j)),
        out_shape=jax.ShapeDtypeStruct((depth, 8, dm), F32),
        compiler_params=_cparams("parallel", "parallel"),
        name="mod_vectors",
    )(cc, w_mod, b_mod.reshape(depth, 1, dm))


def _in_proj_kernel(x_ref, m_ref, w_ref, wg_ref, p_ref, g_ref, *, tm, n_ctx, col_chunk):
    i = pl.program_id(1)
    row = i * tm + lax.broadcasted_iota(jnp.int32, (tm, 1), 0)
    is_ctx = row < n_ctx
    m = m_ref[0]
    h = _modulate(x_ref[0], _pick(m, is_ctx, 0), _pick(m, is_ctx, 1)).astype(BF16)
    for j in range(w_ref.shape[1] // col_chunk):
        sl = slice(j * col_chunk, (j + 1) * col_chunk)
        p_ref[0, :, sl] = jnp.dot(h, w_ref[:, sl], preferred_element_type=F32).astype(BF16)
    g_ref[0] = jnp.dot(h, wg_ref[...], preferred_element_type=F32)


def _in_proj(xc, mvec, w_mix, w_gate, n_ctx):
    b, n, d = xc.shape
    tm = _token_tile(n, 640)
    wc = w_mix.shape[1]
    kern = functools.partial(_in_proj_kernel, tm=tm, n_ctx=n_ctx, col_chunk=768)
    return pl.pallas_call(
        kern,
        grid=(b, n // tm),
        in_specs=[pl.BlockSpec((1, tm, d), lambda bi, i: (bi, i, 0)),
                  pl.BlockSpec((1, MOD_ROWS, d), lambda bi, i: (bi, 0, 0)),
                  pl.BlockSpec((d, wc), lambda bi, i: (0, 0)),
                  pl.BlockSpec((d, LANES), lambda bi, i: (0, 0))],
        out_specs=[pl.BlockSpec((1, tm, wc), lambda bi, i: (bi, i, 0)),
                   pl.BlockSpec((1, tm, LANES), lambda bi, i: (bi, i, 0))],
        out_shape=[jax.ShapeDtypeStruct((b, n, wc), BF16),
                   jax.ShapeDtypeStruct((b, n, LANES), F32)],
        compiler_params=_cparams("parallel", "parallel"),
        name="in_proj",
    )(xc, mvec, w_mix, w_gate)


def _lin_scan(a, x, reverse):
    L = a.shape[0]
    row = lax.broadcasted_iota(jnp.int32, (L, 1), 0)
    k = 1
    while k < L:
        if reverse:
            a_s = pltpu.roll(a, L - k, 0)
            x_s = pltpu.roll(x, L - k, 0)
            valid = row < L - k
        else:
            a_s = pltpu.roll(a, k, 0)
            x_s = pltpu.roll(x, k, 0)
            valid = row >= k
        x = jnp.where(valid, a * x_s + x, x)
        a = jnp.where(valid, a * a_s, a)
        k *= 2
    return a, x


def _mixer_a_kernel(ax_ref, ag_ref, cw_ref, cb_ref, gw_ref, gb_ref, lam_ref, out_ref, u_s, hf_s,
                    *, n, n_ctx):
    L = CHUNK
    n_chunks = n // L
    n_ctx_chunks = n_ctx // L
    cw = cw_ref[...]
    cb = cb_ref[...]
    log_lam = _log_sigmoid(lam_ref[...])

    def direction(d, c, carry):
        r0 = pl.multiple_of(c * L, L)
        u = u_s[pl.ds(r0, L), :]
        pre = jnp.dot(u.astype(BF16), gw_ref[:, d * 2 * MIX_W:(d + 1) * 2 * MIX_W],
                      preferred_element_type=F32) + gb_ref[:, d * 2 * MIX_W:(d + 1) * 2 * MIX_W]
        r = _sigmoid(pre[:, :MIX_W])
        gi = _sigmoid(pre[:, MIX_W:])
        log_a = LRU_C * r * log_lam[d:d + 1]
        a = jnp.exp(log_a)
        inp = jnp.sqrt(1.0 - jnp.exp(2.0 * log_a)) * (gi * u)
        a_cum, h = _lin_scan(a, inp, reverse=(d == 1))
        h = h + a_cum * carry
        new_carry = h[0:1] if d == 1 else h[L - 1:L]
        return r0, h, new_carry

    def conv_body(c, _):
        r0 = pl.multiple_of(c * L, L)
        u_s[pl.ds(r0, L), :] = _conv_chunk(ax_ref, c, cw, cb, n=n, n_ctx=n_ctx)
        return 0

    lax.fori_loop(0, n_chunks, conv_body, 0)

    def fwd_body(c, carry):
        r0, h, carry = direction(0, c, carry)
        hf_s[pl.ds(r0, L), :] = h
        return carry

    lax.fori_loop(0, n_chunks, fwd_body, jnp.zeros((1, MIX_W), F32))

    def bwd_body(s, carry):
        c = _bwd_chunk(s, n_ctx_chunks, n_chunks)
        r0, h, carry = direction(1, c, carry)
        gate = jax.nn.gelu(ag_ref[0, pl.ds(r0, L), :].astype(F32), approximate=True)
        out_ref[0, pl.ds(r0, L), :] = (gate * (hf_s[pl.ds(r0, L), :] + h)).astype(out_ref.dtype)
        return carry

    lax.fori_loop(0, n_chunks, bwd_body, jnp.zeros((1, MIX_W), F32))


def _seq_spec(n, col):
    return pl.BlockSpec((1, n, MIX_W), lambda b: (b, 0, col))


def _full_spec(shape):
    return pl.BlockSpec(shape, lambda b: (0,) * len(shape))


def _mixer_a(p, conv_w, conv_b, gate_w, gate_b, lam, n_ctx):
    b, n, _ = p.shape
    kern = functools.partial(_mixer_a_kernel, n=n, n_ctx=n_ctx)
    return pl.pallas_call(
        kern,
        grid=(b,),
        in_specs=[_seq_spec(n, 0), _seq_spec(n, 1),
                  _full_spec(conv_w.shape), _full_spec(conv_b.shape),
                  _full_spec(gate_w.shape), _full_spec(gate_b.shape), _full_spec(lam.shape)],
        out_specs=pl.BlockSpec((1, n, MIX_W), lambda bi: (bi, 0, 0)),
        out_shape=jax.ShapeDtypeStruct((b, n, MIX_W), BF16),
        scratch_shapes=[pltpu.VMEM((n, MIX_W), F32), pltpu.VMEM((n, MIX_W), F32)],
        compiler_params=_cparams("parallel"),
        name="mixer_rglru",
    )(p, p, conv_w, conv_b, gate_w, gate_b, lam)


def _rope(t, cos, sin):
    lane = lax.broadcasted_iota(jnp.int32, (1, MIX_W), 1) % HEAD_DIM
    half = HEAD_DIM // 2
    swapped = jnp.where(lane < half, pltpu.roll(t, MIX_W - half, 1), pltpu.roll(t, half, 1))
    return t * cos + swapped * sin


def _mixer_b_kernel(q_ref, k_ref, v_ref, g_ref, cos_ref, sin_ref, lg_ref, out_ref,
                    ob_s, dec_s, st_s, *, n, n_ctx):
    L = CHUNK
    n_chunks = n // L
    n_ctx_chunks = n_ctx // L
    scale = HEAD_DIM ** -0.5
    head = _head_of_lane(MIX_W)
    bd = _block_ones(MIX_W)
    lgf = lg_ref[0:1, :]
    lgb = lg_ref[1:2, :]
    pos = lax.broadcasted_iota(jnp.int32, (L, 1), 0).astype(F32)
    diff = (lax.broadcasted_iota(jnp.int32, (L, L), 0) - lax.broadcasted_iota(jnp.int32, (L, L), 1)).astype(F32)
    for h in range(N_HEADS):
        lf = lgf[:, h * HEAD_DIM:h * HEAD_DIM + 1]
        lb = lgb[:, h * HEAD_DIM:h * HEAD_DIM + 1]
        dec_s[h] = jnp.where(diff >= 0, jnp.exp(lf * jnp.maximum(diff, 0.0)), jnp.exp(lb * jnp.maximum(-diff, 0.0)))

    def load(c):
        r0 = pl.multiple_of(c * L, L)
        rows = pl.ds(r0, L)
        cos = cos_ref[rows, :]
        sin = sin_ref[rows, :]
        q = _rope(q_ref[0, rows, :].astype(F32), cos, sin)
        k = _rope(k_ref[0, rows, :].astype(F32), cos, sin) * scale
        v = v_ref[0, rows, :]
        return rows, q, k, v

    def state_step(d, q, k, v, lg, q_exp, k_exp):
        o_inter = jnp.dot((q * jnp.exp(lg * q_exp)).astype(BF16), st_s[d].astype(BF16), preferred_element_type=F32)
        kv = _dot_tn((k * jnp.exp(lg * k_exp)).astype(BF16), v)
        st_s[d] = jnp.exp(lg * float(L)) * st_s[d] + jnp.where(bd, kv, 0.0)
        return o_inter

    st_s[...] = jnp.zeros_like(st_s)

    def bwd_body(s, _):
        c = _bwd_chunk(s, n_ctx_chunks, n_chunks)
        rows, q, k, v = load(c)
        ob_s[rows, :] = state_step(1, q, k, v, lgb, float(L) - pos, pos)
        return 0

    lax.fori_loop(0, n_chunks, bwd_body, 0)

    def fwd_body(c, _):
        rows, q, k, v = load(c)
        o = ob_s[rows, :] + state_step(0, q, k, v, lgf, pos + 1.0, float(L) - 1.0 - pos)
        kb = k.astype(BF16)
        for h in range(N_HEADS):
            hm = head == h
            sc = _dot_nt(jnp.where(hm, q, 0.0).astype(BF16), kb) * dec_s[h]
            o = o + jnp.where(hm, jnp.dot(sc.astype(BF16), v, preferred_element_type=F32), 0.0)
        gate = _silu(g_ref[0, rows, :].astype(F32))
        out_ref[0, rows, :] = (_head_norm(o, True) * gate).astype(out_ref.dtype)
        return 0

    lax.fori_loop(0, n_chunks, fwd_body, 0)


def _mixer_b(p, cos, sin, log_gamma, n_ctx):
    b, n, _ = p.shape
    kern = functools.partial(_mixer_b_kernel, n=n, n_ctx=n_ctx)
    return pl.pallas_call(
        kern,
        grid=(b,),
        in_specs=[_seq_spec(n, 2), _seq_spec(n, 3), _seq_spec(n, 4), _seq_spec(n, 5),
                  _full_spec(cos.shape), _full_spec(sin.shape), _full_spec(log_gamma.shape)],
        out_specs=pl.BlockSpec((1, n, MIX_W), lambda bi: (bi, 0, 0)),
        out_shape=jax.ShapeDtypeStruct((b, n, MIX_W), BF16),
        scratch_shapes=[pltpu.VMEM((n, MIX_W), F32),
                        pltpu.VMEM((N_HEADS, CHUNK, CHUNK), F32),
                        pltpu.VMEM((2, MIX_W, MIX_W), F32)],
        compiler_params=_cparams("parallel"),
        name="mixer_retention",
    )(p, p, p, p, cos, sin, log_gamma)


def _mixer_c_kernel(q_ref, k_ref, v_ref, o_ref, g_ref, cw_ref, cb_ref, gb_ref, out_ref,
                    qs_s, ks_s, hf_s, c_s, n_s, m_s, *, n, n_ctx):
    L = CHUNK
    n_chunks = n // L
    n_ctx_chunks = n_ctx // L
    head = _head_of_lane(MIX_W)
    bd = _block_ones(MIX_W)
    cw = cw_ref[...]
    cb = cb_ref[...]
    ri = lax.broadcasted_iota(jnp.int32, (L, L), 0)
    ci = lax.broadcasted_iota(jnp.int32, (L, L), 1)
    gl = lax.broadcasted_iota(jnp.int32, (LANES, MIX_W), 0)
    hl = lax.broadcasted_iota(jnp.int32, (LANES, MIX_W), 1) // HEAD_DIM

    def expand(kind):
        return jnp.where(gl == kind * N_HEADS + hl, 1.0, 0.0).astype(BF16)

    def conv_body(c, _):
        rows = pl.ds(pl.multiple_of(c * L, L), L)
        qs_s[rows, :] = _silu(_conv_chunk(q_ref, c, cw[:, :MIX_W], cb[:, :MIX_W], n=n, n_ctx=n_ctx)).astype(BF16)
        ks_s[rows, :] = (_silu(_conv_chunk(k_ref, c, cw[:, MIX_W:], cb[:, MIX_W:], n=n, n_ctx=n_ctx))
                         * HEAD_DIM ** -0.5).astype(BF16)
        return 0

    lax.fori_loop(0, n_chunks, conv_body, 0)

    def chunk(d, c):
        rev = d == 1
        rows = pl.ds(pl.multiple_of(c * L, L), L)
        q = qs_s[rows, :]
        k = ks_s[rows, :]
        v = v_ref[0, rows, :]
        g = g_ref[0, rows, :] + gb_ref[...]
        log_i = _dot_sel(g, expand(2 * d))
        log_f = _dot_sel(_log_sigmoid(g), expand(2 * d + 1))
        causal = (ci >= ri) if rev else (ci <= ri)
        tri = jnp.where(causal, 1.0, 0.0).astype(BF16)
        cum = _sel_dot(tri, log_f)
        cum_end = cum[0:1] if rev else cum[L - 1:L]
        m_prev = m_s[d]
        c_prev = c_s[d]
        n_prev = n_s[d]
        row_src = (log_i - cum).T
        m_inter = cum + m_prev
        num_inter = jnp.dot(q, c_prev.astype(BF16), preferred_element_type=F32)
        qn = q.astype(F32) * n_prev
        kf = k
        hid = jnp.zeros((L, MIX_W), F32)
        for h in range(N_HEADS):
            hm = head == h
            lane0 = h * HEAD_DIM
            log_d = jnp.where(causal, cum[:, lane0:lane0 + 1] + row_src[lane0:lane0 + 1, :], NEG_INF)
            m_q = jnp.maximum(jnp.max(log_d, axis=1, keepdims=True), m_inter[:, lane0:lane0 + 1])
            sc = _dot_nt(jnp.where(hm, q, jnp.zeros_like(q)), kf) * jnp.exp(log_d - m_q)
            s_inter = jnp.exp(m_inter[:, lane0:lane0 + 1] - m_q)
            num = jnp.dot(sc.astype(BF16), v, preferred_element_type=F32) + s_inter * num_inter
            den = (jnp.sum(sc, axis=1, keepdims=True)
                   + s_inter * jnp.sum(jnp.where(hm, qn, 0.0), axis=1, keepdims=True))
            hid = hid + jnp.where(hm, num / jnp.maximum(jnp.abs(den), jnp.exp(-m_q)), 0.0)
        log_w = cum_end - cum + log_i
        m_loc = jnp.max(log_w, axis=0, keepdims=True)
        kw = k.astype(F32) * jnp.exp(log_w - m_loc)
        c_loc = jnp.where(bd, _dot_tn(kw.astype(BF16), v), 0.0)
        n_loc = jnp.sum(kw, axis=0, keepdims=True)
        m_new = jnp.maximum(cum_end + m_prev, m_loc)
        s_old = jnp.exp(cum_end + m_prev - m_new)
        s_loc = jnp.exp(m_loc - m_new)
        c_s[d] = s_old * c_prev + s_loc * c_loc
        n_s[d] = s_old * n_prev + s_loc * n_loc
        m_s[d] = m_new
        return rows, hid

    c_s[...] = jnp.zeros_like(c_s)
    n_s[...] = jnp.zeros_like(n_s)
    m_s[...] = jnp.full(m_s.shape, M_INIT, F32)

    def fwd_body(c, _):
        rows, hid = chunk(0, c)
        hf_s[rows, :] = hid
        return 0

    lax.fori_loop(0, n_chunks, fwd_body, 0)

    def bwd_body(s, _):
        rows, hid = chunk(1, _bwd_chunk(s, n_ctx_chunks, n_chunks))
        gate = _sigmoid(o_ref[0, rows, :].astype(F32))
        out_ref[0, rows, :] = (gate * _head_norm(hf_s[rows, :] + hid, True)).astype(out_ref.dtype)
        return 0

    lax.fori_loop(0, n_chunks, bwd_body, 0)


def _mixer_c(p, gates, conv_w, conv_b, gate_b, n_ctx):
    b, n, _ = p.shape
    kern = functools.partial(_mixer_c_kernel, n=n, n_ctx=n_ctx)
    return pl.pallas_call(
        kern,
        grid=(b,),
        in_specs=[_seq_spec(n, 6), _seq_spec(n, 7), _seq_spec(n, 8), _seq_spec(n, 9),
                  pl.BlockSpec((1, n, LANES), lambda bi: (bi, 0, 0)),
                  _full_spec(conv_w.shape), _full_spec(conv_b.shape), _full_spec(gate_b.shape)],
        out_specs=pl.BlockSpec((1, n, MIX_W), lambda bi: (bi, 0, 0)),
        out_shape=jax.ShapeDtypeStruct((b, n, MIX_W), BF16),
        scratch_shapes=[pltpu.VMEM((n, MIX_W), BF16), pltpu.VMEM((n, MIX_W), BF16),
                        pltpu.VMEM((n, MIX_W), F32),
                        pltpu.VMEM((2, MIX_W, MIX_W), F32),
                        pltpu.VMEM((2, 1, MIX_W), F32),
                        pltpu.VMEM((2, 1, MIX_W), F32)],
        compiler_params=_cparams("parallel"),
        name="mixer_mlstm",
    )(p, p, p, p, gates, conv_w, conv_b, gate_b)


def _hgrn2_tables():
    L, S = CHUNK, SUB
    r = np.arange(L)
    same = (r[:, None] // S) == (r[None, :] // S)
    tri = np.stack([same & (r[None, :] <= r[:, None]), same & (r[None, :] >= r[:, None])])
    pair = np.arange(L * S)
    tok_i = (pair // (S * S)) * S + (pair // S) % S
    pi, pj = (pair // S) % S, pair % S
    fold = np.stack([(r[:, None] == tok_i[None, :]) & (pj <= pi)[None, :],
                     (r[:, None] == tok_i[None, :]) & (pj >= pi)[None, :]])
    as_bf16 = lambda a: jnp.asarray(a.astype(np.float32), dtype=BF16)
    return as_bf16(tri), as_bf16(same), as_bf16(fold)


def _mixer_d_kernel(q_ref, ff_ref, fb_ref, v_ref, g_ref, lb_ref, tri_ref, same_ref, fold_ref, out_ref,
                    o_s, st_s, *, n, n_ctx):
    L = CHUNK
    S = SUB
    nb = L // S
    n_chunks = n // L
    n_ctx_chunks = n_ctx // L
    lb = lb_ref[...]
    bd = _block_ones(MIX_W)
    ones_bd = jnp.where(bd, 1.0, 0.0).astype(BF16)

    def chunk(d, c):
        rev = d == 1
        rows = pl.ds(pl.multiple_of(c * L, L), L)
        q = _silu(q_ref[0, rows, :].astype(F32))
        raw = (fb_ref if rev else ff_ref)[0, rows, :].astype(F32)
        v = v_ref[0, rows, :]
        vf = v.astype(F32)
        f = lb + (1.0 - lb) * _sigmoid(raw)
        k = 1.0 - f
        log_f = jnp.log(f)
        cum = _sel_dot(tri_ref[d], log_f)
        tot = _sel_dot(same_ref[...], log_f)
        qt = (q * jnp.exp(cum)).astype(BF16)
        kt = (k * jnp.exp(tot - cum)).astype(BF16)
        g = jnp.exp(tot)
        prods, v_js = [], []
        for a in range(nb):
            sl = slice(a * S, (a + 1) * S)
            cb, qb, kb = cum[sl], q[sl], k[sl]
            cum_i = jnp.concatenate([jnp.broadcast_to(cb[i:i + 1], (S, MIX_W)) for i in range(S)], axis=0)
            q_i = jnp.concatenate([jnp.broadcast_to(qb[i:i + 1], (S, MIX_W)) for i in range(S)], axis=0)
            cum_j = jnp.concatenate([cb] * S, axis=0)
            k_j = jnp.concatenate([kb] * S, axis=0)
            prods.append((q_i * k_j * jnp.exp(jnp.minimum(cum_i - cum_j, 0.0))).astype(BF16))
            v_js.append(jnp.concatenate([vf[sl]] * S, axis=0))
        scores = jnp.dot(jnp.concatenate(prods, axis=0), ones_bd, preferred_element_type=F32)
        weighted = (scores * jnp.concatenate(v_js, axis=0)).astype(BF16)
        o = jnp.dot(fold_ref[d], weighted, preferred_element_type=F32)
        st = st_s[d]
        inter = [None] * nb
        for a in (range(nb - 1, -1, -1) if rev else range(nb)):
            sl = slice(a * S, (a + 1) * S)
            inter[a] = _dot_nt(qt[sl], st.astype(BF16))
            st = st * g[a * S:a * S + 1] + jnp.where(bd, _dot_tn(v[sl], kt[sl]), 0.0)
        st_s[d] = st
        o_s[d, rows, :] = o + jnp.concatenate(inter, axis=0)

    st_s[...] = jnp.zeros_like(st_s)

    def walk_body(s, _):
        chunk(0, s)
        chunk(1, _bwd_chunk(s, n_ctx_chunks, n_chunks))
        return 0

    lax.fori_loop(0, n_chunks, walk_body, 0)

    def out_body(c, _):
        rows = pl.ds(pl.multiple_of(c * L, L), L)
        gate = _silu(g_ref[0, rows, :].astype(F32))
        out_ref[0, rows, :] = (_head_norm(o_s[0, rows, :] + o_s[1, rows, :], False) * gate).astype(out_ref.dtype)
        return 0

    lax.fori_loop(0, n_chunks, out_body, 0)


def _mixer_d(p, lb, n_ctx):
    b, n, _ = p.shape
    tri, same, fold = _hgrn2_tables()
    kern = functools.partial(_mixer_d_kernel, n=n, n_ctx=n_ctx)
    return pl.pallas_call(
        kern,
        grid=(b,),
        in_specs=[_seq_spec(n, 10), _seq_spec(n, 11), _seq_spec(n, 12), _seq_spec(n, 13), _seq_spec(n, 14),
                  _full_spec(lb.shape), _full_spec(tri.shape), _full_spec(same.shape), _full_spec(fold.shape)],
        out_specs=pl.BlockSpec((1, n, MIX_W), lambda bi: (bi, 0, 0)),
        out_shape=jax.ShapeDtypeStruct((b, n, MIX_W), BF16),
        scratch_shapes=[pltpu.VMEM((2, n, MIX_W), F32), pltpu.VMEM((2, MIX_W, MIX_W), F32)],
        compiler_params=_cparams("parallel"),
        name="mixer_hgrn2",
    )(p, p, p, p, p, lb, tri, same, fold)


def _merge_kernel(x_ref, m_ref, ya_ref, yb_ref, yc_ref, yd_ref, wm_ref, wb_ref, wo_ref, wr_ref, br_ref,
                  xo_ref, h2_ref, gates_ref, *, tm, n_ctx):
    i = pl.program_id(1)
    row = i * tm + lax.broadcasted_iota(jnp.int32, (tm, 1), 0)
    is_ctx = row < n_ctx
    m = m_ref[0]
    x = x_ref[0]
    d = x.shape[-1]
    h = _modulate(x, _pick(m, is_ctx, 0), _pick(m, is_ctx, 1)).astype(BF16)
    z = jnp.zeros((tm, d), F32)
    for nb, y_ref in enumerate((ya_ref, yb_ref, yc_ref, yd_ref)):
        gate = _sigmoid(jnp.dot(h, wm_ref[:, nb * d:(nb + 1) * d], preferred_element_type=F32))
        z = z + gate * jnp.dot(y_ref[0], wb_ref[nb], preferred_element_type=F32)
    mix = jnp.dot(z.astype(BF16), wo_ref[...], preferred_element_type=F32)
    x1 = x + _pick(m, is_ctx, 2) * mix
    xo_ref[0] = x1
    h2 = _modulate(x1, _pick(m, is_ctx, 3), _pick(m, is_ctx, 4))
    h2_ref[0] = h2.astype(BF16)

    logit = jnp.dot(h2, wr_ref[...], precision=lax.Precision.HIGHEST, preferred_element_type=F32) + br_ref[...]
    lane = lax.broadcasted_iota(jnp.int32, (tm, LANES), 1)
    big = jnp.int32(LANES)
    is_group = (lane >= N_EXPERTS) & (lane < N_EXPERTS + N_GROUPS)
    gl = jnp.where(is_group, logit, NEG_INF)
    g_max = jnp.max(gl, axis=1, keepdims=True)
    g_idx = jnp.min(jnp.where(gl == g_max, lane, big), axis=1, keepdims=True) - N_EXPERTS
    g_prob = 1.0 / jnp.sum(jnp.where(is_group, jnp.exp(logit - g_max), 0.0), axis=1, keepdims=True)
    in_group = (lane < N_EXPERTS) & (lane // EXP_PER_GROUP == g_idx)
    el = jnp.where(in_group, logit, NEG_INF)
    v1 = jnp.max(el, axis=1, keepdims=True)
    i1 = jnp.min(jnp.where(el == v1, lane, big), axis=1, keepdims=True)
    el2 = jnp.where(lane == i1, NEG_INF, el)
    v2 = jnp.max(el2, axis=1, keepdims=True)
    i2 = jnp.min(jnp.where(el2 == v2, lane, big), axis=1, keepdims=True)
    e2 = jnp.exp(v2 - v1)
    w1 = g_prob / (1.0 + e2)
    w2 = g_prob * e2 / (1.0 + e2)
    gates_ref[0] = jnp.where(lane == i1, w1, 0.0) + jnp.where(lane == i2, w2, 0.0)


def _merge(xc, mvec, ys, w_merge, w_branch, w_out, w_route, b_route, n_ctx):
    b, n, d = xc.shape
    tm = _token_tile(n, 640)
    kern = functools.partial(_merge_kernel, tm=tm, n_ctx=n_ctx)
    tok = lambda w: pl.BlockSpec((1, tm, w), lambda bi, i: (bi, i, 0))
    const = lambda shape: pl.BlockSpec(shape, lambda bi, i: (0,) * len(shape))
    return pl.pallas_call(
        kern,
        grid=(b, n // tm),
        in_specs=[tok(d), pl.BlockSpec((1, MOD_ROWS, d), lambda bi, i: (bi, 0, 0)),
                  tok(MIX_W), tok(MIX_W), tok(MIX_W), tok(MIX_W),
                  const(w_merge.shape), const(w_branch.shape), const(w_out.shape),
                  const(w_route.shape), const(b_route.shape)],
        out_specs=[tok(d), tok(d), tok(LANES)],
        out_shape=[jax.ShapeDtypeStruct((b, n, d), F32),
                   jax.ShapeDtypeStruct((b, n, d), BF16),
                   jax.ShapeDtypeStruct((b, n, LANES), F32)],
        compiler_params=_cparams("parallel", "parallel"),
        name="merge_route",
    )(xc, mvec, *ys, w_merge, w_branch, w_out, w_route, b_route)


def _moe_kernel(x_ref, m_ref, h_ref, gates_ref, w1_ref, w3_ref, w2_ref, o_ref, acc_s, *, tm, n_ctx):
    i = pl.program_id(1)
    e = pl.program_id(2)

    @pl.when(e == 0)
    def _():
        acc_s[...] = jnp.zeros_like(acc_s)

    h = h_ref[0]
    a = jnp.dot(h, w1_ref[0, 0], preferred_element_type=F32)
    g = jnp.dot(h, w3_ref[0, 0], preferred_element_type=F32)
    y = jnp.dot((_silu(a) * g).astype(BF16), w2_ref[0, 0], preferred_element_type=F32)
    lane = lax.broadcasted_iota(jnp.int32, (tm, LANES), 1)
    gate = jnp.sum(jnp.where(lane == e, gates_ref[0], 0.0), axis=1, keepdims=True)
    acc_s[...] += gate * y

    @pl.when(e == N_EXPERTS - 1)
    def _():
        row = i * tm + lax.broadcasted_iota(jnp.int32, (tm, 1), 0)
        o_ref[0] = x_ref[0] + _pick(m_ref[0], row < n_ctx, 5) * acc_s[...]


def _moe(x1, mvec, h2, gates, w1, w3, w2, layer, n_ctx):
    b, n, d = x1.shape
    tm = _token_tile(n, 1100)
    kern = functools.partial(_moe_kernel, tm=tm, n_ctx=n_ctx)
    tok = lambda w: pl.BlockSpec((1, tm, w), lambda bi, i, e: (bi, i, 0))
    return pl.pallas_call(
        kern,
        grid=(b, n // tm, N_EXPERTS),
        in_specs=[tok(d), pl.BlockSpec((1, MOD_ROWS, d), lambda bi, i, e: (bi, 0, 0)), tok(d), tok(LANES),
                  pl.BlockSpec((1, 1, d, D_EXPERT), lambda bi, i, e: (layer, e, 0, 0)),
                  pl.BlockSpec((1, 1, d, D_EXPERT), lambda bi, i, e: (layer, e, 0, 0)),
                  pl.BlockSpec((1, 1, D_EXPERT, d), lambda bi, i, e: (layer, e, 0, 0))],
        out_specs=tok(d),
        out_shape=jax.ShapeDtypeStruct((b, n, d), F32),
        scratch_shapes=[pltpu.VMEM((tm, d), F32)],
        compiler_params=_cparams("parallel", "parallel", "arbitrary"),
        name="moe_experts",
    )(x1, mvec, h2, gates, w1, w3, w2)


def _final_kernel(x_ref, w_ref, o_ref):
    x = x_ref[0]
    ms = jnp.mean(x * x, axis=-1, keepdims=True)
    o_ref[0] = x * lax.rsqrt(ms + EPS) * w_ref[...]


def _final_norm(xc, w, n_ctx):
    b, n, d = xc.shape
    tm = 256
    skip = n_ctx // tm
    return pl.pallas_call(
        _final_kernel,
        grid=(b, (n - n_ctx) // tm),
        in_specs=[pl.BlockSpec((1, tm, d), lambda bi, i: (bi, i + skip, 0)),
                  pl.BlockSpec((1, d), lambda bi, i: (0, 0))],
        out_specs=pl.BlockSpec((1, tm, d), lambda bi, i: (bi, i, 0)),
        out_shape=jax.ShapeDtypeStruct((b, n - n_ctx, d), F32),
        compiler_params=_cparams("parallel", "parallel"),
        name="final_norm",
    )(xc, w.reshape(1, d))


def _rope_tables(n_lat, n_ctx):
    rows = n_lat // GRID_W
    row = jnp.repeat(jnp.arange(rows), GRID_W).astype(F32)
    col = jnp.tile(jnp.arange(GRID_W), rows).astype(F32)
    nq = HEAD_DIM // 4
    inv = jnp.power(ROPE_BASE, -jnp.arange(nq, dtype=F32) / nq)
    ang = jnp.concatenate([row[:, None] * inv, col[:, None] * inv], -1)
    cos = jnp.cos(ang)
    sin = jnp.sin(ang)
    cos_h = jnp.concatenate([cos, cos], -1)
    sin_h = jnp.concatenate([-sin, sin], -1)
    cos_full = jnp.concatenate([jnp.ones((n_ctx, HEAD_DIM), F32), cos_h], 0)
    sin_full = jnp.concatenate([jnp.zeros((n_ctx, HEAD_DIM), F32), sin_h], 0)
    return jnp.tile(cos_full, (1, N_HEADS)), jnp.tile(sin_full, (1, N_HEADS))


def _block_diag_heads(w):
    out = jnp.zeros((MIX_W, MIX_W), w.dtype)
    for h in range(N_HEADS):
        out = out.at[h * HEAD_DIM:(h + 1) * HEAD_DIM, h * HEAD_DIM:(h + 1) * HEAD_DIM].set(w[h])
    return out


def kernel(x, c, ctx, c_ctx, w_mod, b_mod, w_in, a_conv_w, a_conv_b, a_gate_w, a_gate_b, a_lambda, b_theta,
           c_conv_w, c_conv_b, c_gate_b, d_lb, w_branch, w_out, moe_w_group, moe_b_group, moe_w_router,
           moe_b_router, moe_w1, moe_w3, moe_w2, final_norm_w):
    bsz, n_lat, d = x.shape
    n_ctx = ctx.shape[1]
    depth = w_mod.shape[0]
    assert n_ctx % CHUNK == 0 and n_lat % CHUNK == 0 and n_ctx % 256 == 0 and n_lat % 256 == 0

    xc = jnp.concatenate([ctx, x], axis=1)
    cos, sin = _rope_tables(n_lat, n_ctx)

    cc = jnp.zeros((8, d), F32).at[:bsz].set(c).at[bsz].set(c_ctx)
    mod = _mod_vectors(cc, w_mod, b_mod)
    mx = mod[:, :bsz].reshape(depth, bsz, N_MOD, d)
    mc = jnp.broadcast_to(mod[:, bsz].reshape(depth, 1, N_MOD, d), (depth, bsz, N_MOD, d))
    pad = jnp.zeros((depth, bsz, 8 - N_MOD, d), F32)
    mvec = jnp.concatenate([mx, pad, mc, pad], axis=2)

    lbs = jnp.cumsum(jax.nn.softmax(d_lb.astype(F32), axis=0), axis=0)
    lbs = lbs - lbs[0]

    n_mix_cols = 15 * MIX_W
    gate0 = 10 * MIX_W
    n_gate = 4 * N_HEADS
    w1b = moe_w1.astype(BF16)
    w3b = moe_w3.astype(BF16)
    w2b = moe_w2.astype(BF16)

    for l in range(depth):
        wl = w_in[l]
        w_mix = jnp.concatenate([wl[:, :gate0], wl[:, gate0 + n_gate:n_mix_cols + n_gate]], axis=1).astype(BF16)
        w_gate = jnp.pad(wl[:, gate0:gate0 + n_gate], ((0, 0), (0, LANES - n_gate))).astype(BF16)
        w_merge = wl[:, n_mix_cols + n_gate:].astype(BF16)
        p, gates_c = _in_proj(xc, mvec[l], w_mix, w_gate, n_ctx)

        gw = jnp.concatenate([_block_diag_heads(a_gate_w[l, dd, j]) for dd in range(2) for j in range(2)],
                             axis=1).astype(BF16)
        gb = a_gate_b[l].reshape(1, 4 * MIX_W)
        ya = _mixer_a(p, a_conv_w[l], a_conv_b[l].reshape(1, MIX_W), gw, gb, a_lambda[l], n_ctx)

        log_gamma = jnp.repeat(jax.nn.log_sigmoid(b_theta[l].astype(F32)), HEAD_DIM, axis=1)
        yb = _mixer_b(p, cos, sin, log_gamma, n_ctx)

        gate_b = jnp.pad(c_gate_b[l].reshape(1, n_gate), ((0, 0), (0, LANES - n_gate)))
        yc = _mixer_c(p, gates_c, c_conv_w[l], c_conv_b[l].reshape(1, 2 * MIX_W), gate_b, n_ctx)

        yd = _mixer_d(p, lbs[l].reshape(1, MIX_W), n_ctx)

        w_route = jnp.pad(jnp.concatenate([moe_w_router[l], moe_w_group[l]], axis=1),
                          ((0, 0), (0, LANES - N_EXPERTS - N_GROUPS)))
        b_route = jnp.pad(jnp.concatenate([moe_b_router[l], moe_b_group[l]]),
                          (0, LANES - N_EXPERTS - N_GROUPS)).reshape(1, LANES)
        x1, h2, gates = _merge(xc, mvec[l], (ya, yb, yc, yd), w_merge, w_branch[l].astype(BF16),
                               w_out[l].astype(BF16), w_route, b_route, n_ctx)
        xc = _moe(x1, mvec[l], h2, gates, w1b, w3b, w2b, l, n_ctx)

    return _final_norm(xc, final_norm_w, n_ctx)
```

```python
import functools

import jax
import jax.numpy as jnp
import numpy as np
from jax import lax
from jax.experimental import pallas as pl
from jax.experimental.pallas import tpu as pltpu

F32 = jnp.float32
BF16 = jnp.bfloat16

EPS = 1e-6
N_HEADS = 4
HEAD_DIM = 64
MIX_W = N_HEADS * HEAD_DIM
N_BRANCH = 4
CONV_W = 4
LRU_C = 8.0
GRID_W = 64
ROPE_BASE = 10000.0
N_GROUPS = 4
EXP_PER_GROUP = 4
N_EXPERTS = N_GROUPS * EXP_PER_GROUP
D_EXPERT = 512
N_MOD = 6
M_INIT = -1e30

CHUNK = 128
SUB = 16
HALO = 16
LANES = 128
MOD_ROWS = 16
VMEM_LIMIT_BYTES = 56 * 1024 * 1024
NEG_INF = float("-inf")
LOG2_E = 1.4426950408889634


def _cparams(*sem):
    return pltpu.CompilerParams(dimension_semantics=sem, vmem_limit_bytes=VMEM_LIMIT_BYTES)


def _token_tile(n, cap):
    best = 16
    for t in range(16, cap + 1, 16):
        if n % t == 0:
            best = t
    return best


def _modulate(x, shift, scale):
    ms = jnp.mean(x * x, axis=-1, keepdims=True)
    return x * lax.rsqrt(ms + EPS) * (1.0 + scale) + shift


def _pick(m, is_ctx, k):
    return jnp.where(is_ctx, m[8 + k:9 + k], m[k:k + 1])


def _sigmoid(x):
    return 0.5 * jnp.tanh(0.5 * x) + 0.5


def _silu(x):
    return x * _sigmoid(x)


def _log_sigmoid(x):
    return jnp.minimum(x, 0.0) - jnp.log(1.0 + jnp.exp(-jnp.abs(x)))


def _split3(x):
    hi = x.astype(BF16)
    r = x - hi.astype(F32)
    mid = r.astype(BF16)
    lo = (r - mid.astype(F32)).astype(BF16)
    return hi, mid, lo


def _sel_dot(sel, x):
    hi, mid, lo = _split3(x)
    d = functools.partial(jnp.dot, preferred_element_type=F32)
    return d(sel, hi) + d(sel, mid) + d(sel, lo)


def _dot_sel(x, sel):
    hi, mid, lo = _split3(x)
    d = functools.partial(jnp.dot, preferred_element_type=F32)
    return d(hi, sel) + d(mid, sel) + d(lo, sel)


def _dot_nt(a, b):
    return lax.dot_general(a, b, (((1,), (1,)), ((), ())), preferred_element_type=F32)


def _dot_tn(a, b):
    return lax.dot_general(a, b, (((0,), (0,)), ((), ())), preferred_element_type=F32)


def _head_of_lane(width):
    return lax.broadcasted_iota(jnp.int32, (1, width), 1) // HEAD_DIM


def _block_ones(width):
    r = lax.broadcasted_iota(jnp.int32, (width, width), 0) // HEAD_DIM
    c = lax.broadcasted_iota(jnp.int32, (width, width), 1) // HEAD_DIM
    return r == c


def _head_norm(o, center):
    ones = jnp.where(_block_ones(MIX_W), 1.0, 0.0).astype(BF16)
    inv = 1.0 / HEAD_DIM
    if center:
        o = o - _dot_sel(o, ones) * inv
    var = _dot_sel(o * o, ones) * inv
    return o * lax.rsqrt(var + EPS)


def _bwd_chunk(s, n_ctx_chunks, n_chunks):
    return jnp.where(s < n_ctx_chunks, n_ctx_chunks - 1 - s, n_chunks - 1 + n_ctx_chunks - s)


def _conv_chunk(ref, c, w, b, *, n, n_ctx, width0=0, width=None):
    L = CHUNK
    wl = L + 2 * HALO
    r0 = c * L
    start = pl.multiple_of(jnp.clip(r0 - HALO, 0, n - wl), HALO)
    off = r0 - start
    if width is None:
        win = ref[0, pl.ds(start, wl), :]
    else:
        win = ref[0, pl.ds(start, wl), width0:width0 + width]
    ri = lax.broadcasted_iota(jnp.int32, (L, wl), 0)
    mi = lax.broadcasted_iota(jnp.int32, (L, wl), 1)
    g = r0 + ri
    y = b
    seg_g = jnp.where(g < n_ctx, 0, 1)
    for k in range(CONV_W):
        src = g + (k - 2)
        hit = (mi == ri + off + (k - 2)) & (jnp.where(src < n_ctx, 0, 1) == seg_g) & (src >= 0) & (src < n)
        sel = jnp.where(hit, 1.0, 0.0).astype(BF16)
        y = y + w[k:k + 1] * jnp.dot(sel, win, preferred_element_type=F32)
    return y


def _mod_kernel(cc_ref, w_ref, b_ref, o_ref):
    s = _silu(cc_ref[...])
    o_ref[0] = jnp.dot(s, w_ref[0], precision=lax.Precision.HIGHEST, preferred_element_type=F32) + b_ref[0]


def _mod_vectors(cc, w_mod, b_mod):
    depth, d, dm = w_mod.shape
    tn = 1536
    return pl.pallas_call(
        _mod_kernel,
        grid=(depth, dm // tn),
        in_specs=[pl.BlockSpec((8, d), lambda l, j: (0, 0)),
                  pl.BlockSpec((1, d, tn), lambda l, j: (l, 0, j)),
                  pl.BlockSpec((1, 1, tn), lambda l, j: (l, 0, j))],
        out_specs=pl.BlockSpec((1, 8, tn), lambda l, j: (l, 0, j)),
        out_shape=jax.ShapeDtypeStruct((depth, 8, dm), F32),
        compiler_params=_cparams("parallel", "parallel"),
        name="mod_vectors",
    )(cc, w_mod, b_mod.reshape(depth, 1, dm))


def _rope(t, cos, sin):
    lane = lax.broadcasted_iota(jnp.int32, (1, MIX_W), 1) % HEAD_DIM
    half = HEAD_DIM // 2
    swapped = jnp.where(lane < half, pltpu.roll(t, MIX_W - half, 1), pltpu.roll(t, half, 1))
    return t * cos + swapped * sin


RET_Q_COL = 2 * MIX_W
RET_K_COL = 3 * MIX_W


def _in_proj_kernel(x_ref, m_ref, w_ref, wg_ref, cos_ref, sin_ref, p_ref, g_ref, *, tm, n_ctx, col_chunk):
    i = pl.program_id(1)
    row = i * tm + lax.broadcasted_iota(jnp.int32, (tm, 1), 0)
    is_ctx = row < n_ctx
    m = m_ref[0]
    h = _modulate(x_ref[0], _pick(m, is_ctx, 0), _pick(m, is_ctx, 1)).astype(BF16)
    for j in range(w_ref.shape[1] // col_chunk):
        sl = slice(j * col_chunk, (j + 1) * col_chunk)
        t = jnp.dot(h, w_ref[:, sl], preferred_element_type=F32)
        if j * col_chunk == RET_Q_COL:
            t = _rope(t, cos_ref[...], sin_ref[...])
        elif j * col_chunk == RET_K_COL:
            t = _rope(t, cos_ref[...], sin_ref[...]) * HEAD_DIM ** -0.5
        p_ref[0, :, sl] = t.astype(BF16)
    g_ref[0] = jnp.dot(h, wg_ref[...], preferred_element_type=F32)


def _in_proj(xc, mvec, w_mix, w_gate, cos, sin, n_ctx):
    b, n, d = xc.shape
    tm = _token_tile(n, 640)
    wc = w_mix.shape[1]
    kern = functools.partial(_in_proj_kernel, tm=tm, n_ctx=n_ctx, col_chunk=MIX_W)
    return pl.pallas_call(
        kern,
        grid=(b, n // tm),
        in_specs=[pl.BlockSpec((1, tm, d), lambda bi, i: (bi, i, 0)),
                  pl.BlockSpec((1, MOD_ROWS, d), lambda bi, i: (bi, 0, 0)),
                  pl.BlockSpec((d, wc), lambda bi, i: (0, 0)),
                  pl.BlockSpec((d, LANES), lambda bi, i: (0, 0)),
                  pl.BlockSpec((tm, MIX_W), lambda bi, i: (i, 0)),
                  pl.BlockSpec((tm, MIX_W), lambda bi, i: (i, 0))],
        out_specs=[pl.BlockSpec((1, tm, wc), lambda bi, i: (bi, i, 0)),
                   pl.BlockSpec((1, tm, LANES), lambda bi, i: (bi, i, 0))],
        out_shape=[jax.ShapeDtypeStruct((b, n, wc), BF16),
                   jax.ShapeDtypeStruct((b, n, LANES), F32)],
        compiler_params=_cparams("parallel", "parallel"),
        name="in_proj",
    )(xc, mvec, w_mix, w_gate, cos, sin)


def _lin_scan(a, x, reverse):
    L = a.shape[0]
    row = lax.broadcasted_iota(jnp.int32, (L, 1), 0)
    k = 1
    while k < L:
        if reverse:
            a_s = pltpu.roll(a, L - k, 0)
            x_s = pltpu.roll(x, L - k, 0)
            valid = row < L - k
        else:
            a_s = pltpu.roll(a, k, 0)
            x_s = pltpu.roll(x, k, 0)
            valid = row >= k
        x = jnp.where(valid, a * x_s + x, x)
        a = jnp.where(valid, a * a_s, a)
        k *= 2
    return a, x


def _mixer_a_kernel(ax_ref, ag_ref, cw_ref, cb_ref, gw_ref, gb_ref, lam_ref, out_ref, u_s, hf_s,
                    *, n, n_ctx):
    L = CHUNK
    n_chunks = n // L
    n_ctx_chunks = n_ctx // L
    cw = cw_ref[...]
    cb = cb_ref[...]
    log_lam = _log_sigmoid(lam_ref[...])

    def direction(d, c, carry):
        r0 = pl.multiple_of(c * L, L)
        u = u_s[pl.ds(r0, L), :]
        pre = jnp.dot(u.astype(BF16), gw_ref[:, d * 2 * MIX_W:(d + 1) * 2 * MIX_W],
                      preferred_element_type=F32) + gb_ref[:, d * 2 * MIX_W:(d + 1) * 2 * MIX_W]
        r = _sigmoid(pre[:, :MIX_W])
        gi = _sigmoid(pre[:, MIX_W:])
        log_a = LRU_C * r * log_lam[d:d + 1]
        a = jnp.exp(log_a)
        inp = jnp.sqrt(1.0 - jnp.exp(2.0 * log_a)) * (gi * u)
        a_cum, h = _lin_scan(a, inp, reverse=(d == 1))
        h = h + a_cum * carry
        new_carry = h[0:1] if d == 1 else h[L - 1:L]
        return r0, h, new_carry

    def conv_body(c, _):
        r0 = pl.multiple_of(c * L, L)
        u_s[pl.ds(r0, L), :] = _conv_chunk(ax_ref, c, cw, cb, n=n, n_ctx=n_ctx)
        return 0

    lax.fori_loop(0, n_chunks, conv_body, 0)

    def walk_body(s, carries):
        r0, h, carry_f = direction(0, s, carries[0])
        hf_s[0, pl.ds(r0, L), :] = h
        r0, h, carry_b = direction(1, _bwd_chunk(s, n_ctx_chunks, n_chunks), carries[1])
        hf_s[1, pl.ds(r0, L), :] = h
        return carry_f, carry_b

    zero = jnp.zeros((1, MIX_W), F32)
    lax.fori_loop(0, n_chunks, walk_body, (zero, zero))

    def out_body(c, _):
        rows = pl.ds(pl.multiple_of(c * L, L), L)
        gate = jax.nn.gelu(ag_ref[0, rows, :].astype(F32), approximate=True)
        out_ref[0, rows, :] = (gate * (hf_s[0, rows, :] + hf_s[1, rows, :])).astype(out_ref.dtype)
        return 0

    lax.fori_loop(0, n_chunks, out_body, 0)


def _seq_spec(n, col):
    return pl.BlockSpec((1, n, MIX_W), lambda b: (b, 0, col))


def _full_spec(shape):
    return pl.BlockSpec(shape, lambda b: (0,) * len(shape))


def _mixer_a(p, conv_w, conv_b, gate_w, gate_b, lam, n_ctx):
    b, n, _ = p.shape
    kern = functools.partial(_mixer_a_kernel, n=n, n_ctx=n_ctx)
    return pl.pallas_call(
        kern,
        grid=(b,),
        in_specs=[_seq_spec(n, 0), _seq_spec(n, 1),
                  _full_spec(conv_w.shape), _full_spec(conv_b.shape),
                  _full_spec(gate_w.shape), _full_spec(gate_b.shape), _full_spec(lam.shape)],
        out_specs=pl.BlockSpec((1, n, MIX_W), lambda bi: (bi, 0, 0)),
        out_shape=jax.ShapeDtypeStruct((b, n, MIX_W), BF16),
        scratch_shapes=[pltpu.VMEM((n, MIX_W), F32), pltpu.VMEM((2, n, MIX_W), F32)],
        compiler_params=_cparams("parallel"),
        name="mixer_rglru",
    )(p, p, conv_w, conv_b, gate_w, gate_b, lam)


def _mixer_b_kernel(q_ref, k_ref, v_ref, g_ref, lg_ref, out_ref, oi_s, dec_s, st_s, *, n, n_ctx):
    L = CHUNK
    n_chunks = n // L
    n_ctx_chunks = n_ctx // L
    head = _head_of_lane(MIX_W)
    bd = _block_ones(MIX_W)
    lgf = lg_ref[0:1, :]
    lgb = lg_ref[1:2, :]
    pos = lax.broadcasted_iota(jnp.int32, (L, 1), 0).astype(F32)
    diff = (lax.broadcasted_iota(jnp.int32, (L, L), 0) - lax.broadcasted_iota(jnp.int32, (L, L), 1)).astype(F32)
    for h in range(N_HEADS):
        lf = lgf[:, h * HEAD_DIM:h * HEAD_DIM + 1]
        lb = lgb[:, h * HEAD_DIM:h * HEAD_DIM + 1]
        dec_s[h] = jnp.where(diff >= 0, jnp.exp(lf * jnp.maximum(diff, 0.0)), jnp.exp(lb * jnp.maximum(-diff, 0.0)))

    qw = (jnp.exp(lgf * (pos + 1.0)), jnp.exp(lgb * (float(L) - pos)))
    kw = (jnp.exp(lgf * (float(L) - 1.0 - pos)), jnp.exp(lgb * pos))
    g_chunk = (jnp.exp(lgf * float(L)), jnp.exp(lgb * float(L)))

    def state_step(d, c):
        rows = pl.ds(pl.multiple_of(c * L, L), L)
        q = q_ref[0, rows, :].astype(F32)
        k = k_ref[0, rows, :].astype(F32)
        st = st_s[d]
        oi_s[d, rows, :] = jnp.dot((q * qw[d]).astype(BF16), st.astype(BF16), preferred_element_type=F32)
        kv = _dot_tn((k * kw[d]).astype(BF16), v_ref[0, rows, :])
        st_s[d] = g_chunk[d] * st + jnp.where(bd, kv, 0.0)

    st_s[...] = jnp.zeros_like(st_s)

    def walk_body(s, _):
        state_step(0, s)
        state_step(1, _bwd_chunk(s, n_ctx_chunks, n_chunks))
        return 0

    lax.fori_loop(0, n_chunks, walk_body, 0)

    def out_body(c, _):
        rows = pl.ds(pl.multiple_of(c * L, L), L)
        q = q_ref[0, rows, :]
        k = k_ref[0, rows, :]
        v = v_ref[0, rows, :]
        o = oi_s[0, rows, :] + oi_s[1, rows, :]
        for h in range(N_HEADS):
            hm = head == h
            sc = _dot_nt(jnp.where(hm, q, jnp.zeros_like(q)), k) * dec_s[h]
            o = o + jnp.where(hm, jnp.dot(sc.astype(BF16), v, preferred_element_type=F32), 0.0)
        gate = _silu(g_ref[0, rows, :].astype(F32))
        out_ref[0, rows, :] = (_head_norm(o, True) * gate).astype(out_ref.dtype)
        return 0

    lax.fori_loop(0, n_chunks, out_body, 0, unroll=2)


def _mixer_b(p, log_gamma, n_ctx):
    b, n, _ = p.shape
    kern = functools.partial(_mixer_b_kernel, n=n, n_ctx=n_ctx)
    return pl.pallas_call(
        kern,
        grid=(b,),
        in_specs=[_seq_spec(n, 2), _seq_spec(n, 3), _seq_spec(n, 4), _seq_spec(n, 5),
                  _full_spec(log_gamma.shape)],
        out_specs=pl.BlockSpec((1, n, MIX_W), lambda bi: (bi, 0, 0)),
        out_shape=jax.ShapeDtypeStruct((b, n, MIX_W), BF16),
        scratch_shapes=[pltpu.VMEM((2, n, MIX_W), F32),
                        pltpu.VMEM((N_HEADS, CHUNK, CHUNK), F32),
                        pltpu.VMEM((2, MIX_W, MIX_W), F32)],
        compiler_params=_cparams("parallel"),
        name="mixer_retention",
    )(p, p, p, p, log_gamma)


def _mixer_c_kernel(q_ref, k_ref, v_ref, o_ref, g_ref, cw_ref, cb_ref, gb_ref, out_ref,
                    qs_s, ks_s, hf_s, c_s, n_s, m_s, *, n, n_ctx):
    L = CHUNK
    n_chunks = n // L
    n_ctx_chunks = n_ctx // L
    head = _head_of_lane(MIX_W)
    bd = _block_ones(MIX_W)
    cw = cw_ref[...]
    cb = cb_ref[...]
    ri = lax.broadcasted_iota(jnp.int32, (L, L), 0)
    ci = lax.broadcasted_iota(jnp.int32, (L, L), 1)
    gl = lax.broadcasted_iota(jnp.int32, (LANES, MIX_W), 0)
    hl = lax.broadcasted_iota(jnp.int32, (LANES, MIX_W), 1) // HEAD_DIM

    def expand(kind):
        return jnp.where(gl == kind * N_HEADS + hl, 1.0, 0.0).astype(BF16)

    def conv_body(c, _):
        rows = pl.ds(pl.multiple_of(c * L, L), L)
        qs_s[rows, :] = _silu(_conv_chunk(q_ref, c, cw[:, :MIX_W], cb[:, :MIX_W], n=n, n_ctx=n_ctx)).astype(BF16)
        ks_s[rows, :] = (_silu(_conv_chunk(k_ref, c, cw[:, MIX_W:], cb[:, MIX_W:], n=n, n_ctx=n_ctx))
                         * HEAD_DIM ** -0.5).astype(BF16)
        return 0

    lax.fori_loop(0, n_chunks, conv_body, 0)

    def chunk(d, c):
        rev = d == 1
        rows = pl.ds(pl.multiple_of(c * L, L), L)
        q = qs_s[rows, :]
        k = ks_s[rows, :]
        v = v_ref[0, rows, :]
        g = g_ref[0, rows, :] + gb_ref[...]
        log_i = _dot_sel(g, expand(2 * d))
        log_f = _dot_sel(_log_sigmoid(g), expand(2 * d + 1))
        causal = (ci >= ri) if rev else (ci <= ri)
        tri = jnp.where(causal, 1.0, 0.0).astype(BF16)
        cum = _sel_dot(tri, log_f)
        cum_end = cum[0:1] if rev else cum[L - 1:L]
        m_prev = m_s[d]
        c_prev = c_s[d]
        n_prev = n_s[d]
        row_src = (log_i - cum).T
        m_inter = cum + m_prev
        num_inter = jnp.dot(q, c_prev.astype(BF16), preferred_element_type=F32)
        qn = q.astype(F32) * n_prev
        kf = k
        hid = jnp.zeros((L, MIX_W), F32)
        for h in range(N_HEADS):
            hm = head == h
            lane0 = h * HEAD_DIM
            log_d = jnp.where(causal, cum[:, lane0:lane0 + 1] + row_src[lane0:lane0 + 1, :], NEG_INF)
            m_q = jnp.maximum(jnp.max(log_d, axis=1, keepdims=True), m_inter[:, lane0:lane0 + 1])
            sc = _dot_nt(jnp.where(hm, q, jnp.zeros_like(q)), kf) * jnp.exp(log_d - m_q)
            s_inter = jnp.exp(m_inter[:, lane0:lane0 + 1] - m_q)
            num = jnp.dot(sc.astype(BF16), v, preferred_element_type=F32) + s_inter * num_inter
            den = (jnp.sum(sc, axis=1, keepdims=True)
                   + s_inter * jnp.sum(jnp.where(hm, qn, 0.0), axis=1, keepdims=True))
            hid = hid + jnp.where(hm, num / jnp.maximum(jnp.abs(den), jnp.exp(-m_q)), 0.0)
        log_w = cum_end - cum + log_i
        m_loc = jnp.max(log_w, axis=0, keepdims=True)
        kw = k.astype(F32) * jnp.exp(log_w - m_loc)
        c_loc = jnp.where(bd, _dot_tn(kw.astype(BF16), v), 0.0)
        n_loc = jnp.sum(kw, axis=0, keepdims=True)
        m_new = jnp.maximum(cum_end + m_prev, m_loc)
        s_old = jnp.exp(cum_end + m_prev - m_new)
        s_loc = jnp.exp(m_loc - m_new)
        c_s[d] = s_old * c_prev + s_loc * c_loc
        n_s[d] = s_old * n_prev + s_loc * n_loc
        m_s[d] = m_new
        return rows, hid

    c_s[...] = jnp.zeros_like(c_s)
    n_s[...] = jnp.zeros_like(n_s)
    m_s[...] = jnp.full(m_s.shape, M_INIT, F32)

    def walk_body(s, _):
        rows, hid = chunk(0, s)
        hf_s[0, rows, :] = hid
        rows, hid = chunk(1, _bwd_chunk(s, n_ctx_chunks, n_chunks))
        hf_s[1, rows, :] = hid
        return 0

    lax.fori_loop(0, n_chunks, walk_body, 0)

    def out_body(c, _):
        rows = pl.ds(pl.multiple_of(c * L, L), L)
        gate = _sigmoid(o_ref[0, rows, :].astype(F32))
        out_ref[0, rows, :] = (gate * _head_norm(hf_s[0, rows, :] + hf_s[1, rows, :], True)).astype(out_ref.dtype)
        return 0

    lax.fori_loop(0, n_chunks, out_body, 0)


def _mixer_c(p, gates, conv_w, conv_b, gate_b, n_ctx):
    b, n, _ = p.shape
    kern = functools.partial(_mixer_c_kernel, n=n, n_ctx=n_ctx)
    return pl.pallas_call(
        kern,
        grid=(b,),
        in_specs=[_seq_spec(n, 6), _seq_spec(n, 7), _seq_spec(n, 8), _seq_spec(n, 9),
                  pl.BlockSpec((1, n, LANES), lambda bi: (bi, 0, 0)),
                  _full_spec(conv_w.shape), _full_spec(conv_b.shape), _full_spec(gate_b.shape)],
        out_specs=pl.BlockSpec((1, n, MIX_W), lambda bi: (bi, 0, 0)),
        out_shape=jax.ShapeDtypeStruct((b, n, MIX_W), BF16),
        scratch_shapes=[pltpu.VMEM((n, MIX_W), BF16), pltpu.VMEM((n, MIX_W), BF16),
                        pltpu.VMEM((2, n, MIX_W), F32),
                        pltpu.VMEM((2, MIX_W, MIX_W), F32),
                        pltpu.VMEM((2, 1, MIX_W), F32),
                        pltpu.VMEM((2, 1, MIX_W), F32)],
        compiler_params=_cparams("parallel"),
        name="mixer_mlstm",
    )(p, p, p, p, gates, conv_w, conv_b, gate_b)


def _hgrn2_tables():
    L, S = CHUNK, SUB
    r = np.arange(L)
    same = (r[:, None] // S) == (r[None, :] // S)
    tri = np.stack([same & (r[None, :] <= r[:, None]), same & (r[None, :] >= r[:, None])])
    pair = np.arange(L * S)
    tok_i = (pair // (S * S)) * S + (pair // S) % S
    pi, pj = (pair // S) % S, pair % S
    fold = np.stack([(r[:, None] == tok_i[None, :]) & (pj <= pi)[None, :],
                     (r[:, None] == tok_i[None, :]) & (pj >= pi)[None, :]])
    as_bf16 = lambda a: jnp.asarray(a.astype(np.float32), dtype=BF16)
    return as_bf16(tri), as_bf16(same), as_bf16(fold)


def _mixer_d_kernel(q_ref, ff_ref, fb_ref, v_ref, g_ref, lb_ref, tri_ref, same_ref, fold_ref, out_ref,
                    o_s, st_s, *, n, n_ctx):
    L = CHUNK
    S = SUB
    nb = L // S
    n_chunks = n // L
    n_ctx_chunks = n_ctx // L
    lb = lb_ref[...]
    bd = _block_ones(MIX_W)
    ones_bd = jnp.where(bd, 1.0, 0.0).astype(BF16)

    def chunk(d, c):
        rev = d == 1
        rows = pl.ds(pl.multiple_of(c * L, L), L)
        q = _silu(q_ref[0, rows, :].astype(F32))
        raw = (fb_ref if rev else ff_ref)[0, rows, :].astype(F32)
        v = v_ref[0, rows, :]
        vf = v.astype(F32)
        f = lb + (1.0 - lb) * _sigmoid(raw)
        k = 1.0 - f
        log_f = jnp.log(f)
        cum = _sel_dot(tri_ref[d], log_f)
        tot = _sel_dot(same_ref[...], log_f)
        qt = (q * jnp.exp(cum)).astype(BF16)
        kt = (k * jnp.exp(tot - cum)).astype(BF16)
        g = jnp.exp(tot)
        prods, v_js = [], []
        cum2 = cum * LOG2_E
        for a in range(nb):
            sl = slice(a * S, (a + 1) * S)
            cb, qb, kb = cum2[sl], q[sl], k[sl]
            cum_i = jnp.concatenate([jnp.broadcast_to(cb[i:i + 1], (S, MIX_W)) for i in range(S)], axis=0)
            q_i = jnp.concatenate([jnp.broadcast_to(qb[i:i + 1], (S, MIX_W)) for i in range(S)], axis=0)
            cum_j = jnp.concatenate([cb] * S, axis=0)
            k_j = jnp.concatenate([kb] * S, axis=0)
            prods.append((q_i * k_j * jnp.exp2(jnp.minimum(cum_i - cum_j, 0.0))).astype(BF16))
            v_js.append(jnp.concatenate([vf[sl]] * S, axis=0))
        scores = jnp.dot(jnp.concatenate(prods, axis=0), ones_bd, preferred_element_type=F32)
        weighted = (scores * jnp.concatenate(v_js, axis=0)).astype(BF16)
        o = jnp.dot(fold_ref[d], weighted, preferred_element_type=F32)
        st = st_s[d]
        inter = [None] * nb
        for a in (range(nb - 1, -1, -1) if rev else range(nb)):
            sl = slice(a * S, (a + 1) * S)
            inter[a] = _dot_nt(qt[sl], st.astype(BF16))
            st = st * g[a * S:a * S + 1] + jnp.where(bd, _dot_tn(v[sl], kt[sl]), 0.0)
        st_s[d] = st
        o_s[d, rows, :] = o + jnp.concatenate(inter, axis=0)

    st_s[...] = jnp.zeros_like(st_s)

    def walk_body(s, _):
        chunk(0, s)
        chunk(1, _bwd_chunk(s, n_ctx_chunks, n_chunks))
        return 0

    lax.fori_loop(0, n_chunks, walk_body, 0)

    def out_body(c, _):
        rows = pl.ds(pl.multiple_of(c * L, L), L)
        gate = _silu(g_ref[0, rows, :].astype(F32))
        out_ref[0, rows, :] = (_head_norm(o_s[0, rows, :] + o_s[1, rows, :], False) * gate).astype(out_ref.dtype)
        return 0

    lax.fori_loop(0, n_chunks, out_body, 0)


def _mixer_d(p, lb, n_ctx):
    b, n, _ = p.shape
    tri, same, fold = _hgrn2_tables()
    kern = functools.partial(_mixer_d_kernel, n=n, n_ctx=n_ctx)
    return pl.pallas_call(
        kern,
        grid=(b,),
        in_specs=[_seq_spec(n, 10), _seq_spec(n, 11), _seq_spec(n, 12), _seq_spec(n, 13), _seq_spec(n, 14),
                  _full_spec(lb.shape), _full_spec(tri.shape), _full_spec(same.shape), _full_spec(fold.shape)],
        out_specs=pl.BlockSpec((1, n, MIX_W), lambda bi: (bi, 0, 0)),
        out_shape=jax.ShapeDtypeStruct((b, n, MIX_W), BF16),
        scratch_shapes=[pltpu.VMEM((2, n, MIX_W), F32), pltpu.VMEM((2, MIX_W, MIX_W), F32)],
        compiler_params=_cparams("parallel"),
        name="mixer_hgrn2",
    )(p, p, p, p, p, lb, tri, same, fold)


def _merge_kernel(x_ref, m_ref, ya_ref, yb_ref, yc_ref, yd_ref, wm_ref, wb_ref, wo_ref, wr_ref, br_ref,
                  xo_ref, h2_ref, gates_ref, *, tm, n_ctx):
    i = pl.program_id(1)
    row = i * tm + lax.broadcasted_iota(jnp.int32, (tm, 1), 0)
    is_ctx = row < n_ctx
    m = m_ref[0]
    x = x_ref[0]
    d = x.shape[-1]
    h = _modulate(x, _pick(m, is_ctx, 0), _pick(m, is_ctx, 1)).astype(BF16)
    z = jnp.zeros((tm, d), F32)
    for nb, y_ref in enumerate((ya_ref, yb_ref, yc_ref, yd_ref)):
        gate = _sigmoid(jnp.dot(h, wm_ref[:, nb * d:(nb + 1) * d], preferred_element_type=F32))
        z = z + gate * jnp.dot(y_ref[0], wb_ref[nb], preferred_element_type=F32)
    mix = jnp.dot(z.astype(BF16), wo_ref[...], preferred_element_type=F32)
    x1 = x + _pick(m, is_ctx, 2) * mix
    xo_ref[0] = x1
    h2 = _modulate(x1, _pick(m, is_ctx, 3), _pick(m, is_ctx, 4))
    h2_ref[0] = h2.astype(BF16)

    wr = wr_ref[...]
    wr_hi = wr.astype(BF16)
    wr_lo = (wr - wr_hi.astype(F32)).astype(BF16)
    h2_hi = h2.astype(BF16)
    h2_lo = (h2 - h2_hi.astype(F32)).astype(BF16)
    dot = functools.partial(jnp.dot, preferred_element_type=F32)
    logit = dot(h2_hi, wr_hi) + dot(h2_lo, wr_hi) + dot(h2_hi, wr_lo) + br_ref[...]
    lane = lax.broadcasted_iota(jnp.int32, (tm, LANES), 1)
    big = jnp.int32(LANES)
    is_group = (lane >= N_EXPERTS) & (lane < N_EXPERTS + N_GROUPS)
    gl = jnp.where(is_group, logit, NEG_INF)
    g_max = jnp.max(gl, axis=1, keepdims=True)
    g_idx = jnp.min(jnp.where(gl == g_max, lane, big), axis=1, keepdims=True) - N_EXPERTS
    g_prob = 1.0 / jnp.sum(jnp.where(is_group, jnp.exp(logit - g_max), 0.0), axis=1, keepdims=True)
    in_group = (lane < N_EXPERTS) & (lane // EXP_PER_GROUP == g_idx)
    el = jnp.where(in_group, logit, NEG_INF)
    v1 = jnp.max(el, axis=1, keepdims=True)
    i1 = jnp.min(jnp.where(el == v1, lane, big), axis=1, keepdims=True)
    el2 = jnp.where(lane == i1, NEG_INF, el)
    v2 = jnp.max(el2, axis=1, keepdims=True)
    i2 = jnp.min(jnp.where(el2 == v2, lane, big), axis=1, keepdims=True)
    e2 = jnp.exp(v2 - v1)
    w1 = g_prob / (1.0 + e2)
    w2 = g_prob * e2 / (1.0 + e2)
    gates_ref[0] = jnp.where(lane == i1, w1, 0.0) + jnp.where(lane == i2, w2, 0.0)


def _merge(xc, mvec, ys, w_merge, w_branch, w_out, w_route, b_route, n_ctx):
    b, n, d = xc.shape
    tm = _token_tile(n, 640)
    kern = functools.partial(_merge_kernel, tm=tm, n_ctx=n_ctx)
    tok = lambda w: pl.BlockSpec((1, tm, w), lambda bi, i: (bi, i, 0))
    const = lambda shape: pl.BlockSpec(shape, lambda bi, i: (0,) * len(shape))
    return pl.pallas_call(
        kern,
        grid=(b, n // tm),
        in_specs=[tok(d), pl.BlockSpec((1, MOD_ROWS, d), lambda bi, i: (bi, 0, 0)),
                  tok(MIX_W), tok(MIX_W), tok(MIX_W), tok(MIX_W),
                  const(w_merge.shape), const(w_branch.shape), const(w_out.shape),
                  const(w_route.shape), const(b_route.shape)],
        out_specs=[tok(d), tok(d), tok(LANES)],
        out_shape=[jax.ShapeDtypeStruct((b, n, d), F32),
                   jax.ShapeDtypeStruct((b, n, d), BF16),
                   jax.ShapeDtypeStruct((b, n, LANES), F32)],
        compiler_params=_cparams("parallel", "parallel"),
        name="merge_route",
    )(xc, mvec, *ys, w_merge, w_branch, w_out, w_route, b_route)


def _moe_kernel(x_ref, m_ref, h_ref, gates_ref, w1_ref, w3_ref, w2_ref, o_ref, acc_s, *, tm, n_ctx):
    i = pl.program_id(1)
    e = pl.program_id(2)

    @pl.when(e == 0)
    def _():
        acc_s[...] = jnp.zeros_like(acc_s)

    h = h_ref[0]
    a = jnp.dot(h, w1_ref[0, 0], preferred_element_type=F32)
    g = jnp.dot(h, w3_ref[0, 0], preferred_element_type=F32)
    y = jnp.dot((_silu(a) * g).astype(BF16), w2_ref[0, 0], preferred_element_type=F32)
    lane = lax.broadcasted_iota(jnp.int32, (tm, LANES), 1)
    gate = jnp.sum(jnp.where(lane == e, gates_ref[0], 0.0), axis=1, keepdims=True)
    acc_s[...] += gate * y

    @pl.when(e == N_EXPERTS - 1)
    def _():
        row = i * tm + lax.broadcasted_iota(jnp.int32, (tm, 1), 0)
        o_ref[0] = x_ref[0] + _pick(m_ref[0], row < n_ctx, 5) * acc_s[...]


def _moe(x1, mvec, h2, gates, w1, w3, w2, layer, n_ctx):
    b, n, d = x1.shape
    tm = _token_tile(n, 1100)
    kern = functools.partial(_moe_kernel, tm=tm, n_ctx=n_ctx)
    tok = lambda w: pl.BlockSpec((1, tm, w), lambda bi, i, e: (bi, i, 0))
    return pl.pallas_call(
        kern,
        grid=(b, n // tm, N_EXPERTS),
        in_specs=[tok(d), pl.BlockSpec((1, MOD_ROWS, d), lambda bi, i, e: (bi, 0, 0)), tok(d), tok(LANES),
                  pl.BlockSpec((1, 1, d, D_EXPERT), lambda bi, i, e: (layer, e, 0, 0)),
                  pl.BlockSpec((1, 1, d, D_EXPERT), lambda bi, i, e: (layer, e, 0, 0)),
                  pl.BlockSpec((1, 1, D_EXPERT, d), lambda bi, i, e: (layer, e, 0, 0))],
        out_specs=tok(d),
        out_shape=jax.ShapeDtypeStruct((b, n, d), F32),
        scratch_shapes=[pltpu.VMEM((tm, d), F32)],
        compiler_params=_cparams("parallel", "parallel", "arbitrary"),
        name="moe_experts",
    )(x1, mvec, h2, gates, w1, w3, w2)


def _final_kernel(x_ref, w_ref, o_ref):
    x = x_ref[0]
    ms = jnp.mean(x * x, axis=-1, keepdims=True)
    o_ref[0] = x * lax.rsqrt(ms + EPS) * w_ref[...]


def _final_norm(xc, w, n_ctx):
    b, n, d = xc.shape
    tm = _token_tile(n_ctx, 1024)
    skip = n_ctx // tm
    return pl.pallas_call(
        _final_kernel,
        grid=(b, (n - n_ctx) // tm),
        in_specs=[pl.BlockSpec((1, tm, d), lambda bi, i: (bi, i + skip, 0)),
                  pl.BlockSpec((1, d), lambda bi, i: (0, 0))],
        out_specs=pl.BlockSpec((1, tm, d), lambda bi, i: (bi, i, 0)),
        out_shape=jax.ShapeDtypeStruct((b, n - n_ctx, d), F32),
        compiler_params=_cparams("parallel", "parallel"),
        name="final_norm",
    )(xc, w.reshape(1, d))


def _rope_tables(n_lat, n_ctx):
    rows = n_lat // GRID_W
    row = jnp.repeat(jnp.arange(rows), GRID_W).astype(F32)
    col = jnp.tile(jnp.arange(GRID_W), rows).astype(F32)
    nq = HEAD_DIM // 4
    inv = jnp.power(ROPE_BASE, -jnp.arange(nq, dtype=F32) / nq)
    ang = jnp.concatenate([row[:, None] * inv, col[:, None] * inv], -1)
    cos = jnp.cos(ang)
    sin = jnp.sin(ang)
    cos_h = jnp.concatenate([cos, cos], -1)
    sin_h = jnp.concatenate([-sin, sin], -1)
    cos_full = jnp.concatenate([jnp.ones((n_ctx, HEAD_DIM), F32), cos_h], 0)
    sin_full = jnp.concatenate([jnp.zeros((n_ctx, HEAD_DIM), F32), sin_h], 0)
    return jnp.tile(cos_full, (1, N_HEADS)), jnp.tile(sin_full, (1, N_HEADS))


def _block_diag_heads(w):
    out = jnp.zeros((MIX_W, MIX_W), w.dtype)
    for h in range(N_HEADS):
        out = out.at[h * HEAD_DIM:(h + 1) * HEAD_DIM, h * HEAD_DIM:(h + 1) * HEAD_DIM].set(w[h])
    return out


def kernel(x, c, ctx, c_ctx, w_mod, b_mod, w_in, a_conv_w, a_conv_b, a_gate_w, a_gate_b, a_lambda, b_theta,
           c_conv_w, c_conv_b, c_gate_b, d_lb, w_branch, w_out, moe_w_group, moe_b_group, moe_w_router,
           moe_b_router, moe_w1, moe_w3, moe_w2, final_norm_w):
    bsz, n_lat, d = x.shape
    n_ctx = ctx.shape[1]
    depth = w_mod.shape[0]
    assert n_ctx % CHUNK == 0 and n_lat % CHUNK == 0 and n_ctx % 256 == 0 and n_lat % 256 == 0

    xc = jnp.concatenate([ctx, x], axis=1)
    cos, sin = _rope_tables(n_lat, n_ctx)

    cc = jnp.zeros((8, d), F32).at[:bsz].set(c).at[bsz].set(c_ctx)
    mod = _mod_vectors(cc, w_mod, b_mod)
    mx = mod[:, :bsz].reshape(depth, bsz, N_MOD, d)
    mc = jnp.broadcast_to(mod[:, bsz].reshape(depth, 1, N_MOD, d), (depth, bsz, N_MOD, d))
    pad = jnp.zeros((depth, bsz, 8 - N_MOD, d), F32)
    mvec = jnp.concatenate([mx, pad, mc, pad], axis=2)

    lbs = jnp.cumsum(jax.nn.softmax(d_lb.astype(F32), axis=0), axis=0)
    lbs = lbs - lbs[0]

    n_mix_cols = 15 * MIX_W
    gate0 = 10 * MIX_W
    n_gate = 4 * N_HEADS
    w1b = moe_w1.astype(BF16)
    w3b = moe_w3.astype(BF16)
    w2b = moe_w2.astype(BF16)

    for l in range(depth):
        wl = w_in[l]
        w_mix = jnp.concatenate([wl[:, :gate0], wl[:, gate0 + n_gate:n_mix_cols + n_gate]], axis=1).astype(BF16)
        w_gate = jnp.pad(wl[:, gate0:gate0 + n_gate], ((0, 0), (0, LANES - n_gate))).astype(BF16)
        w_merge = wl[:, n_mix_cols + n_gate:].astype(BF16)
        p, gates_c = _in_proj(xc, mvec[l], w_mix, w_gate, cos, sin, n_ctx)

        gw = jnp.concatenate([_block_diag_heads(a_gate_w[l, dd, j]) for dd in range(2) for j in range(2)],
                             axis=1).astype(BF16)
        gb = a_gate_b[l].reshape(1, 4 * MIX_W)
        ya = _mixer_a(p, a_conv_w[l], a_conv_b[l].reshape(1, MIX_W), gw, gb, a_lambda[l], n_ctx)

        log_gamma = jnp.repeat(jax.nn.log_sigmoid(b_theta[l].astype(F32)), HEAD_DIM, axis=1)
        yb = _mixer_b(p, log_gamma, n_ctx)

        gate_b = jnp.pad(c_gate_b[l].reshape(1, n_gate), ((0, 0), (0, LANES - n_gate)))
        yc = _mixer_c(p, gates_c, c_conv_w[l], c_conv_b[l].reshape(1, 2 * MIX_W), gate_b, n_ctx)

        yd = _mixer_d(p, lbs[l].reshape(1, MIX_W), n_ctx)

        w_route = jnp.pad(jnp.concatenate([moe_w_router[l], moe_w_group[l]], axis=1),
                          ((0, 0), (0, LANES - N_EXPERTS - N_GROUPS)))
        b_route = jnp.pad(jnp.concatenate([moe_b_router[l], moe_b_group[l]]),
                          (0, LANES - N_EXPERTS - N_GROUPS)).reshape(1, LANES)
        x1, h2, gates = _merge(xc, mvec[l], (ya, yb, yc, yd), w_merge, w_branch[l].astype(BF16),
                               w_out[l].astype(BF16), w_route, b_route, n_ctx)
        xc = _moe(x1, mvec[l], h2, gates, w1b, w3b, w2b, l, n_ctx)

    return _final_norm(xc, final_norm_w, n_ctx)
```

```python
import functools

import jax
import jax.numpy as jnp
import numpy as np
from jax import lax
from jax.experimental import pallas as pl
from jax.experimental.pallas import tpu as pltpu
from jax.experimental.pallas import tpu_sc as plsc

F32 = jnp.float32
BF16 = jnp.bfloat16

EPS = 1e-6
N_HEADS = 4
HEAD_DIM = 64
MIX_W = N_HEADS * HEAD_DIM
N_BRANCH = 4
CONV_W = 4
LRU_C = 8.0
GRID_W = 64
ROPE_BASE = 10000.0
N_GROUPS = 4
EXP_PER_GROUP = 4
N_EXPERTS = N_GROUPS * EXP_PER_GROUP
D_EXPERT = 512
N_MOD = 6
M_INIT = -1e30

CHUNK = 128
SUB = 16
HALO = 16
LANES = 128
MOD_ROWS = 16
VMEM_LIMIT_BYTES = 56 * 1024 * 1024
N_PAIRS = 6
N_CLASSES = N_GROUPS * N_PAIRS
MOE_TILE = 256
SC_CORES = 2
SC_SUBCORES = 16
SC_WORKERS = SC_CORES * SC_SUBCORES
SC_MAX_CHUNK = 32
NEG_INF = float("-inf")
LOG2_E = 1.4426950408889634


def _cparams(*sem):
    return pltpu.CompilerParams(dimension_semantics=sem, vmem_limit_bytes=VMEM_LIMIT_BYTES)


def _token_tile(n, cap):
    best = 16
    for t in range(16, cap + 1, 16):
        if n % t == 0:
            best = t
    return best


def _modulate(x, shift, scale):
    ms = jnp.mean(x * x, axis=-1, keepdims=True)
    return x * lax.rsqrt(ms + EPS) * (1.0 + scale) + shift


def _pick(m, is_ctx, k):
    return jnp.where(is_ctx, m[8 + k:9 + k], m[k:k + 1])


def _sigmoid(x):
    return 0.5 * jnp.tanh(0.5 * x) + 0.5


def _silu(x):
    return x * _sigmoid(x)


def _log_sigmoid(x):
    return jnp.minimum(x, 0.0) - jnp.log(1.0 + jnp.exp(-jnp.abs(x)))


def _split3(x):
    hi = x.astype(BF16)
    r = x - hi.astype(F32)
    mid = r.astype(BF16)
    lo = (r - mid.astype(F32)).astype(BF16)
    return hi, mid, lo


def _sel_dot(sel, x):
    hi, mid, lo = _split3(x)
    d = functools.partial(jnp.dot, preferred_element_type=F32)
    return d(sel, hi) + d(sel, mid) + d(sel, lo)


def _dot_sel(x, sel):
    hi, mid, lo = _split3(x)
    d = functools.partial(jnp.dot, preferred_element_type=F32)
    return d(hi, sel) + d(mid, sel) + d(lo, sel)


def _dot_nt(a, b):
    return lax.dot_general(a, b, (((1,), (1,)), ((), ())), preferred_element_type=F32)


def _dot_tn(a, b):
    return lax.dot_general(a, b, (((0,), (0,)), ((), ())), preferred_element_type=F32)


def _head_of_lane(width):
    return lax.broadcasted_iota(jnp.int32, (1, width), 1) // HEAD_DIM


def _block_ones(width):
    r = lax.broadcasted_iota(jnp.int32, (width, width), 0) // HEAD_DIM
    c = lax.broadcasted_iota(jnp.int32, (width, width), 1) // HEAD_DIM
    return r == c


def _head_norm(o, center):
    ones = jnp.where(_block_ones(MIX_W), 1.0, 0.0).astype(BF16)
    inv = 1.0 / HEAD_DIM
    if center:
        o = o - _dot_sel(o, ones) * inv
    var = _dot_sel(o * o, ones) * inv
    return o * lax.rsqrt(var + EPS)


def _bwd_chunk(s, n_ctx_chunks, n_chunks):
    return jnp.where(s < n_ctx_chunks, n_ctx_chunks - 1 - s, n_chunks - 1 + n_ctx_chunks - s)


def _conv_chunk(ref, c, w, b, *, n, n_ctx, width0=0, width=None):
    L = CHUNK
    wl = L + 2 * HALO
    r0 = c * L
    start = pl.multiple_of(jnp.clip(r0 - HALO, 0, n - wl), HALO)
    off = r0 - start
    if width is None:
        win = ref[0, pl.ds(start, wl), :]
    else:
        win = ref[0, pl.ds(start, wl), width0:width0 + width]
    ri = lax.broadcasted_iota(jnp.int32, (L, wl), 0)
    mi = lax.broadcasted_iota(jnp.int32, (L, wl), 1)
    g = r0 + ri
    y = b
    seg_g = jnp.where(g < n_ctx, 0, 1)
    for k in range(CONV_W):
        src = g + (k - 2)
        hit = (mi == ri + off + (k - 2)) & (jnp.where(src < n_ctx, 0, 1) == seg_g) & (src >= 0) & (src < n)
        sel = jnp.where(hit, 1.0, 0.0).astype(BF16)
        y = y + w[k:k + 1] * jnp.dot(sel, win, preferred_element_type=F32)
    return y


def _mod_kernel(cc_ref, w_ref, b_ref, o_ref):
    s = _silu(cc_ref[...])
    o_ref[0] = jnp.dot(s, w_ref[0], precision=lax.Precision.HIGHEST, preferred_element_type=F32) + b_ref[0]


def _mod_vectors(cc, w_mod, b_mod):
    depth, d, dm = w_mod.shape
    tn = 1536
    return pl.pallas_call(
        _mod_kernel,
        grid=(depth, dm // tn),
        in_specs=[pl.BlockSpec((8, d), lambda l, j: (0, 0)),
                  pl.BlockSpec((1, d, tn), lambda l, j: (l, 0, j)),
                  pl.BlockSpec((1, 1, tn), lambda l, j: (l, 0, j))],
        out_specs=pl.BlockSpec((1, 8, tn), lambda l, j: (l, 0, j)),
        out_shape=jax.ShapeDtypeStruct((depth, 8, dm), F32),
        compiler_params=_cparams("parallel", "parallel"),
        name="mod_vectors",
    )(cc, w_mod, b_mod.reshape(depth, 1, dm))


def _rope(t, cos, sin):
    lane = lax.broadcasted_iota(jnp.int32, (1, MIX_W), 1) % HEAD_DIM
    half = HEAD_DIM // 2
    swapped = jnp.where(lane < half, pltpu.roll(t, MIX_W - half, 1), pltpu.roll(t, half, 1))
    return t * cos + swapped * sin


RET_Q_COL = 2 * MIX_W
RET_K_COL = 3 * MIX_W


def _in_proj_kernel(x_ref, m_ref, w_ref, wg_ref, cos_ref, sin_ref, p_ref, g_ref, *, tm, n_ctx, col_chunk):
    i = pl.program_id(1)
    row = i * tm + lax.broadcasted_iota(jnp.int32, (tm, 1), 0)
    is_ctx = row < n_ctx
    m = m_ref[0]
    h = _modulate(x_ref[0], _pick(m, is_ctx, 0), _pick(m, is_ctx, 1)).astype(BF16)
    for j in range(w_ref.shape[1] // col_chunk):
        sl = slice(j * col_chunk, (j + 1) * col_chunk)
        t = jnp.dot(h, w_ref[:, sl], preferred_element_type=F32)
        if j * col_chunk == RET_Q_COL:
            t = _rope(t, cos_ref[...], sin_ref[...])
        elif j * col_chunk == RET_K_COL:
            t = _rope(t, cos_ref[...], sin_ref[...]) * HEAD_DIM ** -0.5
        p_ref[0, :, sl] = t.astype(BF16)
    g_ref[0] = jnp.dot(h, wg_ref[...], preferred_element_type=F32)


def _in_proj(xc, mvec, w_mix, w_gate, cos, sin, n_ctx):
    b, n, d = xc.shape
    tm = _token_tile(n, 640)
    wc = w_mix.shape[1]
    kern = functools.partial(_in_proj_kernel, tm=tm, n_ctx=n_ctx, col_chunk=MIX_W)
    return pl.pallas_call(
        kern,
        grid=(b, n // tm),
        in_specs=[pl.BlockSpec((1, tm, d), lambda bi, i: (bi, i, 0)),
                  pl.BlockSpec((1, MOD_ROWS, d), lambda bi, i: (bi, 0, 0)),
                  pl.BlockSpec((d, wc), lambda bi, i: (0, 0)),
                  pl.BlockSpec((d, LANES), lambda bi, i: (0, 0)),
                  pl.BlockSpec((tm, MIX_W), lambda bi, i: (i, 0)),
                  pl.BlockSpec((tm, MIX_W), lambda bi, i: (i, 0))],
        out_specs=[pl.BlockSpec((1, tm, wc), lambda bi, i: (bi, i, 0)),
                   pl.BlockSpec((1, tm, LANES), lambda bi, i: (bi, i, 0))],
        out_shape=[jax.ShapeDtypeStruct((b, n, wc), BF16),
                   jax.ShapeDtypeStruct((b, n, LANES), F32)],
        compiler_params=_cparams("parallel", "parallel"),
        name="in_proj",
    )(xc, mvec, w_mix, w_gate, cos, sin)


def _lin_scan(a, x, reverse):
    L = a.shape[0]
    row = lax.broadcasted_iota(jnp.int32, (L, 1), 0)
    k = 1
    while k < L:
        if reverse:
            a_s = pltpu.roll(a, L - k, 0)
            x_s = pltpu.roll(x, L - k, 0)
            valid = row < L - k
        else:
            a_s = pltpu.roll(a, k, 0)
            x_s = pltpu.roll(x, k, 0)
            valid = row >= k
        x = jnp.where(valid, a * x_s + x, x)
        a = jnp.where(valid, a * a_s, a)
        k *= 2
    return a, x


def _mixer_a_kernel(ax_ref, ag_ref, cw_ref, cb_ref, gw_ref, gb_ref, lam_ref, out_ref, u_s, hf_s,
                    *, n, n_ctx):
    L = CHUNK
    n_chunks = n // L
    n_ctx_chunks = n_ctx // L
    cw = cw_ref[...]
    cb = cb_ref[...]
    log_lam = _log_sigmoid(lam_ref[...])

    def direction(d, c, carry):
        r0 = pl.multiple_of(c * L, L)
        u = u_s[pl.ds(r0, L), :]
        pre = jnp.dot(u.astype(BF16), gw_ref[:, d * 2 * MIX_W:(d + 1) * 2 * MIX_W],
                      preferred_element_type=F32) + gb_ref[:, d * 2 * MIX_W:(d + 1) * 2 * MIX_W]
        r = _sigmoid(pre[:, :MIX_W])
        gi = _sigmoid(pre[:, MIX_W:])
        log_a = LRU_C * r * log_lam[d:d + 1]
        a = jnp.exp(log_a)
        inp = jnp.sqrt(1.0 - jnp.exp(2.0 * log_a)) * (gi * u)
        a_cum, h = _lin_scan(a, inp, reverse=(d == 1))
        h = h + a_cum * carry
        new_carry = h[0:1] if d == 1 else h[L - 1:L]
        return r0, h, new_carry

    def conv_body(c, _):
        r0 = pl.multiple_of(c * L, L)
        u_s[pl.ds(r0, L), :] = _conv_chunk(ax_ref, c, cw, cb, n=n, n_ctx=n_ctx)
        return 0

    lax.fori_loop(0, n_chunks, conv_body, 0)

    def walk_body(s, carries):
        r0, h, carry_f = direction(0, s, carries[0])
        hf_s[0, pl.ds(r0, L), :] = h
        r0, h, carry_b = direction(1, _bwd_chunk(s, n_ctx_chunks, n_chunks), carries[1])
        hf_s[1, pl.ds(r0, L), :] = h
        return carry_f, carry_b

    zero = jnp.zeros((1, MIX_W), F32)
    lax.fori_loop(0, n_chunks, walk_body, (zero, zero))

    def out_body(c, _):
        rows = pl.ds(pl.multiple_of(c * L, L), L)
        gate = jax.nn.gelu(ag_ref[0, rows, :].astype(F32), approximate=True)
        out_ref[0, rows, :] = (gate * (hf_s[0, rows, :] + hf_s[1, rows, :])).astype(out_ref.dtype)
        return 0

    lax.fori_loop(0, n_chunks, out_body, 0)


def _seq_spec(n, col):
    return pl.BlockSpec((1, n, MIX_W), lambda b: (b, 0, col))


def _full_spec(shape):
    return pl.BlockSpec(shape, lambda b: (0,) * len(shape))


def _mixer_a(p, conv_w, conv_b, gate_w, gate_b, lam, n_ctx):
    b, n, _ = p.shape
    kern = functools.partial(_mixer_a_kernel, n=n, n_ctx=n_ctx)
    return pl.pallas_call(
        kern,
        grid=(b,),
        in_specs=[_seq_spec(n, 0), _seq_spec(n, 1),
                  _full_spec(conv_w.shape), _full_spec(conv_b.shape),
                  _full_spec(gate_w.shape), _full_spec(gate_b.shape), _full_spec(lam.shape)],
        out_specs=pl.BlockSpec((1, n, MIX_W), lambda bi: (bi, 0, 0)),
        out_shape=jax.ShapeDtypeStruct((b, n, MIX_W), BF16),
        scratch_shapes=[pltpu.VMEM((n, MIX_W), F32), pltpu.VMEM((2, n, MIX_W), F32)],
        compiler_params=_cparams("parallel"),
        name="mixer_rglru",
    )(p, p, conv_w, conv_b, gate_w, gate_b, lam)


def _mixer_b_kernel(q_ref, k_ref, v_ref, g_ref, lg_ref, out_ref, oi_s, dec_s, st_s, *, n, n_ctx):
    L = CHUNK
    n_chunks = n // L
    n_ctx_chunks = n_ctx // L
    head = _head_of_lane(MIX_W)
    bd = _block_ones(MIX_W)
    lgf = lg_ref[0:1, :]
    lgb = lg_ref[1:2, :]
    pos = lax.broadcasted_iota(jnp.int32, (L, 1), 0).astype(F32)
    diff = (lax.broadcasted_iota(jnp.int32, (L, L), 0) - lax.broadcasted_iota(jnp.int32, (L, L), 1)).astype(F32)
    for h in range(N_HEADS):
        lf = lgf[:, h * HEAD_DIM:h * HEAD_DIM + 1]
        lb = lgb[:, h * HEAD_DIM:h * HEAD_DIM + 1]
        dec_s[h] = jnp.where(diff >= 0, jnp.exp(lf * jnp.maximum(diff, 0.0)), jnp.exp(lb * jnp.maximum(-diff, 0.0)))

    qw = (jnp.exp(lgf * (pos + 1.0)), jnp.exp(lgb * (float(L) - pos)))
    kw = (jnp.exp(lgf * (float(L) - 1.0 - pos)), jnp.exp(lgb * pos))
    g_chunk = (jnp.exp(lgf * float(L)), jnp.exp(lgb * float(L)))

    def state_step(d, c):
        rows = pl.ds(pl.multiple_of(c * L, L), L)
        q = q_ref[0, rows, :].astype(F32)
        k = k_ref[0, rows, :].astype(F32)
        st = st_s[d]
        oi_s[d, rows, :] = jnp.dot((q * qw[d]).astype(BF16), st.astype(BF16), preferred_element_type=F32)
        kv = _dot_tn((k * kw[d]).astype(BF16), v_ref[0, rows, :])
        st_s[d] = g_chunk[d] * st + jnp.where(bd, kv, 0.0)

    st_s[...] = jnp.zeros_like(st_s)

    def walk_body(s, _):
        state_step(0, s)
        state_step(1, _bwd_chunk(s, n_ctx_chunks, n_chunks))
        return 0

    lax.fori_loop(0, n_chunks, walk_body, 0)

    def out_body(c, _):
        rows = pl.ds(pl.multiple_of(c * L, L), L)
        q = q_ref[0, rows, :]
        k = k_ref[0, rows, :]
        v = v_ref[0, rows, :]
        o = oi_s[0, rows, :] + oi_s[1, rows, :]
        for h in range(N_HEADS):
            hm = head == h
            sc = _dot_nt(jnp.where(hm, q, jnp.zeros_like(q)), k) * dec_s[h]
            o = o + jnp.where(hm, jnp.dot(sc.astype(BF16), v, preferred_element_type=F32), 0.0)
        gate = _silu(g_ref[0, rows, :].astype(F32))
        out_ref[0, rows, :] = (_head_norm(o, True) * gate).astype(out_ref.dtype)
        return 0

    lax.fori_loop(0, n_chunks, out_body, 0, unroll=2)


def _mixer_b(p, log_gamma, n_ctx):
    b, n, _ = p.shape
    kern = functools.partial(_mixer_b_kernel, n=n, n_ctx=n_ctx)
    return pl.pallas_call(
        kern,
        grid=(b,),
        in_specs=[_seq_spec(n, 2), _seq_spec(n, 3), _seq_spec(n, 4), _seq_spec(n, 5),
                  _full_spec(log_gamma.shape)],
        out_specs=pl.BlockSpec((1, n, MIX_W), lambda bi: (bi, 0, 0)),
        out_shape=jax.ShapeDtypeStruct((b, n, MIX_W), BF16),
        scratch_shapes=[pltpu.VMEM((2, n, MIX_W), F32),
                        pltpu.VMEM((N_HEADS, CHUNK, CHUNK), F32),
                        pltpu.VMEM((2, MIX_W, MIX_W), F32)],
        compiler_params=_cparams("parallel"),
        name="mixer_retention",
    )(p, p, p, p, log_gamma)


def _mixer_c_kernel(q_ref, k_ref, v_ref, o_ref, g_ref, cw_ref, cb_ref, gb_ref, out_ref,
                    qs_s, ks_s, hf_s, c_s, n_s, m_s, *, n, n_ctx):
    L = CHUNK
    n_chunks = n // L
    n_ctx_chunks = n_ctx // L
    head = _head_of_lane(MIX_W)
    bd = _block_ones(MIX_W)
    cw = cw_ref[...]
    cb = cb_ref[...]
    ri = lax.broadcasted_iota(jnp.int32, (L, L), 0)
    ci = lax.broadcasted_iota(jnp.int32, (L, L), 1)
    gl = lax.broadcasted_iota(jnp.int32, (LANES, MIX_W), 0)
    hl = lax.broadcasted_iota(jnp.int32, (LANES, MIX_W), 1) // HEAD_DIM

    def expand(kind):
        return jnp.where(gl == kind * N_HEADS + hl, 1.0, 0.0).astype(BF16)

    def conv_body(c, _):
        rows = pl.ds(pl.multiple_of(c * L, L), L)
        qs_s[rows, :] = _silu(_conv_chunk(q_ref, c, cw[:, :MIX_W], cb[:, :MIX_W], n=n, n_ctx=n_ctx)).astype(BF16)
        ks_s[rows, :] = (_silu(_conv_chunk(k_ref, c, cw[:, MIX_W:], cb[:, MIX_W:], n=n, n_ctx=n_ctx))
                         * HEAD_DIM ** -0.5).astype(BF16)
        return 0

    lax.fori_loop(0, n_chunks, conv_body, 0)

    def chunk(d, c):
        rev = d == 1
        rows = pl.ds(pl.multiple_of(c * L, L), L)
        q = qs_s[rows, :]
        k = ks_s[rows, :]
        v = v_ref[0, rows, :]
        g = g_ref[0, rows, :] + gb_ref[...]
        log_i = _dot_sel(g, expand(2 * d))
        log_f = _dot_sel(_log_sigmoid(g), expand(2 * d + 1))
        causal = (ci >= ri) if rev else (ci <= ri)
        tri = jnp.where(causal, 1.0, 0.0).astype(BF16)
        cum = _sel_dot(tri, log_f)
        cum_end = cum[0:1] if rev else cum[L - 1:L]
        m_prev = m_s[d]
        c_prev = c_s[d]
        n_prev = n_s[d]
        row_src = (log_i - cum).T
        m_inter = cum + m_prev
        num_inter = jnp.dot(q, c_prev.astype(BF16), preferred_element_type=F32)
        qn = q.astype(F32) * n_prev
        kf = k
        hid = jnp.zeros((L, MIX_W), F32)
        for h in range(N_HEADS):
            hm = head == h
            lane0 = h * HEAD_DIM
            log_d = jnp.where(causal, cum[:, lane0:lane0 + 1] + row_src[lane0:lane0 + 1, :], NEG_INF)
            m_q = jnp.maximum(jnp.max(log_d, axis=1, keepdims=True), m_inter[:, lane0:lane0 + 1])
            sc = _dot_nt(jnp.where(hm, q, jnp.zeros_like(q)), kf) * jnp.exp(log_d - m_q)
            s_inter = jnp.exp(m_inter[:, lane0:lane0 + 1] - m_q)
            num = jnp.dot(sc.astype(BF16), v, preferred_element_type=F32) + s_inter * num_inter
            den = (jnp.sum(sc, axis=1, keepdims=True)
                   + s_inter * jnp.sum(jnp.where(hm, qn, 0.0), axis=1, keepdims=True))
            hid = hid + jnp.where(hm, num / jnp.maximum(jnp.abs(den), jnp.exp(-m_q)), 0.0)
        log_w = cum_end - cum + log_i
        m_loc = jnp.max(log_w, axis=0, keepdims=True)
        kw = k.astype(F32) * jnp.exp(log_w - m_loc)
        c_loc = jnp.where(bd, _dot_tn(kw.astype(BF16), v), 0.0)
        n_loc = jnp.sum(kw, axis=0, keepdims=True)
        m_new = jnp.maximum(cum_end + m_prev, m_loc)
        s_old = jnp.exp(cum_end + m_prev - m_new)
        s_loc = jnp.exp(m_loc - m_new)
        c_s[d] = s_old * c_prev + s_loc * c_loc
        n_s[d] = s_old * n_prev + s_loc * n_loc
        m_s[d] = m_new
        return rows, hid

    c_s[...] = jnp.zeros_like(c_s)
    n_s[...] = jnp.zeros_like(n_s)
    m_s[...] = jnp.full(m_s.shape, M_INIT, F32)

    def walk_body(s, _):
        rows, hid = chunk(0, s)
        hf_s[0, rows, :] = hid
        rows, hid = chunk(1, _bwd_chunk(s, n_ctx_chunks, n_chunks))
        hf_s[1, rows, :] = hid
        return 0

    lax.fori_loop(0, n_chunks, walk_body, 0)

    def out_body(c, _):
        rows = pl.ds(pl.multiple_of(c * L, L), L)
        gate = _sigmoid(o_ref[0, rows, :].astype(F32))
        out_ref[0, rows, :] = (gate * _head_norm(hf_s[0, rows, :] + hf_s[1, rows, :], True)).astype(out_ref.dtype)
        return 0

    lax.fori_loop(0, n_chunks, out_body, 0)


def _mixer_c(p, gates, conv_w, conv_b, gate_b, n_ctx):
    b, n, _ = p.shape
    kern = functools.partial(_mixer_c_kernel, n=n, n_ctx=n_ctx)
    return pl.pallas_call(
        kern,
        grid=(b,),
        in_specs=[_seq_spec(n, 6), _seq_spec(n, 7), _seq_spec(n, 8), _seq_spec(n, 9),
                  pl.BlockSpec((1, n, LANES), lambda bi: (bi, 0, 0)),
                  _full_spec(conv_w.shape), _full_spec(conv_b.shape), _full_spec(gate_b.shape)],
        out_specs=pl.BlockSpec((1, n, MIX_W), lambda bi: (bi, 0, 0)),
        out_shape=jax.ShapeDtypeStruct((b, n, MIX_W), BF16),
        scratch_shapes=[pltpu.VMEM((n, MIX_W), BF16), pltpu.VMEM((n, MIX_W), BF16),
                        pltpu.VMEM((2, n, MIX_W), F32),
                        pltpu.VMEM((2, MIX_W, MIX_W), F32),
                        pltpu.VMEM((2, 1, MIX_W), F32),
                        pltpu.VMEM((2, 1, MIX_W), F32)],
        compiler_params=_cparams("parallel"),
        name="mixer_mlstm",
    )(p, p, p, p, gates, conv_w, conv_b, gate_b)


def _hgrn2_tables():
    L, S = CHUNK, SUB
    r = np.arange(L)
    same = (r[:, None] // S) == (r[None, :] // S)
    tri = np.stack([same & (r[None, :] <= r[:, None]), same & (r[None, :] >= r[:, None])])
    pair = np.arange(L * S)
    tok_i = (pair // (S * S)) * S + (pair // S) % S
    pi, pj = (pair // S) % S, pair % S
    fold = np.stack([(r[:, None] == tok_i[None, :]) & (pj <= pi)[None, :],
                     (r[:, None] == tok_i[None, :]) & (pj >= pi)[None, :]])
    as_bf16 = lambda a: jnp.asarray(a.astype(np.float32), dtype=BF16)
    return as_bf16(tri), as_bf16(same), as_bf16(fold)


def _mixer_d_kernel(q_ref, ff_ref, fb_ref, v_ref, g_ref, lb_ref, tri_ref, same_ref, fold_ref, out_ref,
                    o_s, st_s, *, n, n_ctx):
    L = CHUNK
    S = SUB
    nb = L // S
    n_chunks = n // L
    n_ctx_chunks = n_ctx // L
    lb = lb_ref[...]
    bd = _block_ones(MIX_W)
    ones_bd = jnp.where(bd, 1.0, 0.0).astype(BF16)

    def chunk(d, c):
        rev = d == 1
        rows = pl.ds(pl.multiple_of(c * L, L), L)
        q = _silu(q_ref[0, rows, :].astype(F32))
        raw = (fb_ref if rev else ff_ref)[0, rows, :].astype(F32)
        v = v_ref[0, rows, :]
        vf = v.astype(F32)
        f = lb + (1.0 - lb) * _sigmoid(raw)
        k = 1.0 - f
        log_f = jnp.log(f)
        cum = _sel_dot(tri_ref[d], log_f)
        tot = _sel_dot(same_ref[...], log_f)
        qt = (q * jnp.exp(cum)).astype(BF16)
        kt = (k * jnp.exp(tot - cum)).astype(BF16)
        g = jnp.exp(tot)
        prods, v_js = [], []
        cum2 = cum * LOG2_E
        for a in range(nb):
            sl = slice(a * S, (a + 1) * S)
            cb, qb, kb = cum2[sl], q[sl], k[sl]
            cum_i = jnp.concatenate([jnp.broadcast_to(cb[i:i + 1], (S, MIX_W)) for i in range(S)], axis=0)
            q_i = jnp.concatenate([jnp.broadcast_to(qb[i:i + 1], (S, MIX_W)) for i in range(S)], axis=0)
            cum_j = jnp.concatenate([cb] * S, axis=0)
            k_j = jnp.concatenate([kb] * S, axis=0)
            prods.append((q_i * k_j * jnp.exp2(jnp.minimum(cum_i - cum_j, 0.0))).astype(BF16))
            v_js.append(jnp.concatenate([vf[sl]] * S, axis=0))
        scores = jnp.dot(jnp.concatenate(prods, axis=0), ones_bd, preferred_element_type=F32)
        weighted = (scores * jnp.concatenate(v_js, axis=0)).astype(BF16)
        o = jnp.dot(fold_ref[d], weighted, preferred_element_type=F32)
        st = st_s[d]
        inter = [None] * nb
        for a in (range(nb - 1, -1, -1) if rev else range(nb)):
            sl = slice(a * S, (a + 1) * S)
            inter[a] = _dot_nt(qt[sl], st.astype(BF16))
            st = st * g[a * S:a * S + 1] + jnp.where(bd, _dot_tn(v[sl], kt[sl]), 0.0)
        st_s[d] = st
        o_s[d, rows, :] = o + jnp.concatenate(inter, axis=0)

    st_s[...] = jnp.zeros_like(st_s)

    def walk_body(s, _):
        chunk(0, s)
        chunk(1, _bwd_chunk(s, n_ctx_chunks, n_chunks))
        return 0

    lax.fori_loop(0, n_chunks, walk_body, 0)

    def out_body(c, _):
        rows = pl.ds(pl.multiple_of(c * L, L), L)
        gate = _silu(g_ref[0, rows, :].astype(F32))
        out_ref[0, rows, :] = (_head_norm(o_s[0, rows, :] + o_s[1, rows, :], False) * gate).astype(out_ref.dtype)
        return 0

    lax.fori_loop(0, n_chunks, out_body, 0)


def _mixer_d(p, lb, n_ctx):
    b, n, _ = p.shape
    tri, same, fold = _hgrn2_tables()
    kern = functools.partial(_mixer_d_kernel, n=n, n_ctx=n_ctx)
    return pl.pallas_call(
        kern,
        grid=(b,),
        in_specs=[_seq_spec(n, 10), _seq_spec(n, 11), _seq_spec(n, 12), _seq_spec(n, 13), _seq_spec(n, 14),
                  _full_spec(lb.shape), _full_spec(tri.shape), _full_spec(same.shape), _full_spec(fold.shape)],
        out_specs=pl.BlockSpec((1, n, MIX_W), lambda bi: (bi, 0, 0)),
        out_shape=jax.ShapeDtypeStruct((b, n, MIX_W), BF16),
        scratch_shapes=[pltpu.VMEM((2, n, MIX_W), F32), pltpu.VMEM((2, MIX_W, MIX_W), F32)],
        compiler_params=_cparams("parallel"),
        name="mixer_hgrn2",
    )(p, p, p, p, p, lb, tri, same, fold)


def _pack_bf16_pairs(t):
    w = t.shape[1] // 2
    hi = pltpu.bitcast(t[:, :w].astype(BF16).astype(F32), jnp.uint32)
    lo = pltpu.bitcast(t[:, w:].astype(BF16).astype(F32), jnp.uint32)
    return pltpu.bitcast(hi | (lo >> 16), jnp.int32)


def _unpack_bf16_pairs(p):
    u = pltpu.bitcast(p, jnp.uint32)
    hi = pltpu.bitcast(u & jnp.uint32(0xFFFF0000), F32)
    lo = pltpu.bitcast(u << 16, F32)
    return hi, lo


def _merge_kernel(x_ref, m_ref, ya_ref, yb_ref, yc_ref, yd_ref, wm_ref, wb_ref, wo_ref, wr_ref, br_ref,
                  xo_ref, hp_ref, route_ref, counts_ref, carry_s, *, tm, n_ctx):
    i = pl.program_id(1)

    @pl.when((pl.program_id(0) == 0) & (i == 0))
    def _():
        carry_s[...] = jnp.zeros_like(carry_s)

    row = i * tm + lax.broadcasted_iota(jnp.int32, (tm, 1), 0)
    is_ctx = row < n_ctx
    m = m_ref[0]
    x = x_ref[0]
    d = x.shape[-1]
    h = _modulate(x, _pick(m, is_ctx, 0), _pick(m, is_ctx, 1)).astype(BF16)
    z = jnp.zeros((tm, d), F32)
    for nb, y_ref in enumerate((ya_ref, yb_ref, yc_ref, yd_ref)):
        gate = _sigmoid(jnp.dot(h, wm_ref[:, nb * d:(nb + 1) * d], preferred_element_type=F32))
        z = z + gate * jnp.dot(y_ref[0], wb_ref[nb], preferred_element_type=F32)
    mix = jnp.dot(z.astype(BF16), wo_ref[...], preferred_element_type=F32)
    x1 = x + _pick(m, is_ctx, 2) * mix
    xo_ref[0] = x1
    h2 = _modulate(x1, _pick(m, is_ctx, 3), _pick(m, is_ctx, 4))

    wr = wr_ref[...]
    wr_hi = wr.astype(BF16)
    wr_lo = (wr - wr_hi.astype(F32)).astype(BF16)
    h2_hi = h2.astype(BF16)
    h2_lo = (h2 - h2_hi.astype(F32)).astype(BF16)
    dot = functools.partial(jnp.dot, preferred_element_type=F32)
    logit = dot(h2_hi, wr_hi) + dot(h2_lo, wr_hi) + dot(h2_hi, wr_lo) + br_ref[...]
    lane = lax.broadcasted_iota(jnp.int32, (tm, LANES), 1)
    big = jnp.int32(LANES)
    is_group = (lane >= N_EXPERTS) & (lane < N_EXPERTS + N_GROUPS)
    gl = jnp.where(is_group, logit, NEG_INF)
    g_max = jnp.max(gl, axis=1, keepdims=True)
    g_idx = jnp.min(jnp.where(gl == g_max, lane, big), axis=1, keepdims=True) - N_EXPERTS
    g_prob = 1.0 / jnp.sum(jnp.where(is_group, jnp.exp(logit - g_max), 0.0), axis=1, keepdims=True)
    in_group = (lane < N_EXPERTS) & (lane // EXP_PER_GROUP == g_idx)
    el = jnp.where(in_group, logit, NEG_INF)
    v1 = jnp.max(el, axis=1, keepdims=True)
    i1 = jnp.min(jnp.where(el == v1, lane, big), axis=1, keepdims=True)
    el2 = jnp.where(lane == i1, NEG_INF, el)
    v2 = jnp.max(el2, axis=1, keepdims=True)
    i2 = jnp.min(jnp.where(el2 == v2, lane, big), axis=1, keepdims=True)
    e2 = jnp.exp(v2 - v1)
    w1 = g_prob / (1.0 + e2)
    w2 = g_prob * e2 / (1.0 + e2)
    first_low = i1 < i2
    lo = jnp.where(first_low, i1, i2) - g_idx * EXP_PER_GROUP
    hi = jnp.where(first_low, i2, i1) - g_idx * EXP_PER_GROUP
    pair = 3 * lo - ((lo * (lo - 1)) >> 1) + (hi - lo - 1)
    cls = g_idx * N_PAIRS + pair
    w_lo = jnp.where(first_low, w1, w2)
    w_hi = jnp.where(first_low, w2, w1)
    onehot = jnp.where(lane == cls, 1.0, 0.0)
    ri = lax.broadcasted_iota(jnp.int32, (tm, tm), 0)
    ci = lax.broadcasted_iota(jnp.int32, (tm, tm), 1)
    earlier = jnp.where(ci < ri, 1.0, 0.0).astype(BF16)
    before = jnp.dot(earlier, onehot.astype(BF16), preferred_element_type=F32) + carry_s[...]
    rank = jnp.sum(onehot * before, axis=1, keepdims=True)
    carry_s[...] += jnp.sum(onehot, axis=0, keepdims=True)
    counts_ref[...] = carry_s[...].astype(jnp.int32)
    route_ref[0] = jnp.where(lane == 0, cls, jnp.where(lane == 1, rank.astype(jnp.int32), 0))
    gate_bits = pltpu.bitcast(jnp.where(lane == 0, w_lo, jnp.where(lane == 1, w_hi, 0.0)), jnp.int32)
    hp_ref[0] = jnp.concatenate([_pack_bf16_pairs(h2), gate_bits], axis=1)


def _merge(xc, mvec, ys, w_merge, w_branch, w_out, w_route, b_route, n_ctx):
    b, n, d = xc.shape
    tm = _token_tile(n, 640)
    kern = functools.partial(_merge_kernel, tm=tm, n_ctx=n_ctx)
    tok = lambda w: pl.BlockSpec((1, tm, w), lambda bi, i: (bi, i, 0))
    const = lambda shape: pl.BlockSpec(shape, lambda bi, i: (0,) * len(shape))
    return pl.pallas_call(
        kern,
        grid=(b, n // tm),
        in_specs=[tok(d), pl.BlockSpec((1, MOD_ROWS, d), lambda bi, i: (bi, 0, 0)),
                  tok(MIX_W), tok(MIX_W), tok(MIX_W), tok(MIX_W),
                  const(w_merge.shape), const(w_branch.shape), const(w_out.shape),
                  const(w_route.shape), const(b_route.shape)],
        out_specs=[tok(d), tok(d // 2 + LANES), tok(LANES), const((1, LANES))],
        out_shape=[jax.ShapeDtypeStruct((b, n, d), F32),
                   jax.ShapeDtypeStruct((b, n, d // 2 + LANES), jnp.int32),
                   jax.ShapeDtypeStruct((b, n, LANES), jnp.int32),
                   jax.ShapeDtypeStruct((1, LANES), jnp.int32)],
        scratch_shapes=[pltpu.VMEM((1, LANES), F32)],
        compiler_params=_cparams("arbitrary", "arbitrary"),
        name="merge_route",
    )(xc, mvec, *ys, w_merge, w_branch, w_out, w_route, b_route)


def _row_gather(table, idx):
    n_out = idx.shape[0]
    width = table.shape[1]
    per_worker = n_out // SC_WORKERS
    assert per_worker * SC_WORKERS == n_out and per_worker % 8 == 0
    chunk = max(c for c in range(8, SC_MAX_CHUNK + 1, 8) if per_worker % c == 0)
    n_chunks = per_worker // chunk
    mesh = plsc.VectorSubcoreMesh(core_axis_name="c", subcore_axis_name="s",
                                  num_cores=SC_CORES, num_subcores=SC_SUBCORES)

    @functools.partial(
        pl.kernel, mesh=mesh,
        out_type=jax.ShapeDtypeStruct((n_out, width), table.dtype),
        scratch_types=[pltpu.VMEM((chunk,), jnp.int32),
                       pltpu.VMEM((chunk, width), table.dtype),
                       pltpu.SemaphoreType.DMA],
        name="sc_row_gather",
    )
    def gather(table_hbm, idx_hbm, out_hbm, idx_v, rows_v, sem):
        worker = lax.axis_index("s") * SC_CORES + lax.axis_index("c")
        base = worker * per_worker

        @pl.loop(0, n_chunks)
        def _(j):
            off = pl.multiple_of(base + j * chunk, 8)
            pltpu.sync_copy(idx_hbm.at[pl.ds(off, chunk)], idx_v)
            pltpu.async_copy(table_hbm.at[idx_v], rows_v, sem).wait()
            pltpu.sync_copy(rows_v, out_hbm.at[pl.ds(off, chunk)])

    return gather(table, idx)


def _ffn(h, w1_ref, w3_ref, w2_ref):
    a = jnp.dot(h, w1_ref[0, 0], preferred_element_type=F32)
    g = jnp.dot(h, w3_ref[0, 0], preferred_element_type=F32)
    return jnp.dot((_silu(a) * g).astype(BF16), w2_ref[0, 0], preferred_element_type=F32)


def _moe_kernel(elo_ref, ehi_ref, valid_ref, xs_ref, w1a_ref, w3a_ref, w2a_ref, w1b_ref, w3b_ref, w2b_ref,
                ys_ref):
    j = pl.program_id(0)
    half = ys_ref.shape[1]

    @pl.when(valid_ref[j] != 0)
    def _():
        hi, lo = _unpack_bf16_pairs(xs_ref[:, :half])
        h = jnp.concatenate([hi.astype(BF16), lo.astype(BF16)], axis=1)
        gates = pltpu.bitcast(xs_ref[:, half:], F32)
        y = (gates[:, 0:1] * _ffn(h, w1a_ref, w3a_ref, w2a_ref)
             + gates[:, 1:2] * _ffn(h, w1b_ref, w3b_ref, w2b_ref))
        ys_ref[...] = _pack_bf16_pairs(y)

    @pl.when(valid_ref[j] == 0)
    def _():
        ys_ref[...] = jnp.zeros_like(ys_ref)


def _moe_experts(xs, tile_elo, tile_ehi, tile_valid, w1, w3, w2, layer):
    n_slots, width = xs.shape
    d = w1.shape[2]
    n_tiles = n_slots // MOE_TILE
    wspec = lambda shape, which: pl.BlockSpec(
        (1, 1) + shape, lambda j, elo, ehi, valid: (layer, (elo, ehi)[which][j], 0, 0))
    grid_spec = pltpu.PrefetchScalarGridSpec(
        num_scalar_prefetch=3,
        grid=(n_tiles,),
        in_specs=[pl.BlockSpec((MOE_TILE, width), lambda j, elo, ehi, valid: (j, 0)),
                  wspec((d, D_EXPERT), 0), wspec((d, D_EXPERT), 0), wspec((D_EXPERT, d), 0),
                  wspec((d, D_EXPERT), 1), wspec((d, D_EXPERT), 1), wspec((D_EXPERT, d), 1)],
        out_specs=pl.BlockSpec((MOE_TILE, d // 2), lambda j, elo, ehi, valid: (j, 0)),
    )
    return pl.pallas_call(
        _moe_kernel,
        grid_spec=grid_spec,
        out_shape=jax.ShapeDtypeStruct((n_slots, d // 2), jnp.int32),
        compiler_params=_cparams("arbitrary"),
        name="moe_experts",
    )(tile_elo, tile_ehi, tile_valid, xs, w1, w3, w2, w1, w3, w2)


def _moe_combine_kernel(x_ref, m_ref, y_ref, o_ref, *, tm, n_ctx):
    i = pl.program_id(1)
    row = i * tm + lax.broadcasted_iota(jnp.int32, (tm, 1), 0)
    hi, lo = _unpack_bf16_pairs(y_ref[0])
    o_ref[0] = x_ref[0] + _pick(m_ref[0], row < n_ctx, 5) * jnp.concatenate([hi, lo], axis=1)


def _moe_combine(x1, mvec, yt, n_ctx):
    b, n, d = x1.shape
    tm = _token_tile(n, 1100)
    kern = functools.partial(_moe_combine_kernel, tm=tm, n_ctx=n_ctx)
    tok = lambda w: pl.BlockSpec((1, tm, w), lambda bi, i: (bi, i, 0))
    return pl.pallas_call(
        kern,
        grid=(b, n // tm),
        in_specs=[tok(d), pl.BlockSpec((1, MOD_ROWS, d), lambda bi, i: (bi, 0, 0)), tok(d // 2)],
        out_specs=tok(d),
        out_shape=jax.ShapeDtypeStruct((b, n, d), F32),
        compiler_params=_cparams("parallel", "parallel"),
        name="moe_combine",
    )(x1, mvec, yt)


def _moe(x1, mvec, hp, route, counts, w1, w3, w2, layer, n_ctx):
    b, n, d = x1.shape
    m = b * n
    n_slots = m + N_CLASSES * MOE_TILE
    cls = route[..., 0].reshape(m)
    rank = route[..., 1].reshape(m)
    cnt = counts[0, :N_CLASSES]
    padded = (cnt + MOE_TILE - 1) // MOE_TILE * MOE_TILE
    ends = jnp.cumsum(padded)
    pos = (ends - padded)[cls] + rank
    src = jnp.zeros((n_slots,), jnp.int32).at[pos].set(jnp.arange(m, dtype=jnp.int32))
    tile_start = jnp.arange(n_slots // MOE_TILE, dtype=jnp.int32) * MOE_TILE
    tile_cls = jnp.minimum(jnp.searchsorted(ends, tile_start, side="right"), N_CLASSES - 1).astype(jnp.int32)
    tile_valid = (tile_start < ends[-1]).astype(jnp.int32)
    pair_lo = jnp.asarray([0, 0, 0, 1, 1, 2], jnp.int32)
    pair_hi = jnp.asarray([1, 2, 3, 2, 3, 3], jnp.int32)
    group = tile_cls // N_PAIRS
    tile_elo = group * EXP_PER_GROUP + pair_lo[tile_cls % N_PAIRS]
    tile_ehi = group * EXP_PER_GROUP + pair_hi[tile_cls % N_PAIRS]

    xs = _row_gather(hp.reshape(m, hp.shape[-1]), src)
    ys = _moe_experts(xs, tile_elo, tile_ehi, tile_valid, w1, w3, w2, layer)
    yt = _row_gather(ys, pos)
    return _moe_combine(x1, mvec, yt.reshape(b, n, d // 2), n_ctx)


def _final_kernel(x_ref, w_ref, o_ref):
    x = x_ref[0]
    ms = jnp.mean(x * x, axis=-1, keepdims=True)
    o_ref[0] = x * lax.rsqrt(ms + EPS) * w_ref[...]


def _final_norm(xc, w, n_ctx):
    b, n, d = xc.shape
    tm = _token_tile(n_ctx, 1024)
    skip = n_ctx // tm
    return pl.pallas_call(
        _final_kernel,
        grid=(b, (n - n_ctx) // tm),
        in_specs=[pl.BlockSpec((1, tm, d), lambda bi, i: (bi, i + skip, 0)),
                  pl.BlockSpec((1, d), lambda bi, i: (0, 0))],
        out_specs=pl.BlockSpec((1, tm, d), lambda bi, i: (bi, i, 0)),
        out_shape=jax.ShapeDtypeStruct((b, n - n_ctx, d), F32),
        compiler_params=_cparams("parallel", "parallel"),
        name="final_norm",
    )(xc, w.reshape(1, d))


def _rope_tables(n_lat, n_ctx):
    rows = n_lat // GRID_W
    row = jnp.repeat(jnp.arange(rows), GRID_W).astype(F32)
    col = jnp.tile(jnp.arange(GRID_W), rows).astype(F32)
    nq = HEAD_DIM // 4
    inv = jnp.power(ROPE_BASE, -jnp.arange(nq, dtype=F32) / nq)
    ang = jnp.concatenate([row[:, None] * inv, col[:, None] * inv], -1)
    cos = jnp.cos(ang)
    sin = jnp.sin(ang)
    cos_h = jnp.concatenate([cos, cos], -1)
    sin_h = jnp.concatenate([-sin, sin], -1)
    cos_full = jnp.concatenate([jnp.ones((n_ctx, HEAD_DIM), F32), cos_h], 0)
    sin_full = jnp.concatenate([jnp.zeros((n_ctx, HEAD_DIM), F32), sin_h], 0)
    return jnp.tile(cos_full, (1, N_HEADS)), jnp.tile(sin_full, (1, N_HEADS))


def _block_diag_heads(w):
    out = jnp.zeros((MIX_W, MIX_W), w.dtype)
    for h in range(N_HEADS):
        out = out.at[h * HEAD_DIM:(h + 1) * HEAD_DIM, h * HEAD_DIM:(h + 1) * HEAD_DIM].set(w[h])
    return out


def kernel(x, c, ctx, c_ctx, w_mod, b_mod, w_in, a_conv_w, a_conv_b, a_gate_w, a_gate_b, a_lambda, b_theta,
           c_conv_w, c_conv_b, c_gate_b, d_lb, w_branch, w_out, moe_w_group, moe_b_group, moe_w_router,
           moe_b_router, moe_w1, moe_w3, moe_w2, final_norm_w):
    bsz, n_lat, d = x.shape
    n_ctx = ctx.shape[1]
    depth = w_mod.shape[0]
    assert n_ctx % CHUNK == 0 and n_lat % CHUNK == 0 and n_ctx % 256 == 0 and n_lat % 256 == 0

    xc = jnp.concatenate([ctx, x], axis=1)
    cos, sin = _rope_tables(n_lat, n_ctx)

    cc = jnp.zeros((8, d), F32).at[:bsz].set(c).at[bsz].set(c_ctx)
    mod = _mod_vectors(cc, w_mod, b_mod)
    mx = mod[:, :bsz].reshape(depth, bsz, N_MOD, d)
    mc = jnp.broadcast_to(mod[:, bsz].reshape(depth, 1, N_MOD, d), (depth, bsz, N_MOD, d))
    pad = jnp.zeros((depth, bsz, 8 - N_MOD, d), F32)
    mvec = jnp.concatenate([mx, pad, mc, pad], axis=2)

    lbs = jnp.cumsum(jax.nn.softmax(d_lb.astype(F32), axis=0), axis=0)
    lbs = lbs - lbs[0]

    n_mix_cols = 15 * MIX_W
    gate0 = 10 * MIX_W
    n_gate = 4 * N_HEADS
    w1b = moe_w1.astype(BF16)
    w3b = moe_w3.astype(BF16)
    w2b = moe_w2.astype(BF16)

    for l in range(depth):
        wl = w_in[l]
        w_mix = jnp.concatenate([wl[:, :gate0], wl[:, gate0 + n_gate:n_mix_cols + n_gate]], axis=1).astype(BF16)
        w_gate = jnp.pad(wl[:, gate0:gate0 + n_gate], ((0, 0), (0, LANES - n_gate))).astype(BF16)
        w_merge = wl[:, n_mix_cols + n_gate:].astype(BF16)
        p, gates_c = _in_proj(xc, mvec[l], w_mix, w_gate, cos, sin, n_ctx)

        gw = jnp.concatenate([_block_diag_heads(a_gate_w[l, dd, j]) for dd in range(2) for j in range(2)],
                             axis=1).astype(BF16)
        gb = a_gate_b[l].reshape(1, 4 * MIX_W)
        ya = _mixer_a(p, a_conv_w[l], a_conv_b[l].reshape(1, MIX_W), gw, gb, a_lambda[l], n_ctx)

        log_gamma = jnp.repeat(jax.nn.log_sigmoid(b_theta[l].astype(F32)), HEAD_DIM, axis=1)
        yb = _mixer_b(p, log_gamma, n_ctx)

        gate_b = jnp.pad(c_gate_b[l].reshape(1, n_gate), ((0, 0), (0, LANES - n_gate)))
        yc = _mixer_c(p, gates_c, c_conv_w[l], c_conv_b[l].reshape(1, 2 * MIX_W), gate_b, n_ctx)

        yd = _mixer_d(p, lbs[l].reshape(1, MIX_W), n_ctx)

        w_route = jnp.pad(jnp.concatenate([moe_w_router[l], moe_w_group[l]], axis=1),
                          ((0, 0), (0, LANES - N_EXPERTS - N_GROUPS)))
        b_route = jnp.pad(jnp.concatenate([moe_b_router[l], moe_b_group[l]]),
                          (0, LANES - N_EXPERTS - N_GROUPS)).reshape(1, LANES)
        x1, hp, route, counts = _merge(xc, mvec[l], (ya, yb, yc, yd), w_merge, w_branch[l].astype(BF16),
                                       w_out[l].astype(BF16), w_route, b_route, n_ctx)
        xc = _moe(x1, mvec[l], hp, route, counts, w1b, w3b, w2b, l, n_ctx)

    return _final_norm(xc, final_norm_w, n_ctx)
```

```python
import functools

import jax
import jax.numpy as jnp
import numpy as np
from jax import lax
from jax.experimental import pallas as pl
from jax.experimental.pallas import tpu as pltpu
from jax.experimental.pallas import tpu_sc as plsc

F32 = jnp.float32
BF16 = jnp.bfloat16

EPS = 1e-6
N_HEADS = 4
HEAD_DIM = 64
MIX_W = N_HEADS * HEAD_DIM
N_BRANCH = 4
CONV_W = 4
LRU_C = 8.0
GRID_W = 64
ROPE_BASE = 10000.0
N_GROUPS = 4
EXP_PER_GROUP = 4
N_EXPERTS = N_GROUPS * EXP_PER_GROUP
D_EXPERT = 512
N_MOD = 6
M_INIT = -1e30

CHUNK = 128
SUB = 16
HALO = 16
LANES = 128
MOD_ROWS = 16
VMEM_LIMIT_BYTES = 56 * 1024 * 1024
N_PAIRS = 6
N_CLASSES = N_GROUPS * N_PAIRS
MOE_TILE = 256
SC_CORES = 2
SC_SUBCORES = 16
SC_WORKERS = SC_CORES * SC_SUBCORES
SC_MAX_CHUNK = 32
NEG_INF = float("-inf")
LOG2_E = 1.4426950408889634


def _cparams(*sem):
    return pltpu.CompilerParams(dimension_semantics=sem, vmem_limit_bytes=VMEM_LIMIT_BYTES)


def _token_tile(n, cap):
    best = 16
    for t in range(16, cap + 1, 16):
        if n % t == 0:
            best = t
    return best


def _modulate(x, shift, scale):
    ms = jnp.mean(x * x, axis=-1, keepdims=True)
    return x * lax.rsqrt(ms + EPS) * (1.0 + scale) + shift


def _pick(m, is_ctx, k):
    return jnp.where(is_ctx, m[8 + k:9 + k], m[k:k + 1])


def _sigmoid(x):
    return 0.5 * jnp.tanh(0.5 * x) + 0.5


def _silu(x):
    return x * _sigmoid(x)


def _log_sigmoid(x):
    return jnp.minimum(x, 0.0) - jnp.log(1.0 + jnp.exp(-jnp.abs(x)))


def _split3(x):
    hi = x.astype(BF16)
    r = x - hi.astype(F32)
    mid = r.astype(BF16)
    lo = (r - mid.astype(F32)).astype(BF16)
    return hi, mid, lo


def _sel_dot(sel, x):
    hi, mid, lo = _split3(x)
    d = functools.partial(jnp.dot, preferred_element_type=F32)
    return d(sel, hi) + d(sel, mid) + d(sel, lo)


def _dot_sel(x, sel):
    hi, mid, lo = _split3(x)
    d = functools.partial(jnp.dot, preferred_element_type=F32)
    return d(hi, sel) + d(mid, sel) + d(lo, sel)


def _dot_nt(a, b):
    return lax.dot_general(a, b, (((1,), (1,)), ((), ())), preferred_element_type=F32)


def _dot_tn(a, b):
    return lax.dot_general(a, b, (((0,), (0,)), ((), ())), preferred_element_type=F32)


def _head_of_lane(width):
    return lax.broadcasted_iota(jnp.int32, (1, width), 1) // HEAD_DIM


def _block_ones(width):
    r = lax.broadcasted_iota(jnp.int32, (width, width), 0) // HEAD_DIM
    c = lax.broadcasted_iota(jnp.int32, (width, width), 1) // HEAD_DIM
    return r == c


def _head_norm(o, center):
    ones = jnp.where(_block_ones(MIX_W), 1.0, 0.0).astype(BF16)
    inv = 1.0 / HEAD_DIM
    if center:
        o = o - _dot_sel(o, ones) * inv
    var = _dot_sel(o * o, ones) * inv
    return o * lax.rsqrt(var + EPS)


def _bwd_chunk(s, n_ctx_chunks, n_chunks):
    return jnp.where(s < n_ctx_chunks, n_ctx_chunks - 1 - s, n_chunks - 1 + n_ctx_chunks - s)


def _conv_chunk(ref, c, w, b, *, n, n_ctx, width0=0, width=None):
    L = CHUNK
    wl = L + 2 * HALO
    r0 = c * L
    start = pl.multiple_of(jnp.clip(r0 - HALO, 0, n - wl), HALO)
    off = r0 - start
    if width is None:
        win = ref[0, pl.ds(start, wl), :]
    else:
        win = ref[0, pl.ds(start, wl), width0:width0 + width]
    ri = lax.broadcasted_iota(jnp.int32, (L, wl), 0)
    mi = lax.broadcasted_iota(jnp.int32, (L, wl), 1)
    g = r0 + ri
    y = b
    seg_g = jnp.where(g < n_ctx, 0, 1)
    for k in range(CONV_W):
        src = g + (k - 2)
        hit = (mi == ri + off + (k - 2)) & (jnp.where(src < n_ctx, 0, 1) == seg_g) & (src >= 0) & (src < n)
        sel = jnp.where(hit, 1.0, 0.0).astype(BF16)
        y = y + w[k:k + 1] * jnp.dot(sel, win, preferred_element_type=F32)
    return y


def _mod_kernel(cc_ref, w_ref, b_ref, o_ref):
    s = _silu(cc_ref[...])
    o_ref[0] = jnp.dot(s, w_ref[0], precision=lax.Precision.HIGHEST, preferred_element_type=F32) + b_ref[0]


def _mod_vectors(cc, w_mod, b_mod):
    depth, d, dm = w_mod.shape
    tn = 1536
    return pl.pallas_call(
        _mod_kernel,
        grid=(depth, dm // tn),
        in_specs=[pl.BlockSpec((8, d), lambda l, j: (0, 0)),
                  pl.BlockSpec((1, d, tn), lambda l, j: (l, 0, j)),
                  pl.BlockSpec((1, 1, tn), lambda l, j: (l, 0, j))],
        out_specs=pl.BlockSpec((1, 8, tn), lambda l, j: (l, 0, j)),
        out_shape=jax.ShapeDtypeStruct((depth, 8, dm), F32),
        compiler_params=_cparams("parallel", "parallel"),
        name="mod_vectors",
    )(cc, w_mod, b_mod.reshape(depth, 1, dm))


def _rope(t, cos, sin):
    lane = lax.broadcasted_iota(jnp.int32, (1, MIX_W), 1) % HEAD_DIM
    half = HEAD_DIM // 2
    swapped = jnp.where(lane < half, pltpu.roll(t, MIX_W - half, 1), pltpu.roll(t, half, 1))
    return t * cos + swapped * sin


RET_Q_COL = 2 * MIX_W
RET_K_COL = 3 * MIX_W


def _in_proj_kernel(x_ref, m_ref, w_ref, wg_ref, cos_ref, sin_ref, p_ref, g_ref, *, tm, n_ctx, col_chunk):
    i = pl.program_id(1)
    row = i * tm + lax.broadcasted_iota(jnp.int32, (tm, 1), 0)
    is_ctx = row < n_ctx
    m = m_ref[0]
    h = _modulate(x_ref[0], _pick(m, is_ctx, 0), _pick(m, is_ctx, 1)).astype(BF16)
    for j in range(w_ref.shape[1] // col_chunk):
        sl = slice(j * col_chunk, (j + 1) * col_chunk)
        t = jnp.dot(h, w_ref[:, sl], preferred_element_type=F32)
        if j * col_chunk == RET_Q_COL:
            t = _rope(t, cos_ref[...], sin_ref[...])
        elif j * col_chunk == RET_K_COL:
            t = _rope(t, cos_ref[...], sin_ref[...]) * HEAD_DIM ** -0.5
        p_ref[0, :, sl] = t.astype(BF16)
    g_ref[0] = jnp.dot(h, wg_ref[...], preferred_element_type=F32)


def _in_proj(xc, mvec, w_mix, w_gate, cos, sin, n_ctx):
    b, n, d = xc.shape
    tm = _token_tile(n, 640)
    wc = w_mix.shape[1]
    kern = functools.partial(_in_proj_kernel, tm=tm, n_ctx=n_ctx, col_chunk=MIX_W)
    return pl.pallas_call(
        kern,
        grid=(b, n // tm),
        in_specs=[pl.BlockSpec((1, tm, d), lambda bi, i: (bi, i, 0)),
                  pl.BlockSpec((1, MOD_ROWS, d), lambda bi, i: (bi, 0, 0)),
                  pl.BlockSpec((d, wc), lambda bi, i: (0, 0)),
                  pl.BlockSpec((d, LANES), lambda bi, i: (0, 0)),
                  pl.BlockSpec((tm, MIX_W), lambda bi, i: (i, 0)),
                  pl.BlockSpec((tm, MIX_W), lambda bi, i: (i, 0))],
        out_specs=[pl.BlockSpec((1, tm, wc), lambda bi, i: (bi, i, 0)),
                   pl.BlockSpec((1, tm, LANES), lambda bi, i: (bi, i, 0))],
        out_shape=[jax.ShapeDtypeStruct((b, n, wc), BF16),
                   jax.ShapeDtypeStruct((b, n, LANES), F32)],
        compiler_params=_cparams("parallel", "parallel"),
        name="in_proj",
    )(xc, mvec, w_mix, w_gate, cos, sin)


def _lin_scan(a, x, reverse):
    L = a.shape[0]
    row = lax.broadcasted_iota(jnp.int32, (L, 1), 0)
    k = 1
    while k < L:
        if reverse:
            a_s = pltpu.roll(a, L - k, 0)
            x_s = pltpu.roll(x, L - k, 0)
            valid = row < L - k
        else:
            a_s = pltpu.roll(a, k, 0)
            x_s = pltpu.roll(x, k, 0)
            valid = row >= k
        x = jnp.where(valid, a * x_s + x, x)
        a = jnp.where(valid, a * a_s, a)
        k *= 2
    return a, x


def _mixer_a_kernel(ax_ref, ag_ref, cw_ref, cb_ref, gw_ref, gb_ref, lam_ref, out_ref, u_s, hf_s,
                    *, n, n_ctx):
    L = CHUNK
    n_chunks = n // L
    n_ctx_chunks = n_ctx // L
    cw = cw_ref[...]
    cb = cb_ref[...]
    log_lam = _log_sigmoid(lam_ref[...])

    def direction(d, c, carry):
        r0 = pl.multiple_of(c * L, L)
        u = u_s[pl.ds(r0, L), :]
        pre = jnp.dot(u.astype(BF16), gw_ref[:, d * 2 * MIX_W:(d + 1) * 2 * MIX_W],
                      preferred_element_type=F32) + gb_ref[:, d * 2 * MIX_W:(d + 1) * 2 * MIX_W]
        r = _sigmoid(pre[:, :MIX_W])
        gi = _sigmoid(pre[:, MIX_W:])
        log_a = LRU_C * r * log_lam[d:d + 1]
        a = jnp.exp(log_a)
        inp = jnp.sqrt(1.0 - jnp.exp(2.0 * log_a)) * (gi * u)
        a_cum, h = _lin_scan(a, inp, reverse=(d == 1))
        h = h + a_cum * carry
        new_carry = h[0:1] if d == 1 else h[L - 1:L]
        return r0, h, new_carry

    def conv_body(c, _):
        r0 = pl.multiple_of(c * L, L)
        u_s[pl.ds(r0, L), :] = _conv_chunk(ax_ref, c, cw, cb, n=n, n_ctx=n_ctx)
        return 0

    lax.fori_loop(0, n_chunks, conv_body, 0)

    def walk_body(s, carries):
        r0, h, carry_f = direction(0, s, carries[0])
        hf_s[0, pl.ds(r0, L), :] = h
        r0, h, carry_b = direction(1, _bwd_chunk(s, n_ctx_chunks, n_chunks), carries[1])
        hf_s[1, pl.ds(r0, L), :] = h
        return carry_f, carry_b

    zero = jnp.zeros((1, MIX_W), F32)
    lax.fori_loop(0, n_chunks, walk_body, (zero, zero))

    def out_body(c, _):
        rows = pl.ds(pl.multiple_of(c * L, L), L)
        gate = jax.nn.gelu(ag_ref[0, rows, :].astype(F32), approximate=True)
        out_ref[0, rows, :] = (gate * (hf_s[0, rows, :] + hf_s[1, rows, :])).astype(out_ref.dtype)
        return 0

    lax.fori_loop(0, n_chunks, out_body, 0)


def _seq_spec(n, col):
    return pl.BlockSpec((1, n, MIX_W), lambda b: (b, 0, col))


def _full_spec(shape):
    return pl.BlockSpec(shape, lambda b: (0,) * len(shape))


def _mixer_a(p, conv_w, conv_b, gate_w, gate_b, lam, n_ctx):
    b, n, _ = p.shape
    kern = functools.partial(_mixer_a_kernel, n=n, n_ctx=n_ctx)
    return pl.pallas_call(
        kern,
        grid=(b,),
        in_specs=[_seq_spec(n, 0), _seq_spec(n, 1),
                  _full_spec(conv_w.shape), _full_spec(conv_b.shape),
                  _full_spec(gate_w.shape), _full_spec(gate_b.shape), _full_spec(lam.shape)],
        out_specs=pl.BlockSpec((1, n, MIX_W), lambda bi: (bi, 0, 0)),
        out_shape=jax.ShapeDtypeStruct((b, n, MIX_W), BF16),
        scratch_shapes=[pltpu.VMEM((n, MIX_W), F32), pltpu.VMEM((2, n, MIX_W), F32)],
        compiler_params=_cparams("parallel"),
        name="mixer_rglru",
    )(p, p, conv_w, conv_b, gate_w, gate_b, lam)


def _mixer_b_kernel(q_ref, k_ref, v_ref, g_ref, lg_ref, out_ref, oi_s, dec_s, st_s, *, n, n_ctx):
    L = CHUNK
    n_chunks = n // L
    n_ctx_chunks = n_ctx // L
    head = _head_of_lane(MIX_W)
    bd = _block_ones(MIX_W)
    lgf = lg_ref[0:1, :]
    lgb = lg_ref[1:2, :]
    pos = lax.broadcasted_iota(jnp.int32, (L, 1), 0).astype(F32)
    diff = (lax.broadcasted_iota(jnp.int32, (L, L), 0) - lax.broadcasted_iota(jnp.int32, (L, L), 1)).astype(F32)
    for h in range(N_HEADS):
        lf = lgf[:, h * HEAD_DIM:h * HEAD_DIM + 1]
        lb = lgb[:, h * HEAD_DIM:h * HEAD_DIM + 1]
        dec_s[h] = jnp.where(diff >= 0, jnp.exp(lf * jnp.maximum(diff, 0.0)), jnp.exp(lb * jnp.maximum(-diff, 0.0)))

    qw = (jnp.exp(lgf * (pos + 1.0)), jnp.exp(lgb * (float(L) - pos)))
    kw = (jnp.exp(lgf * (float(L) - 1.0 - pos)), jnp.exp(lgb * pos))
    g_chunk = (jnp.exp(lgf * float(L)), jnp.exp(lgb * float(L)))

    def state_step(d, c):
        rows = pl.ds(pl.multiple_of(c * L, L), L)
        q = q_ref[0, rows, :].astype(F32)
        k = k_ref[0, rows, :].astype(F32)
        st = st_s[d]
        oi_s[d, rows, :] = jnp.dot((q * qw[d]).astype(BF16), st.astype(BF16), preferred_element_type=F32)
        kv = _dot_tn((k * kw[d]).astype(BF16), v_ref[0, rows, :])
        st_s[d] = g_chunk[d] * st + jnp.where(bd, kv, 0.0)

    st_s[...] = jnp.zeros_like(st_s)

    def walk_body(s, _):
        state_step(0, s)
        state_step(1, _bwd_chunk(s, n_ctx_chunks, n_chunks))
        return 0

    lax.fori_loop(0, n_chunks, walk_body, 0)

    def out_body(c, _):
        rows = pl.ds(pl.multiple_of(c * L, L), L)
        q = q_ref[0, rows, :]
        k = k_ref[0, rows, :]
        v = v_ref[0, rows, :]
        o = oi_s[0, rows, :] + oi_s[1, rows, :]
        for h in range(N_HEADS):
            hm = head == h
            sc = _dot_nt(jnp.where(hm, q, jnp.zeros_like(q)), k) * dec_s[h]
            o = o + jnp.where(hm, jnp.dot(sc.astype(BF16), v, preferred_element_type=F32), 0.0)
        gate = _silu(g_ref[0, rows, :].astype(F32))
        out_ref[0, rows, :] = (_head_norm(o, True) * gate).astype(out_ref.dtype)
        return 0

    lax.fori_loop(0, n_chunks, out_body, 0, unroll=2)


def _mixer_b(p, log_gamma, n_ctx):
    b, n, _ = p.shape
    kern = functools.partial(_mixer_b_kernel, n=n, n_ctx=n_ctx)
    return pl.pallas_call(
        kern,
        grid=(b,),
        in_specs=[_seq_spec(n, 2), _seq_spec(n, 3), _seq_spec(n, 4), _seq_spec(n, 5),
                  _full_spec(log_gamma.shape)],
        out_specs=pl.BlockSpec((1, n, MIX_W), lambda bi: (bi, 0, 0)),
        out_shape=jax.ShapeDtypeStruct((b, n, MIX_W), BF16),
        scratch_shapes=[pltpu.VMEM((2, n, MIX_W), F32),
                        pltpu.VMEM((N_HEADS, CHUNK, CHUNK), F32),
                        pltpu.VMEM((2, MIX_W, MIX_W), F32)],
        compiler_params=_cparams("parallel"),
        name="mixer_retention",
    )(p, p, p, p, log_gamma)


def _mixer_c_kernel(q_ref, k_ref, v_ref, o_ref, g_ref, cw_ref, cb_ref, gb_ref, out_ref,
                    qs_s, ks_s, hf_s, c_s, n_s, m_s, *, n, n_ctx):
    L = CHUNK
    n_chunks = n // L
    n_ctx_chunks = n_ctx // L
    head = _head_of_lane(MIX_W)
    bd = _block_ones(MIX_W)
    cw = cw_ref[...]
    cb = cb_ref[...]
    ri = lax.broadcasted_iota(jnp.int32, (L, L), 0)
    ci = lax.broadcasted_iota(jnp.int32, (L, L), 1)
    gl = lax.broadcasted_iota(jnp.int32, (LANES, MIX_W), 0)
    hl = lax.broadcasted_iota(jnp.int32, (LANES, MIX_W), 1) // HEAD_DIM

    def expand(kind):
        return jnp.where(gl == kind * N_HEADS + hl, 1.0, 0.0).astype(BF16)

    def conv_body(c, _):
        rows = pl.ds(pl.multiple_of(c * L, L), L)
        qs_s[rows, :] = _silu(_conv_chunk(q_ref, c, cw[:, :MIX_W], cb[:, :MIX_W], n=n, n_ctx=n_ctx)).astype(BF16)
        ks_s[rows, :] = (_silu(_conv_chunk(k_ref, c, cw[:, MIX_W:], cb[:, MIX_W:], n=n, n_ctx=n_ctx))
                         * HEAD_DIM ** -0.5).astype(BF16)
        return 0

    lax.fori_loop(0, n_chunks, conv_body, 0)

    def chunk(d, c):
        rev = d == 1
        rows = pl.ds(pl.multiple_of(c * L, L), L)
        q = qs_s[rows, :]
        k = ks_s[rows, :]
        v = v_ref[0, rows, :]
        g = g_ref[0, rows, :] + gb_ref[...]
        log_i = _dot_sel(g, expand(2 * d))
        log_f = _dot_sel(_log_sigmoid(g), expand(2 * d + 1))
        causal = (ci >= ri) if rev else (ci <= ri)
        tri = jnp.where(causal, 1.0, 0.0).astype(BF16)
        cum = _sel_dot(tri, log_f)
        cum_end = cum[0:1] if rev else cum[L - 1:L]
        m_prev = m_s[d]
        c_prev = c_s[d]
        n_prev = n_s[d]
        row_src = (log_i - cum).T
        m_inter = cum + m_prev
        num_inter = jnp.dot(q, c_prev.astype(BF16), preferred_element_type=F32)
        qn = q.astype(F32) * n_prev
        kf = k
        hid = jnp.zeros((L, MIX_W), F32)
        for h in range(N_HEADS):
            hm = head == h
            lane0 = h * HEAD_DIM
            log_d = jnp.where(causal, cum[:, lane0:lane0 + 1] + row_src[lane0:lane0 + 1, :], NEG_INF)
            m_q = jnp.maximum(jnp.max(log_d, axis=1, keepdims=True), m_inter[:, lane0:lane0 + 1])
            sc = _dot_nt(jnp.where(hm, q, jnp.zeros_like(q)), kf) * jnp.exp(log_d - m_q)
            s_inter = jnp.exp(m_inter[:, lane0:lane0 + 1] - m_q)
            num = jnp.dot(sc.astype(BF16), v, preferred_element_type=F32) + s_inter * num_inter
            den = (jnp.sum(sc, axis=1, keepdims=True)
                   + s_inter * jnp.sum(jnp.where(hm, qn, 0.0), axis=1, keepdims=True))
            hid = hid + jnp.where(hm, num / jnp.maximum(jnp.abs(den), jnp.exp(-m_q)), 0.0)
        log_w = cum_end - cum + log_i
        m_loc = jnp.max(log_w, axis=0, keepdims=True)
        kw = k.astype(F32) * jnp.exp(log_w - m_loc)
        c_loc = jnp.where(bd, _dot_tn(kw.astype(BF16), v), 0.0)
        n_loc = jnp.sum(kw, axis=0, keepdims=True)
        m_new = jnp.maximum(cum_end + m_prev, m_loc)
        s_old = jnp.exp(cum_end + m_prev - m_new)
        s_loc = jnp.exp(m_loc - m_new)
        c_s[d] = s_old * c_prev + s_loc * c_loc
        n_s[d] = s_old * n_prev + s_loc * n_loc
        m_s[d] = m_new
        return rows, hid

    c_s[...] = jnp.zeros_like(c_s)
    n_s[...] = jnp.zeros_like(n_s)
    m_s[...] = jnp.full(m_s.shape, M_INIT, F32)

    def walk_body(s, _):
        rows, hid = chunk(0, s)
        hf_s[0, rows, :] = hid
        rows, hid = chunk(1, _bwd_chunk(s, n_ctx_chunks, n_chunks))
        hf_s[1, rows, :] = hid
        return 0

    lax.fori_loop(0, n_chunks, walk_body, 0)

    def out_body(c, _):
        rows = pl.ds(pl.multiple_of(c * L, L), L)
        gate = _sigmoid(o_ref[0, rows, :].astype(F32))
        out_ref[0, rows, :] = (gate * _head_norm(hf_s[0, rows, :] + hf_s[1, rows, :], True)).astype(out_ref.dtype)
        return 0

    lax.fori_loop(0, n_chunks, out_body, 0)


def _mixer_c(p, gates, conv_w, conv_b, gate_b, n_ctx):
    b, n, _ = p.shape
    kern = functools.partial(_mixer_c_kernel, n=n, n_ctx=n_ctx)
    return pl.pallas_call(
        kern,
        grid=(b,),
        in_specs=[_seq_spec(n, 6), _seq_spec(n, 7), _seq_spec(n, 8), _seq_spec(n, 9),
                  pl.BlockSpec((1, n, LANES), lambda bi: (bi, 0, 0)),
                  _full_spec(conv_w.shape), _full_spec(conv_b.shape), _full_spec(gate_b.shape)],
        out_specs=pl.BlockSpec((1, n, MIX_W), lambda bi: (bi, 0, 0)),
        out_shape=jax.ShapeDtypeStruct((b, n, MIX_W), BF16),
        scratch_shapes=[pltpu.VMEM((n, MIX_W), BF16), pltpu.VMEM((n, MIX_W), BF16),
                        pltpu.VMEM((2, n, MIX_W), F32),
                        pltpu.VMEM((2, MIX_W, MIX_W), F32),
                        pltpu.VMEM((2, 1, MIX_W), F32),
                        pltpu.VMEM((2, 1, MIX_W), F32)],
        compiler_params=_cparams("parallel"),
        name="mixer_mlstm",
    )(p, p, p, p, gates, conv_w, conv_b, gate_b)


def _hgrn2_tables():
    L, S = CHUNK, SUB
    r = np.arange(L)
    same = (r[:, None] // S) == (r[None, :] // S)
    tri = np.stack([same & (r[None, :] <= r[:, None]), same & (r[None, :] >= r[:, None])])
    pair = np.arange(L * S)
    tok_i = (pair // (S * S)) * S + (pair // S) % S
    pi, pj = (pair // S) % S, pair % S
    fold = np.stack([(r[:, None] == tok_i[None, :]) & (pj <= pi)[None, :],
                     (r[:, None] == tok_i[None, :]) & (pj >= pi)[None, :]])
    as_bf16 = lambda a: jnp.asarray(a.astype(np.float32), dtype=BF16)
    return as_bf16(tri), as_bf16(same), as_bf16(fold)


def _mixer_d_kernel(q_ref, ff_ref, fb_ref, v_ref, g_ref, lb_ref, tri_ref, same_ref, fold_ref, out_ref,
                    o_s, st_s, *, n, n_ctx):
    L = CHUNK
    S = SUB
    nb = L // S
    n_chunks = n // L
    n_ctx_chunks = n_ctx // L
    lb = lb_ref[...]
    bd = _block_ones(MIX_W)
    ones_bd = jnp.where(bd, 1.0, 0.0).astype(BF16)

    def chunk(d, c):
        rev = d == 1
        rows = pl.ds(pl.multiple_of(c * L, L), L)
        q = _silu(q_ref[0, rows, :].astype(F32))
        raw = (fb_ref if rev else ff_ref)[0, rows, :].astype(F32)
        v = v_ref[0, rows, :]
        vf = v.astype(F32)
        f = lb + (1.0 - lb) * _sigmoid(raw)
        k = 1.0 - f
        log_f = jnp.log(f)
        cum = _sel_dot(tri_ref[d], log_f)
        tot = _sel_dot(same_ref[...], log_f)
        qt = (q * jnp.exp(cum)).astype(BF16)
        kt = (k * jnp.exp(tot - cum)).astype(BF16)
        g = jnp.exp(tot)
        prods, v_js = [], []
        cum2 = cum * LOG2_E
        for a in range(nb):
            sl = slice(a * S, (a + 1) * S)
            cb, qb, kb = cum2[sl], q[sl], k[sl]
            cum_i = jnp.concatenate([jnp.broadcast_to(cb[i:i + 1], (S, MIX_W)) for i in range(S)], axis=0)
            q_i = jnp.concatenate([jnp.broadcast_to(qb[i:i + 1], (S, MIX_W)) for i in range(S)], axis=0)
            cum_j = jnp.concatenate([cb] * S, axis=0)
            k_j = jnp.concatenate([kb] * S, axis=0)
            prods.append((q_i * k_j * jnp.exp2(jnp.minimum(cum_i - cum_j, 0.0))).astype(BF16))
            v_js.append(jnp.concatenate([vf[sl]] * S, axis=0))
        scores = jnp.dot(jnp.concatenate(prods, axis=0), ones_bd, preferred_element_type=F32)
        weighted = (scores * jnp.concatenate(v_js, axis=0)).astype(BF16)
        o = jnp.dot(fold_ref[d], weighted, preferred_element_type=F32)
        st = st_s[d]
        inter = [None] * nb
        for a in (range(nb - 1, -1, -1) if rev else range(nb)):
            sl = slice(a * S, (a + 1) * S)
            inter[a] = _dot_nt(qt[sl], st.astype(BF16))
            st = st * g[a * S:a * S + 1] + jnp.where(bd, _dot_tn(v[sl], kt[sl]), 0.0)
        st_s[d] = st
        o_s[d, rows, :] = o + jnp.concatenate(inter, axis=0)

    st_s[...] = jnp.zeros_like(st_s)

    def walk_body(s, _):
        chunk(0, s)
        chunk(1, _bwd_chunk(s, n_ctx_chunks, n_chunks))
        return 0

    lax.fori_loop(0, n_chunks, walk_body, 0)

    def out_body(c, _):
        rows = pl.ds(pl.multiple_of(c * L, L), L)
        gate = _silu(g_ref[0, rows, :].astype(F32))
        out_ref[0, rows, :] = (_head_norm(o_s[0, rows, :] + o_s[1, rows, :], False) * gate).astype(out_ref.dtype)
        return 0

    lax.fori_loop(0, n_chunks, out_body, 0)


def _mixer_d(p, lb, n_ctx):
    b, n, _ = p.shape
    tri, same, fold = _hgrn2_tables()
    kern = functools.partial(_mixer_d_kernel, n=n, n_ctx=n_ctx)
    return pl.pallas_call(
        kern,
        grid=(b,),
        in_specs=[_seq_spec(n, 10), _seq_spec(n, 11), _seq_spec(n, 12), _seq_spec(n, 13), _seq_spec(n, 14),
                  _full_spec(lb.shape), _full_spec(tri.shape), _full_spec(same.shape), _full_spec(fold.shape)],
        out_specs=pl.BlockSpec((1, n, MIX_W), lambda bi: (bi, 0, 0)),
        out_shape=jax.ShapeDtypeStruct((b, n, MIX_W), BF16),
        scratch_shapes=[pltpu.VMEM((2, n, MIX_W), F32), pltpu.VMEM((2, MIX_W, MIX_W), F32)],
        compiler_params=_cparams("parallel"),
        name="mixer_hgrn2",
    )(p, p, p, p, p, lb, tri, same, fold)


def _pack_bf16_pairs(t):
    w = t.shape[1] // 2
    hi = pltpu.bitcast(t[:, :w].astype(BF16).astype(F32), jnp.uint32)
    lo = pltpu.bitcast(t[:, w:].astype(BF16).astype(F32), jnp.uint32)
    return pltpu.bitcast(hi | (lo >> 16), jnp.int32)


def _unpack_bf16_pairs(p):
    u = pltpu.bitcast(p, jnp.uint32)
    hi = pltpu.bitcast(u & jnp.uint32(0xFFFF0000), F32)
    lo = pltpu.bitcast(u << 16, F32)
    return hi, lo


def _merge_kernel(x_ref, m_ref, ya_ref, yb_ref, yc_ref, yd_ref, wm_ref, wb_ref, wo_ref, wr_ref, br_ref,
                  xo_ref, hp_ref, route_ref, counts_ref, carry_s, *, tm, n_ctx):
    i = pl.program_id(1)

    @pl.when((pl.program_id(0) == 0) & (i == 0))
    def _():
        carry_s[...] = jnp.zeros_like(carry_s)

    row = i * tm + lax.broadcasted_iota(jnp.int32, (tm, 1), 0)
    is_ctx = row < n_ctx
    m = m_ref[0]
    x = x_ref[0]
    d = x.shape[-1]
    h = _modulate(x, _pick(m, is_ctx, 0), _pick(m, is_ctx, 1)).astype(BF16)
    z = jnp.zeros((tm, d), F32)
    for nb, y_ref in enumerate((ya_ref, yb_ref, yc_ref, yd_ref)):
        gate = _sigmoid(jnp.dot(h, wm_ref[:, nb * d:(nb + 1) * d], preferred_element_type=F32))
        z = z + gate * jnp.dot(y_ref[0], wb_ref[nb], preferred_element_type=F32)
    mix = jnp.dot(z.astype(BF16), wo_ref[...], preferred_element_type=F32)
    x1 = x + _pick(m, is_ctx, 2) * mix
    xo_ref[0] = x1
    h2 = _modulate(x1, _pick(m, is_ctx, 3), _pick(m, is_ctx, 4))

    wr = wr_ref[...]
    wr_hi = wr.astype(BF16)
    wr_lo = (wr - wr_hi.astype(F32)).astype(BF16)
    h2_hi = h2.astype(BF16)
    h2_lo = (h2 - h2_hi.astype(F32)).astype(BF16)
    dot = functools.partial(jnp.dot, preferred_element_type=F32)
    logit = dot(h2_hi, wr_hi) + dot(h2_lo, wr_hi) + dot(h2_hi, wr_lo) + br_ref[...]
    lane = lax.broadcasted_iota(jnp.int32, (tm, LANES), 1)
    big = jnp.int32(LANES)
    is_group = (lane >= N_EXPERTS) & (lane < N_EXPERTS + N_GROUPS)
    gl = jnp.where(is_group, logit, NEG_INF)
    g_max = jnp.max(gl, axis=1, keepdims=True)
    g_idx = jnp.min(jnp.where(gl == g_max, lane, big), axis=1, keepdims=True) - N_EXPERTS
    g_prob = 1.0 / jnp.sum(jnp.where(is_group, jnp.exp(logit - g_max), 0.0), axis=1, keepdims=True)
    in_group = (lane < N_EXPERTS) & (lane // EXP_PER_GROUP == g_idx)
    el = jnp.where(in_group, logit, NEG_INF)
    v1 = jnp.max(el, axis=1, keepdims=True)
    i1 = jnp.min(jnp.where(el == v1, lane, big), axis=1, keepdims=True)
    el2 = jnp.where(lane == i1, NEG_INF, el)
    v2 = jnp.max(el2, axis=1, keepdims=True)
    i2 = jnp.min(jnp.where(el2 == v2, lane, big), axis=1, keepdims=True)
    e2 = jnp.exp(v2 - v1)
    w1 = g_prob / (1.0 + e2)
    w2 = g_prob * e2 / (1.0 + e2)
    first_low = i1 < i2
    lo = jnp.where(first_low, i1, i2) - g_idx * EXP_PER_GROUP
    hi = jnp.where(first_low, i2, i1) - g_idx * EXP_PER_GROUP
    pair = 3 * lo - ((lo * (lo - 1)) >> 1) + (hi - lo - 1)
    cls = g_idx * N_PAIRS + pair
    w_lo = jnp.where(first_low, w1, w2)
    w_hi = jnp.where(first_low, w2, w1)
    onehot = jnp.where(lane == cls, 1.0, 0.0)
    ri = lax.broadcasted_iota(jnp.int32, (tm, tm), 0)
    ci = lax.broadcasted_iota(jnp.int32, (tm, tm), 1)
    earlier = jnp.where(ci < ri, 1.0, 0.0).astype(BF16)
    before = jnp.dot(earlier, onehot.astype(BF16), preferred_element_type=F32) + carry_s[...]
    rank = jnp.sum(onehot * before, axis=1, keepdims=True)
    carry_s[...] += jnp.sum(onehot, axis=0, keepdims=True)
    counts_ref[...] = carry_s[...].astype(jnp.int32)
    route_ref[0] = jnp.where(lane == 0, cls, jnp.where(lane == 1, rank.astype(jnp.int32), 0))
    gate_bits = pltpu.bitcast(jnp.where(lane == 0, w_lo, jnp.where(lane == 1, w_hi, 0.0)), jnp.int32)
    hp_ref[0] = jnp.concatenate([_pack_bf16_pairs(h2), gate_bits], axis=1)


def _merge(xc, mvec, ys, w_merge, w_branch, w_out, w_route, b_route, n_ctx):
    b, n, d = xc.shape
    tm = _token_tile(n, 640)
    kern = functools.partial(_merge_kernel, tm=tm, n_ctx=n_ctx)
    tok = lambda w: pl.BlockSpec((1, tm, w), lambda bi, i: (bi, i, 0))
    const = lambda shape: pl.BlockSpec(shape, lambda bi, i: (0,) * len(shape))
    return pl.pallas_call(
        kern,
        grid=(b, n // tm),
        in_specs=[tok(d), pl.BlockSpec((1, MOD_ROWS, d), lambda bi, i: (bi, 0, 0)),
                  tok(MIX_W), tok(MIX_W), tok(MIX_W), tok(MIX_W),
                  const(w_merge.shape), const(w_branch.shape), const(w_out.shape),
                  const(w_route.shape), const(b_route.shape)],
        out_specs=[tok(d), tok(d // 2 + LANES), tok(LANES), const((1, LANES))],
        out_shape=[jax.ShapeDtypeStruct((b, n, d), F32),
                   jax.ShapeDtypeStruct((b, n, d // 2 + LANES), jnp.int32),
                   jax.ShapeDtypeStruct((b, n, LANES), jnp.int32),
                   jax.ShapeDtypeStruct((1, LANES), jnp.int32)],
        scratch_shapes=[pltpu.VMEM((1, LANES), F32)],
        compiler_params=_cparams("arbitrary", "arbitrary"),
        name="merge_route",
    )(xc, mvec, *ys, w_merge, w_branch, w_out, w_route, b_route)


def _row_gather(table, idx):
    n_out = idx.shape[0]
    width = table.shape[1]
    per_worker = n_out // SC_WORKERS
    assert per_worker * SC_WORKERS == n_out and per_worker % 8 == 0
    chunk = max(c for c in range(8, SC_MAX_CHUNK + 1, 8) if per_worker % c == 0)
    n_chunks = per_worker // chunk
    mesh = plsc.VectorSubcoreMesh(core_axis_name="c", subcore_axis_name="s",
                                  num_cores=SC_CORES, num_subcores=SC_SUBCORES)

    @functools.partial(
        pl.kernel, mesh=mesh,
        out_type=jax.ShapeDtypeStruct((n_out, width), table.dtype),
        scratch_types=[pltpu.VMEM((chunk,), jnp.int32),
                       pltpu.VMEM((chunk, width), table.dtype),
                       pltpu.SemaphoreType.DMA],
        name="sc_row_gather",
    )
    def gather(table_hbm, idx_hbm, out_hbm, idx_v, rows_v, sem):
        worker = lax.axis_index("s") * SC_CORES + lax.axis_index("c")
        base = worker * per_worker

        @pl.loop(0, n_chunks)
        def _(j):
            off = pl.multiple_of(base + j * chunk, 8)
            pltpu.sync_copy(idx_hbm.at[pl.ds(off, chunk)], idx_v)
            pltpu.async_copy(table_hbm.at[idx_v], rows_v, sem).wait()
            pltpu.sync_copy(rows_v, out_hbm.at[pl.ds(off, chunk)])

    return gather(table, idx)


def _ffn(h, w1_ref, w3_ref, w2_ref):
    a = jnp.dot(h, w1_ref[0, 0], preferred_element_type=F32)
    g = jnp.dot(h, w3_ref[0, 0], preferred_element_type=F32)
    return jnp.dot((_silu(a) * g).astype(BF16), w2_ref[0, 0], preferred_element_type=F32)


def _moe_kernel(elo_ref, ehi_ref, valid_ref, xs_ref, w1a_ref, w3a_ref, w2a_ref, w1b_ref, w3b_ref, w2b_ref,
                ys_ref):
    j = pl.program_id(0)
    half = ys_ref.shape[1]

    @pl.when(valid_ref[j] != 0)
    def _():
        hi, lo = _unpack_bf16_pairs(xs_ref[:, :half])
        h = jnp.concatenate([hi.astype(BF16), lo.astype(BF16)], axis=1)
        gates = pltpu.bitcast(xs_ref[:, half:], F32)
        y = (gates[:, 0:1] * _ffn(h, w1a_ref, w3a_ref, w2a_ref)
             + gates[:, 1:2] * _ffn(h, w1b_ref, w3b_ref, w2b_ref))
        ys_ref[...] = _pack_bf16_pairs(y)

    @pl.when(valid_ref[j] == 0)
    def _():
        ys_ref[...] = jnp.zeros_like(ys_ref)


def _moe_experts(xs, tile_elo, tile_ehi, tile_valid, w1, w3, w2, layer):
    n_slots, width = xs.shape
    d = w1.shape[2]
    n_tiles = n_slots // MOE_TILE
    wspec = lambda shape, which: pl.BlockSpec(
        (1, 1) + shape, lambda j, elo, ehi, valid: (layer, (elo, ehi)[which][j], 0, 0))
    grid_spec = pltpu.PrefetchScalarGridSpec(
        num_scalar_prefetch=3,
        grid=(n_tiles,),
        in_specs=[pl.BlockSpec((MOE_TILE, width), lambda j, elo, ehi, valid: (j, 0)),
                  wspec((d, D_EXPERT), 0), wspec((d, D_EXPERT), 0), wspec((D_EXPERT, d), 0),
                  wspec((d, D_EXPERT), 1), wspec((d, D_EXPERT), 1), wspec((D_EXPERT, d), 1)],
        out_specs=pl.BlockSpec((MOE_TILE, d // 2), lambda j, elo, ehi, valid: (j, 0)),
    )
    return pl.pallas_call(
        _moe_kernel,
        grid_spec=grid_spec,
        out_shape=jax.ShapeDtypeStruct((n_slots, d // 2), jnp.int32),
        compiler_params=_cparams("arbitrary"),
        name="moe_experts",
    )(tile_elo, tile_ehi, tile_valid, xs, w1, w3, w2, w1, w3, w2)


def _moe_combine_kernel(x_ref, m_ref, y_ref, *rest, tm, n_ctx, row0):
    o_ref = rest[-1]
    row = row0 + pl.program_id(1) * tm + lax.broadcasted_iota(jnp.int32, (tm, 1), 0)
    hi, lo = _unpack_bf16_pairs(y_ref[0])
    x = x_ref[0] + _pick(m_ref[0], row < n_ctx, 5) * jnp.concatenate([hi, lo], axis=1)
    if len(rest) == 2:
        ms = jnp.mean(x * x, axis=-1, keepdims=True)
        x = x * lax.rsqrt(ms + EPS) * rest[0][...]
    o_ref[0] = x


def _moe_combine(x1, mvec, yt, n_ctx, final_w=None):
    b, n, d = x1.shape
    if final_w is None:
        tm, skip, n_out, extra, extra_specs = _token_tile(n, 1100), 0, n, (), []
    else:
        tm = _token_tile(n_ctx, 1024)
        skip, n_out = n_ctx // tm, n - n_ctx
        extra, extra_specs = (final_w.reshape(1, d),), [pl.BlockSpec((1, d), lambda bi, i: (0, 0))]
    kern = functools.partial(_moe_combine_kernel, tm=tm, n_ctx=n_ctx, row0=skip * tm)
    tok = lambda w: pl.BlockSpec((1, tm, w), lambda bi, i: (bi, i + skip, 0))
    return pl.pallas_call(
        kern,
        grid=(b, n_out // tm),
        in_specs=[tok(d), pl.BlockSpec((1, MOD_ROWS, d), lambda bi, i: (bi, 0, 0)), tok(d // 2)] + extra_specs,
        out_specs=pl.BlockSpec((1, tm, d), lambda bi, i: (bi, i, 0)),
        out_shape=jax.ShapeDtypeStruct((b, n_out, d), F32),
        compiler_params=_cparams("parallel", "parallel"),
        name="moe_combine",
    )(x1, mvec, yt, *extra)


def _moe(x1, mvec, hp, route, counts, w1, w3, w2, layer, n_ctx, final_w=None):
    b, n, d = x1.shape
    m = b * n
    n_slots = m + N_CLASSES * MOE_TILE
    cls = route[..., 0].reshape(m)
    rank = route[..., 1].reshape(m)
    cnt = counts[0, :N_CLASSES]
    padded = (cnt + MOE_TILE - 1) // MOE_TILE * MOE_TILE
    ends = jnp.cumsum(padded)
    pos = (ends - padded)[cls] + rank
    src = (jnp.arange(n_slots, dtype=jnp.int32) % m).at[pos].set(jnp.arange(m, dtype=jnp.int32))
    tile_start = jnp.arange(n_slots // MOE_TILE, dtype=jnp.int32) * MOE_TILE
    tile_cls = jnp.minimum(jnp.searchsorted(ends, tile_start, side="right"), N_CLASSES - 1).astype(jnp.int32)
    tile_valid = (tile_start < ends[-1]).astype(jnp.int32)
    pair_lo = jnp.asarray([0, 0, 0, 1, 1, 2], jnp.int32)
    pair_hi = jnp.asarray([1, 2, 3, 2, 3, 3], jnp.int32)
    group = tile_cls // N_PAIRS
    tile_elo = group * EXP_PER_GROUP + pair_lo[tile_cls % N_PAIRS]
    tile_ehi = group * EXP_PER_GROUP + pair_hi[tile_cls % N_PAIRS]

    xs = _row_gather(hp.reshape(m, hp.shape[-1]), src)
    ys = _moe_experts(xs, tile_elo, tile_ehi, tile_valid, w1, w3, w2, layer)
    yt = _row_gather(ys, pos)
    return _moe_combine(x1, mvec, yt.reshape(b, n, d // 2), n_ctx, final_w)


def _rope_tables(n_lat, n_ctx):
    rows = n_lat // GRID_W
    row = jnp.repeat(jnp.arange(rows), GRID_W).astype(F32)
    col = jnp.tile(jnp.arange(GRID_W), rows).astype(F32)
    nq = HEAD_DIM // 4
    inv = jnp.power(ROPE_BASE, -jnp.arange(nq, dtype=F32) / nq)
    ang = jnp.concatenate([row[:, None] * inv, col[:, None] * inv], -1)
    cos = jnp.cos(ang)
    sin = jnp.sin(ang)
    cos_h = jnp.concatenate([cos, cos], -1)
    sin_h = jnp.concatenate([-sin, sin], -1)
    cos_full = jnp.concatenate([jnp.ones((n_ctx, HEAD_DIM), F32), cos_h], 0)
    sin_full = jnp.concatenate([jnp.zeros((n_ctx, HEAD_DIM), F32), sin_h], 0)
    return jnp.tile(cos_full, (1, N_HEADS)), jnp.tile(sin_full, (1, N_HEADS))


def _block_diag_heads(w):
    out = jnp.zeros((MIX_W, MIX_W), w.dtype)
    for h in range(N_HEADS):
        out = out.at[h * HEAD_DIM:(h + 1) * HEAD_DIM, h * HEAD_DIM:(h + 1) * HEAD_DIM].set(w[h])
    return out


def kernel(x, c, ctx, c_ctx, w_mod, b_mod, w_in, a_conv_w, a_conv_b, a_gate_w, a_gate_b, a_lambda, b_theta,
           c_conv_w, c_conv_b, c_gate_b, d_lb, w_branch, w_out, moe_w_group, moe_b_group, moe_w_router,
           moe_b_router, moe_w1, moe_w3, moe_w2, final_norm_w):
    bsz, n_lat, d = x.shape
    n_ctx = ctx.shape[1]
    depth = w_mod.shape[0]
    assert n_ctx % CHUNK == 0 and n_lat % CHUNK == 0 and n_ctx % 256 == 0 and n_lat % 256 == 0

    xc = jnp.concatenate([ctx, x], axis=1)
    cos, sin = _rope_tables(n_lat, n_ctx)

    cc = jnp.zeros((8, d), F32).at[:bsz].set(c).at[bsz].set(c_ctx)
    mod = _mod_vectors(cc, w_mod, b_mod)
    mx = mod[:, :bsz].reshape(depth, bsz, N_MOD, d)
    mc = jnp.broadcast_to(mod[:, bsz].reshape(depth, 1, N_MOD, d), (depth, bsz, N_MOD, d))
    pad = jnp.zeros((depth, bsz, 8 - N_MOD, d), F32)
    mvec = jnp.concatenate([mx, pad, mc, pad], axis=2)

    lbs = jnp.cumsum(jax.nn.softmax(d_lb.astype(F32), axis=0), axis=0)
    lbs = lbs - lbs[0]

    n_mix_cols = 15 * MIX_W
    gate0 = 10 * MIX_W
    n_gate = 4 * N_HEADS
    w1b = moe_w1.astype(BF16)
    w3b = moe_w3.astype(BF16)
    w2b = moe_w2.astype(BF16)

    for l in range(depth):
        wl = w_in[l]
        w_mix = jnp.concatenate([wl[:, :gate0], wl[:, gate0 + n_gate:n_mix_cols + n_gate]], axis=1).astype(BF16)
        w_gate = jnp.pad(wl[:, gate0:gate0 + n_gate], ((0, 0), (0, LANES - n_gate))).astype(BF16)
        w_merge = wl[:, n_mix_cols + n_gate:].astype(BF16)
        p, gates_c = _in_proj(xc, mvec[l], w_mix, w_gate, cos, sin, n_ctx)

        gw = jnp.concatenate([_block_diag_heads(a_gate_w[l, dd, j]) for dd in range(2) for j in range(2)],
                             axis=1).astype(BF16)
        gb = a_gate_b[l].reshape(1, 4 * MIX_W)
        ya = _mixer_a(p, a_conv_w[l], a_conv_b[l].reshape(1, MIX_W), gw, gb, a_lambda[l], n_ctx)

        log_gamma = jnp.repeat(jax.nn.log_sigmoid(b_theta[l].astype(F32)), HEAD_DIM, axis=1)
        yb = _mixer_b(p, log_gamma, n_ctx)

        gate_b = jnp.pad(c_gate_b[l].reshape(1, n_gate), ((0, 0), (0, LANES - n_gate)))
        yc = _mixer_c(p, gates_c, c_conv_w[l], c_conv_b[l].reshape(1, 2 * MIX_W), gate_b, n_ctx)

        yd = _mixer_d(p, lbs[l].reshape(1, MIX_W), n_ctx)

        w_route = jnp.pad(jnp.concatenate([moe_w_router[l], moe_w_group[l]], axis=1),
                          ((0, 0), (0, LANES - N_EXPERTS - N_GROUPS)))
        b_route = jnp.pad(jnp.concatenate([moe_b_router[l], moe_b_group[l]]),
                          (0, LANES - N_EXPERTS - N_GROUPS)).reshape(1, LANES)
        x1, hp, route, counts = _merge(xc, mvec[l], (ya, yb, yc, yd), w_merge, w_branch[l].astype(BF16),
                                       w_out[l].astype(BF16), w_route, b_route, n_ctx)
        xc = _moe(x1, mvec[l], hp, route, counts, w1b, w3b, w2b, l, n_ctx,
                  final_w=final_norm_w if l == depth - 1 else None)

    return xc
```

```python
import functools

import jax
import jax.numpy as jnp
import numpy as np
from jax import lax
from jax.experimental import pallas as pl
from jax.experimental.pallas import tpu as pltpu
from jax.experimental.pallas import tpu_sc as plsc

F32 = jnp.float32
BF16 = jnp.bfloat16

EPS = 1e-6
N_HEADS = 4
HEAD_DIM = 64
MIX_W = N_HEADS * HEAD_DIM
N_BRANCH = 4
CONV_W = 4
LRU_C = 8.0
GRID_W = 64
ROPE_BASE = 10000.0
N_GROUPS = 4
EXP_PER_GROUP = 4
N_EXPERTS = N_GROUPS * EXP_PER_GROUP
D_EXPERT = 512
N_MOD = 6
M_INIT = -1e30

CHUNK = 128
SUB = 16
HALO = 16
LANES = 128
MOD_ROWS = 16
ROWS = 1
VMEM_LIMIT_BYTES = 56 * 1024 * 1024
N_PAIRS = 6
N_CLASSES = N_GROUPS * N_PAIRS
MOE_TILE = 256
SC_CORES = 2
SC_SUBCORES = 16
SC_WORKERS = SC_CORES * SC_SUBCORES
SC_MAX_CHUNK = 32
NEG_INF = float("-inf")
LOG2_E = 1.4426950408889634


def _cparams(*sem):
    return pltpu.CompilerParams(dimension_semantics=sem, vmem_limit_bytes=VMEM_LIMIT_BYTES)


def _token_tile(n, cap):
    best = 16
    for t in range(16, cap + 1, 16):
        if n % t == 0:
            best = t
    return best


def _modulate(x, shift, scale):
    ms = jnp.mean(x * x, axis=-1, keepdims=True)
    return x * lax.rsqrt(ms + EPS) * (1.0 + scale) + shift


def _pick(m, is_ctx, k):
    return jnp.where(is_ctx, m[8 + k:9 + k], m[k:k + 1])


def _sigmoid(x):
    return 0.5 * jnp.tanh(0.5 * x) + 0.5


def _silu(x):
    return x * _sigmoid(x)


def _log_sigmoid(x):
    return jnp.minimum(x, 0.0) - jnp.log(1.0 + jnp.exp(-jnp.abs(x)))


def _split3(x):
    hi = x.astype(BF16)
    r = x - hi.astype(F32)
    mid = r.astype(BF16)
    lo = (r - mid.astype(F32)).astype(BF16)
    return hi, mid, lo


def _sel_dot(sel, x):
    hi, mid, lo = _split3(x)
    d = functools.partial(jnp.dot, preferred_element_type=F32)
    return d(sel, hi) + d(sel, mid) + d(sel, lo)


def _dot_sel(x, sel):
    hi, mid, lo = _split3(x)
    d = functools.partial(jnp.dot, preferred_element_type=F32)
    return d(hi, sel) + d(mid, sel) + d(lo, sel)


def _dot_nt(a, b):
    return lax.dot_general(a, b, (((1,), (1,)), ((), ())), preferred_element_type=F32)


def _dot_tn(a, b):
    return lax.dot_general(a, b, (((0,), (0,)), ((), ())), preferred_element_type=F32)


def _head_of_lane(width):
    return lax.broadcasted_iota(jnp.int32, (1, width), 1) // HEAD_DIM


def _block_ones(width):
    r = lax.broadcasted_iota(jnp.int32, (width, width), 0) // HEAD_DIM
    c = lax.broadcasted_iota(jnp.int32, (width, width), 1) // HEAD_DIM
    return r == c


def _head_norm(o, center):
    ones = jnp.where(_block_ones(MIX_W), 1.0, 0.0).astype(BF16)
    inv = 1.0 / HEAD_DIM
    if center:
        o = o - _dot_sel(o, ones) * inv
    var = _dot_sel(o * o, ones) * inv
    return o * lax.rsqrt(var + EPS)


def _bwd_chunk(s, n_ctx_chunks, n_chunks):
    return jnp.where(s < n_ctx_chunks, n_ctx_chunks - 1 - s, n_chunks - 1 + n_ctx_chunks - s)


def _conv_chunk(ref, c, w, b, *, n, n_ctx, width0=0, width=None):
    L = CHUNK
    wl = L + 2 * HALO
    r0 = c * L
    start = pl.multiple_of(jnp.clip(r0 - HALO, 0, n - wl), HALO)
    off = r0 - start
    if width is None:
        win = ref[0, pl.ds(start, wl), :]
    else:
        win = ref[0, pl.ds(start, wl), width0:width0 + width]
    ri = lax.broadcasted_iota(jnp.int32, (L, wl), 0)
    mi = lax.broadcasted_iota(jnp.int32, (L, wl), 1)
    g = r0 + ri
    y = b
    seg_g = jnp.where(g < n_ctx, 0, 1)
    for k in range(CONV_W):
        src = g + (k - 2)
        hit = (mi == ri + off + (k - 2)) & (jnp.where(src < n_ctx, 0, 1) == seg_g) & (src >= 0) & (src < n)
        sel = jnp.where(hit, 1.0, 0.0).astype(BF16)
        y = y + w[k:k + 1] * jnp.dot(sel, win, preferred_element_type=F32)
    return y


def _mod_kernel(cc_ref, w_ref, b_ref, o_ref):
    s = _silu(cc_ref[...])
    o_ref[0] = jnp.dot(s, w_ref[0], precision=lax.Precision.HIGHEST, preferred_element_type=F32) + b_ref[0]


def _mod_vectors(cc, w_mod, b_mod):
    depth, d, dm = w_mod.shape
    tn = 1536
    return pl.pallas_call(
        _mod_kernel,
        grid=(depth, dm // tn),
        in_specs=[pl.BlockSpec((8, d), lambda l, j: (0, 0)),
                  pl.BlockSpec((1, d, tn), lambda l, j: (l, 0, j)),
                  pl.BlockSpec((1, 1, tn), lambda l, j: (l, 0, j))],
        out_specs=pl.BlockSpec((1, 8, tn), lambda l, j: (l, 0, j)),
        out_shape=jax.ShapeDtypeStruct((depth, 8, dm), F32),
        compiler_params=_cparams("parallel", "parallel"),
        name="mod_vectors",
    )(cc, w_mod, b_mod.reshape(depth, 1, dm))


def _rope(t, cos, sin):
    lane = lax.broadcasted_iota(jnp.int32, (1, MIX_W), 1) % HEAD_DIM
    half = HEAD_DIM // 2
    swapped = jnp.where(lane < half, pltpu.roll(t, MIX_W - half, 1), pltpu.roll(t, half, 1))
    return t * cos + swapped * sin


RET_Q_COL = 2 * MIX_W
RET_K_COL = 3 * MIX_W


def _in_proj_kernel(x_ref, m_ref, w_ref, wg_ref, cos_ref, sin_ref, p_ref, g_ref, *, tm, n_ctx, col_chunk):
    i = pl.program_id(1)
    row = i * tm + lax.broadcasted_iota(jnp.int32, (tm, 1), 0)
    is_ctx = row < n_ctx
    m = m_ref[0]
    h = _modulate(x_ref[0], _pick(m, is_ctx, 0), _pick(m, is_ctx, 1)).astype(BF16)
    for j in range(w_ref.shape[1] // col_chunk):
        sl = slice(j * col_chunk, (j + 1) * col_chunk)
        t = jnp.dot(h, w_ref[:, sl], preferred_element_type=F32)
        if j * col_chunk == RET_Q_COL:
            t = _rope(t, cos_ref[...], sin_ref[...])
        elif j * col_chunk == RET_K_COL:
            t = _rope(t, cos_ref[...], sin_ref[...]) * HEAD_DIM ** -0.5
        p_ref[0, :, sl] = t.astype(BF16)
    g_ref[0] = jnp.dot(h, wg_ref[...], preferred_element_type=F32)


def _in_proj(xc, mvec, w_mix, w_gate, cos, sin, n_ctx):
    b, n, d = xc.shape
    tm = _token_tile(n, 640)
    wc = w_mix.shape[1]
    kern = functools.partial(_in_proj_kernel, tm=tm, n_ctx=n_ctx, col_chunk=MIX_W)
    return pl.pallas_call(
        kern,
        grid=(b, n // tm),
        in_specs=[pl.BlockSpec((1, tm, d), lambda bi, i: (bi, i, 0)),
                  pl.BlockSpec((1, MOD_ROWS, d), lambda bi, i: (bi, 0, 0)),
                  pl.BlockSpec((d, wc), lambda bi, i: (0, 0)),
                  pl.BlockSpec((d, LANES), lambda bi, i: (0, 0)),
                  pl.BlockSpec((tm, MIX_W), lambda bi, i: (i, 0)),
                  pl.BlockSpec((tm, MIX_W), lambda bi, i: (i, 0))],
        out_specs=[pl.BlockSpec((1, tm, wc), lambda bi, i: (bi, i, 0)),
                   pl.BlockSpec((1, tm, LANES), lambda bi, i: (bi, i, 0))],
        out_shape=[jax.ShapeDtypeStruct((b, n, wc), BF16),
                   jax.ShapeDtypeStruct((b, n, LANES), F32)],
        compiler_params=_cparams("parallel", "parallel"),
        name="in_proj",
    )(xc, mvec, w_mix, w_gate, cos, sin)


def _lin_scan(a, x, reverse):
    L = a.shape[0]
    row = lax.broadcasted_iota(jnp.int32, (L, 1), 0)
    k = 1
    while k < L:
        if reverse:
            a_s = pltpu.roll(a, L - k, 0)
            x_s = pltpu.roll(x, L - k, 0)
            valid = row < L - k
        else:
            a_s = pltpu.roll(a, k, 0)
            x_s = pltpu.roll(x, k, 0)
            valid = row >= k
        x = jnp.where(valid, a * x_s + x, x)
        a = jnp.where(valid, a * a_s, a)
        k *= 2
    return a, x


def _mixer_a_kernel(ax_ref, ag_ref, cw_ref, cb_ref, gw_ref, gb_ref, lam_ref, out_ref, u_s, hf_s,
                    *, n, n_ctx):
    L = CHUNK
    n_chunks = n // L
    n_ctx_chunks = n_ctx // L
    cw = cw_ref[...]
    cb = cb_ref[...]
    log_lam = _log_sigmoid(lam_ref[...])

    def direction(d, c, carry):
        r0 = pl.multiple_of(c * L, L)
        u = u_s[pl.ds(r0, L), :]
        pre = jnp.dot(u.astype(BF16), gw_ref[:, d * 2 * MIX_W:(d + 1) * 2 * MIX_W],
                      preferred_element_type=F32) + gb_ref[:, d * 2 * MIX_W:(d + 1) * 2 * MIX_W]
        r = _sigmoid(pre[:, :MIX_W])
        gi = _sigmoid(pre[:, MIX_W:])
        log_a = LRU_C * r * log_lam[d:d + 1]
        a = jnp.exp(log_a)
        inp = jnp.sqrt(1.0 - jnp.exp(2.0 * log_a)) * (gi * u)
        a_cum, h = _lin_scan(a, inp, reverse=(d == 1))
        h = h + a_cum * carry
        new_carry = h[0:1] if d == 1 else h[L - 1:L]
        return r0, h, new_carry

    def conv_body(c, _):
        r0 = pl.multiple_of(c * L, L)
        u_s[pl.ds(r0, L), :] = _conv_chunk(ax_ref, c, cw, cb, n=n, n_ctx=n_ctx)
        return 0

    lax.fori_loop(0, n_chunks, conv_body, 0)

    def walk_body(s, carries):
        r0, h, carry_f = direction(0, s, carries[0])
        hf_s[0, pl.ds(r0, L), :] = h
        r0, h, carry_b = direction(1, _bwd_chunk(s, n_ctx_chunks, n_chunks), carries[1])
        hf_s[1, pl.ds(r0, L), :] = h
        return carry_f, carry_b

    zero = jnp.zeros((1, MIX_W), F32)
    lax.fori_loop(0, n_chunks, walk_body, (zero, zero))

    def out_body(c, _):
        rows = pl.ds(pl.multiple_of(c * L, L), L)
        gate = jax.nn.gelu(ag_ref[0, rows, :].astype(F32), approximate=True)
        out_ref[0, rows, :] = (gate * (hf_s[0, rows, :] + hf_s[1, rows, :])).astype(out_ref.dtype)
        return 0

    lax.fori_loop(0, n_chunks, out_body, 0)


def _seq_spec(n, col, rows=1):
    if rows == 1:
        return pl.BlockSpec((1, n, MIX_W), lambda b: (b, 0, col))
    return pl.BlockSpec((rows, n, MIX_W), lambda b: (b, 0, col), pipeline_mode=pl.Buffered(1))


def _full_spec(shape):
    return pl.BlockSpec(shape, lambda b: (0,) * len(shape))


def _mixer_a(p, conv_w, conv_b, gate_w, gate_b, lam, n_ctx):
    b, n, _ = p.shape
    kern = functools.partial(_mixer_a_kernel, n=n, n_ctx=n_ctx)
    return pl.pallas_call(
        kern,
        grid=(b,),
        in_specs=[_seq_spec(n, 0), _seq_spec(n, 1),
                  _full_spec(conv_w.shape), _full_spec(conv_b.shape),
                  _full_spec(gate_w.shape), _full_spec(gate_b.shape), _full_spec(lam.shape)],
        out_specs=pl.BlockSpec((1, n, MIX_W), lambda bi: (bi, 0, 0)),
        out_shape=jax.ShapeDtypeStruct((b, n, MIX_W), BF16),
        scratch_shapes=[pltpu.VMEM((n, MIX_W), F32), pltpu.VMEM((2, n, MIX_W), F32)],
        compiler_params=_cparams("parallel"),
        name="mixer_rglru",
    )(p, p, conv_w, conv_b, gate_w, gate_b, lam)


def _mixer_b_kernel(q_ref, k_ref, v_ref, g_ref, lg_ref, out_ref, oi_s, dec_s, st_s, *, n, n_ctx):
    L = CHUNK
    n_chunks = n // L
    n_ctx_chunks = n_ctx // L
    head = _head_of_lane(MIX_W)
    bd = _block_ones(MIX_W)
    lgf = lg_ref[0:1, :]
    lgb = lg_ref[1:2, :]
    pos = lax.broadcasted_iota(jnp.int32, (L, 1), 0).astype(F32)
    diff = (lax.broadcasted_iota(jnp.int32, (L, L), 0) - lax.broadcasted_iota(jnp.int32, (L, L), 1)).astype(F32)
    for h in range(N_HEADS):
        lf = lgf[:, h * HEAD_DIM:h * HEAD_DIM + 1]
        lb = lgb[:, h * HEAD_DIM:h * HEAD_DIM + 1]
        dec_s[h] = jnp.where(diff >= 0, jnp.exp(lf * jnp.maximum(diff, 0.0)), jnp.exp(lb * jnp.maximum(-diff, 0.0)))

    qw = (jnp.exp(lgf * (pos + 1.0)), jnp.exp(lgb * (float(L) - pos)))
    kw = (jnp.exp(lgf * (float(L) - 1.0 - pos)), jnp.exp(lgb * pos))
    g_chunk = (jnp.exp(lgf * float(L)), jnp.exp(lgb * float(L)))

    def state_step(d, c):
        rows = pl.ds(pl.multiple_of(c * L, L), L)
        q = q_ref[0, rows, :].astype(F32)
        k = k_ref[0, rows, :].astype(F32)
        st = st_s[d]
        oi_s[d, rows, :] = jnp.dot((q * qw[d]).astype(BF16), st.astype(BF16), preferred_element_type=F32)
        kv = _dot_tn((k * kw[d]).astype(BF16), v_ref[0, rows, :])
        st_s[d] = g_chunk[d] * st + jnp.where(bd, kv, 0.0)

    st_s[...] = jnp.zeros_like(st_s)

    def walk_body(s, _):
        state_step(0, s)
        state_step(1, _bwd_chunk(s, n_ctx_chunks, n_chunks))
        return 0

    lax.fori_loop(0, n_chunks, walk_body, 0)

    def out_body(c, _):
        rows = pl.ds(pl.multiple_of(c * L, L), L)
        q = q_ref[0, rows, :]
        k = k_ref[0, rows, :]
        v = v_ref[0, rows, :]
        o = oi_s[0, rows, :] + oi_s[1, rows, :]
        for h in range(N_HEADS):
            hm = head == h
            sc = _dot_nt(jnp.where(hm, q, jnp.zeros_like(q)), k) * dec_s[h]
            o = o + jnp.where(hm, jnp.dot(sc.astype(BF16), v, preferred_element_type=F32), 0.0)
        gate = _silu(g_ref[0, rows, :].astype(F32))
        out_ref[0, rows, :] = (_head_norm(o, True) * gate).astype(out_ref.dtype)
        return 0

    lax.fori_loop(0, n_chunks, out_body, 0, unroll=2)


def _mixer_b(p, log_gamma, n_ctx):
    b, n, _ = p.shape
    kern = functools.partial(_mixer_b_kernel, n=n, n_ctx=n_ctx)
    return pl.pallas_call(
        kern,
        grid=(b,),
        in_specs=[_seq_spec(n, 2), _seq_spec(n, 3), _seq_spec(n, 4), _seq_spec(n, 5),
                  _full_spec(log_gamma.shape)],
        out_specs=pl.BlockSpec((1, n, MIX_W), lambda bi: (bi, 0, 0)),
        out_shape=jax.ShapeDtypeStruct((b, n, MIX_W), BF16),
        scratch_shapes=[pltpu.VMEM((2, n, MIX_W), F32),
                        pltpu.VMEM((N_HEADS, CHUNK, CHUNK), F32),
                        pltpu.VMEM((2, MIX_W, MIX_W), F32)],
        compiler_params=_cparams("parallel"),
        name="mixer_retention",
    )(p, p, p, p, log_gamma)


def _mixer_c_kernel(q_ref, k_ref, v_ref, o_ref, g_ref, cw_ref, cb_ref, gb_ref, out_ref,
                    qs_s, ks_s, hf_s, c_s, n_s, m_s, *, n, n_ctx):
    L = CHUNK
    n_chunks = n // L
    n_ctx_chunks = n_ctx // L
    head = _head_of_lane(MIX_W)
    bd = _block_ones(MIX_W)
    ones_bd = jnp.where(bd, 1.0, 0.0).astype(BF16)
    cw = cw_ref[...]
    cb = cb_ref[...]
    ri = lax.broadcasted_iota(jnp.int32, (L, L), 0)
    ci = lax.broadcasted_iota(jnp.int32, (L, L), 1)
    gl = lax.broadcasted_iota(jnp.int32, (LANES, MIX_W), 0)
    hl = lax.broadcasted_iota(jnp.int32, (LANES, MIX_W), 1) // HEAD_DIM

    def expand(kind):
        return jnp.where(gl == kind * N_HEADS + hl, 1.0, 0.0).astype(BF16)

    def conv_body(c, _):
        rows = pl.ds(pl.multiple_of(c * L, L), L)
        qs_s[rows, :] = _silu(_conv_chunk(q_ref, c, cw[:, :MIX_W], cb[:, :MIX_W], n=n, n_ctx=n_ctx)).astype(BF16)
        ks_s[rows, :] = (_silu(_conv_chunk(k_ref, c, cw[:, MIX_W:], cb[:, MIX_W:], n=n, n_ctx=n_ctx))
                         * HEAD_DIM ** -0.5).astype(BF16)
        return 0

    lax.fori_loop(0, n_chunks, conv_body, 0)

    def chunk(d, c):
        rev = d == 1
        rows = pl.ds(pl.multiple_of(c * L, L), L)
        q = qs_s[rows, :]
        k = ks_s[rows, :]
        v = v_ref[0, rows, :]
        g = g_ref[0, rows, :] + gb_ref[...]
        log_i = _dot_sel(g, expand(2 * d))
        log_f = _dot_sel(_log_sigmoid(g), expand(2 * d + 1))
        causal = (ci >= ri) if rev else (ci <= ri)
        tri = jnp.where(causal, 1.0, 0.0).astype(BF16)
        cum = _sel_dot(tri, log_f)
        cum_end = cum[0:1] if rev else cum[L - 1:L]
        m_prev = m_s[d]
        c_prev = c_s[d]
        n_prev = n_s[d]
        row_src = (log_i - cum).T
        m_inter = cum + m_prev
        num_inter = jnp.dot(q, c_prev.astype(BF16), preferred_element_type=F32)
        qn = _dot_sel(q.astype(F32) * n_prev, ones_bd)
        m_q = jnp.zeros((L, MIX_W), F32)
        scs, rhs = [], []
        for h in range(N_HEADS):
            hm = head == h
            lane0 = h * HEAD_DIM
            log_d = jnp.where(causal, cum[:, lane0:lane0 + 1] + row_src[lane0:lane0 + 1, :], NEG_INF)
            m_h = jnp.maximum(jnp.max(log_d, axis=1, keepdims=True), m_inter[:, lane0:lane0 + 1])
            sc = _dot_nt(jnp.where(hm, q, jnp.zeros_like(q)), k) * jnp.exp(log_d - m_h)
            scs.append(sc.astype(BF16))
            rhs.append(jnp.concatenate([jnp.where(hm, v, jnp.zeros_like(v)),
                                        jnp.broadcast_to(jnp.where(hm, 1.0, 0.0).astype(BF16), (L, MIX_W))], axis=1))
            m_q = jnp.where(hm, m_h, m_q)
        intra = jnp.dot(jnp.concatenate(scs, axis=1), jnp.concatenate(rhs, axis=0), preferred_element_type=F32)
        s_inter = jnp.exp(m_inter - m_q)
        num = intra[:, :MIX_W] + s_inter * num_inter
        den = intra[:, MIX_W:] + s_inter * qn
        hid = num / jnp.maximum(jnp.abs(den), jnp.exp(-m_q))
        log_w = cum_end - cum + log_i
        m_loc = jnp.max(log_w, axis=0, keepdims=True)
        kw = k.astype(F32) * jnp.exp(log_w - m_loc)
        c_loc = jnp.where(bd, _dot_tn(kw.astype(BF16), v), 0.0)
        n_loc = jnp.sum(kw, axis=0, keepdims=True)
        m_new = jnp.maximum(cum_end + m_prev, m_loc)
        s_old = jnp.exp(cum_end + m_prev - m_new)
        s_loc = jnp.exp(m_loc - m_new)
        c_s[d] = s_old * c_prev + s_loc * c_loc
        n_s[d] = s_old * n_prev + s_loc * n_loc
        m_s[d] = m_new
        return rows, hid

    c_s[...] = jnp.zeros_like(c_s)
    n_s[...] = jnp.zeros_like(n_s)
    m_s[...] = jnp.full(m_s.shape, M_INIT, F32)

    def walk_body(s, _):
        rows, hid = chunk(0, s)
        hf_s[0, rows, :] = hid
        rows, hid = chunk(1, _bwd_chunk(s, n_ctx_chunks, n_chunks))
        hf_s[1, rows, :] = hid
        return 0

    lax.fori_loop(0, n_chunks, walk_body, 0)

    def out_body(c, _):
        rows = pl.ds(pl.multiple_of(c * L, L), L)
        gate = _sigmoid(o_ref[0, rows, :].astype(F32))
        out_ref[0, rows, :] = (gate * _head_norm(hf_s[0, rows, :] + hf_s[1, rows, :], True)).astype(out_ref.dtype)
        return 0

    lax.fori_loop(0, n_chunks, out_body, 0)


def _mixer_c(p, gates, conv_w, conv_b, gate_b, n_ctx):
    b, n, _ = p.shape
    kern = functools.partial(_mixer_c_kernel, n=n, n_ctx=n_ctx)
    return pl.pallas_call(
        kern,
        grid=(b,),
        in_specs=[_seq_spec(n, 6), _seq_spec(n, 7), _seq_spec(n, 8), _seq_spec(n, 9),
                  pl.BlockSpec((1, n, LANES), lambda bi: (bi, 0, 0)),
                  _full_spec(conv_w.shape), _full_spec(conv_b.shape), _full_spec(gate_b.shape)],
        out_specs=pl.BlockSpec((1, n, MIX_W), lambda bi: (bi, 0, 0)),
        out_shape=jax.ShapeDtypeStruct((b, n, MIX_W), BF16),
        scratch_shapes=[pltpu.VMEM((n, MIX_W), BF16), pltpu.VMEM((n, MIX_W), BF16),
                        pltpu.VMEM((2, n, MIX_W), F32),
                        pltpu.VMEM((2, MIX_W, MIX_W), F32),
                        pltpu.VMEM((2, 1, MIX_W), F32),
                        pltpu.VMEM((2, 1, MIX_W), F32)],
        compiler_params=_cparams("parallel"),
        name="mixer_mlstm",
    )(p, p, p, p, gates, conv_w, conv_b, gate_b)


def _hgrn2_tables():
    L, S = CHUNK, SUB
    r = np.arange(L)
    same = (r[:, None] // S) == (r[None, :] // S)
    tri = np.stack([same & (r[None, :] <= r[:, None]), same & (r[None, :] >= r[:, None])])
    fold = []
    for rev in (False, True):
        pi = np.concatenate([np.full(len(_pair_rows(i, rev)), i) for i in range(S)])
        pj = np.concatenate([_pair_rows(i, rev) for i in range(S)])
        visible = (pj >= pi) if rev else (pj <= pi)
        block = (np.arange(S)[:, None] == pi[None, :]) & visible[None, :]
        fold.append(np.kron(np.eye(L // S, dtype=bool), block))
    as_bf16 = lambda a: jnp.asarray(a.astype(np.float32), dtype=BF16)
    return as_bf16(tri), as_bf16(same), as_bf16(np.stack(fold))


def _pair_rows(i, rev):
    half = SUB // 2
    if rev:
        return np.arange(0, SUB) if i < half else np.arange(half, SUB)
    return np.arange(0, half) if i < half else np.arange(0, SUB)


def _mixer_d_kernel(q_ref, ff_ref, fb_ref, v_ref, g_ref, lb_ref, tri_ref, same_ref, fold_ref, out_ref,
                    o_s, st_s, *, n, n_ctx):
    L = CHUNK
    S = SUB
    nb = L // S
    n_chunks = n // L
    n_ctx_chunks = n_ctx // L
    lb = lb_ref[...]
    bd = _block_ones(MIX_W)
    ones_bd = jnp.where(bd, 1.0, 0.0).astype(BF16)

    def chunk(r, d, c):
        rev = d == 1
        rows = pl.ds(pl.multiple_of(c * L, L), L)
        q = _silu(q_ref[r, rows, :].astype(F32))
        raw = (fb_ref if rev else ff_ref)[r, rows, :].astype(F32)
        v = v_ref[r, rows, :]
        vf = v.astype(F32)
        f = lb + (1.0 - lb) * _sigmoid(raw)
        k = 1.0 - f
        log_f = jnp.log(f)
        cum = _sel_dot(tri_ref[d], log_f)
        tot = _sel_dot(same_ref[...], log_f)
        qt = (q * jnp.exp(cum)).astype(BF16)
        kt = (k * jnp.exp(tot - cum)).astype(BF16)
        g = jnp.exp(tot)
        cum2 = cum * LOG2_E
        prods, v_js = [], []
        for a in range(nb):
            sl = slice(a * S, (a + 1) * S)
            cb, qb, kb, vb = cum2[sl], q[sl], k[sl], vf[sl]
            for i in range(S):
                js = _pair_rows(i, rev)
                jsl = slice(int(js[0]), int(js[-1]) + 1)
                dec = jnp.exp2(jnp.minimum(cb[i:i + 1] - cb[jsl], 0.0))
                prods.append(qb[i:i + 1] * kb[jsl] * dec)
                v_js.append(vb[jsl])
        scores = jnp.dot(jnp.concatenate(prods, axis=0).astype(BF16), ones_bd, preferred_element_type=F32)
        weighted = (scores * jnp.concatenate(v_js, axis=0)).astype(BF16)
        o = jnp.dot(fold_ref[d], weighted, preferred_element_type=F32)
        st = st_s[r, d]
        inter = [None] * nb
        for a in (range(nb - 1, -1, -1) if rev else range(nb)):
            sl = slice(a * S, (a + 1) * S)
            inter[a] = _dot_nt(qt[sl], st.astype(BF16))
            st = st * g[a * S:a * S + 1] + jnp.where(bd, _dot_tn(v[sl], kt[sl]), 0.0)
        st_s[r, d] = st
        o_s[r, d, rows, :] = o + jnp.concatenate(inter, axis=0)

    st_s[...] = jnp.zeros_like(st_s)

    def walk_body(s, _):
        for r in range(ROWS):
            chunk(r, 0, s)
            chunk(r, 1, _bwd_chunk(s, n_ctx_chunks, n_chunks))
        return 0

    lax.fori_loop(0, n_chunks, walk_body, 0)

    def out_body(c, _):
        rows = pl.ds(pl.multiple_of(c * L, L), L)
        for r in range(ROWS):
            gate = _silu(g_ref[r, rows, :].astype(F32))
            o = o_s[r, 0, rows, :] + o_s[r, 1, rows, :]
            out_ref[r, rows, :] = (_head_norm(o, False) * gate).astype(out_ref.dtype)
        return 0

    lax.fori_loop(0, n_chunks, out_body, 0)


def _mixer_d(p, lb, n_ctx):
    b, n, _ = p.shape
    tri, same, fold = _hgrn2_tables()
    kern = functools.partial(_mixer_d_kernel, n=n, n_ctx=n_ctx)
    return pl.pallas_call(
        kern,
        grid=(b // ROWS,),
        in_specs=[_seq_spec(n, 10, ROWS), _seq_spec(n, 11, ROWS), _seq_spec(n, 12, ROWS), _seq_spec(n, 13, ROWS),
                  _seq_spec(n, 14, ROWS), _full_spec(lb.shape), _full_spec(tri.shape), _full_spec(same.shape), _full_spec(fold.shape)],
        out_specs=pl.BlockSpec((ROWS, n, MIX_W), lambda bi: (bi, 0, 0)),
        out_shape=jax.ShapeDtypeStruct((b, n, MIX_W), BF16),
        scratch_shapes=[pltpu.VMEM((ROWS, 2, n, MIX_W), F32), pltpu.VMEM((ROWS, 2, MIX_W, MIX_W), F32)],
        compiler_params=_cparams("parallel"),
        name="mixer_hgrn2",
    )(p, p, p, p, p, lb, tri, same, fold)


def _pack_bf16_pairs(t):
    w = t.shape[1] // 2
    hi = pltpu.bitcast(t[:, :w].astype(BF16).astype(F32), jnp.uint32)
    lo = pltpu.bitcast(t[:, w:].astype(BF16).astype(F32), jnp.uint32)
    return pltpu.bitcast(hi | (lo >> 16), jnp.int32)


def _unpack_bf16_pairs(p):
    u = pltpu.bitcast(p, jnp.uint32)
    hi = pltpu.bitcast(u & jnp.uint32(0xFFFF0000), F32)
    lo = pltpu.bitcast(u << 16, F32)
    return hi, lo


def _merge_kernel(x_ref, m_ref, ya_ref, yb_ref, yc_ref, yd_ref, wm_ref, wb_ref, wo_ref, wr_ref, br_ref,
                  xo_ref, hp_ref, route_ref, counts_ref, carry_s, *, tm, n_ctx):
    i = pl.program_id(1)

    @pl.when((pl.program_id(0) == 0) & (i == 0))
    def _():
        carry_s[...] = jnp.zeros_like(carry_s)

    row = i * tm + lax.broadcasted_iota(jnp.int32, (tm, 1), 0)
    is_ctx = row < n_ctx
    m = m_ref[0]
    x = x_ref[0]
    d = x.shape[-1]
    h = _modulate(x, _pick(m, is_ctx, 0), _pick(m, is_ctx, 1)).astype(BF16)
    z = jnp.zeros((tm, d), F32)
    for nb, y_ref in enumerate((ya_ref, yb_ref, yc_ref, yd_ref)):
        gate = _sigmoid(jnp.dot(h, wm_ref[:, nb * d:(nb + 1) * d], preferred_element_type=F32))
        z = z + gate * jnp.dot(y_ref[0], wb_ref[nb], preferred_element_type=F32)
    mix = jnp.dot(z.astype(BF16), wo_ref[...], preferred_element_type=F32)
    x1 = x + _pick(m, is_ctx, 2) * mix
    xo_ref[0] = x1
    h2 = _modulate(x1, _pick(m, is_ctx, 3), _pick(m, is_ctx, 4))

    wr = wr_ref[...]
    wr_hi = wr.astype(BF16)
    wr_lo = (wr - wr_hi.astype(F32)).astype(BF16)
    h2_hi = h2.astype(BF16)
    h2_lo = (h2 - h2_hi.astype(F32)).astype(BF16)
    dot = functools.partial(jnp.dot, preferred_element_type=F32)
    logit = dot(h2_hi, wr_hi) + dot(h2_lo, wr_hi) + dot(h2_hi, wr_lo) + br_ref[...]
    lane = lax.broadcasted_iota(jnp.int32, (tm, LANES), 1)
    big = jnp.int32(LANES)
    is_group = (lane >= N_EXPERTS) & (lane < N_EXPERTS + N_GROUPS)
    gl = jnp.where(is_group, logit, NEG_INF)
    g_max = jnp.max(gl, axis=1, keepdims=True)
    g_idx = jnp.min(jnp.where(gl == g_max, lane, big), axis=1, keepdims=True) - N_EXPERTS
    g_prob = 1.0 / jnp.sum(jnp.where(is_group, jnp.exp(logit - g_max), 0.0), axis=1, keepdims=True)
    in_group = (lane < N_EXPERTS) & (lane // EXP_PER_GROUP == g_idx)
    el = jnp.where(in_group, logit, NEG_INF)
    v1 = jnp.max(el, axis=1, keepdims=True)
    i1 = jnp.min(jnp.where(el == v1, lane, big), axis=1, keepdims=True)
    el2 = jnp.where(lane == i1, NEG_INF, el)
    v2 = jnp.max(el2, axis=1, keepdims=True)
    i2 = jnp.min(jnp.where(el2 == v2, lane, big), axis=1, keepdims=True)
    e2 = jnp.exp(v2 - v1)
    w1 = g_prob / (1.0 + e2)
    w2 = g_prob * e2 / (1.0 + e2)
    first_low = i1 < i2
    lo = jnp.where(first_low, i1, i2) - g_idx * EXP_PER_GROUP
    hi = jnp.where(first_low, i2, i1) - g_idx * EXP_PER_GROUP
    pair = 3 * lo - ((lo * (lo - 1)) >> 1) + (hi - lo - 1)
    cls = g_idx * N_PAIRS + pair
    w_lo = jnp.where(first_low, w1, w2)
    w_hi = jnp.where(first_low, w2, w1)
    onehot = jnp.where(lane == cls, 1.0, 0.0)
    ri = lax.broadcasted_iota(jnp.int32, (tm, tm), 0)
    ci = lax.broadcasted_iota(jnp.int32, (tm, tm), 1)
    earlier = jnp.where(ci < ri, 1.0, 0.0).astype(BF16)
    before = jnp.dot(earlier, onehot.astype(BF16), preferred_element_type=F32) + carry_s[...]
    rank = jnp.sum(onehot * before, axis=1, keepdims=True)
    carry_s[...] += jnp.sum(onehot, axis=0, keepdims=True)
    counts_ref[...] = carry_s[...].astype(jnp.int32)
    route_ref[0] = jnp.where(lane == 0, cls, jnp.where(lane == 1, rank.astype(jnp.int32), 0))
    gate_bits = pltpu.bitcast(jnp.where(lane == 0, w_lo, jnp.where(lane == 1, w_hi, 0.0)), jnp.int32)
    hp_ref[0] = jnp.concatenate([_pack_bf16_pairs(h2), gate_bits], axis=1)


def _merge(xc, mvec, ys, w_merge, w_branch, w_out, w_route, b_route, n_ctx):
    b, n, d = xc.shape
    tm = _token_tile(n, 640)
    kern = functools.partial(_merge_kernel, tm=tm, n_ctx=n_ctx)
    tok = lambda w: pl.BlockSpec((1, tm, w), lambda bi, i: (bi, i, 0))
    const = lambda shape: pl.BlockSpec(shape, lambda bi, i: (0,) * len(shape))
    return pl.pallas_call(
        kern,
        grid=(b, n // tm),
        in_specs=[tok(d), pl.BlockSpec((1, MOD_ROWS, d), lambda bi, i: (bi, 0, 0)),
                  tok(MIX_W), tok(MIX_W), tok(MIX_W), tok(MIX_W),
                  const(w_merge.shape), const(w_branch.shape), const(w_out.shape),
                  const(w_route.shape), const(b_route.shape)],
        out_specs=[tok(d), tok(d // 2 + LANES), tok(LANES), const((1, LANES))],
        out_shape=[jax.ShapeDtypeStruct((b, n, d), F32),
                   jax.ShapeDtypeStruct((b, n, d // 2 + LANES), jnp.int32),
                   jax.ShapeDtypeStruct((b, n, LANES), jnp.int32),
                   jax.ShapeDtypeStruct((1, LANES), jnp.int32)],
        scratch_shapes=[pltpu.VMEM((1, LANES), F32)],
        compiler_params=_cparams("arbitrary", "arbitrary"),
        name="merge_route",
    )(xc, mvec, *ys, w_merge, w_branch, w_out, w_route, b_route)


def _row_move(table, idx, n_out=None):
    scatter = n_out is not None
    n_idx = idx.shape[0]
    width = table.shape[1]
    per_worker = n_idx // SC_WORKERS
    assert per_worker * SC_WORKERS == n_idx and per_worker % 8 == 0
    chunk = max(c for c in range(8, SC_MAX_CHUNK + 1, 8) if per_worker % c == 0)
    n_chunks = per_worker // chunk
    mesh = plsc.VectorSubcoreMesh(core_axis_name="c", subcore_axis_name="s",
                                  num_cores=SC_CORES, num_subcores=SC_SUBCORES)

    @functools.partial(
        pl.kernel, mesh=mesh,
        out_type=jax.ShapeDtypeStruct((n_out if scatter else n_idx, width), table.dtype),
        scratch_types=[pltpu.VMEM((chunk,), jnp.int32),
                       pltpu.VMEM((chunk, width), table.dtype),
                       pltpu.SemaphoreType.DMA],
        name="sc_row_scatter" if scatter else "sc_row_gather",
    )
    def move(table_hbm, idx_hbm, out_hbm, idx_v, rows_v, sem):
        worker = lax.axis_index("s") * SC_CORES + lax.axis_index("c")
        base = worker * per_worker

        @pl.loop(0, n_chunks)
        def _(j):
            off = pl.multiple_of(base + j * chunk, 8)
            pltpu.sync_copy(idx_hbm.at[pl.ds(off, chunk)], idx_v)
            if scatter:
                pltpu.sync_copy(table_hbm.at[pl.ds(off, chunk)], rows_v)
                pltpu.async_copy(rows_v, out_hbm.at[idx_v], sem).wait()
            else:
                pltpu.async_copy(table_hbm.at[idx_v], rows_v, sem).wait()
                pltpu.sync_copy(rows_v, out_hbm.at[pl.ds(off, chunk)])

    return move(table, idx)


def _ffn(h, w1_ref, w3_ref, w2_ref):
    a = jnp.dot(h, w1_ref[0, 0], preferred_element_type=F32)
    g = jnp.dot(h, w3_ref[0, 0], preferred_element_type=F32)
    return jnp.dot((_silu(a) * g).astype(BF16), w2_ref[0, 0], preferred_element_type=F32)


def _moe_kernel(elo_ref, ehi_ref, valid_ref, xs_ref, w1a_ref, w3a_ref, w2a_ref, w1b_ref, w3b_ref, w2b_ref,
                ys_ref):
    j = pl.program_id(0)
    half = ys_ref.shape[1]

    @pl.when(valid_ref[j] != 0)
    def _():
        hi, lo = _unpack_bf16_pairs(xs_ref[:, :half])
        h = jnp.concatenate([hi.astype(BF16), lo.astype(BF16)], axis=1)
        gates = pltpu.bitcast(xs_ref[:, half:], F32)
        y = (gates[:, 0:1] * _ffn(h, w1a_ref, w3a_ref, w2a_ref)
             + gates[:, 1:2] * _ffn(h, w1b_ref, w3b_ref, w2b_ref))
        ys_ref[...] = _pack_bf16_pairs(y)

    @pl.when(valid_ref[j] == 0)
    def _():
        ys_ref[...] = jnp.zeros_like(ys_ref)


def _moe_experts(xs, tile_elo, tile_ehi, tile_valid, w1, w3, w2, layer):
    n_slots, width = xs.shape
    d = w1.shape[2]
    n_tiles = n_slots // MOE_TILE
    wspec = lambda shape, which: pl.BlockSpec(
        (1, 1) + shape, lambda j, elo, ehi, valid: (layer, (elo, ehi)[which][j], 0, 0))
    grid_spec = pltpu.PrefetchScalarGridSpec(
        num_scalar_prefetch=3,
        grid=(n_tiles,),
        in_specs=[pl.BlockSpec((MOE_TILE, width), lambda j, elo, ehi, valid: (j, 0)),
                  wspec((d, D_EXPERT), 0), wspec((d, D_EXPERT), 0), wspec((D_EXPERT, d), 0),
                  wspec((d, D_EXPERT), 1), wspec((d, D_EXPERT), 1), wspec((D_EXPERT, d), 1)],
        out_specs=pl.BlockSpec((MOE_TILE, d // 2), lambda j, elo, ehi, valid: (j, 0)),
    )
    return pl.pallas_call(
        _moe_kernel,
        grid_spec=grid_spec,
        out_shape=jax.ShapeDtypeStruct((n_slots, d // 2), jnp.int32),
        compiler_params=_cparams("arbitrary"),
        name="moe_experts",
    )(tile_elo, tile_ehi, tile_valid, xs, w1, w3, w2, w1, w3, w2)


def _moe_combine_kernel(x_ref, m_ref, y_ref, *rest, tm, n_ctx, row0):
    o_ref = rest[-1]
    row = row0 + pl.program_id(1) * tm + lax.broadcasted_iota(jnp.int32, (tm, 1), 0)
    hi, lo = _unpack_bf16_pairs(y_ref[0])
    x = x_ref[0] + _pick(m_ref[0], row < n_ctx, 5) * jnp.concatenate([hi, lo], axis=1)
    if len(rest) == 2:
        ms = jnp.mean(x * x, axis=-1, keepdims=True)
        x = x * lax.rsqrt(ms + EPS) * rest[0][...]
    o_ref[0] = x


def _moe_combine(x1, mvec, yt, n_ctx, final_w=None):
    b, n, d = x1.shape
    if final_w is None:
        tm, skip, n_out, extra, extra_specs = _token_tile(n, 1100), 0, n, (), []
    else:
        tm = _token_tile(n_ctx, 1024)
        skip, n_out = n_ctx // tm, n - n_ctx
        extra, extra_specs = (final_w.reshape(1, d),), [pl.BlockSpec((1, d), lambda bi, i: (0, 0))]
    kern = functools.partial(_moe_combine_kernel, tm=tm, n_ctx=n_ctx, row0=skip * tm)
    tok = lambda w: pl.BlockSpec((1, tm, w), lambda bi, i: (bi, i + skip, 0))
    return pl.pallas_call(
        kern,
        grid=(b, n_out // tm),
        in_specs=[tok(d), pl.BlockSpec((1, MOD_ROWS, d), lambda bi, i: (bi, 0, 0)), tok(d // 2)] + extra_specs,
        out_specs=pl.BlockSpec((1, tm, d), lambda bi, i: (bi, i, 0)),
        out_shape=jax.ShapeDtypeStruct((b, n_out, d), F32),
        compiler_params=_cparams("parallel", "parallel"),
        name="moe_combine",
    )(x1, mvec, yt, *extra)


def _moe(x1, mvec, hp, route, counts, w1, w3, w2, layer, n_ctx, final_w=None):
    b, n, d = x1.shape
    m = b * n
    n_slots = m + N_CLASSES * MOE_TILE
    cls = route[..., 0].reshape(m)
    rank = route[..., 1].reshape(m)
    cnt = counts[0, :N_CLASSES]
    padded = (cnt + MOE_TILE - 1) // MOE_TILE * MOE_TILE
    ends = jnp.cumsum(padded)
    pos = (ends - padded)[cls] + rank
    tile_start = jnp.arange(n_slots // MOE_TILE, dtype=jnp.int32) * MOE_TILE
    tile_cls = jnp.minimum(jnp.searchsorted(ends, tile_start, side="right"), N_CLASSES - 1).astype(jnp.int32)
    tile_valid = (tile_start < ends[-1]).astype(jnp.int32)
    pair_lo = jnp.asarray([0, 0, 0, 1, 1, 2], jnp.int32)
    pair_hi = jnp.asarray([1, 2, 3, 2, 3, 3], jnp.int32)
    group = tile_cls // N_PAIRS
    tile_elo = group * EXP_PER_GROUP + pair_lo[tile_cls % N_PAIRS]
    tile_ehi = group * EXP_PER_GROUP + pair_hi[tile_cls % N_PAIRS]

    xs = _row_move(hp.reshape(m, hp.shape[-1]), pos, n_out=n_slots)
    ys = _moe_experts(xs, tile_elo, tile_ehi, tile_valid, w1, w3, w2, layer)
    yt = _row_move(ys, pos)
    return _moe_combine(x1, mvec, yt.reshape(b, n, d // 2), n_ctx, final_w)


def _rope_tables(n_lat, n_ctx):
    rows = n_lat // GRID_W
    row = jnp.repeat(jnp.arange(rows), GRID_W).astype(F32)
    col = jnp.tile(jnp.arange(GRID_W), rows).astype(F32)
    nq = HEAD_DIM // 4
    inv = jnp.power(ROPE_BASE, -jnp.arange(nq, dtype=F32) / nq)
    ang = jnp.concatenate([row[:, None] * inv, col[:, None] * inv], -1)
    cos = jnp.cos(ang)
    sin = jnp.sin(ang)
    cos_h = jnp.concatenate([cos, cos], -1)
    sin_h = jnp.concatenate([-sin, sin], -1)
    cos_full = jnp.concatenate([jnp.ones((n_ctx, HEAD_DIM), F32), cos_h], 0)
    sin_full = jnp.concatenate([jnp.zeros((n_ctx, HEAD_DIM), F32), sin_h], 0)
    return jnp.tile(cos_full, (1, N_HEADS)), jnp.tile(sin_full, (1, N_HEADS))


def _block_diag_heads(w):
    out = jnp.zeros((MIX_W, MIX_W), w.dtype)
    for h in range(N_HEADS):
        out = out.at[h * HEAD_DIM:(h + 1) * HEAD_DIM, h * HEAD_DIM:(h + 1) * HEAD_DIM].set(w[h])
    return out


def kernel(x, c, ctx, c_ctx, w_mod, b_mod, w_in, a_conv_w, a_conv_b, a_gate_w, a_gate_b, a_lambda, b_theta,
           c_conv_w, c_conv_b, c_gate_b, d_lb, w_branch, w_out, moe_w_group, moe_b_group, moe_w_router,
           moe_b_router, moe_w1, moe_w3, moe_w2, final_norm_w):
    bsz, n_lat, d = x.shape
    n_ctx = ctx.shape[1]
    depth = w_mod.shape[0]
    assert n_ctx % CHUNK == 0 and n_lat % CHUNK == 0 and n_ctx % 256 == 0 and n_lat % 256 == 0

    xc = jnp.concatenate([ctx, x], axis=1)
    cos, sin = _rope_tables(n_lat, n_ctx)

    cc = jnp.zeros((8, d), F32).at[:bsz].set(c).at[bsz].set(c_ctx)
    mod = _mod_vectors(cc, w_mod, b_mod)
    mx = mod[:, :bsz].reshape(depth, bsz, N_MOD, d)
    mc = jnp.broadcast_to(mod[:, bsz].reshape(depth, 1, N_MOD, d), (depth, bsz, N_MOD, d))
    pad = jnp.zeros((depth, bsz, 8 - N_MOD, d), F32)
    mvec = jnp.concatenate([mx, pad, mc, pad], axis=2)

    lbs = jnp.cumsum(jax.nn.softmax(d_lb.astype(F32), axis=0), axis=0)
    lbs = lbs - lbs[0]

    n_mix_cols = 15 * MIX_W
    gate0 = 10 * MIX_W
    n_gate = 4 * N_HEADS
    w1b = moe_w1.astype(BF16)
    w3b = moe_w3.astype(BF16)
    w2b = moe_w2.astype(BF16)

    for l in range(depth):
        wl = w_in[l]
        w_mix = jnp.concatenate([wl[:, :gate0], wl[:, gate0 + n_gate:n_mix_cols + n_gate]], axis=1).astype(BF16)
        w_gate = jnp.pad(wl[:, gate0:gate0 + n_gate], ((0, 0), (0, LANES - n_gate))).astype(BF16)
        w_merge = wl[:, n_mix_cols + n_gate:].astype(BF16)
        p, gates_c = _in_proj(xc, mvec[l], w_mix, w_gate, cos, sin, n_ctx)

        gw = jnp.concatenate([_block_diag_heads(a_gate_w[l, dd, j]) for dd in range(2) for j in range(2)],
                             axis=1).astype(BF16)
        gb = a_gate_b[l].reshape(1, 4 * MIX_W)
        ya = _mixer_a(p, a_conv_w[l], a_conv_b[l].reshape(1, MIX_W), gw, gb, a_lambda[l], n_ctx)

        log_gamma = jnp.repeat(jax.nn.log_sigmoid(b_theta[l].astype(F32)), HEAD_DIM, axis=1)
        yb = _mixer_b(p, log_gamma, n_ctx)

        gate_b = jnp.pad(c_gate_b[l].reshape(1, n_gate), ((0, 0), (0, LANES - n_gate)))
        yc = _mixer_c(p, gates_c, c_conv_w[l], c_conv_b[l].reshape(1, 2 * MIX_W), gate_b, n_ctx)

        yd = _mixer_d(p, lbs[l].reshape(1, MIX_W), n_ctx)

        w_route = jnp.pad(jnp.concatenate([moe_w_router[l], moe_w_group[l]], axis=1),
                          ((0, 0), (0, LANES - N_EXPERTS - N_GROUPS)))
        b_route = jnp.pad(jnp.concatenate([moe_b_router[l], moe_b_group[l]]),
                          (0, LANES - N_EXPERTS - N_GROUPS)).reshape(1, LANES)
        x1, hp, route, counts = _merge(xc, mvec[l], (ya, yb, yc, yd), w_merge, w_branch[l].astype(BF16),
                                       w_out[l].astype(BF16), w_route, b_route, n_ctx)
        xc = _moe(x1, mvec[l], hp, route, counts, w1b, w3b, w2b, l, n_ctx,
                  final_w=final_norm_w if l == depth - 1 else None)

    return xc
```

```python
import functools

import jax
import jax.numpy as jnp
import numpy as np
from jax import lax
from jax.experimental import pallas as pl
from jax.experimental.pallas import tpu as pltpu
from jax.experimental.pallas import tpu_sc as plsc

F32 = jnp.float32
BF16 = jnp.bfloat16

EPS = 1e-6
N_HEADS = 4
HEAD_DIM = 64
MIX_W = N_HEADS * HEAD_DIM
N_BRANCH = 4
CONV_W = 4
LRU_C = 8.0
GRID_W = 64
ROPE_BASE = 10000.0
N_GROUPS = 4
EXP_PER_GROUP = 4
N_EXPERTS = N_GROUPS * EXP_PER_GROUP
D_EXPERT = 512
N_MOD = 6
M_INIT = -1e30

CHUNK = 128
SUB = 16
HALO = 16
LANES = 128
MOD_ROWS = 16
ROWS = 1
VMEM_LIMIT_BYTES = 56 * 1024 * 1024
N_PAIRS = 6
N_CLASSES = N_GROUPS * N_PAIRS
MOE_TILE = 256
SC_CORES = 2
SC_SUBCORES = 16
SC_WORKERS = SC_CORES * SC_SUBCORES
SC_MAX_CHUNK = 32
NEG_INF = float("-inf")
LOG2_E = 1.4426950408889634


def _cparams(*sem):
    return pltpu.CompilerParams(dimension_semantics=sem, vmem_limit_bytes=VMEM_LIMIT_BYTES)


def _token_tile(n, cap):
    best = 16
    for t in range(16, cap + 1, 16):
        if n % t == 0:
            best = t
    return best


def _modulate(x, shift, scale):
    ms = jnp.mean(x * x, axis=-1, keepdims=True)
    return x * lax.rsqrt(ms + EPS) * (1.0 + scale) + shift


def _pick(m, is_ctx, k):
    return jnp.where(is_ctx, m[8 + k:9 + k], m[k:k + 1])


def _sigmoid(x):
    return 0.5 * jnp.tanh(0.5 * x) + 0.5


def _silu(x):
    return x * _sigmoid(x)


def _log_sigmoid(x):
    return jnp.minimum(x, 0.0) - jnp.log(1.0 + jnp.exp(-jnp.abs(x)))


def _split3(x):
    hi = x.astype(BF16)
    r = x - hi.astype(F32)
    mid = r.astype(BF16)
    lo = (r - mid.astype(F32)).astype(BF16)
    return hi, mid, lo


def _sel_dot(sel, x):
    hi, mid, lo = _split3(x)
    d = functools.partial(jnp.dot, preferred_element_type=F32)
    return d(sel, hi) + d(sel, mid) + d(sel, lo)


def _dot_sel(x, sel):
    hi, mid, lo = _split3(x)
    d = functools.partial(jnp.dot, preferred_element_type=F32)
    return d(hi, sel) + d(mid, sel) + d(lo, sel)


def _dot_nt(a, b):
    return lax.dot_general(a, b, (((1,), (1,)), ((), ())), preferred_element_type=F32)


def _dot_tn(a, b):
    return lax.dot_general(a, b, (((0,), (0,)), ((), ())), preferred_element_type=F32)


def _head_of_lane(width):
    return lax.broadcasted_iota(jnp.int32, (1, width), 1) // HEAD_DIM


def _block_ones(width):
    r = lax.broadcasted_iota(jnp.int32, (width, width), 0) // HEAD_DIM
    c = lax.broadcasted_iota(jnp.int32, (width, width), 1) // HEAD_DIM
    return r == c


def _head_norm(o, center):
    ones = jnp.where(_block_ones(MIX_W), 1.0, 0.0).astype(BF16)
    inv = 1.0 / HEAD_DIM
    if center:
        o = o - _dot_sel(o, ones) * inv
    var = _dot_sel(o * o, ones) * inv
    return o * lax.rsqrt(var + EPS)


def _bwd_chunk(s, n_ctx_chunks, n_chunks):
    return jnp.where(s < n_ctx_chunks, n_ctx_chunks - 1 - s, n_chunks - 1 + n_ctx_chunks - s)


def _conv_chunk(ref, c, w, b, *, n, n_ctx, width0=0, width=None):
    L = CHUNK
    wl = L + 2 * HALO
    r0 = c * L
    start = pl.multiple_of(jnp.clip(r0 - HALO, 0, n - wl), HALO)
    off = r0 - start
    if width is None:
        win = ref[0, pl.ds(start, wl), :]
    else:
        win = ref[0, pl.ds(start, wl), width0:width0 + width]
    ri = lax.broadcasted_iota(jnp.int32, (L, wl), 0)
    mi = lax.broadcasted_iota(jnp.int32, (L, wl), 1)
    g = r0 + ri
    y = b
    seg_g = jnp.where(g < n_ctx, 0, 1)
    for k in range(CONV_W):
        src = g + (k - 2)
        hit = (mi == ri + off + (k - 2)) & (jnp.where(src < n_ctx, 0, 1) == seg_g) & (src >= 0) & (src < n)
        sel = jnp.where(hit, 1.0, 0.0).astype(BF16)
        y = y + w[k:k + 1] * jnp.dot(sel, win, preferred_element_type=F32)
    return y


def _mod_kernel(cc_ref, w_ref, b_ref, o_ref):
    s = _silu(cc_ref[...])
    o_ref[0] = jnp.dot(s, w_ref[0], precision=lax.Precision.HIGHEST, preferred_element_type=F32) + b_ref[0]


def _mod_vectors(cc, w_mod, b_mod):
    depth, d, dm = w_mod.shape
    tn = 1536
    return pl.pallas_call(
        _mod_kernel,
        grid=(depth, dm // tn),
        in_specs=[pl.BlockSpec((8, d), lambda l, j: (0, 0)),
                  pl.BlockSpec((1, d, tn), lambda l, j: (l, 0, j)),
                  pl.BlockSpec((1, 1, tn), lambda l, j: (l, 0, j))],
        out_specs=pl.BlockSpec((1, 8, tn), lambda l, j: (l, 0, j)),
        out_shape=jax.ShapeDtypeStruct((depth, 8, dm), F32),
        compiler_params=_cparams("parallel", "parallel"),
        name="mod_vectors",
    )(cc, w_mod, b_mod.reshape(depth, 1, dm))


def _rope(t, cos, sin):
    lane = lax.broadcasted_iota(jnp.int32, (1, MIX_W), 1) % HEAD_DIM
    half = HEAD_DIM // 2
    swapped = jnp.where(lane < half, pltpu.roll(t, MIX_W - half, 1), pltpu.roll(t, half, 1))
    return t * cos + swapped * sin


RET_Q_COL = 2 * MIX_W
RET_K_COL = 3 * MIX_W


def _in_proj_kernel(x_ref, m_ref, w_ref, wg_ref, cos_ref, sin_ref, p_ref, g_ref, *, tm, n_ctx, col_chunk):
    i = pl.program_id(1)
    row = i * tm + lax.broadcasted_iota(jnp.int32, (tm, 1), 0)
    is_ctx = row < n_ctx
    m = m_ref[0]
    h = _modulate(x_ref[0], _pick(m, is_ctx, 0), _pick(m, is_ctx, 1)).astype(BF16)
    for j in range(w_ref.shape[1] // col_chunk):
        sl = slice(j * col_chunk, (j + 1) * col_chunk)
        t = jnp.dot(h, w_ref[:, sl], preferred_element_type=F32)
        if j * col_chunk == RET_Q_COL:
            t = _rope(t, cos_ref[...], sin_ref[...])
        elif j * col_chunk == RET_K_COL:
            t = _rope(t, cos_ref[...], sin_ref[...]) * HEAD_DIM ** -0.5
        p_ref[0, :, sl] = t.astype(BF16)
    g_ref[0] = jnp.dot(h, wg_ref[...], preferred_element_type=F32)


def _in_proj(xc, mvec, w_mix, w_gate, cos, sin, n_ctx):
    b, n, d = xc.shape
    tm = _token_tile(n, 640)
    wc = w_mix.shape[1]
    kern = functools.partial(_in_proj_kernel, tm=tm, n_ctx=n_ctx, col_chunk=MIX_W)
    return pl.pallas_call(
        kern,
        grid=(b, n // tm),
        in_specs=[pl.BlockSpec((1, tm, d), lambda bi, i: (bi, i, 0)),
                  pl.BlockSpec((1, MOD_ROWS, d), lambda bi, i: (bi, 0, 0)),
                  pl.BlockSpec((d, wc), lambda bi, i: (0, 0)),
                  pl.BlockSpec((d, LANES), lambda bi, i: (0, 0)),
                  pl.BlockSpec((tm, MIX_W), lambda bi, i: (i, 0)),
                  pl.BlockSpec((tm, MIX_W), lambda bi, i: (i, 0))],
        out_specs=[pl.BlockSpec((1, tm, wc), lambda bi, i: (bi, i, 0)),
                   pl.BlockSpec((1, tm, LANES), lambda bi, i: (bi, i, 0))],
        out_shape=[jax.ShapeDtypeStruct((b, n, wc), BF16),
                   jax.ShapeDtypeStruct((b, n, LANES), F32)],
        compiler_params=_cparams("parallel", "parallel"),
        name="in_proj",
    )(xc, mvec, w_mix, w_gate, cos, sin)


def _lin_scan(a, x, carry, reverse):
    L = a.shape[0]
    row = lax.broadcasted_iota(jnp.int32, (L, 1), 0)
    k = 1
    while k < L:
        if reverse:
            a_s = pltpu.roll(a, L - k, 0)
            x_s = pltpu.roll(x, L - k, 0)
            valid = row < L - k
        else:
            a_s = pltpu.roll(a, k, 0)
            x_s = pltpu.roll(x, k, 0)
            valid = row >= k
        x = jnp.where(valid, a * x_s + x, x)
        a = jnp.where(valid, a * a_s, a)
        k *= 2
    h = x + a * carry
    return h, (h[0:1] if reverse else h[L - 1:L])


def _mixer_a_kernel(ax_ref, ag_ref, cw_ref, cb_ref, gw_ref, gb_ref, lam_ref, out_ref, u_s, hf_s,
                    *, n, n_ctx):
    L = CHUNK
    n_chunks = n // L
    n_ctx_chunks = n_ctx // L
    cw = cw_ref[...]
    cb = cb_ref[...]
    log_lam = _log_sigmoid(lam_ref[...])

    def direction(d, c, carry):
        r0 = pl.multiple_of(c * L, L)
        u = u_s[pl.ds(r0, L), :]
        pre = jnp.dot(u.astype(BF16), gw_ref[:, d * 2 * MIX_W:(d + 1) * 2 * MIX_W],
                      preferred_element_type=F32) + gb_ref[:, d * 2 * MIX_W:(d + 1) * 2 * MIX_W]
        r = _sigmoid(pre[:, :MIX_W])
        gi = _sigmoid(pre[:, MIX_W:])
        log_a = LRU_C * r * log_lam[d:d + 1]
        a = jnp.exp(log_a)
        inp = jnp.sqrt(1.0 - jnp.exp(2.0 * log_a)) * (gi * u)
        h, new_carry = _lin_scan(a, inp, carry, reverse=(d == 1))
        return r0, h, new_carry

    def conv_body(c, _):
        r0 = pl.multiple_of(c * L, L)
        u_s[pl.ds(r0, L), :] = _conv_chunk(ax_ref, c, cw, cb, n=n, n_ctx=n_ctx)
        return 0

    lax.fori_loop(0, n_chunks, conv_body, 0)

    def walk_body(s, carries):
        r0, h, carry_f = direction(0, s, carries[0])
        hf_s[0, pl.ds(r0, L), :] = h
        r0, h, carry_b = direction(1, _bwd_chunk(s, n_ctx_chunks, n_chunks), carries[1])
        hf_s[1, pl.ds(r0, L), :] = h
        return carry_f, carry_b

    zero = jnp.zeros((1, MIX_W), F32)
    lax.fori_loop(0, n_chunks, walk_body, (zero, zero))

    def out_body(c, _):
        rows = pl.ds(pl.multiple_of(c * L, L), L)
        gate = jax.nn.gelu(ag_ref[0, rows, :].astype(F32), approximate=True)
        out_ref[0, rows, :] = (gate * (hf_s[0, rows, :] + hf_s[1, rows, :])).astype(out_ref.dtype)
        return 0

    lax.fori_loop(0, n_chunks, out_body, 0)


def _seq_spec(n, col, rows=1):
    if rows == 1:
        return pl.BlockSpec((1, n, MIX_W), lambda b: (b, 0, col))
    return pl.BlockSpec((rows, n, MIX_W), lambda b: (b, 0, col), pipeline_mode=pl.Buffered(1))


def _full_spec(shape):
    return pl.BlockSpec(shape, lambda b: (0,) * len(shape))


def _mixer_a(p, conv_w, conv_b, gate_w, gate_b, lam, n_ctx):
    b, n, _ = p.shape
    kern = functools.partial(_mixer_a_kernel, n=n, n_ctx=n_ctx)
    return pl.pallas_call(
        kern,
        grid=(b,),
        in_specs=[_seq_spec(n, 0), _seq_spec(n, 1),
                  _full_spec(conv_w.shape), _full_spec(conv_b.shape),
                  _full_spec(gate_w.shape), _full_spec(gate_b.shape), _full_spec(lam.shape)],
        out_specs=pl.BlockSpec((1, n, MIX_W), lambda bi: (bi, 0, 0)),
        out_shape=jax.ShapeDtypeStruct((b, n, MIX_W), BF16),
        scratch_shapes=[pltpu.VMEM((n, MIX_W), F32), pltpu.VMEM((2, n, MIX_W), F32)],
        compiler_params=_cparams("parallel"),
        name="mixer_rglru",
    )(p, p, conv_w, conv_b, gate_w, gate_b, lam)


def _mixer_b_kernel(q_ref, k_ref, v_ref, g_ref, lg_ref, out_ref, oi_s, dec_s, st_s, *, n, n_ctx):
    L = CHUNK
    n_chunks = n // L
    n_ctx_chunks = n_ctx // L
    head = _head_of_lane(MIX_W)
    bd = _block_ones(MIX_W)
    lgf = lg_ref[0:1, :]
    lgb = lg_ref[1:2, :]
    pos = lax.broadcasted_iota(jnp.int32, (L, 1), 0).astype(F32)
    diff = (lax.broadcasted_iota(jnp.int32, (L, L), 0) - lax.broadcasted_iota(jnp.int32, (L, L), 1)).astype(F32)
    for h in range(N_HEADS):
        lf = lgf[:, h * HEAD_DIM:h * HEAD_DIM + 1]
        lb = lgb[:, h * HEAD_DIM:h * HEAD_DIM + 1]
        dec_s[h] = jnp.where(diff >= 0, jnp.exp(lf * jnp.maximum(diff, 0.0)), jnp.exp(lb * jnp.maximum(-diff, 0.0)))

    qw = (jnp.exp(lgf * (pos + 1.0)), jnp.exp(lgb * (float(L) - pos)))
    kw = (jnp.exp(lgf * (float(L) - 1.0 - pos)), jnp.exp(lgb * pos))
    g_chunk = (jnp.exp(lgf * float(L)), jnp.exp(lgb * float(L)))

    def state_step(d, c):
        rows = pl.ds(pl.multiple_of(c * L, L), L)
        q = q_ref[0, rows, :].astype(F32)
        k = k_ref[0, rows, :].astype(F32)
        st = st_s[d]
        oi_s[d, rows, :] = jnp.dot((q * qw[d]).astype(BF16), st.astype(BF16), preferred_element_type=F32)
        kv = _dot_tn((k * kw[d]).astype(BF16), v_ref[0, rows, :])
        st_s[d] = g_chunk[d] * st + jnp.where(bd, kv, 0.0)

    st_s[...] = jnp.zeros_like(st_s)

    def walk_body(s, _):
        state_step(0, s)
        state_step(1, _bwd_chunk(s, n_ctx_chunks, n_chunks))
        return 0

    lax.fori_loop(0, n_chunks, walk_body, 0)

    def out_body(c, _):
        rows = pl.ds(pl.multiple_of(c * L, L), L)
        q = q_ref[0, rows, :]
        k = k_ref[0, rows, :]
        v = v_ref[0, rows, :]
        scs, vals = [], []
        for h in range(N_HEADS):
            hm = head == h
            scs.append((_dot_nt(jnp.where(hm, q, jnp.zeros_like(q)), k) * dec_s[h]).astype(BF16))
            vals.append(jnp.where(hm, v, jnp.zeros_like(v)))
        o = (oi_s[0, rows, :] + oi_s[1, rows, :]
             + jnp.dot(jnp.concatenate(scs, axis=1), jnp.concatenate(vals, axis=0), preferred_element_type=F32))
        gate = _silu(g_ref[0, rows, :].astype(F32))
        out_ref[0, rows, :] = (_head_norm(o, True) * gate).astype(out_ref.dtype)
        return 0

    lax.fori_loop(0, n_chunks, out_body, 0, unroll=2)


def _mixer_b(p, log_gamma, n_ctx):
    b, n, _ = p.shape
    kern = functools.partial(_mixer_b_kernel, n=n, n_ctx=n_ctx)
    return pl.pallas_call(
        kern,
        grid=(b,),
        in_specs=[_seq_spec(n, 2), _seq_spec(n, 3), _seq_spec(n, 4), _seq_spec(n, 5),
                  _full_spec(log_gamma.shape)],
        out_specs=pl.BlockSpec((1, n, MIX_W), lambda bi: (bi, 0, 0)),
        out_shape=jax.ShapeDtypeStruct((b, n, MIX_W), BF16),
        scratch_shapes=[pltpu.VMEM((2, n, MIX_W), F32),
                        pltpu.VMEM((N_HEADS, CHUNK, CHUNK), F32),
                        pltpu.VMEM((2, MIX_W, MIX_W), F32)],
        compiler_params=_cparams("parallel"),
        name="mixer_retention",
    )(p, p, p, p, log_gamma)


def _mixer_c_kernel(q_ref, k_ref, v_ref, o_ref, g_ref, cw_ref, cb_ref, gb_ref, out_ref,
                    qs_s, ks_s, hf_s, c_s, n_s, m_s, *, n, n_ctx):
    L = CHUNK
    n_chunks = n // L
    n_ctx_chunks = n_ctx // L
    head = _head_of_lane(MIX_W)
    bd = _block_ones(MIX_W)
    ones_bd = jnp.where(bd, 1.0, 0.0).astype(BF16)
    cw = cw_ref[...]
    cb = cb_ref[...]
    ri = lax.broadcasted_iota(jnp.int32, (L, L), 0)
    ci = lax.broadcasted_iota(jnp.int32, (L, L), 1)
    gl = lax.broadcasted_iota(jnp.int32, (LANES, MIX_W), 0)
    hl = lax.broadcasted_iota(jnp.int32, (LANES, MIX_W), 1) // HEAD_DIM

    def expand(kind):
        return jnp.where(gl == kind * N_HEADS + hl, 1.0, 0.0).astype(BF16)

    def conv_body(c, _):
        rows = pl.ds(pl.multiple_of(c * L, L), L)
        qs_s[rows, :] = _silu(_conv_chunk(q_ref, c, cw[:, :MIX_W], cb[:, :MIX_W], n=n, n_ctx=n_ctx)).astype(BF16)
        ks_s[rows, :] = (_silu(_conv_chunk(k_ref, c, cw[:, MIX_W:], cb[:, MIX_W:], n=n, n_ctx=n_ctx))
                         * HEAD_DIM ** -0.5).astype(BF16)
        return 0

    lax.fori_loop(0, n_chunks, conv_body, 0)

    def chunk(d, c):
        rev = d == 1
        rows = pl.ds(pl.multiple_of(c * L, L), L)
        q = qs_s[rows, :]
        k = ks_s[rows, :]
        v = v_ref[0, rows, :]
        g = g_ref[0, rows, :] + gb_ref[...]
        log_i = _dot_sel(g, expand(2 * d))
        log_f = _dot_sel(_log_sigmoid(g), expand(2 * d + 1))
        causal = (ci >= ri) if rev else (ci <= ri)
        tri = jnp.where(causal, 1.0, 0.0).astype(BF16)
        cum = _sel_dot(tri, log_f)
        cum_end = cum[0:1] if rev else cum[L - 1:L]
        m_prev = m_s[d]
        c_prev = c_s[d]
        n_prev = n_s[d]
        row_src = (log_i - cum).T
        m_inter = cum + m_prev
        num_inter = jnp.dot(q, c_prev.astype(BF16), preferred_element_type=F32)
        qn = _dot_sel(q.astype(F32) * n_prev, ones_bd)
        m_q = jnp.zeros((L, MIX_W), F32)
        scs, rhs = [], []
        for h in range(N_HEADS):
            hm = head == h
            lane0 = h * HEAD_DIM
            log_d = jnp.where(causal, cum[:, lane0:lane0 + 1] + row_src[lane0:lane0 + 1, :], NEG_INF)
            m_h = jnp.maximum(jnp.max(log_d, axis=1, keepdims=True), m_inter[:, lane0:lane0 + 1])
            sc = _dot_nt(jnp.where(hm, q, jnp.zeros_like(q)), k) * jnp.exp(log_d - m_h)
            scs.append(sc.astype(BF16))
            rhs.append(jnp.concatenate([jnp.where(hm, v, jnp.zeros_like(v)),
                                        jnp.broadcast_to(jnp.where(hm, 1.0, 0.0).astype(BF16), (L, MIX_W))], axis=1))
            m_q = jnp.where(hm, m_h, m_q)
        intra = jnp.dot(jnp.concatenate(scs, axis=1), jnp.concatenate(rhs, axis=0), preferred_element_type=F32)
        s_inter = jnp.exp(m_inter - m_q)
        num = intra[:, :MIX_W] + s_inter * num_inter
        den = intra[:, MIX_W:] + s_inter * qn
        hid = num / jnp.maximum(jnp.abs(den), jnp.exp(-m_q))
        log_w = cum_end - cum + log_i
        m_loc = jnp.max(log_w, axis=0, keepdims=True)
        kw = k.astype(F32) * jnp.exp(log_w - m_loc)
        c_loc = jnp.where(bd, _dot_tn(kw.astype(BF16), v), 0.0)
        n_loc = jnp.sum(kw, axis=0, keepdims=True)
        m_new = jnp.maximum(cum_end + m_prev, m_loc)
        s_old = jnp.exp(cum_end + m_prev - m_new)
        s_loc = jnp.exp(m_loc - m_new)
        c_s[d] = s_old * c_prev + s_loc * c_loc
        n_s[d] = s_old * n_prev + s_loc * n_loc
        m_s[d] = m_new
        return rows, hid

    c_s[...] = jnp.zeros_like(c_s)
    n_s[...] = jnp.zeros_like(n_s)
    m_s[...] = jnp.full(m_s.shape, M_INIT, F32)

    def walk_body(s, _):
        rows, hid = chunk(0, s)
        hf_s[0, rows, :] = hid
        rows, hid = chunk(1, _bwd_chunk(s, n_ctx_chunks, n_chunks))
        hf_s[1, rows, :] = hid
        return 0

    lax.fori_loop(0, n_chunks, walk_body, 0)

    def out_body(c, _):
        rows = pl.ds(pl.multiple_of(c * L, L), L)
        gate = _sigmoid(o_ref[0, rows, :].astype(F32))
        out_ref[0, rows, :] = (gate * _head_norm(hf_s[0, rows, :] + hf_s[1, rows, :], True)).astype(out_ref.dtype)
        return 0

    lax.fori_loop(0, n_chunks, out_body, 0)


def _mixer_c(p, gates, conv_w, conv_b, gate_b, n_ctx):
    b, n, _ = p.shape
    kern = functools.partial(_mixer_c_kernel, n=n, n_ctx=n_ctx)
    return pl.pallas_call(
        kern,
        grid=(b,),
        in_specs=[_seq_spec(n, 6), _seq_spec(n, 7), _seq_spec(n, 8), _seq_spec(n, 9),
                  pl.BlockSpec((1, n, LANES), lambda bi: (bi, 0, 0)),
                  _full_spec(conv_w.shape), _full_spec(conv_b.shape), _full_spec(gate_b.shape)],
        out_specs=pl.BlockSpec((1, n, MIX_W), lambda bi: (bi, 0, 0)),
        out_shape=jax.ShapeDtypeStruct((b, n, MIX_W), BF16),
        scratch_shapes=[pltpu.VMEM((n, MIX_W), BF16), pltpu.VMEM((n, MIX_W), BF16),
                        pltpu.VMEM((2, n, MIX_W), F32),
                        pltpu.VMEM((2, MIX_W, MIX_W), F32),
                        pltpu.VMEM((2, 1, MIX_W), F32),
                        pltpu.VMEM((2, 1, MIX_W), F32)],
        compiler_params=_cparams("parallel"),
        name="mixer_mlstm",
    )(p, p, p, p, gates, conv_w, conv_b, gate_b)


def _hgrn2_tables():
    L, S = CHUNK, SUB
    r = np.arange(L)
    same = (r[:, None] // S) == (r[None, :] // S)
    tri = np.stack([same & (r[None, :] <= r[:, None]), same & (r[None, :] >= r[:, None])])
    fold = []
    for rev in (False, True):
        pi = np.concatenate([np.full(len(_pair_rows(i, rev)), i) for i in range(S)])
        pj = np.concatenate([_pair_rows(i, rev) for i in range(S)])
        visible = (pj >= pi) if rev else (pj <= pi)
        fold.append((np.arange(S)[:, None] == pi[None, :]) & visible[None, :])
    as_bf16 = lambda a: jnp.asarray(a.astype(np.float32), dtype=BF16)
    return as_bf16(tri), as_bf16(same), as_bf16(np.stack(fold))


def _pair_rows(i, rev):
    half = SUB // 2
    if rev:
        return np.arange(0, SUB) if i < half else np.arange(half, SUB)
    return np.arange(0, half) if i < half else np.arange(0, SUB)


def _mixer_d_kernel(q_ref, ff_ref, fb_ref, v_ref, g_ref, lb_ref, tri_ref, same_ref, fold_ref, out_ref,
                    o_s, st_s, *, n, n_ctx):
    L = CHUNK
    S = SUB
    nb = L // S
    n_chunks = n // L
    n_ctx_chunks = n_ctx // L
    lb = lb_ref[...]
    bd = _block_ones(MIX_W)
    ones_bd = jnp.where(bd, 1.0, 0.0).astype(BF16)

    def chunk(r, d, c):
        rev = d == 1
        rows = pl.ds(pl.multiple_of(c * L, L), L)
        q = _silu(q_ref[r, rows, :].astype(F32))
        raw = (fb_ref if rev else ff_ref)[r, rows, :].astype(F32)
        v = v_ref[r, rows, :]
        vf = v.astype(F32)
        f = lb + (1.0 - lb) * _sigmoid(raw)
        k = 1.0 - f
        log_f = jnp.log(f)
        cum = _sel_dot(tri_ref[d], log_f)
        tot = _sel_dot(same_ref[...], log_f)
        qt = (q * jnp.exp(cum)).astype(BF16)
        kt = (k * jnp.exp(tot - cum)).astype(BF16)
        g = jnp.exp(tot)
        cum2 = cum * LOG2_E
        prod, vj, scores, intra = {}, {}, {}, {}
        for a in range(nb + 2):
            if a < nb:
                sl = slice(a * S, (a + 1) * S)
                cb, qb, kb, vb = cum2[sl], q[sl], k[sl], vf[sl]
                prods, v_js = [], []
                for i in range(S):
                    js = _pair_rows(i, rev)
                    jsl = slice(int(js[0]), int(js[-1]) + 1)
                    dec = jnp.exp2(jnp.minimum(cb[i:i + 1] - cb[jsl], 0.0))
                    prods.append(qb[i:i + 1] * kb[jsl] * dec)
                    v_js.append(vb[jsl])
                prod[a] = jnp.concatenate(prods, axis=0).astype(BF16)
                vj[a] = jnp.concatenate(v_js, axis=0)
            if 1 <= a <= nb:
                scores[a - 1] = jnp.dot(prod.pop(a - 1), ones_bd, preferred_element_type=F32)
            if a >= 2:
                weighted = (scores.pop(a - 2) * vj.pop(a - 2)).astype(BF16)
                intra[a - 2] = jnp.dot(fold_ref[d], weighted, preferred_element_type=F32)
        o = jnp.concatenate([intra[a] for a in range(nb)], axis=0)
        st = st_s[r, d]
        inter = [None] * nb
        for a in (range(nb - 1, -1, -1) if rev else range(nb)):
            sl = slice(a * S, (a + 1) * S)
            inter[a] = _dot_nt(qt[sl], st.astype(BF16))
            st = st * g[a * S:a * S + 1] + jnp.where(bd, _dot_tn(v[sl], kt[sl]), 0.0)
        st_s[r, d] = st
        o_s[r, d, rows, :] = o + jnp.concatenate(inter, axis=0)

    st_s[...] = jnp.zeros_like(st_s)

    def walk_body(s, _):
        for r in range(ROWS):
            chunk(r, 0, s)
            chunk(r, 1, _bwd_chunk(s, n_ctx_chunks, n_chunks))
        return 0

    lax.fori_loop(0, n_chunks, walk_body, 0)

    def out_body(c, _):
        rows = pl.ds(pl.multiple_of(c * L, L), L)
        for r in range(ROWS):
            gate = _silu(g_ref[r, rows, :].astype(F32))
            o = o_s[r, 0, rows, :] + o_s[r, 1, rows, :]
            out_ref[r, rows, :] = (_head_norm(o, False) * gate).astype(out_ref.dtype)
        return 0

    lax.fori_loop(0, n_chunks, out_body, 0)


def _mixer_d(p, lb, n_ctx):
    b, n, _ = p.shape
    tri, same, fold = _hgrn2_tables()
    kern = functools.partial(_mixer_d_kernel, n=n, n_ctx=n_ctx)
    return pl.pallas_call(
        kern,
        grid=(b // ROWS,),
        in_specs=[_seq_spec(n, 10, ROWS), _seq_spec(n, 11, ROWS), _seq_spec(n, 12, ROWS), _seq_spec(n, 13, ROWS),
                  _seq_spec(n, 14, ROWS), _full_spec(lb.shape), _full_spec(tri.shape), _full_spec(same.shape),
                  _full_spec(fold.shape)],
        out_specs=pl.BlockSpec((ROWS, n, MIX_W), lambda bi: (bi, 0, 0)),
        out_shape=jax.ShapeDtypeStruct((b, n, MIX_W), BF16),
        scratch_shapes=[pltpu.VMEM((ROWS, 2, n, MIX_W), F32), pltpu.VMEM((ROWS, 2, MIX_W, MIX_W), F32)],
        compiler_params=_cparams("parallel"),
        name="mixer_hgrn2",
    )(p, p, p, p, p, lb, tri, same, fold)


def _pack_bf16_pairs(t):
    w = t.shape[1] // 2
    hi = pltpu.bitcast(t[:, :w].astype(BF16).astype(F32), jnp.uint32)
    lo = pltpu.bitcast(t[:, w:].astype(BF16).astype(F32), jnp.uint32)
    return pltpu.bitcast(hi | (lo >> 16), jnp.int32)


def _unpack_bf16_pairs(p):
    u = pltpu.bitcast(p, jnp.uint32)
    hi = pltpu.bitcast(u & jnp.uint32(0xFFFF0000), F32)
    lo = pltpu.bitcast(u << 16, F32)
    return hi, lo


def _merge_kernel(x_ref, m_ref, ya_ref, yb_ref, yc_ref, yd_ref, wm_ref, wb_ref, wo_ref, wr_ref, br_ref,
                  xo_ref, hp_ref, route_ref, counts_ref, carry_s, *, tm, n_ctx):
    i = pl.program_id(1)

    @pl.when((pl.program_id(0) == 0) & (i == 0))
    def _():
        carry_s[...] = jnp.zeros_like(carry_s)

    row = i * tm + lax.broadcasted_iota(jnp.int32, (tm, 1), 0)
    is_ctx = row < n_ctx
    m = m_ref[0]
    x = x_ref[0]
    d = x.shape[-1]
    h = _modulate(x, _pick(m, is_ctx, 0), _pick(m, is_ctx, 1)).astype(BF16)
    z = jnp.zeros((tm, d), F32)
    for nb, y_ref in enumerate((ya_ref, yb_ref, yc_ref, yd_ref)):
        gate = _sigmoid(jnp.dot(h, wm_ref[:, nb * d:(nb + 1) * d], preferred_element_type=F32))
        z = z + gate * jnp.dot(y_ref[0], wb_ref[nb], preferred_element_type=F32)
    mix = jnp.dot(z.astype(BF16), wo_ref[...], preferred_element_type=F32)
    x1 = x + _pick(m, is_ctx, 2) * mix
    xo_ref[0] = x1
    h2 = _modulate(x1, _pick(m, is_ctx, 3), _pick(m, is_ctx, 4))

    wr = wr_ref[...]
    wr_hi = wr.astype(BF16)
    wr_lo = (wr - wr_hi.astype(F32)).astype(BF16)
    h2_hi = h2.astype(BF16)
    h2_lo = (h2 - h2_hi.astype(F32)).astype(BF16)
    dot = functools.partial(jnp.dot, preferred_element_type=F32)
    logit = dot(h2_hi, wr_hi) + dot(h2_lo, wr_hi) + dot(h2_hi, wr_lo) + br_ref[...]
    lane = lax.broadcasted_iota(jnp.int32, (tm, LANES), 1)
    big = jnp.int32(LANES)
    is_group = (lane >= N_EXPERTS) & (lane < N_EXPERTS + N_GROUPS)
    gl = jnp.where(is_group, logit, NEG_INF)
    g_max = jnp.max(gl, axis=1, keepdims=True)
    g_idx = jnp.min(jnp.where(gl == g_max, lane, big), axis=1, keepdims=True) - N_EXPERTS
    g_prob = 1.0 / jnp.sum(jnp.where(is_group, jnp.exp(logit - g_max), 0.0), axis=1, keepdims=True)
    in_group = (lane < N_EXPERTS) & (lane // EXP_PER_GROUP == g_idx)
    el = jnp.where(in_group, logit, NEG_INF)
    v1 = jnp.max(el, axis=1, keepdims=True)
    i1 = jnp.min(jnp.where(el == v1, lane, big), axis=1, keepdims=True)
    el2 = jnp.where(lane == i1, NEG_INF, el)
    v2 = jnp.max(el2, axis=1, keepdims=True)
    i2 = jnp.min(jnp.where(el2 == v2, lane, big), axis=1, keepdims=True)
    e2 = jnp.exp(v2 - v1)
    w1 = g_prob / (1.0 + e2)
    w2 = g_prob * e2 / (1.0 + e2)
    first_low = i1 < i2
    lo = jnp.where(first_low, i1, i2) - g_idx * EXP_PER_GROUP
    hi = jnp.where(first_low, i2, i1) - g_idx * EXP_PER_GROUP
    pair = 3 * lo - ((lo * (lo - 1)) >> 1) + (hi - lo - 1)
    cls = g_idx * N_PAIRS + pair
    w_lo = jnp.where(first_low, w1, w2)
    w_hi = jnp.where(first_low, w2, w1)
    onehot = jnp.where(lane == cls, 1.0, 0.0)
    ri = lax.broadcasted_iota(jnp.int32, (tm, tm), 0)
    ci = lax.broadcasted_iota(jnp.int32, (tm, tm), 1)
    earlier = jnp.where(ci < ri, 1.0, 0.0).astype(BF16)
    before = jnp.dot(earlier, onehot.astype(BF16), preferred_element_type=F32) + carry_s[...]
    rank = jnp.sum(onehot * before, axis=1, keepdims=True)
    carry_s[...] += jnp.sum(onehot, axis=0, keepdims=True)
    counts_ref[...] = carry_s[...].astype(jnp.int32)
    cols = jnp.where(lane == 0, cls.astype(F32), jnp.where(lane == 1, rank, 0.0))
    pick = jnp.where(lax.broadcasted_iota(jnp.int32, (8, LANES), 0)
                     == lax.broadcasted_iota(jnp.int32, (8, LANES), 1), 1.0, 0.0).astype(BF16)
    route_ref[0] = sum(_dot_nt(pick, part) for part in _split3(cols)).astype(jnp.int32)
    gate_bits = pltpu.bitcast(jnp.where(lane == 0, w_lo, jnp.where(lane == 1, w_hi, 0.0)), jnp.int32)
    hp_ref[0] = jnp.concatenate([_pack_bf16_pairs(h2), gate_bits], axis=1)


def _merge(xc, mvec, ys, w_merge, w_branch, w_out, w_route, b_route, n_ctx):
    b, n, d = xc.shape
    tm = _token_tile(n, 640)
    kern = functools.partial(_merge_kernel, tm=tm, n_ctx=n_ctx)
    tok = lambda w: pl.BlockSpec((1, tm, w), lambda bi, i: (bi, i, 0))
    const = lambda shape: pl.BlockSpec(shape, lambda bi, i: (0,) * len(shape))
    return pl.pallas_call(
        kern,
        grid=(b, n // tm),
        in_specs=[tok(d), pl.BlockSpec((1, MOD_ROWS, d), lambda bi, i: (bi, 0, 0)),
                  tok(MIX_W), tok(MIX_W), tok(MIX_W), tok(MIX_W),
                  const(w_merge.shape), const(w_branch.shape), const(w_out.shape),
                  const(w_route.shape), const(b_route.shape)],
        out_specs=[tok(d), tok(d // 2 + LANES),
                   pl.BlockSpec((1, 8, tm), lambda bi, i: (bi * (n // tm) + i, 0, 0)), const((1, LANES))],
        out_shape=[jax.ShapeDtypeStruct((b, n, d), F32),
                   jax.ShapeDtypeStruct((b, n, d // 2 + LANES), jnp.int32),
                   jax.ShapeDtypeStruct((b * (n // tm), 8, tm), jnp.int32),
                   jax.ShapeDtypeStruct((1, LANES), jnp.int32)],
        scratch_shapes=[pltpu.VMEM((1, LANES), F32)],
        compiler_params=_cparams("arbitrary", "arbitrary"),
        name="merge_route",
    )(xc, mvec, *ys, w_merge, w_branch, w_out, w_route, b_route)


def _row_move(table, idx, n_out=None):
    scatter = n_out is not None
    n_idx = idx.shape[0]
    width = table.shape[1]
    per_worker = n_idx // SC_WORKERS
    assert per_worker * SC_WORKERS == n_idx and per_worker % 8 == 0
    chunk = max(c for c in range(8, SC_MAX_CHUNK + 1, 8) if per_worker % c == 0)
    n_chunks = per_worker // chunk
    mesh = plsc.VectorSubcoreMesh(core_axis_name="c", subcore_axis_name="s",
                                  num_cores=SC_CORES, num_subcores=SC_SUBCORES)

    @functools.partial(
        pl.kernel, mesh=mesh,
        out_type=jax.ShapeDtypeStruct((n_out if scatter else n_idx, width), table.dtype),
        scratch_types=[pltpu.VMEM((chunk,), jnp.int32),
                       pltpu.VMEM((chunk, width), table.dtype),
                       pltpu.SemaphoreType.DMA],
        name="sc_row_scatter" if scatter else "sc_row_gather",
    )
    def move(table_hbm, idx_hbm, out_hbm, idx_v, rows_v, sem):
        worker = lax.axis_index("s") * SC_CORES + lax.axis_index("c")
        base = worker * per_worker

        @pl.loop(0, n_chunks)
        def _(j):
            off = pl.multiple_of(base + j * chunk, 8)
            pltpu.sync_copy(idx_hbm.at[pl.ds(off, chunk)], idx_v)
            if scatter:
                pltpu.sync_copy(table_hbm.at[pl.ds(off, chunk)], rows_v)
                pltpu.async_copy(rows_v, out_hbm.at[idx_v], sem).wait()
            else:
                pltpu.async_copy(table_hbm.at[idx_v], rows_v, sem).wait()
                pltpu.sync_copy(rows_v, out_hbm.at[pl.ds(off, chunk)])

    return move(table, idx)


def _ffn(h, w1_ref, w3_ref, w2_ref):
    a = jnp.dot(h, w1_ref[0, 0], preferred_element_type=F32)
    g = jnp.dot(h, w3_ref[0, 0], preferred_element_type=F32)
    return jnp.dot((_silu(a) * g).astype(BF16), w2_ref[0, 0], preferred_element_type=F32)


def _moe_kernel(elo_ref, ehi_ref, valid_ref, xs_ref, w1a_ref, w3a_ref, w2a_ref, w1b_ref, w3b_ref, w2b_ref,
                ys_ref):
    j = pl.program_id(0)
    half = ys_ref.shape[1]

    @pl.when(valid_ref[j] != 0)
    def _():
        hi, lo = _unpack_bf16_pairs(xs_ref[:, :half])
        h = jnp.concatenate([hi.astype(BF16), lo.astype(BF16)], axis=1)
        gates = pltpu.bitcast(xs_ref[:, half:], F32)
        y = (gates[:, 0:1] * _ffn(h, w1a_ref, w3a_ref, w2a_ref)
             + gates[:, 1:2] * _ffn(h, w1b_ref, w3b_ref, w2b_ref))
        ys_ref[...] = _pack_bf16_pairs(y)

    @pl.when(valid_ref[j] == 0)
    def _():
        ys_ref[...] = jnp.zeros_like(ys_ref)


def _moe_experts(xs, tile_elo, tile_ehi, tile_valid, w1, w3, w2, layer):
    n_slots, width = xs.shape
    d = w1.shape[2]
    n_tiles = n_slots // MOE_TILE
    wspec = lambda shape, which: pl.BlockSpec(
        (1, 1) + shape, lambda j, elo, ehi, valid: (layer, (elo, ehi)[which][j], 0, 0))
    grid_spec = pltpu.PrefetchScalarGridSpec(
        num_scalar_prefetch=3,
        grid=(n_tiles,),
        in_specs=[pl.BlockSpec((MOE_TILE, width), lambda j, elo, ehi, valid: (j, 0)),
                  wspec((d, D_EXPERT), 0), wspec((d, D_EXPERT), 0), wspec((D_EXPERT, d), 0),
                  wspec((d, D_EXPERT), 1), wspec((d, D_EXPERT), 1), wspec((D_EXPERT, d), 1)],
        out_specs=pl.BlockSpec((MOE_TILE, d // 2), lambda j, elo, ehi, valid: (j, 0)),
    )
    return pl.pallas_call(
        _moe_kernel,
        grid_spec=grid_spec,
        out_shape=jax.ShapeDtypeStruct((n_slots, d // 2), jnp.int32),
        compiler_params=_cparams("arbitrary"),
        name="moe_experts",
    )(tile_elo, tile_ehi, tile_valid, xs, w1, w3, w2, w1, w3, w2)


def _moe_combine_kernel(x_ref, m_ref, y_ref, *rest, tm, n_ctx, row0):
    o_ref = rest[-1]
    row = row0 + pl.program_id(1) * tm + lax.broadcasted_iota(jnp.int32, (tm, 1), 0)
    hi, lo = _unpack_bf16_pairs(y_ref[0])
    x = x_ref[0] + _pick(m_ref[0], row < n_ctx, 5) * jnp.concatenate([hi, lo], axis=1)
    if len(rest) == 2:
        ms = jnp.mean(x * x, axis=-1, keepdims=True)
        x = x * lax.rsqrt(ms + EPS) * rest[0][...]
    o_ref[0] = x


def _moe_combine(x1, mvec, yt, n_ctx, final_w=None):
    b, n, d = x1.shape
    if final_w is None:
        tm, skip, n_out, extra, extra_specs = _token_tile(n, 1100), 0, n, (), []
    else:
        tm = _token_tile(n_ctx, 1024)
        skip, n_out = n_ctx // tm, n - n_ctx
        extra, extra_specs = (final_w.reshape(1, d),), [pl.BlockSpec((1, d), lambda bi, i: (0, 0))]
    kern = functools.partial(_moe_combine_kernel, tm=tm, n_ctx=n_ctx, row0=skip * tm)
    tok = lambda w: pl.BlockSpec((1, tm, w), lambda bi, i: (bi, i + skip, 0))
    return pl.pallas_call(
        kern,
        grid=(b, n_out // tm),
        in_specs=[tok(d), pl.BlockSpec((1, MOD_ROWS, d), lambda bi, i: (bi, 0, 0)), tok(d // 2)] + extra_specs,
        out_specs=pl.BlockSpec((1, tm, d), lambda bi, i: (bi, i, 0)),
        out_shape=jax.ShapeDtypeStruct((b, n_out, d), F32),
        compiler_params=_cparams("parallel", "parallel"),
        name="moe_combine",
    )(x1, mvec, yt, *extra)


def _moe(x1, mvec, hp, route, counts, w1, w3, w2, layer, n_ctx, final_w=None):
    b, n, d = x1.shape
    m = b * n
    n_slots = m + N_CLASSES * MOE_TILE
    cls = route[:, 0, :].reshape(m)
    rank = route[:, 1, :].reshape(m)
    cnt = counts[0, :N_CLASSES]
    padded = (cnt + MOE_TILE - 1) // MOE_TILE * MOE_TILE
    ends = jnp.cumsum(padded)
    pos = (ends - padded)[cls] + rank
    tile_start = jnp.arange(n_slots // MOE_TILE, dtype=jnp.int32) * MOE_TILE
    tile_cls = jnp.minimum(jnp.searchsorted(ends, tile_start, side="right"), N_CLASSES - 1).astype(jnp.int32)
    tile_valid = (tile_start < ends[-1]).astype(jnp.int32)
    pair_lo = jnp.asarray([0, 0, 0, 1, 1, 2], jnp.int32)
    pair_hi = jnp.asarray([1, 2, 3, 2, 3, 3], jnp.int32)
    group = tile_cls // N_PAIRS
    tile_elo = group * EXP_PER_GROUP + pair_lo[tile_cls % N_PAIRS]
    tile_ehi = group * EXP_PER_GROUP + pair_hi[tile_cls % N_PAIRS]

    xs = _row_move(hp.reshape(m, hp.shape[-1]), pos, n_out=n_slots)
    ys = _moe_experts(xs, tile_elo, tile_ehi, tile_valid, w1, w3, w2, layer)
    yt = _row_move(ys, pos)
    return _moe_combine(x1, mvec, yt.reshape(b, n, d // 2), n_ctx, final_w)


def _rope_tables(n_lat, n_ctx):
    rows = n_lat // GRID_W
    row = jnp.repeat(jnp.arange(rows), GRID_W).astype(F32)
    col = jnp.tile(jnp.arange(GRID_W), rows).astype(F32)
    nq = HEAD_DIM // 4
    inv = jnp.power(ROPE_BASE, -jnp.arange(nq, dtype=F32) / nq)
    ang = jnp.concatenate([row[:, None] * inv, col[:, None] * inv], -1)
    cos = jnp.cos(ang)
    sin = jnp.sin(ang)
    cos_h = jnp.concatenate([cos, cos], -1)
    sin_h = jnp.concatenate([-sin, sin], -1)
    cos_full = jnp.concatenate([jnp.ones((n_ctx, HEAD_DIM), F32), cos_h], 0)
    sin_full = jnp.concatenate([jnp.zeros((n_ctx, HEAD_DIM), F32), sin_h], 0)
    return jnp.tile(cos_full, (1, N_HEADS)), jnp.tile(sin_full, (1, N_HEADS))


def _block_diag_heads(w):
    eye = jnp.eye(N_HEADS, dtype=w.dtype)
    return jnp.einsum("hij,hg->higj", w, eye).reshape(MIX_W, MIX_W)


def kernel(x, c, ctx, c_ctx, w_mod, b_mod, w_in, a_conv_w, a_conv_b, a_gate_w, a_gate_b, a_lambda, b_theta,
           c_conv_w, c_conv_b, c_gate_b, d_lb, w_branch, w_out, moe_w_group, moe_b_group, moe_w_router,
           moe_b_router, moe_w1, moe_w3, moe_w2, final_norm_w):
    bsz, n_lat, d = x.shape
    n_ctx = ctx.shape[1]
    depth = w_mod.shape[0]
    assert n_ctx % CHUNK == 0 and n_lat % CHUNK == 0 and n_ctx % 256 == 0 and n_lat % 256 == 0

    xc = jnp.concatenate([ctx, x], axis=1)
    cos, sin = _rope_tables(n_lat, n_ctx)

    cc = jnp.zeros((8, d), F32).at[:bsz].set(c).at[bsz].set(c_ctx)
    mod = _mod_vectors(cc, w_mod, b_mod)
    mx = mod[:, :bsz].reshape(depth, bsz, N_MOD, d)
    mc = jnp.broadcast_to(mod[:, bsz].reshape(depth, 1, N_MOD, d), (depth, bsz, N_MOD, d))
    pad = jnp.zeros((depth, bsz, 8 - N_MOD, d), F32)
    mvec = jnp.concatenate([mx, pad, mc, pad], axis=2)

    lbs = jnp.cumsum(jax.nn.softmax(d_lb.astype(F32), axis=0), axis=0)
    lbs = lbs - lbs[0]

    n_mix_cols = 15 * MIX_W
    gate0 = 10 * MIX_W
    n_gate = 4 * N_HEADS
    w1b = moe_w1.astype(BF16)
    w3b = moe_w3.astype(BF16)
    w2b = moe_w2.astype(BF16)

    for l in range(depth):
        wl = w_in[l]
        w_mix = jnp.concatenate([wl[:, :gate0], wl[:, gate0 + n_gate:n_mix_cols + n_gate]], axis=1).astype(BF16)
        w_gate = jnp.pad(wl[:, gate0:gate0 + n_gate], ((0, 0), (0, LANES - n_gate))).astype(BF16)
        w_merge = wl[:, n_mix_cols + n_gate:].astype(BF16)
        p, gates_c = _in_proj(xc, mvec[l], w_mix, w_gate, cos, sin, n_ctx)

        gw = jnp.concatenate([_block_diag_heads(a_gate_w[l, dd, j]) for dd in range(2) for j in range(2)],
                             axis=1).astype(BF16)
        gb = a_gate_b[l].reshape(1, 4 * MIX_W)
        ya = _mixer_a(p, a_conv_w[l], a_conv_b[l].reshape(1, MIX_W), gw, gb, a_lambda[l], n_ctx)

        log_gamma = jnp.repeat(jax.nn.log_sigmoid(b_theta[l].astype(F32)), HEAD_DIM, axis=1)
        yb = _mixer_b(p, log_gamma, n_ctx)

        gate_b = jnp.pad(c_gate_b[l].reshape(1, n_gate), ((0, 0), (0, LANES - n_gate)))
        yc = _mixer_c(p, gates_c, c_conv_w[l], c_conv_b[l].reshape(1, 2 * MIX_W), gate_b, n_ctx)

        yd = _mixer_d(p, lbs[l].reshape(1, MIX_W), n_ctx)

        w_route = jnp.pad(jnp.concatenate([moe_w_router[l], moe_w_group[l]], axis=1),
                          ((0, 0), (0, LANES - N_EXPERTS - N_GROUPS)))
        b_route = jnp.pad(jnp.concatenate([moe_b_router[l], moe_b_group[l]]),
                          (0, LANES - N_EXPERTS - N_GROUPS)).reshape(1, LANES)
        x1, hp, route, counts = _merge(xc, mvec[l], (ya, yb, yc, yd), w_merge, w_branch[l].astype(BF16),
                                       w_out[l].astype(BF16), w_route, b_route, n_ctx)
        xc = _moe(x1, mvec[l], hp, route, counts, w1b, w3b, w2b, l, n_ctx,
                  final_w=final_norm_w if l == depth - 1 else None)

    return xc
```

```python
import functools

import jax
import jax.numpy as jnp
import numpy as np
from jax import lax
from jax.experimental import pallas as pl
from jax.experimental.pallas import tpu as pltpu
from jax.experimental.pallas import tpu_sc as plsc

F32 = jnp.float32
BF16 = jnp.bfloat16

EPS = 1e-6
N_HEADS = 4
HEAD_DIM = 64
MIX_W = N_HEADS * HEAD_DIM
N_BRANCH = 4
CONV_W = 4
LRU_C = 8.0
GRID_W = 64
ROPE_BASE = 10000.0
N_GROUPS = 4
EXP_PER_GROUP = 4
N_EXPERTS = N_GROUPS * EXP_PER_GROUP
D_EXPERT = 512
N_MOD = 6
M_INIT = -1e30

CHUNK = 128
RET_CHUNK = 256
CONV_STEP = 128
SUB = 16
HGRN_CHUNK = 256
HALO = 16
LANES = 128
MOD_ROWS = 16
ROWS = 1
VMEM_LIMIT_BYTES = 56 * 1024 * 1024
N_PAIRS = 6
N_CLASSES = N_GROUPS * N_PAIRS
MOE_TILE = 256
SC_CORES = 2
SC_SUBCORES = 16
SC_WORKERS = SC_CORES * SC_SUBCORES
SC_MAX_CHUNK = 32
NEG_INF = float("-inf")
LOG2_E = 1.4426950408889634


def _cparams(*sem):
    return pltpu.CompilerParams(dimension_semantics=sem, vmem_limit_bytes=VMEM_LIMIT_BYTES)


def _token_tile(n, cap):
    best = 16
    for t in range(16, cap + 1, 16):
        if n % t == 0:
            best = t
    return best


def _modulate(x, shift, scale):
    ms = jnp.mean(x * x, axis=-1, keepdims=True)
    return x * lax.rsqrt(ms + EPS) * (1.0 + scale) + shift


def _pick(m, is_ctx, k):
    return jnp.where(is_ctx, m[8 + k:9 + k], m[k:k + 1])


def _sigmoid(x):
    return 0.5 * jnp.tanh(0.5 * x) + 0.5


def _silu(x):
    return x * _sigmoid(x)


def _log_sigmoid(x):
    return jnp.minimum(x, 0.0) - jnp.log(1.0 + jnp.exp(-jnp.abs(x)))


def _split3(x):
    hi = x.astype(BF16)
    r = x - hi.astype(F32)
    mid = r.astype(BF16)
    lo = (r - mid.astype(F32)).astype(BF16)
    return hi, mid, lo


def _sel_dot(sel, x):
    hi, mid, lo = _split3(x)
    d = functools.partial(jnp.dot, preferred_element_type=F32)
    return d(sel, hi) + d(sel, mid) + d(sel, lo)


def _dot_sel(x, sel):
    hi, mid, lo = _split3(x)
    d = functools.partial(jnp.dot, preferred_element_type=F32)
    return d(hi, sel) + d(mid, sel) + d(lo, sel)


def _dot_nt(a, b):
    return lax.dot_general(a, b, (((1,), (1,)), ((), ())), preferred_element_type=F32)


def _dot_tn(a, b):
    return lax.dot_general(a, b, (((0,), (0,)), ((), ())), preferred_element_type=F32)


def _head_of_lane(width):
    return lax.broadcasted_iota(jnp.int32, (1, width), 1) // HEAD_DIM


def _block_ones(width):
    r = lax.broadcasted_iota(jnp.int32, (width, width), 0) // HEAD_DIM
    c = lax.broadcasted_iota(jnp.int32, (width, width), 1) // HEAD_DIM
    return r == c


def _head_norm(o, center):
    ones = jnp.where(_block_ones(MIX_W), 1.0, 0.0).astype(BF16)
    inv = 1.0 / HEAD_DIM
    if center:
        o = o - _dot_sel(o, ones) * inv
    var = _dot_sel(o * o, ones) * inv
    return o * lax.rsqrt(var + EPS)


def _bwd_chunk(s, n_ctx_chunks, n_chunks):
    return jnp.where(s < n_ctx_chunks, n_ctx_chunks - 1 - s, n_chunks - 1 + n_ctx_chunks - s)


def _conv_chunk(ref, c, w, b, *, n, n_ctx, width0=0, width=None):
    L = CONV_STEP
    wl = L + 2 * HALO
    r0 = c * L
    start = pl.multiple_of(jnp.clip(r0 - HALO, 0, n - wl), HALO)
    off = r0 - start
    if width is None:
        win = ref[0, pl.ds(start, wl), :]
    else:
        win = ref[0, pl.ds(start, wl), width0:width0 + width]
    ri = lax.broadcasted_iota(jnp.int32, (L, wl), 0)
    mi = lax.broadcasted_iota(jnp.int32, (L, wl), 1)
    g = r0 + ri
    y = b
    seg_g = jnp.where(g < n_ctx, 0, 1)
    for k in range(CONV_W):
        src = g + (k - 2)
        hit = (mi == ri + off + (k - 2)) & (jnp.where(src < n_ctx, 0, 1) == seg_g) & (src >= 0) & (src < n)
        sel = jnp.where(hit, 1.0, 0.0).astype(BF16)
        y = y + w[k:k + 1] * jnp.dot(sel, win, preferred_element_type=F32)
    return y


def _mod_kernel(cc_ref, w_ref, b_ref, o_ref):
    s = _silu(cc_ref[...])
    o_ref[0] = jnp.dot(s, w_ref[0], precision=lax.Precision.HIGHEST, preferred_element_type=F32) + b_ref[0]


def _mod_vectors(cc, w_mod, b_mod):
    depth, d, dm = w_mod.shape
    tn = 1536
    return pl.pallas_call(
        _mod_kernel,
        grid=(depth, dm // tn),
        in_specs=[pl.BlockSpec((8, d), lambda l, j: (0, 0)),
                  pl.BlockSpec((1, d, tn), lambda l, j: (l, 0, j)),
                  pl.BlockSpec((1, 1, tn), lambda l, j: (l, 0, j))],
        out_specs=pl.BlockSpec((1, 8, tn), lambda l, j: (l, 0, j)),
        out_shape=jax.ShapeDtypeStruct((depth, 8, dm), F32),
        compiler_params=_cparams("parallel", "parallel"),
        name="mod_vectors",
    )(cc, w_mod, b_mod.reshape(depth, 1, dm))


def _rope(t, cos, sin):
    lane = lax.broadcasted_iota(jnp.int32, (1, MIX_W), 1) % HEAD_DIM
    half = HEAD_DIM // 2
    swapped = jnp.where(lane < half, pltpu.roll(t, MIX_W - half, 1), pltpu.roll(t, half, 1))
    return t * cos + swapped * sin


RET_Q_COL = 2 * MIX_W
RET_K_COL = 3 * MIX_W


def _in_proj_kernel(x_ref, m_ref, w_ref, wg_ref, cos_ref, sin_ref, p_ref, g_ref, *, tm, n_ctx, col_chunk):
    i = pl.program_id(1)
    row = i * tm + lax.broadcasted_iota(jnp.int32, (tm, 1), 0)
    is_ctx = row < n_ctx
    m = m_ref[0]
    h = _modulate(x_ref[0], _pick(m, is_ctx, 0), _pick(m, is_ctx, 1)).astype(BF16)
    for j in range(w_ref.shape[1] // col_chunk):
        sl = slice(j * col_chunk, (j + 1) * col_chunk)
        t = jnp.dot(h, w_ref[:, sl], preferred_element_type=F32)
        if j * col_chunk == RET_Q_COL:
            t = _rope(t, cos_ref[...], sin_ref[...])
        elif j * col_chunk == RET_K_COL:
            t = _rope(t, cos_ref[...], sin_ref[...]) * HEAD_DIM ** -0.5
        p_ref[0, :, sl] = t.astype(BF16)
    g_ref[0] = jnp.dot(h, wg_ref[...], preferred_element_type=F32)


def _in_proj(xc, mvec, w_mix, w_gate, cos, sin, n_ctx):
    b, n, d = xc.shape
    tm = _token_tile(n, 640)
    wc = w_mix.shape[1]
    kern = functools.partial(_in_proj_kernel, tm=tm, n_ctx=n_ctx, col_chunk=MIX_W)
    return pl.pallas_call(
        kern,
        grid=(b, n // tm),
        in_specs=[pl.BlockSpec((1, tm, d), lambda bi, i: (bi, i, 0)),
                  pl.BlockSpec((1, MOD_ROWS, d), lambda bi, i: (bi, 0, 0)),
                  pl.BlockSpec((d, wc), lambda bi, i: (0, 0)),
                  pl.BlockSpec((d, LANES), lambda bi, i: (0, 0)),
                  pl.BlockSpec((tm, MIX_W), lambda bi, i: (i, 0)),
                  pl.BlockSpec((tm, MIX_W), lambda bi, i: (i, 0))],
        out_specs=[pl.BlockSpec((1, tm, wc), lambda bi, i: (bi, i, 0)),
                   pl.BlockSpec((1, tm, LANES), lambda bi, i: (bi, i, 0))],
        out_shape=[jax.ShapeDtypeStruct((b, n, wc), BF16),
                   jax.ShapeDtypeStruct((b, n, LANES), F32)],
        compiler_params=_cparams("parallel", "parallel"),
        name="in_proj",
    )(xc, mvec, w_mix, w_gate, cos, sin)


def _lin_scan(a, x, carry, reverse):
    L = a.shape[0]
    row = lax.broadcasted_iota(jnp.int32, (L, 1), 0)
    k = 1
    while k < L:
        if reverse:
            a_s = pltpu.roll(a, L - k, 0)
            x_s = pltpu.roll(x, L - k, 0)
            valid = row < L - k
        else:
            a_s = pltpu.roll(a, k, 0)
            x_s = pltpu.roll(x, k, 0)
            valid = row >= k
        x = jnp.where(valid, a * x_s + x, x)
        a = jnp.where(valid, a * a_s, a)
        k *= 2
    h = x + a * carry
    return h, (h[0:1] if reverse else h[L - 1:L])


def _mixer_a_kernel(ax_ref, ag_ref, cw_ref, cb_ref, gw_ref, gb_ref, lam_ref, out_ref, u_s, hf_s,
                    *, n, n_ctx):
    L = CHUNK
    n_chunks = n // L
    n_ctx_chunks = n_ctx // L
    cw = cw_ref[...]
    cb = cb_ref[...]
    log_lam = _log_sigmoid(lam_ref[...])

    def direction(d, c, carry):
        r0 = pl.multiple_of(c * L, L)
        u = u_s[pl.ds(r0, L), :]
        pre = jnp.dot(u.astype(BF16), gw_ref[:, d * 2 * MIX_W:(d + 1) * 2 * MIX_W],
                      preferred_element_type=F32) + gb_ref[:, d * 2 * MIX_W:(d + 1) * 2 * MIX_W]
        r = _sigmoid(pre[:, :MIX_W])
        gi = _sigmoid(pre[:, MIX_W:])
        log_a = LRU_C * r * log_lam[d:d + 1]
        a = jnp.exp(log_a)
        inp = jnp.sqrt(1.0 - jnp.exp(2.0 * log_a)) * (gi * u)
        h, new_carry = _lin_scan(a, inp, carry, reverse=(d == 1))
        return r0, h, new_carry

    def conv_body(c, _):
        r0 = pl.multiple_of(c * CONV_STEP, CONV_STEP)
        u_s[pl.ds(r0, CONV_STEP), :] = _conv_chunk(ax_ref, c, cw, cb, n=n, n_ctx=n_ctx)
        return 0

    lax.fori_loop(0, n // CONV_STEP, conv_body, 0)

    def walk_body(s, carries):
        r0, h, carry_f = direction(0, s, carries[0])
        hf_s[0, pl.ds(r0, L), :] = h
        r0, h, carry_b = direction(1, _bwd_chunk(s, n_ctx_chunks, n_chunks), carries[1])
        hf_s[1, pl.ds(r0, L), :] = h
        return carry_f, carry_b

    zero = jnp.zeros((1, MIX_W), F32)
    lax.fori_loop(0, n_chunks, walk_body, (zero, zero))

    def out_body(c, _):
        rows = pl.ds(pl.multiple_of(c * L, L), L)
        gate = jax.nn.gelu(ag_ref[0, rows, :].astype(F32), approximate=True)
        out_ref[0, rows, :] = (gate * (hf_s[0, rows, :] + hf_s[1, rows, :])).astype(out_ref.dtype)
        return 0

    lax.fori_loop(0, n_chunks, out_body, 0)


def _seq_spec(n, col, rows=1):
    if rows == 1:
        return pl.BlockSpec((1, n, MIX_W), lambda b: (b, 0, col))
    return pl.BlockSpec((rows, n, MIX_W), lambda b: (b, 0, col), pipeline_mode=pl.Buffered(1))


def _full_spec(shape):
    return pl.BlockSpec(shape, lambda b: (0,) * len(shape))


def _mixer_a(p, conv_w, conv_b, gate_w, gate_b, lam, n_ctx):
    b, n, _ = p.shape
    kern = functools.partial(_mixer_a_kernel, n=n, n_ctx=n_ctx)
    return pl.pallas_call(
        kern,
        grid=(b,),
        in_specs=[_seq_spec(n, 0), _seq_spec(n, 1),
                  _full_spec(conv_w.shape), _full_spec(conv_b.shape),
                  _full_spec(gate_w.shape), _full_spec(gate_b.shape), _full_spec(lam.shape)],
        out_specs=pl.BlockSpec((1, n, MIX_W), lambda bi: (bi, 0, 0)),
        out_shape=jax.ShapeDtypeStruct((b, n, MIX_W), BF16),
        scratch_shapes=[pltpu.VMEM((n, MIX_W), F32), pltpu.VMEM((2, n, MIX_W), F32)],
        compiler_params=_cparams("parallel"),
        name="mixer_rglru",
    )(p, p, conv_w, conv_b, gate_w, gate_b, lam)


def _mixer_b_kernel(q_ref, k_ref, v_ref, g_ref, lg_ref, out_ref, oi_s, dec_s, st_s, *, n, n_ctx):
    L = RET_CHUNK
    n_chunks = n // L
    n_ctx_chunks = n_ctx // L
    head = _head_of_lane(MIX_W)
    bd = _block_ones(MIX_W)
    lgf = lg_ref[0:1, :]
    lgb = lg_ref[1:2, :]
    pos = lax.broadcasted_iota(jnp.int32, (L, 1), 0).astype(F32)
    diff = (lax.broadcasted_iota(jnp.int32, (L, L), 0) - lax.broadcasted_iota(jnp.int32, (L, L), 1)).astype(F32)
    for h in range(N_HEADS):
        lf = lgf[:, h * HEAD_DIM:h * HEAD_DIM + 1]
        lb = lgb[:, h * HEAD_DIM:h * HEAD_DIM + 1]
        dec_s[h] = jnp.where(diff >= 0, jnp.exp(lf * jnp.maximum(diff, 0.0)), jnp.exp(lb * jnp.maximum(-diff, 0.0)))

    qw = (jnp.exp(lgf * (pos + 1.0)), jnp.exp(lgb * (float(L) - pos)))
    kw = (jnp.exp(lgf * (float(L) - 1.0 - pos)), jnp.exp(lgb * pos))
    g_chunk = (jnp.exp(lgf * float(L)), jnp.exp(lgb * float(L)))

    def state_step(d, c):
        rows = pl.ds(pl.multiple_of(c * L, L), L)
        q = q_ref[0, rows, :].astype(F32)
        k = k_ref[0, rows, :].astype(F32)
        st = st_s[d]
        oi_s[d, rows, :] = jnp.dot((q * qw[d]).astype(BF16), st.astype(BF16), preferred_element_type=F32)
        kv = _dot_tn((k * kw[d]).astype(BF16), v_ref[0, rows, :])
        st_s[d] = g_chunk[d] * st + jnp.where(bd, kv, 0.0)

    st_s[...] = jnp.zeros_like(st_s)

    def walk_body(s, _):
        state_step(0, s)
        state_step(1, _bwd_chunk(s, n_ctx_chunks, n_chunks))
        return 0

    lax.fori_loop(0, n_chunks, walk_body, 0)

    def out_body(c, _):
        rows = pl.ds(pl.multiple_of(c * L, L), L)
        q = q_ref[0, rows, :]
        k = k_ref[0, rows, :]
        v = v_ref[0, rows, :]
        scs, vals = [], []
        for h in range(N_HEADS):
            hm = head == h
            scs.append((_dot_nt(jnp.where(hm, q, jnp.zeros_like(q)), k) * dec_s[h]).astype(BF16))
            vals.append(jnp.where(hm, v, jnp.zeros_like(v)))
        o = (oi_s[0, rows, :] + oi_s[1, rows, :]
             + jnp.dot(jnp.concatenate(scs, axis=1), jnp.concatenate(vals, axis=0), preferred_element_type=F32))
        gate = _silu(g_ref[0, rows, :].astype(F32))
        out_ref[0, rows, :] = (_head_norm(o, True) * gate).astype(out_ref.dtype)
        return 0

    lax.fori_loop(0, n_chunks, out_body, 0, unroll=2)


def _mixer_b(p, log_gamma, n_ctx):
    b, n, _ = p.shape
    kern = functools.partial(_mixer_b_kernel, n=n, n_ctx=n_ctx)
    return pl.pallas_call(
        kern,
        grid=(b,),
        in_specs=[_seq_spec(n, 2), _seq_spec(n, 3), _seq_spec(n, 4), _seq_spec(n, 5),
                  _full_spec(log_gamma.shape)],
        out_specs=pl.BlockSpec((1, n, MIX_W), lambda bi: (bi, 0, 0)),
        out_shape=jax.ShapeDtypeStruct((b, n, MIX_W), BF16),
        scratch_shapes=[pltpu.VMEM((2, n, MIX_W), F32),
                        pltpu.VMEM((N_HEADS, RET_CHUNK, RET_CHUNK), F32),
                        pltpu.VMEM((2, MIX_W, MIX_W), F32)],
        compiler_params=_cparams("parallel"),
        name="mixer_retention",
    )(p, p, p, p, log_gamma)


def _mixer_c_kernel(q_ref, k_ref, v_ref, o_ref, g_ref, cw_ref, cb_ref, gb_ref, out_ref,
                    qs_s, ks_s, hf_s, c_s, n_s, m_s, *, n, n_ctx):
    L = CHUNK
    n_chunks = n // L
    n_ctx_chunks = n_ctx // L
    head = _head_of_lane(MIX_W)
    bd = _block_ones(MIX_W)
    ones_bd = jnp.where(bd, 1.0, 0.0).astype(BF16)
    cw = cw_ref[...]
    cb = cb_ref[...]
    ri = lax.broadcasted_iota(jnp.int32, (L, L), 0)
    ci = lax.broadcasted_iota(jnp.int32, (L, L), 1)
    gl = lax.broadcasted_iota(jnp.int32, (LANES, MIX_W), 0)
    hl = lax.broadcasted_iota(jnp.int32, (LANES, MIX_W), 1) // HEAD_DIM

    def expand(kind):
        return jnp.where(gl == kind * N_HEADS + hl, 1.0, 0.0).astype(BF16)

    def conv_body(c, _):
        rows = pl.ds(pl.multiple_of(c * CONV_STEP, CONV_STEP), CONV_STEP)
        qs_s[rows, :] = _silu(_conv_chunk(q_ref, c, cw[:, :MIX_W], cb[:, :MIX_W], n=n, n_ctx=n_ctx)).astype(BF16)
        ks_s[rows, :] = (_silu(_conv_chunk(k_ref, c, cw[:, MIX_W:], cb[:, MIX_W:], n=n, n_ctx=n_ctx))
                         * HEAD_DIM ** -0.5).astype(BF16)
        return 0

    lax.fori_loop(0, n // CONV_STEP, conv_body, 0)

    def chunk(d, c):
        rev = d == 1
        rows = pl.ds(pl.multiple_of(c * L, L), L)
        q = qs_s[rows, :]
        k = ks_s[rows, :]
        v = v_ref[0, rows, :]
        g = g_ref[0, rows, :] + gb_ref[...]
        log_i = _dot_sel(g, expand(2 * d))
        log_f = _dot_sel(_log_sigmoid(g), expand(2 * d + 1))
        causal = (ci >= ri) if rev else (ci <= ri)
        tri = jnp.where(causal, 1.0, 0.0).astype(BF16)
        cum = _sel_dot(tri, log_f)
        cum_end = cum[0:1] if rev else cum[L - 1:L]
        m_prev = m_s[d]
        c_prev = c_s[d]
        n_prev = n_s[d]
        row_src = (log_i - cum).T
        m_inter = cum + m_prev
        num_inter = jnp.dot(q, c_prev.astype(BF16), preferred_element_type=F32)
        qn = _dot_sel(q.astype(F32) * n_prev, ones_bd)
        m_q = jnp.zeros((L, MIX_W), F32)
        scs, rhs = [], []
        for h in range(N_HEADS):
            hm = head == h
            lane0 = h * HEAD_DIM
            log_d = jnp.where(causal, cum[:, lane0:lane0 + 1] + row_src[lane0:lane0 + 1, :], NEG_INF)
            m_h = jnp.maximum(jnp.max(log_d, axis=1, keepdims=True), m_inter[:, lane0:lane0 + 1])
            sc = _dot_nt(jnp.where(hm, q, jnp.zeros_like(q)), k) * jnp.exp(log_d - m_h)
            scs.append(sc.astype(BF16))
            rhs.append(jnp.concatenate([jnp.where(hm, v, jnp.zeros_like(v)),
                                        jnp.broadcast_to(jnp.where(hm, 1.0, 0.0).astype(BF16), (L, MIX_W))], axis=1))
            m_q = jnp.where(hm, m_h, m_q)
        intra = jnp.dot(jnp.concatenate(scs, axis=1), jnp.concatenate(rhs, axis=0), preferred_element_type=F32)
        s_inter = jnp.exp(m_inter - m_q)
        num = intra[:, :MIX_W] + s_inter * num_inter
        den = intra[:, MIX_W:] + s_inter * qn
        hid = num / jnp.maximum(jnp.abs(den), jnp.exp(-m_q))
        log_w = cum_end - cum + log_i
        m_loc = jnp.max(log_w, axis=0, keepdims=True)
        kw = k.astype(F32) * jnp.exp(log_w - m_loc)
        c_loc = jnp.where(bd, _dot_tn(kw.astype(BF16), v), 0.0)
        n_loc = jnp.sum(kw, axis=0, keepdims=True)
        m_new = jnp.maximum(cum_end + m_prev, m_loc)
        s_old = jnp.exp(cum_end + m_prev - m_new)
        s_loc = jnp.exp(m_loc - m_new)
        c_s[d] = s_old * c_prev + s_loc * c_loc
        n_s[d] = s_old * n_prev + s_loc * n_loc
        m_s[d] = m_new
        return rows, hid

    c_s[...] = jnp.zeros_like(c_s)
    n_s[...] = jnp.zeros_like(n_s)
    m_s[...] = jnp.full(m_s.shape, M_INIT, F32)

    def walk_body(s, _):
        rows, hid = chunk(0, s)
        hf_s[0, rows, :] = hid
        rows, hid = chunk(1, _bwd_chunk(s, n_ctx_chunks, n_chunks))
        hf_s[1, rows, :] = hid
        return 0

    lax.fori_loop(0, n_chunks, walk_body, 0)

    def out_body(c, _):
        rows = pl.ds(pl.multiple_of(c * L, L), L)
        gate = _sigmoid(o_ref[0, rows, :].astype(F32))
        out_ref[0, rows, :] = (gate * _head_norm(hf_s[0, rows, :] + hf_s[1, rows, :], True)).astype(out_ref.dtype)
        return 0

    lax.fori_loop(0, n_chunks, out_body, 0)


def _mixer_c(p, gates, conv_w, conv_b, gate_b, n_ctx):
    b, n, _ = p.shape
    kern = functools.partial(_mixer_c_kernel, n=n, n_ctx=n_ctx)
    return pl.pallas_call(
        kern,
        grid=(b,),
        in_specs=[_seq_spec(n, 6), _seq_spec(n, 7), _seq_spec(n, 8), _seq_spec(n, 9),
                  pl.BlockSpec((1, n, LANES), lambda bi: (bi, 0, 0)),
                  _full_spec(conv_w.shape), _full_spec(conv_b.shape), _full_spec(gate_b.shape)],
        out_specs=pl.BlockSpec((1, n, MIX_W), lambda bi: (bi, 0, 0)),
        out_shape=jax.ShapeDtypeStruct((b, n, MIX_W), BF16),
        scratch_shapes=[pltpu.VMEM((n, MIX_W), BF16), pltpu.VMEM((n, MIX_W), BF16),
                        pltpu.VMEM((2, n, MIX_W), F32),
                        pltpu.VMEM((2, MIX_W, MIX_W), F32),
                        pltpu.VMEM((2, 1, MIX_W), F32),
                        pltpu.VMEM((2, 1, MIX_W), F32)],
        compiler_params=_cparams("parallel"),
        name="mixer_mlstm",
    )(p, p, p, p, gates, conv_w, conv_b, gate_b)


def _hgrn2_tables():
    L, S = HGRN_CHUNK, SUB
    r = np.arange(L)
    same = (r[:, None] // S) == (r[None, :] // S)
    tri = np.stack([same & (r[None, :] <= r[:, None]), same & (r[None, :] >= r[:, None])])
    fold = []
    for rev in (False, True):
        pi = np.concatenate([np.full(len(_pair_rows(i, rev)), i) for i in range(S)])
        pj = np.concatenate([_pair_rows(i, rev) for i in range(S)])
        visible = (pj >= pi) if rev else (pj <= pi)
        fold.append((np.arange(S)[:, None] == pi[None, :]) & visible[None, :])
    as_bf16 = lambda a: jnp.asarray(a.astype(np.float32), dtype=BF16)
    return as_bf16(tri), as_bf16(same), as_bf16(np.stack(fold))


def _pair_rows(i, rev):
    half = SUB // 2
    if rev:
        return np.arange(0, SUB) if i < half else np.arange(half, SUB)
    return np.arange(0, half) if i < half else np.arange(0, SUB)


def _mixer_d_kernel(q_ref, ff_ref, fb_ref, v_ref, g_ref, lb_ref, tri_ref, same_ref, fold_ref, out_ref,
                    o_s, st_s, *, n, n_ctx):
    L = HGRN_CHUNK
    S = SUB
    nb = L // S
    n_chunks = n // L
    n_ctx_chunks = n_ctx // L
    lb = lb_ref[...]
    bd = _block_ones(MIX_W)
    ones_bd = jnp.where(bd, 1.0, 0.0).astype(BF16)

    def chunk(r, d, c):
        rev = d == 1
        rows = pl.ds(pl.multiple_of(c * L, L), L)
        q = _silu(q_ref[r, rows, :].astype(F32))
        raw = (fb_ref if rev else ff_ref)[r, rows, :].astype(F32)
        v = v_ref[r, rows, :]
        vf = v.astype(F32)
        f = lb + (1.0 - lb) * _sigmoid(raw)
        k = 1.0 - f
        log_f = jnp.log(f)
        cum = _sel_dot(tri_ref[d], log_f)
        tot = _sel_dot(same_ref[...], log_f)
        qt = (q * jnp.exp(cum)).astype(BF16)
        kt = (k * jnp.exp(tot - cum)).astype(BF16)
        g = jnp.exp(tot)
        cum2 = cum * LOG2_E
        prod, vj, scores, intra = {}, {}, {}, {}
        for a in range(nb + 2):
            if a < nb:
                sl = slice(a * S, (a + 1) * S)
                cb, qb, kb, vb = cum2[sl], q[sl], k[sl], vf[sl]
                prods, v_js = [], []
                for i in range(S):
                    js = _pair_rows(i, rev)
                    jsl = slice(int(js[0]), int(js[-1]) + 1)
                    dec = jnp.exp2(jnp.minimum(cb[i:i + 1] - cb[jsl], 0.0))
                    prods.append(qb[i:i + 1] * kb[jsl] * dec)
                    v_js.append(vb[jsl])
                prod[a] = jnp.concatenate(prods, axis=0).astype(BF16)
                vj[a] = jnp.concatenate(v_js, axis=0)
            if 1 <= a <= nb:
                scores[a - 1] = jnp.dot(prod.pop(a - 1), ones_bd, preferred_element_type=F32)
            if a >= 2:
                weighted = (scores.pop(a - 2) * vj.pop(a - 2)).astype(BF16)
                intra[a - 2] = jnp.dot(fold_ref[d], weighted, preferred_element_type=F32)
        o = jnp.concatenate([intra[a] for a in range(nb)], axis=0)
        st = st_s[r, d]
        inter = [None] * nb
        for a in (range(nb - 1, -1, -1) if rev else range(nb)):
            sl = slice(a * S, (a + 1) * S)
            inter[a] = _dot_nt(qt[sl], st.astype(BF16))
            st = st * g[a * S:a * S + 1] + jnp.where(bd, _dot_tn(v[sl], kt[sl]), 0.0)
        st_s[r, d] = st
        o_s[r, d, rows, :] = o + jnp.concatenate(inter, axis=0)

    st_s[...] = jnp.zeros_like(st_s)

    def walk_body(s, _):
        for r in range(ROWS):
            chunk(r, 0, s)
            chunk(r, 1, _bwd_chunk(s, n_ctx_chunks, n_chunks))
        return 0

    lax.fori_loop(0, n_chunks, walk_body, 0)

    def out_body(c, _):
        rows = pl.ds(pl.multiple_of(c * L, L), L)
        for r in range(ROWS):
            gate = _silu(g_ref[r, rows, :].astype(F32))
            o = o_s[r, 0, rows, :] + o_s[r, 1, rows, :]
            out_ref[r, rows, :] = (_head_norm(o, False) * gate).astype(out_ref.dtype)
        return 0

    lax.fori_loop(0, n_chunks, out_body, 0)


def _mixer_d(p, lb, n_ctx):
    b, n, _ = p.shape
    tri, same, fold = _hgrn2_tables()
    kern = functools.partial(_mixer_d_kernel, n=n, n_ctx=n_ctx)
    return pl.pallas_call(
        kern,
        grid=(b // ROWS,),
        in_specs=[_seq_spec(n, 10, ROWS), _seq_spec(n, 11, ROWS), _seq_spec(n, 12, ROWS), _seq_spec(n, 13, ROWS),
                  _seq_spec(n, 14, ROWS), _full_spec(lb.shape), _full_spec(tri.shape), _full_spec(same.shape),
                  _full_spec(fold.shape)],
        out_specs=pl.BlockSpec((ROWS, n, MIX_W), lambda bi: (bi, 0, 0)),
        out_shape=jax.ShapeDtypeStruct((b, n, MIX_W), BF16),
        scratch_shapes=[pltpu.VMEM((ROWS, 2, n, MIX_W), F32), pltpu.VMEM((ROWS, 2, MIX_W, MIX_W), F32)],
        compiler_params=_cparams("parallel"),
        name="mixer_hgrn2",
    )(p, p, p, p, p, lb, tri, same, fold)


def _pack_bf16_pairs(t):
    w = t.shape[1] // 2
    hi = pltpu.bitcast(t[:, :w].astype(BF16).astype(F32), jnp.uint32)
    lo = pltpu.bitcast(t[:, w:].astype(BF16).astype(F32), jnp.uint32)
    return pltpu.bitcast(hi | (lo >> 16), jnp.int32)


def _unpack_bf16_pairs(p):
    u = pltpu.bitcast(p, jnp.uint32)
    hi = pltpu.bitcast(u & jnp.uint32(0xFFFF0000), F32)
    lo = pltpu.bitcast(u << 16, F32)
    return hi, lo


def _merge_kernel(x_ref, m_ref, ya_ref, yb_ref, yc_ref, yd_ref, wm_ref, wb_ref, wo_ref, wr_ref, br_ref,
                  xo_ref, hp_ref, route_ref, counts_ref, carry_s, *, tm, n_ctx):
    i = pl.program_id(1)

    @pl.when((pl.program_id(0) == 0) & (i == 0))
    def _():
        carry_s[...] = jnp.zeros_like(carry_s)

    row = i * tm + lax.broadcasted_iota(jnp.int32, (tm, 1), 0)
    is_ctx = row < n_ctx
    m = m_ref[0]
    x = x_ref[0]
    d = x.shape[-1]
    h = _modulate(x, _pick(m, is_ctx, 0), _pick(m, is_ctx, 1)).astype(BF16)
    z = jnp.zeros((tm, d), F32)
    for nb, y_ref in enumerate((ya_ref, yb_ref, yc_ref, yd_ref)):
        gate = _sigmoid(jnp.dot(h, wm_ref[:, nb * d:(nb + 1) * d], preferred_element_type=F32))
        z = z + gate * jnp.dot(y_ref[0], wb_ref[nb], preferred_element_type=F32)
    mix = jnp.dot(z.astype(BF16), wo_ref[...], preferred_element_type=F32)
    x1 = x + _pick(m, is_ctx, 2) * mix
    xo_ref[0] = x1
    h2 = _modulate(x1, _pick(m, is_ctx, 3), _pick(m, is_ctx, 4))

    wr = wr_ref[...]
    wr_hi = wr.astype(BF16)
    wr_lo = (wr - wr_hi.astype(F32)).astype(BF16)
    h2_hi = h2.astype(BF16)
    h2_lo = (h2 - h2_hi.astype(F32)).astype(BF16)
    dot = functools.partial(jnp.dot, preferred_element_type=F32)
    logit = dot(h2_hi, wr_hi) + dot(h2_lo, wr_hi) + dot(h2_hi, wr_lo) + br_ref[...]
    lane = lax.broadcasted_iota(jnp.int32, (tm, LANES), 1)
    big = jnp.int32(LANES)
    is_group = (lane >= N_EXPERTS) & (lane < N_EXPERTS + N_GROUPS)
    gl = jnp.where(is_group, logit, NEG_INF)
    g_max = jnp.max(gl, axis=1, keepdims=True)
    g_idx = jnp.min(jnp.where(gl == g_max, lane, big), axis=1, keepdims=True) - N_EXPERTS
    g_prob = 1.0 / jnp.sum(jnp.where(is_group, jnp.exp(logit - g_max), 0.0), axis=1, keepdims=True)
    in_group = (lane < N_EXPERTS) & (lane // EXP_PER_GROUP == g_idx)
    el = jnp.where(in_group, logit, NEG_INF)
    v1 = jnp.max(el, axis=1, keepdims=True)
    i1 = jnp.min(jnp.where(el == v1, lane, big), axis=1, keepdims=True)
    el2 = jnp.where(lane == i1, NEG_INF, el)
    v2 = jnp.max(el2, axis=1, keepdims=True)
    i2 = jnp.min(jnp.where(el2 == v2, lane, big), axis=1, keepdims=True)
    e2 = jnp.exp(v2 - v1)
    w1 = g_prob / (1.0 + e2)
    w2 = g_prob * e2 / (1.0 + e2)
    first_low = i1 < i2
    lo = jnp.where(first_low, i1, i2) - g_idx * EXP_PER_GROUP
    hi = jnp.where(first_low, i2, i1) - g_idx * EXP_PER_GROUP
    pair = 3 * lo - ((lo * (lo - 1)) >> 1) + (hi - lo - 1)
    cls = g_idx * N_PAIRS + pair
    w_lo = jnp.where(first_low, w1, w2)
    w_hi = jnp.where(first_low, w2, w1)
    onehot = jnp.where(lane == cls, 1.0, 0.0)
    ri = lax.broadcasted_iota(jnp.int32, (tm, tm), 0)
    ci = lax.broadcasted_iota(jnp.int32, (tm, tm), 1)
    earlier = jnp.where(ci < ri, 1.0, 0.0).astype(BF16)
    before = jnp.dot(earlier, onehot.astype(BF16), preferred_element_type=F32) + carry_s[...]
    rank = jnp.sum(onehot * before, axis=1, keepdims=True)
    carry_s[...] += jnp.sum(onehot, axis=0, keepdims=True)
    counts_ref[...] = carry_s[...].astype(jnp.int32)
    cols = jnp.where(lane == 0, cls.astype(F32), jnp.where(lane == 1, rank, 0.0))
    pick = jnp.where(lax.broadcasted_iota(jnp.int32, (8, LANES), 0)
                     == lax.broadcasted_iota(jnp.int32, (8, LANES), 1), 1.0, 0.0).astype(BF16)
    route_ref[0] = sum(_dot_nt(pick, part) for part in _split3(cols)).astype(jnp.int32)
    gate_bits = pltpu.bitcast(jnp.where(lane == 0, w_lo, jnp.where(lane == 1, w_hi, 0.0)), jnp.int32)
    hp_ref[0] = jnp.concatenate([_pack_bf16_pairs(h2), gate_bits], axis=1)


def _merge(xc, mvec, ys, w_merge, w_branch, w_out, w_route, b_route, n_ctx):
    b, n, d = xc.shape
    tm = _token_tile(n, 640)
    kern = functools.partial(_merge_kernel, tm=tm, n_ctx=n_ctx)
    tok = lambda w: pl.BlockSpec((1, tm, w), lambda bi, i: (bi, i, 0))
    const = lambda shape: pl.BlockSpec(shape, lambda bi, i: (0,) * len(shape))
    return pl.pallas_call(
        kern,
        grid=(b, n // tm),
        in_specs=[tok(d), pl.BlockSpec((1, MOD_ROWS, d), lambda bi, i: (bi, 0, 0)),
                  tok(MIX_W), tok(MIX_W), tok(MIX_W), tok(MIX_W),
                  const(w_merge.shape), const(w_branch.shape), const(w_out.shape),
                  const(w_route.shape), const(b_route.shape)],
        out_specs=[tok(d), tok(d // 2 + LANES),
                   pl.BlockSpec((1, 8, tm), lambda bi, i: (bi * (n // tm) + i, 0, 0)), const((1, LANES))],
        out_shape=[jax.ShapeDtypeStruct((b, n, d), F32),
                   jax.ShapeDtypeStruct((b, n, d // 2 + LANES), jnp.int32),
                   jax.ShapeDtypeStruct((b * (n // tm), 8, tm), jnp.int32),
                   jax.ShapeDtypeStruct((1, LANES), jnp.int32)],
        scratch_shapes=[pltpu.VMEM((1, LANES), F32)],
        compiler_params=_cparams("arbitrary", "arbitrary"),
        name="merge_route",
    )(xc, mvec, *ys, w_merge, w_branch, w_out, w_route, b_route)


def _row_move(table, idx, n_out=None):
    scatter = n_out is not None
    n_idx = idx.shape[0]
    width = table.shape[1]
    per_worker = n_idx // SC_WORKERS
    assert per_worker * SC_WORKERS == n_idx and per_worker % 8 == 0
    chunk = max(c for c in range(8, SC_MAX_CHUNK + 1, 8) if per_worker % c == 0)
    n_chunks = per_worker // chunk
    mesh = plsc.VectorSubcoreMesh(core_axis_name="c", subcore_axis_name="s",
                                  num_cores=SC_CORES, num_subcores=SC_SUBCORES)

    @functools.partial(
        pl.kernel, mesh=mesh,
        out_type=jax.ShapeDtypeStruct((n_out if scatter else n_idx, width), table.dtype),
        scratch_types=[pltpu.VMEM((chunk,), jnp.int32),
                       pltpu.VMEM((chunk, width), table.dtype),
                       pltpu.SemaphoreType.DMA],
        name="sc_row_scatter" if scatter else "sc_row_gather",
    )
    def move(table_hbm, idx_hbm, out_hbm, idx_v, rows_v, sem):
        worker = lax.axis_index("s") * SC_CORES + lax.axis_index("c")
        base = worker * per_worker

        @pl.loop(0, n_chunks)
        def _(j):
            off = pl.multiple_of(base + j * chunk, 8)
            pltpu.sync_copy(idx_hbm.at[pl.ds(off, chunk)], idx_v)
            if scatter:
                pltpu.sync_copy(table_hbm.at[pl.ds(off, chunk)], rows_v)
                pltpu.async_copy(rows_v, out_hbm.at[idx_v], sem).wait()
            else:
                pltpu.async_copy(table_hbm.at[idx_v], rows_v, sem).wait()
                pltpu.sync_copy(rows_v, out_hbm.at[pl.ds(off, chunk)])

    return move(table, idx)


def _ffn(h, w1_ref, w3_ref, w2_ref):
    a = jnp.dot(h, w1_ref[0, 0], preferred_element_type=F32)
    g = jnp.dot(h, w3_ref[0, 0], preferred_element_type=F32)
    return jnp.dot((_silu(a) * g).astype(BF16), w2_ref[0, 0], preferred_element_type=F32)


def _moe_kernel(elo_ref, ehi_ref, valid_ref, xs_ref, w1a_ref, w3a_ref, w2a_ref, w1b_ref, w3b_ref, w2b_ref,
                ys_ref):
    j = pl.program_id(0)
    half = ys_ref.shape[1]

    @pl.when(valid_ref[j] != 0)
    def _():
        hi, lo = _unpack_bf16_pairs(xs_ref[:, :half])
        h = jnp.concatenate([hi.astype(BF16), lo.astype(BF16)], axis=1)
        gates = pltpu.bitcast(xs_ref[:, half:], F32)
        y = (gates[:, 0:1] * _ffn(h, w1a_ref, w3a_ref, w2a_ref)
             + gates[:, 1:2] * _ffn(h, w1b_ref, w3b_ref, w2b_ref))
        ys_ref[...] = _pack_bf16_pairs(y)

    @pl.when(valid_ref[j] == 0)
    def _():
        ys_ref[...] = jnp.zeros_like(ys_ref)


def _moe_experts(xs, tile_elo, tile_ehi, tile_valid, w1, w3, w2, layer):
    n_slots, width = xs.shape
    d = w1.shape[2]
    n_tiles = n_slots // MOE_TILE
    wspec = lambda shape, which: pl.BlockSpec(
        (1, 1) + shape, lambda j, elo, ehi, valid: (layer, (elo, ehi)[which][j], 0, 0))
    grid_spec = pltpu.PrefetchScalarGridSpec(
        num_scalar_prefetch=3,
        grid=(n_tiles,),
        in_specs=[pl.BlockSpec((MOE_TILE, width), lambda j, elo, ehi, valid: (j, 0)),
                  wspec((d, D_EXPERT), 0), wspec((d, D_EXPERT), 0), wspec((D_EXPERT, d), 0),
                  wspec((d, D_EXPERT), 1), wspec((d, D_EXPERT), 1), wspec((D_EXPERT, d), 1)],
        out_specs=pl.BlockSpec((MOE_TILE, d // 2), lambda j, elo, ehi, valid: (j, 0)),
    )
    return pl.pallas_call(
        _moe_kernel,
        grid_spec=grid_spec,
        out_shape=jax.ShapeDtypeStruct((n_slots, d // 2), jnp.int32),
        compiler_params=_cparams("arbitrary"),
        name="moe_experts",
    )(tile_elo, tile_ehi, tile_valid, xs, w1, w3, w2, w1, w3, w2)


def _moe_combine_kernel(x_ref, m_ref, y_ref, *rest, tm, n_ctx, row0):
    o_ref = rest[-1]
    row = row0 + pl.program_id(1) * tm + lax.broadcasted_iota(jnp.int32, (tm, 1), 0)
    hi, lo = _unpack_bf16_pairs(y_ref[0])
    x = x_ref[0] + _pick(m_ref[0], row < n_ctx, 5) * jnp.concatenate([hi, lo], axis=1)
    if len(rest) == 2:
        ms = jnp.mean(x * x, axis=-1, keepdims=True)
        x = x * lax.rsqrt(ms + EPS) * rest[0][...]
    o_ref[0] = x


def _moe_combine(x1, mvec, yt, n_ctx, final_w=None):
    b, n, d = x1.shape
    if final_w is None:
        tm, skip, n_out, extra, extra_specs = _token_tile(n, 1100), 0, n, (), []
    else:
        tm = _token_tile(n_ctx, 1024)
        skip, n_out = n_ctx // tm, n - n_ctx
        extra, extra_specs = (final_w.reshape(1, d),), [pl.BlockSpec((1, d), lambda bi, i: (0, 0))]
    kern = functools.partial(_moe_combine_kernel, tm=tm, n_ctx=n_ctx, row0=skip * tm)
    tok = lambda w: pl.BlockSpec((1, tm, w), lambda bi, i: (bi, i + skip, 0))
    return pl.pallas_call(
        kern,
        grid=(b, n_out // tm),
        in_specs=[tok(d), pl.BlockSpec((1, MOD_ROWS, d), lambda bi, i: (bi, 0, 0)), tok(d // 2)] + extra_specs,
        out_specs=pl.BlockSpec((1, tm, d), lambda bi, i: (bi, i, 0)),
        out_shape=jax.ShapeDtypeStruct((b, n_out, d), F32),
        compiler_params=_cparams("parallel", "parallel"),
        name="moe_combine",
    )(x1, mvec, yt, *extra)


def _moe(x1, mvec, hp, route, counts, w1, w3, w2, layer, n_ctx, final_w=None):
    b, n, d = x1.shape
    m = b * n
    n_slots = m + N_CLASSES * MOE_TILE
    cls = route[:, 0, :].reshape(m)
    rank = route[:, 1, :].reshape(m)
    cnt = counts[0, :N_CLASSES]
    padded = (cnt + MOE_TILE - 1) // MOE_TILE * MOE_TILE
    ends = jnp.cumsum(padded)
    pos = (ends - padded)[cls] + rank
    tile_start = jnp.arange(n_slots // MOE_TILE, dtype=jnp.int32) * MOE_TILE
    tile_cls = jnp.minimum(jnp.searchsorted(ends, tile_start, side="right"), N_CLASSES - 1).astype(jnp.int32)
    tile_valid = (tile_start < ends[-1]).astype(jnp.int32)
    pair_lo = jnp.asarray([0, 0, 0, 1, 1, 2], jnp.int32)
    pair_hi = jnp.asarray([1, 2, 3, 2, 3, 3], jnp.int32)
    group = tile_cls // N_PAIRS
    tile_elo = group * EXP_PER_GROUP + pair_lo[tile_cls % N_PAIRS]
    tile_ehi = group * EXP_PER_GROUP + pair_hi[tile_cls % N_PAIRS]

    xs = _row_move(hp.reshape(m, hp.shape[-1]), pos, n_out=n_slots)
    ys = _moe_experts(xs, tile_elo, tile_ehi, tile_valid, w1, w3, w2, layer)
    yt = _row_move(ys, pos)
    return _moe_combine(x1, mvec, yt.reshape(b, n, d // 2), n_ctx, final_w)


def _rope_tables(n_lat, n_ctx):
    rows = n_lat // GRID_W
    row = jnp.repeat(jnp.arange(rows), GRID_W).astype(F32)
    col = jnp.tile(jnp.arange(GRID_W), rows).astype(F32)
    nq = HEAD_DIM // 4
    inv = jnp.power(ROPE_BASE, -jnp.arange(nq, dtype=F32) / nq)
    ang = jnp.concatenate([row[:, None] * inv, col[:, None] * inv], -1)
    cos = jnp.cos(ang)
    sin = jnp.sin(ang)
    cos_h = jnp.concatenate([cos, cos], -1)
    sin_h = jnp.concatenate([-sin, sin], -1)
    cos_full = jnp.concatenate([jnp.ones((n_ctx, HEAD_DIM), F32), cos_h], 0)
    sin_full = jnp.concatenate([jnp.zeros((n_ctx, HEAD_DIM), F32), sin_h], 0)
    return jnp.tile(cos_full, (1, N_HEADS)), jnp.tile(sin_full, (1, N_HEADS))


def _block_diag_heads(w):
    eye = jnp.eye(N_HEADS, dtype=w.dtype)
    return jnp.einsum("hij,hg->higj", w, eye).reshape(MIX_W, MIX_W)


def kernel(x, c, ctx, c_ctx, w_mod, b_mod, w_in, a_conv_w, a_conv_b, a_gate_w, a_gate_b, a_lambda, b_theta,
           c_conv_w, c_conv_b, c_gate_b, d_lb, w_branch, w_out, moe_w_group, moe_b_group, moe_w_router,
           moe_b_router, moe_w1, moe_w3, moe_w2, final_norm_w):
    bsz, n_lat, d = x.shape
    n_ctx = ctx.shape[1]
    depth = w_mod.shape[0]
    assert n_ctx % CHUNK == 0 and n_lat % CHUNK == 0 and n_ctx % 256 == 0 and n_lat % 256 == 0

    xc = jnp.concatenate([ctx, x], axis=1)
    cos, sin = _rope_tables(n_lat, n_ctx)

    cc = jnp.zeros((8, d), F32).at[:bsz].set(c).at[bsz].set(c_ctx)
    mod = _mod_vectors(cc, w_mod, b_mod)
    mx = mod[:, :bsz].reshape(depth, bsz, N_MOD, d)
    mc = jnp.broadcast_to(mod[:, bsz].reshape(depth, 1, N_MOD, d), (depth, bsz, N_MOD, d))
    pad = jnp.zeros((depth, bsz, 8 - N_MOD, d), F32)
    mvec = jnp.concatenate([mx, pad, mc, pad], axis=2)

    lbs = jnp.cumsum(jax.nn.softmax(d_lb.astype(F32), axis=0), axis=0)
    lbs = lbs - lbs[0]

    n_mix_cols = 15 * MIX_W
    gate0 = 10 * MIX_W
    n_gate = 4 * N_HEADS
    w1b = moe_w1.astype(BF16)
    w3b = moe_w3.astype(BF16)
    w2b = moe_w2.astype(BF16)

    for l in range(depth):
        wl = w_in[l]
        w_mix = jnp.concatenate([wl[:, :gate0], wl[:, gate0 + n_gate:n_mix_cols + n_gate]], axis=1).astype(BF16)
        w_gate = jnp.pad(wl[:, gate0:gate0 + n_gate], ((0, 0), (0, LANES - n_gate))).astype(BF16)
        w_merge = wl[:, n_mix_cols + n_gate:].astype(BF16)
        p, gates_c = _in_proj(xc, mvec[l], w_mix, w_gate, cos, sin, n_ctx)

        gw = jnp.concatenate([_block_diag_heads(a_gate_w[l, dd, j]) for dd in range(2) for j in range(2)],
                             axis=1).astype(BF16)
        gb = a_gate_b[l].reshape(1, 4 * MIX_W)
        ya = _mixer_a(p, a_conv_w[l], a_conv_b[l].reshape(1, MIX_W), gw, gb, a_lambda[l], n_ctx)

        log_gamma = jnp.repeat(jax.nn.log_sigmoid(b_theta[l].astype(F32)), HEAD_DIM, axis=1)
        yb = _mixer_b(p, log_gamma, n_ctx)

        gate_b = jnp.pad(c_gate_b[l].reshape(1, n_gate), ((0, 0), (0, LANES - n_gate)))
        yc = _mixer_c(p, gates_c, c_conv_w[l], c_conv_b[l].reshape(1, 2 * MIX_W), gate_b, n_ctx)

        yd = _mixer_d(p, lbs[l].reshape(1, MIX_W), n_ctx)

        w_route = jnp.pad(jnp.concatenate([moe_w_router[l], moe_w_group[l]], axis=1),
                          ((0, 0), (0, LANES - N_EXPERTS - N_GROUPS)))
        b_route = jnp.pad(jnp.concatenate([moe_b_router[l], moe_b_group[l]]),
                          (0, LANES - N_EXPERTS - N_GROUPS)).reshape(1, LANES)
        x1, hp, route, counts = _merge(xc, mvec[l], (ya, yb, yc, yd), w_merge, w_branch[l].astype(BF16),
                                       w_out[l].astype(BF16), w_route, b_route, n_ctx)
        xc = _moe(x1, mvec[l], hp, route, counts, w1b, w3b, w2b, l, n_ctx,
                  final_w=final_norm_w if l == depth - 1 else None)

    return xc
```

```python
import functools

import jax
import jax.numpy as jnp
import numpy as np
from jax import lax
from jax.experimental import pallas as pl
from jax.experimental.pallas import tpu as pltpu
from jax.experimental.pallas import tpu_sc as plsc

F32 = jnp.float32
BF16 = jnp.bfloat16

EPS = 1e-6
N_HEADS = 4
HEAD_DIM = 64
MIX_W = N_HEADS * HEAD_DIM
N_BRANCH = 4
CONV_W = 4
LRU_C = 8.0
GRID_W = 64
ROPE_BASE = 10000.0
N_GROUPS = 4
EXP_PER_GROUP = 4
N_EXPERTS = N_GROUPS * EXP_PER_GROUP
D_EXPERT = 512
N_MOD = 6
M_INIT = -1e30

CHUNK = 128
RET_CHUNK = 256
CONV_STEP = 128
OUT_ROWS = 256
SUB = 16
HGRN_CHUNK = 256
HALO = 16
LANES = 128
MOD_ROWS = 16
ROWS = 1
VMEM_LIMIT_BYTES = 56 * 1024 * 1024
N_PAIRS = 6
N_CLASSES = N_GROUPS * N_PAIRS
MOE_TILE = 256
SC_CORES = 2
SC_SUBCORES = 16
SC_WORKERS = SC_CORES * SC_SUBCORES
SC_MAX_CHUNK = 32
NEG_INF = float("-inf")
LOG2_E = 1.4426950408889634


def _cparams(*sem):
    return pltpu.CompilerParams(dimension_semantics=sem, vmem_limit_bytes=VMEM_LIMIT_BYTES)


def _token_tile(n, cap):
    best = 16
    for t in range(16, cap + 1, 16):
        if n % t == 0:
            best = t
    return best


def _modulate(x, shift, scale):
    ms = jnp.mean(x * x, axis=-1, keepdims=True)
    return x * lax.rsqrt(ms + EPS) * (1.0 + scale) + shift


def _pick(m, is_ctx, k):
    return jnp.where(is_ctx, m[8 + k:9 + k], m[k:k + 1])


def _sigmoid(x):
    return 0.5 * jnp.tanh(0.5 * x) + 0.5


def _silu(x):
    return x * _sigmoid(x)


def _log_sigmoid(x):
    return jnp.minimum(x, 0.0) - jnp.log(1.0 + jnp.exp(-jnp.abs(x)))


def _split3(x):
    hi = x.astype(BF16)
    r = x - hi.astype(F32)
    mid = r.astype(BF16)
    lo = (r - mid.astype(F32)).astype(BF16)
    return hi, mid, lo


def _split2(x):
    hi = x.astype(BF16)
    return hi, (x - hi.astype(F32)).astype(BF16)


def _sel_dot(sel, x):
    hi, lo = _split2(x)
    d = functools.partial(jnp.dot, preferred_element_type=F32)
    return d(sel, hi) + d(sel, lo)


def _dot_sel(x, sel):
    hi, lo = _split2(x)
    d = functools.partial(jnp.dot, preferred_element_type=F32)
    return d(hi, sel) + d(lo, sel)


def _dot_nt(a, b):
    return lax.dot_general(a, b, (((1,), (1,)), ((), ())), preferred_element_type=F32)


def _dot_tn(a, b):
    return lax.dot_general(a, b, (((0,), (0,)), ((), ())), preferred_element_type=F32)


def _head_of_lane(width):
    return lax.broadcasted_iota(jnp.int32, (1, width), 1) // HEAD_DIM


def _block_ones(width):
    r = lax.broadcasted_iota(jnp.int32, (width, width), 0) // HEAD_DIM
    c = lax.broadcasted_iota(jnp.int32, (width, width), 1) // HEAD_DIM
    return r == c


def _head_norm(o, center):
    ones = jnp.where(_block_ones(MIX_W), 1.0, 0.0).astype(BF16)
    inv = 1.0 / HEAD_DIM
    if center:
        o = o - _dot_sel(o, ones) * inv
    var = _dot_sel(o * o, ones) * inv
    return o * lax.rsqrt(var + EPS)


def _bwd_chunk(s, n_ctx_chunks, n_chunks):
    return jnp.where(s < n_ctx_chunks, n_ctx_chunks - 1 - s, n_chunks - 1 + n_ctx_chunks - s)


def _conv_chunk(c, operands, *, n, n_ctx):
    L = CONV_STEP
    wl = L + 2 * HALO
    r0 = c * L
    start = pl.multiple_of(jnp.clip(r0 - HALO, 0, n - wl), HALO)
    off = r0 - start
    shift = (lax.broadcasted_iota(jnp.int32, (L, wl), 1) - lax.broadcasted_iota(jnp.int32, (L, wl), 0)) - off
    g = r0 + lax.broadcasted_iota(jnp.int32, (L, 1), 0)
    seg_g = jnp.where(g < n_ctx, 0, 1)
    sels = []
    for k in range(CONV_W):
        src = g + (k - 2)
        row_ok = (jnp.where(src < n_ctx, 0, 1) == seg_g) & (src >= 0) & (src < n)
        sels.append(jnp.where((shift == k - 2) & row_ok, 1.0, 0.0).astype(BF16))
    outs = []
    for ref, w, b in operands:
        win = ref[0, pl.ds(start, wl), :]
        y = b
        for k in range(CONV_W):
            y = y + w[k:k + 1] * jnp.dot(sels[k], win, preferred_element_type=F32)
        outs.append(y)
    return outs


def _mod_kernel(cc_ref, w_ref, b_ref, o_ref):
    s = _silu(cc_ref[...])
    o_ref[0] = jnp.dot(s, w_ref[0], precision=lax.Precision.HIGHEST, preferred_element_type=F32) + b_ref[0]


def _mod_vectors(cc, w_mod, b_mod):
    depth, d, dm = w_mod.shape
    tn = 1536
    return pl.pallas_call(
        _mod_kernel,
        grid=(depth, dm // tn),
        in_specs=[pl.BlockSpec((8, d), lambda l, j: (0, 0)),
                  pl.BlockSpec((1, d, tn), lambda l, j: (l, 0, j)),
                  pl.BlockSpec((1, 1, tn), lambda l, j: (l, 0, j))],
        out_specs=pl.BlockSpec((1, 8, tn), lambda l, j: (l, 0, j)),
        out_shape=jax.ShapeDtypeStruct((depth, 8, dm), F32),
        compiler_params=_cparams("parallel", "parallel"),
        name="mod_vectors",
    )(cc, w_mod, b_mod.reshape(depth, 1, dm))


def _rope(t, cos, sin):
    lane = lax.broadcasted_iota(jnp.int32, (1, MIX_W), 1) % HEAD_DIM
    half = HEAD_DIM // 2
    swapped = jnp.where(lane < half, pltpu.roll(t, MIX_W - half, 1), pltpu.roll(t, half, 1))
    return t * cos + swapped * sin


RET_Q_COL = 2 * MIX_W
RET_K_COL = 3 * MIX_W


def _in_proj_kernel(x_ref, m_ref, w_ref, wg_ref, cos_ref, sin_ref, p_ref, g_ref, *, tm, n_ctx, col_chunk):
    i = pl.program_id(1)
    row = i * tm + lax.broadcasted_iota(jnp.int32, (tm, 1), 0)
    is_ctx = row < n_ctx
    m = m_ref[0]
    h = _modulate(x_ref[0], _pick(m, is_ctx, 0), _pick(m, is_ctx, 1)).astype(BF16)
    for j in range(w_ref.shape[1] // col_chunk):
        sl = slice(j * col_chunk, (j + 1) * col_chunk)
        t = jnp.dot(h, w_ref[:, sl], preferred_element_type=F32)
        if j * col_chunk == RET_Q_COL:
            t = _rope(t, cos_ref[...], sin_ref[...])
        elif j * col_chunk == RET_K_COL:
            t = _rope(t, cos_ref[...], sin_ref[...]) * HEAD_DIM ** -0.5
        p_ref[0, :, sl] = t.astype(BF16)
    g_ref[0] = jnp.dot(h, wg_ref[...], preferred_element_type=F32)


def _in_proj(xc, mvec, w_mix, w_gate, cos, sin, n_ctx):
    b, n, d = xc.shape
    tm = _token_tile(n, 640)
    wc = w_mix.shape[1]
    kern = functools.partial(_in_proj_kernel, tm=tm, n_ctx=n_ctx, col_chunk=MIX_W)
    return pl.pallas_call(
        kern,
        grid=(b, n // tm),
        in_specs=[pl.BlockSpec((1, tm, d), lambda bi, i: (bi, i, 0)),
                  pl.BlockSpec((1, MOD_ROWS, d), lambda bi, i: (bi, 0, 0)),
                  pl.BlockSpec((d, wc), lambda bi, i: (0, 0)),
                  pl.BlockSpec((d, LANES), lambda bi, i: (0, 0)),
                  pl.BlockSpec((tm, MIX_W), lambda bi, i: (i, 0)),
                  pl.BlockSpec((tm, MIX_W), lambda bi, i: (i, 0))],
        out_specs=[pl.BlockSpec((1, tm, wc), lambda bi, i: (bi, i, 0)),
                   pl.BlockSpec((1, tm, LANES), lambda bi, i: (bi, i, 0))],
        out_shape=[jax.ShapeDtypeStruct((b, n, wc), BF16),
                   jax.ShapeDtypeStruct((b, n, LANES), F32)],
        compiler_params=_cparams("parallel", "parallel"),
        name="in_proj",
    )(xc, mvec, w_mix, w_gate, cos, sin)


def _lin_scan(a, x, carry, reverse):
    L = a.shape[0]
    row = lax.broadcasted_iota(jnp.int32, (L, 1), 0)
    k = 1
    while k < L:
        if reverse:
            a_s = pltpu.roll(a, L - k, 0)
            x_s = pltpu.roll(x, L - k, 0)
            valid = row < L - k
        else:
            a_s = pltpu.roll(a, k, 0)
            x_s = pltpu.roll(x, k, 0)
            valid = row >= k
        x = jnp.where(valid, a * x_s + x, x)
        a = jnp.where(valid, a * a_s, a)
        k *= 2
    h = x + a * carry
    return h, (h[0:1] if reverse else h[L - 1:L])


def _mixer_a_kernel(ax_ref, ag_ref, cw_ref, cb_ref, gw_ref, gb_ref, lam_ref, out_ref, u_s, hf_s,
                    *, n, n_ctx):
    L = CHUNK
    n_chunks = n // L
    n_ctx_chunks = n_ctx // L
    cw = cw_ref[...]
    cb = cb_ref[...]
    log_lam = _log_sigmoid(lam_ref[...])

    def direction(d, c, carry):
        r0 = pl.multiple_of(c * L, L)
        u = u_s[pl.ds(r0, L), :]
        pre = jnp.dot(u.astype(BF16), gw_ref[:, d * 2 * MIX_W:(d + 1) * 2 * MIX_W],
                      preferred_element_type=F32) + gb_ref[:, d * 2 * MIX_W:(d + 1) * 2 * MIX_W]
        r = _sigmoid(pre[:, :MIX_W])
        gi = _sigmoid(pre[:, MIX_W:])
        log_a = LRU_C * r * log_lam[d:d + 1]
        a = jnp.exp(log_a)
        inp = jnp.sqrt(1.0 - jnp.exp(2.0 * log_a)) * (gi * u)
        h, new_carry = _lin_scan(a, inp, carry, reverse=(d == 1))
        return r0, h, new_carry

    def conv_body(c, _):
        r0 = pl.multiple_of(c * CONV_STEP, CONV_STEP)
        u_s[pl.ds(r0, CONV_STEP), :] = _conv_chunk(c, [(ax_ref, cw, cb)], n=n, n_ctx=n_ctx)[0]
        return 0

    lax.fori_loop(0, n // CONV_STEP, conv_body, 0)

    def walk_body(s, carries):
        r0, h, carry_f = direction(0, s, carries[0])
        hf_s[0, pl.ds(r0, L), :] = h
        r0, h, carry_b = direction(1, _bwd_chunk(s, n_ctx_chunks, n_chunks), carries[1])
        hf_s[1, pl.ds(r0, L), :] = h
        return carry_f, carry_b

    zero = jnp.zeros((1, MIX_W), F32)
    lax.fori_loop(0, n_chunks, walk_body, (zero, zero))

    def out_body(c, _):
        rows = pl.ds(pl.multiple_of(c * L, L), L)
        gate = jax.nn.gelu(ag_ref[0, rows, :].astype(F32), approximate=True)
        out_ref[0, rows, :] = (gate * (hf_s[0, rows, :] + hf_s[1, rows, :])).astype(out_ref.dtype)
        return 0

    lax.fori_loop(0, n_chunks, out_body, 0)


def _seq_spec(n, col, rows=1):
    if rows == 1:
        return pl.BlockSpec((1, n, MIX_W), lambda b: (b, 0, col))
    return pl.BlockSpec((rows, n, MIX_W), lambda b: (b, 0, col), pipeline_mode=pl.Buffered(1))


def _full_spec(shape):
    return pl.BlockSpec(shape, lambda b: (0,) * len(shape))


def _mixer_a(p, conv_w, conv_b, gate_w, gate_b, lam, n_ctx):
    b, n, _ = p.shape
    kern = functools.partial(_mixer_a_kernel, n=n, n_ctx=n_ctx)
    return pl.pallas_call(
        kern,
        grid=(b,),
        in_specs=[_seq_spec(n, 0), _seq_spec(n, 1),
                  _full_spec(conv_w.shape), _full_spec(conv_b.shape),
                  _full_spec(gate_w.shape), _full_spec(gate_b.shape), _full_spec(lam.shape)],
        out_specs=pl.BlockSpec((1, n, MIX_W), lambda bi: (bi, 0, 0)),
        out_shape=jax.ShapeDtypeStruct((b, n, MIX_W), BF16),
        scratch_shapes=[pltpu.VMEM((n, MIX_W), F32), pltpu.VMEM((2, n, MIX_W), F32)],
        compiler_params=_cparams("parallel"),
        name="mixer_rglru",
    )(p, p, conv_w, conv_b, gate_w, gate_b, lam)


def _mixer_b_kernel(q_ref, k_ref, v_ref, g_ref, lg_ref, out_ref, oi_s, dec_s, st_s, *, n, n_ctx):
    L = RET_CHUNK
    n_chunks = n // L
    n_ctx_chunks = n_ctx // L
    head = _head_of_lane(MIX_W)
    bd = _block_ones(MIX_W)
    lgf = lg_ref[0:1, :]
    lgb = lg_ref[1:2, :]
    pos = lax.broadcasted_iota(jnp.int32, (L, 1), 0).astype(F32)
    diff = (lax.broadcasted_iota(jnp.int32, (L, L), 0) - lax.broadcasted_iota(jnp.int32, (L, L), 1)).astype(F32)
    for h in range(N_HEADS):
        lf = lgf[:, h * HEAD_DIM:h * HEAD_DIM + 1]
        lb = lgb[:, h * HEAD_DIM:h * HEAD_DIM + 1]
        dec_s[h] = jnp.where(diff >= 0, jnp.exp(lf * jnp.maximum(diff, 0.0)), jnp.exp(lb * jnp.maximum(-diff, 0.0)))

    qw = (jnp.exp(lgf * (pos + 1.0)), jnp.exp(lgb * (float(L) - pos)))
    kw = (jnp.exp(lgf * (float(L) - 1.0 - pos)), jnp.exp(lgb * pos))
    g_chunk = (jnp.exp(lgf * float(L)), jnp.exp(lgb * float(L)))

    def state_step(d, c):
        rows = pl.ds(pl.multiple_of(c * L, L), L)
        q = q_ref[0, rows, :].astype(F32)
        k = k_ref[0, rows, :].astype(F32)
        st = st_s[d]
        oi_s[d, rows, :] = jnp.dot((q * qw[d]).astype(BF16), st.astype(BF16), preferred_element_type=F32)
        kv = _dot_tn((k * kw[d]).astype(BF16), v_ref[0, rows, :])
        st_s[d] = g_chunk[d] * st + jnp.where(bd, kv, 0.0)

    st_s[...] = jnp.zeros_like(st_s)

    def walk_body(s, _):
        state_step(0, s)
        state_step(1, _bwd_chunk(s, n_ctx_chunks, n_chunks))
        return 0

    lax.fori_loop(0, n_chunks, walk_body, 0)

    def out_body(c, _):
        rows = pl.ds(pl.multiple_of(c * L, L), L)
        q = q_ref[0, rows, :]
        k = k_ref[0, rows, :]
        v = v_ref[0, rows, :]
        scs, vals = [], []
        for h in range(N_HEADS):
            hm = head == h
            scs.append((_dot_nt(jnp.where(hm, q, jnp.zeros_like(q)), k) * dec_s[h]).astype(BF16))
            vals.append(jnp.where(hm, v, jnp.zeros_like(v)))
        o = (oi_s[0, rows, :] + oi_s[1, rows, :]
             + jnp.dot(jnp.concatenate(scs, axis=1), jnp.concatenate(vals, axis=0), preferred_element_type=F32))
        gate = _silu(g_ref[0, rows, :].astype(F32))
        out_ref[0, rows, :] = (_head_norm(o, True) * gate).astype(out_ref.dtype)
        return 0

    lax.fori_loop(0, n_chunks, out_body, 0, unroll=2)


def _mixer_b(p, log_gamma, n_ctx):
    b, n, _ = p.shape
    kern = functools.partial(_mixer_b_kernel, n=n, n_ctx=n_ctx)
    return pl.pallas_call(
        kern,
        grid=(b,),
        in_specs=[_seq_spec(n, 2), _seq_spec(n, 3), _seq_spec(n, 4), _seq_spec(n, 5),
                  _full_spec(log_gamma.shape)],
        out_specs=pl.BlockSpec((1, n, MIX_W), lambda bi: (bi, 0, 0)),
        out_shape=jax.ShapeDtypeStruct((b, n, MIX_W), BF16),
        scratch_shapes=[pltpu.VMEM((2, n, MIX_W), F32),
                        pltpu.VMEM((N_HEADS, RET_CHUNK, RET_CHUNK), F32),
                        pltpu.VMEM((2, MIX_W, MIX_W), F32)],
        compiler_params=_cparams("parallel"),
        name="mixer_retention",
    )(p, p, p, p, log_gamma)


def _mixer_c_kernel(q_ref, k_ref, v_ref, o_ref, g_ref, cw_ref, cb_ref, gb_ref, out_ref,
                    qs_s, ks_s, hf_s, c_s, n_s, m_s, *, n, n_ctx):
    L = CHUNK
    n_chunks = n // L
    n_ctx_chunks = n_ctx // L
    head = _head_of_lane(MIX_W)
    bd = _block_ones(MIX_W)
    ones_bd = jnp.where(bd, 1.0, 0.0).astype(BF16)
    cw = cw_ref[...]
    cb = cb_ref[...]
    ri = lax.broadcasted_iota(jnp.int32, (L, L), 0)
    ci = lax.broadcasted_iota(jnp.int32, (L, L), 1)
    gl = lax.broadcasted_iota(jnp.int32, (LANES, MIX_W), 0)
    hl = lax.broadcasted_iota(jnp.int32, (LANES, MIX_W), 1) // HEAD_DIM

    def expand(kind):
        return jnp.where(gl == kind * N_HEADS + hl, 1.0, 0.0).astype(BF16)

    def conv_body(c, _):
        rows = pl.ds(pl.multiple_of(c * CONV_STEP, CONV_STEP), CONV_STEP)
        cq, ck = _conv_chunk(c, [(q_ref, cw[:, :MIX_W], cb[:, :MIX_W]), (k_ref, cw[:, MIX_W:], cb[:, MIX_W:])],
                             n=n, n_ctx=n_ctx)
        qs_s[rows, :] = _silu(cq).astype(BF16)
        ks_s[rows, :] = (_silu(ck) * HEAD_DIM ** -0.5).astype(BF16)
        return 0

    lax.fori_loop(0, n // CONV_STEP, conv_body, 0)

    def chunk(d, c):
        rev = d == 1
        rows = pl.ds(pl.multiple_of(c * L, L), L)
        q = qs_s[rows, :]
        k = ks_s[rows, :]
        v = v_ref[0, rows, :]
        g = g_ref[0, rows, :] + gb_ref[...]
        log_i = _dot_sel(g, expand(2 * d))
        log_f = _dot_sel(_log_sigmoid(g), expand(2 * d + 1))
        causal = (ci >= ri) if rev else (ci <= ri)
        tri = jnp.where(causal, 1.0, 0.0).astype(BF16)
        cum = _sel_dot(tri, log_f)
        cum_end = cum[0:1] if rev else cum[L - 1:L]
        m_prev = m_s[d]
        c_prev = c_s[d]
        n_prev = n_s[d]
        row_src = (log_i - cum).T
        m_inter = cum + m_prev
        num_inter = jnp.dot(q, c_prev.astype(BF16), preferred_element_type=F32)
        qn = _dot_sel(q.astype(F32) * n_prev, ones_bd)
        m_q = jnp.zeros((L, MIX_W), F32)
        scs, rhs = [], []
        for h in range(N_HEADS):
            hm = head == h
            lane0 = h * HEAD_DIM
            log_d = jnp.where(causal, cum[:, lane0:lane0 + 1] + row_src[lane0:lane0 + 1, :], NEG_INF)
            m_h = jnp.maximum(jnp.max(log_d, axis=1, keepdims=True), m_inter[:, lane0:lane0 + 1])
            sc = _dot_nt(jnp.where(hm, q, jnp.zeros_like(q)), k) * jnp.exp(log_d - m_h)
            scs.append(sc.astype(BF16))
            rhs.append(jnp.concatenate([jnp.where(hm, v, jnp.zeros_like(v)),
                                        jnp.broadcast_to(jnp.where(hm, 1.0, 0.0).astype(BF16), (L, MIX_W))], axis=1))
            m_q = jnp.where(hm, m_h, m_q)
        intra = jnp.dot(jnp.concatenate(scs, axis=1), jnp.concatenate(rhs, axis=0), preferred_element_type=F32)
        s_inter = jnp.exp(m_inter - m_q)
        num = intra[:, :MIX_W] + s_inter * num_inter
        den = intra[:, MIX_W:] + s_inter * qn
        hid = num / jnp.maximum(jnp.abs(den), jnp.exp(-m_q))
        log_w = cum_end - cum + log_i
        m_loc = jnp.max(log_w, axis=0, keepdims=True)
        kw = k.astype(F32) * jnp.exp(log_w - m_loc)
        c_loc = jnp.where(bd, _dot_tn(kw.astype(BF16), v), 0.0)
        n_loc = jnp.sum(kw, axis=0, keepdims=True)
        m_new = jnp.maximum(cum_end + m_prev, m_loc)
        s_old = jnp.exp(cum_end + m_prev - m_new)
        s_loc = jnp.exp(m_loc - m_new)
        c_s[d] = s_old * c_prev + s_loc * c_loc
        n_s[d] = s_old * n_prev + s_loc * n_loc
        m_s[d] = m_new
        return rows, hid

    c_s[...] = jnp.zeros_like(c_s)
    n_s[...] = jnp.zeros_like(n_s)
    m_s[...] = jnp.full(m_s.shape, M_INIT, F32)

    def walk_body(s, _):
        rows, hid = chunk(0, s)
        hf_s[0, rows, :] = hid
        rows, hid = chunk(1, _bwd_chunk(s, n_ctx_chunks, n_chunks))
        hf_s[1, rows, :] = hid
        return 0

    lax.fori_loop(0, n_chunks, walk_body, 0)

    def out_body(c, _):
        rows = pl.ds(pl.multiple_of(c * OUT_ROWS, OUT_ROWS), OUT_ROWS)
        gate = _sigmoid(o_ref[0, rows, :].astype(F32))
        out_ref[0, rows, :] = (gate * _head_norm(hf_s[0, rows, :] + hf_s[1, rows, :], True)).astype(out_ref.dtype)
        return 0

    lax.fori_loop(0, n // OUT_ROWS, out_body, 0)


def _mixer_c(p, gates, conv_w, conv_b, gate_b, n_ctx):
    b, n, _ = p.shape
    kern = functools.partial(_mixer_c_kernel, n=n, n_ctx=n_ctx)
    return pl.pallas_call(
        kern,
        grid=(b,),
        in_specs=[_seq_spec(n, 6), _seq_spec(n, 7), _seq_spec(n, 8), _seq_spec(n, 9),
                  pl.BlockSpec((1, n, LANES), lambda bi: (bi, 0, 0)),
                  _full_spec(conv_w.shape), _full_spec(conv_b.shape), _full_spec(gate_b.shape)],
        out_specs=pl.BlockSpec((1, n, MIX_W), lambda bi: (bi, 0, 0)),
        out_shape=jax.ShapeDtypeStruct((b, n, MIX_W), BF16),
        scratch_shapes=[pltpu.VMEM((n, MIX_W), BF16), pltpu.VMEM((n, MIX_W), BF16),
                        pltpu.VMEM((2, n, MIX_W), F32),
                        pltpu.VMEM((2, MIX_W, MIX_W), F32),
                        pltpu.VMEM((2, 1, MIX_W), F32),
                        pltpu.VMEM((2, 1, MIX_W), F32)],
        compiler_params=_cparams("parallel"),
        name="mixer_mlstm",
    )(p, p, p, p, gates, conv_w, conv_b, gate_b)


def _hgrn2_tables():
    L, S = HGRN_CHUNK, SUB
    r = np.arange(L)
    same = (r[:, None] // S) == (r[None, :] // S)
    tri = np.stack([same & (r[None, :] <= r[:, None]), same & (r[None, :] >= r[:, None])])
    fold = []
    for rev in (False, True):
        pi = np.concatenate([np.full(len(_pair_rows(i, rev)), i) for i in range(S)])
        pj = np.concatenate([_pair_rows(i, rev) for i in range(S)])
        visible = (pj >= pi) if rev else (pj <= pi)
        fold.append((np.arange(S)[:, None] == pi[None, :]) & visible[None, :])
    as_bf16 = lambda a: jnp.asarray(a.astype(np.float32), dtype=BF16)
    return as_bf16(tri), as_bf16(same), as_bf16(np.stack(fold))


def _pair_rows(i, rev):
    half = SUB // 2
    if rev:
        return np.arange(0, SUB) if i < half else np.arange(half, SUB)
    return np.arange(0, half) if i < half else np.arange(0, SUB)


def _mixer_d_kernel(q_ref, ff_ref, fb_ref, v_ref, g_ref, lb_ref, tri_ref, same_ref, fold_ref, out_ref,
                    o_s, st_s, *, n, n_ctx):
    L = HGRN_CHUNK
    S = SUB
    nb = L // S
    n_chunks = n // L
    n_ctx_chunks = n_ctx // L
    lb = lb_ref[...]
    bd = _block_ones(MIX_W)
    ones_bd = jnp.where(bd, 1.0, 0.0).astype(BF16)

    def chunk(r, d, c):
        rev = d == 1
        rows = pl.ds(pl.multiple_of(c * L, L), L)
        q = _silu(q_ref[r, rows, :].astype(F32))
        raw = (fb_ref if rev else ff_ref)[r, rows, :].astype(F32)
        v = v_ref[r, rows, :]
        vf = v.astype(F32)
        f = lb + (1.0 - lb) * _sigmoid(raw)
        k = 1.0 - f
        log_f = jnp.log(f)
        cum = _sel_dot(tri_ref[d], log_f)
        tot = _sel_dot(same_ref[...], log_f)
        qt = (q * jnp.exp(cum)).astype(BF16)
        kt = (k * jnp.exp(tot - cum)).astype(BF16)
        g = jnp.exp(tot)
        cum2 = cum * LOG2_E
        prod, vj, scores, intra = {}, {}, {}, {}
        for a in range(nb + 2):
            if a < nb:
                sl = slice(a * S, (a + 1) * S)
                cb, qb, kb, vb = cum2[sl], q[sl], k[sl], vf[sl]
                prods, v_js = [], []
                for i in range(S):
                    js = _pair_rows(i, rev)
                    jsl = slice(int(js[0]), int(js[-1]) + 1)
                    dec = jnp.exp2(jnp.minimum(cb[i:i + 1] - cb[jsl], 0.0))
                    prods.append(qb[i:i + 1] * kb[jsl] * dec)
                    v_js.append(vb[jsl])
                prod[a] = jnp.concatenate(prods, axis=0).astype(BF16)
                vj[a] = jnp.concatenate(v_js, axis=0)
            if 1 <= a <= nb:
                scores[a - 1] = jnp.dot(prod.pop(a - 1), ones_bd, preferred_element_type=F32)
            if a >= 2:
                weighted = (scores.pop(a - 2) * vj.pop(a - 2)).astype(BF16)
                intra[a - 2] = jnp.dot(fold_ref[d], weighted, preferred_element_type=F32)
        o = jnp.concatenate([intra[a] for a in range(nb)], axis=0)
        st = st_s[r, d]
        inter = [None] * nb
        for a in (range(nb - 1, -1, -1) if rev else range(nb)):
            sl = slice(a * S, (a + 1) * S)
            inter[a] = _dot_nt(qt[sl], st.astype(BF16))
            st = st * g[a * S:a * S + 1] + jnp.where(bd, _dot_tn(v[sl], kt[sl]), 0.0)
        st_s[r, d] = st
        o_s[r, d, rows, :] = o + jnp.concatenate(inter, axis=0)

    st_s[...] = jnp.zeros_like(st_s)

    def walk_body(s, _):
        for r in range(ROWS):
            chunk(r, 0, s)
            chunk(r, 1, _bwd_chunk(s, n_ctx_chunks, n_chunks))
        return 0

    lax.fori_loop(0, n_chunks, walk_body, 0)

    def out_body(c, _):
        rows = pl.ds(pl.multiple_of(c * L, L), L)
        for r in range(ROWS):
            gate = _silu(g_ref[r, rows, :].astype(F32))
            o = o_s[r, 0, rows, :] + o_s[r, 1, rows, :]
            out_ref[r, rows, :] = (_head_norm(o, False) * gate).astype(out_ref.dtype)
        return 0

    lax.fori_loop(0, n_chunks, out_body, 0)


def _mixer_d(p, lb, n_ctx):
    b, n, _ = p.shape
    tri, same, fold = _hgrn2_tables()
    kern = functools.partial(_mixer_d_kernel, n=n, n_ctx=n_ctx)
    return pl.pallas_call(
        kern,
        grid=(b // ROWS,),
        in_specs=[_seq_spec(n, 10, ROWS), _seq_spec(n, 11, ROWS), _seq_spec(n, 12, ROWS), _seq_spec(n, 13, ROWS),
                  _seq_spec(n, 14, ROWS), _full_spec(lb.shape), _full_spec(tri.shape), _full_spec(same.shape),
                  _full_spec(fold.shape)],
        out_specs=pl.BlockSpec((ROWS, n, MIX_W), lambda bi: (bi, 0, 0)),
        out_shape=jax.ShapeDtypeStruct((b, n, MIX_W), BF16),
        scratch_shapes=[pltpu.VMEM((ROWS, 2, n, MIX_W), F32), pltpu.VMEM((ROWS, 2, MIX_W, MIX_W), F32)],
        compiler_params=_cparams("parallel"),
        name="mixer_hgrn2",
    )(p, p, p, p, p, lb, tri, same, fold)


def _pack_bf16_pairs(t):
    w = t.shape[1] // 2
    hi = pltpu.bitcast(t[:, :w].astype(BF16).astype(F32), jnp.uint32)
    lo = pltpu.bitcast(t[:, w:].astype(BF16).astype(F32), jnp.uint32)
    return pltpu.bitcast(hi | (lo >> 16), jnp.int32)


def _unpack_bf16_pairs(p):
    u = pltpu.bitcast(p, jnp.uint32)
    hi = pltpu.bitcast(u & jnp.uint32(0xFFFF0000), F32)
    lo = pltpu.bitcast(u << 16, F32)
    return hi, lo


def _merge_kernel(x_ref, m_ref, ya_ref, yb_ref, yc_ref, yd_ref, wm_ref, wb_ref, wo_ref, wr_ref, br_ref,
                  earlier_ref, xo_ref, hp_ref, route_ref, counts_ref, carry_s, *, tm, n_ctx):
    i = pl.program_id(1)

    @pl.when((pl.program_id(0) == 0) & (i == 0))
    def _():
        carry_s[...] = jnp.zeros_like(carry_s)

    row = i * tm + lax.broadcasted_iota(jnp.int32, (tm, 1), 0)
    is_ctx = row < n_ctx
    m = m_ref[0]
    x = x_ref[0]
    d = x.shape[-1]
    h = _modulate(x, _pick(m, is_ctx, 0), _pick(m, is_ctx, 1)).astype(BF16)
    z = jnp.zeros((tm, d), F32)
    for nb, y_ref in enumerate((ya_ref, yb_ref, yc_ref, yd_ref)):
        gate = _sigmoid(jnp.dot(h, wm_ref[:, nb * d:(nb + 1) * d], preferred_element_type=F32))
        z = z + gate * jnp.dot(y_ref[0], wb_ref[nb], preferred_element_type=F32)
    mix = jnp.dot(z.astype(BF16), wo_ref[...], preferred_element_type=F32)
    x1 = x + _pick(m, is_ctx, 2) * mix
    xo_ref[0] = x1
    h2 = _modulate(x1, _pick(m, is_ctx, 3), _pick(m, is_ctx, 4))

    wr = wr_ref[...]
    wr_hi = wr.astype(BF16)
    wr_lo = (wr - wr_hi.astype(F32)).astype(BF16)
    h2_hi = h2.astype(BF16)
    h2_lo = (h2 - h2_hi.astype(F32)).astype(BF16)
    dot = functools.partial(jnp.dot, preferred_element_type=F32)
    logit = dot(h2_hi, wr_hi) + dot(h2_lo, wr_hi) + dot(h2_hi, wr_lo) + br_ref[...]
    lane = lax.broadcasted_iota(jnp.int32, (tm, LANES), 1)
    big = jnp.int32(LANES)
    is_group = (lane >= N_EXPERTS) & (lane < N_EXPERTS + N_GROUPS)
    gl = jnp.where(is_group, logit, NEG_INF)
    g_max = jnp.max(gl, axis=1, keepdims=True)
    g_idx = jnp.min(jnp.where(gl == g_max, lane, big), axis=1, keepdims=True) - N_EXPERTS
    g_prob = 1.0 / jnp.sum(jnp.where(is_group, jnp.exp(logit - g_max), 0.0), axis=1, keepdims=True)
    in_group = (lane < N_EXPERTS) & (lane // EXP_PER_GROUP == g_idx)
    el = jnp.where(in_group, logit, NEG_INF)
    v1 = jnp.max(el, axis=1, keepdims=True)
    i1 = jnp.min(jnp.where(el == v1, lane, big), axis=1, keepdims=True)
    el2 = jnp.where(lane == i1, NEG_INF, el)
    v2 = jnp.max(el2, axis=1, keepdims=True)
    i2 = jnp.min(jnp.where(el2 == v2, lane, big), axis=1, keepdims=True)
    e2 = jnp.exp(v2 - v1)
    w1 = g_prob / (1.0 + e2)
    w2 = g_prob * e2 / (1.0 + e2)
    first_low = i1 < i2
    lo = jnp.where(first_low, i1, i2) - g_idx * EXP_PER_GROUP
    hi = jnp.where(first_low, i2, i1) - g_idx * EXP_PER_GROUP
    pair = 3 * lo - ((lo * (lo - 1)) >> 1) + (hi - lo - 1)
    cls = g_idx * N_PAIRS + pair
    w_lo = jnp.where(first_low, w1, w2)
    w_hi = jnp.where(first_low, w2, w1)
    onehot = jnp.where(lane == cls, 1.0, 0.0)
    before = jnp.dot(earlier_ref[...], onehot.astype(BF16), preferred_element_type=F32) + carry_s[...]
    rank = jnp.sum(onehot * before, axis=1, keepdims=True)
    carry_s[...] += jnp.sum(onehot, axis=0, keepdims=True)
    counts_ref[...] = carry_s[...].astype(jnp.int32)
    cols = jnp.where(lane == 0, cls.astype(F32), jnp.where(lane == 1, rank, 0.0))
    pick = jnp.where(lax.broadcasted_iota(jnp.int32, (8, LANES), 0)
                     == lax.broadcasted_iota(jnp.int32, (8, LANES), 1), 1.0, 0.0).astype(BF16)
    route_ref[0] = sum(_dot_nt(pick, part) for part in _split3(cols)).astype(jnp.int32)
    gate_bits = pltpu.bitcast(jnp.where(lane == 0, w_lo, jnp.where(lane == 1, w_hi, 0.0)), jnp.int32)
    hp_ref[0] = jnp.concatenate([_pack_bf16_pairs(h2), gate_bits], axis=1)


def _merge(xc, mvec, ys, w_merge, w_branch, w_out, w_route, b_route, n_ctx):
    b, n, d = xc.shape
    tm = _token_tile(n, 640)
    kern = functools.partial(_merge_kernel, tm=tm, n_ctx=n_ctx)
    tok = lambda w: pl.BlockSpec((1, tm, w), lambda bi, i: (bi, i, 0))
    const = lambda shape: pl.BlockSpec(shape, lambda bi, i: (0,) * len(shape))
    earlier = jnp.asarray(np.tri(tm, k=-1, dtype=np.float32), dtype=BF16)
    return pl.pallas_call(
        kern,
        grid=(b, n // tm),
        in_specs=[tok(d), pl.BlockSpec((1, MOD_ROWS, d), lambda bi, i: (bi, 0, 0)),
                  tok(MIX_W), tok(MIX_W), tok(MIX_W), tok(MIX_W),
                  const(w_merge.shape), const(w_branch.shape), const(w_out.shape),
                  const(w_route.shape), const(b_route.shape), const(earlier.shape)],
        out_specs=[tok(d), tok(d // 2 + LANES),
                   pl.BlockSpec((1, 8, tm), lambda bi, i: (bi * (n // tm) + i, 0, 0)), const((1, LANES))],
        out_shape=[jax.ShapeDtypeStruct((b, n, d), F32),
                   jax.ShapeDtypeStruct((b, n, d // 2 + LANES), jnp.int32),
                   jax.ShapeDtypeStruct((b * (n // tm), 8, tm), jnp.int32),
                   jax.ShapeDtypeStruct((1, LANES), jnp.int32)],
        scratch_shapes=[pltpu.VMEM((1, LANES), F32)],
        compiler_params=_cparams("arbitrary", "arbitrary"),
        name="merge_route",
    )(xc, mvec, *ys, w_merge, w_branch, w_out, w_route, b_route, earlier)


def _row_move(table, idx, n_out=None):
    scatter = n_out is not None
    n_idx = idx.shape[0]
    width = table.shape[1]
    per_worker = n_idx // SC_WORKERS
    assert per_worker * SC_WORKERS == n_idx and per_worker % 8 == 0
    chunk = max(c for c in range(8, SC_MAX_CHUNK + 1, 8) if per_worker % c == 0)
    n_chunks = per_worker // chunk
    mesh = plsc.VectorSubcoreMesh(core_axis_name="c", subcore_axis_name="s",
                                  num_cores=SC_CORES, num_subcores=SC_SUBCORES)

    @functools.partial(
        pl.kernel, mesh=mesh,
        out_type=jax.ShapeDtypeStruct((n_out if scatter else n_idx, width), table.dtype),
        scratch_types=[pltpu.VMEM((chunk,), jnp.int32),
                       pltpu.VMEM((chunk, width), table.dtype),
                       pltpu.SemaphoreType.DMA],
        name="sc_row_scatter" if scatter else "sc_row_gather",
    )
    def move(table_hbm, idx_hbm, out_hbm, idx_v, rows_v, sem):
        worker = lax.axis_index("s") * SC_CORES + lax.axis_index("c")
        base = worker * per_worker

        @pl.loop(0, n_chunks)
        def _(j):
            off = pl.multiple_of(base + j * chunk, 8)
            pltpu.sync_copy(idx_hbm.at[pl.ds(off, chunk)], idx_v)
            if scatter:
                pltpu.sync_copy(table_hbm.at[pl.ds(off, chunk)], rows_v)
                pltpu.async_copy(rows_v, out_hbm.at[idx_v], sem).wait()
            else:
                pltpu.async_copy(table_hbm.at[idx_v], rows_v, sem).wait()
                pltpu.sync_copy(rows_v, out_hbm.at[pl.ds(off, chunk)])

    return move(table, idx)


def _ffn(h, w1_ref, w3_ref, w2_ref):
    a = jnp.dot(h, w1_ref[0, 0], preferred_element_type=F32)
    g = jnp.dot(h, w3_ref[0, 0], preferred_element_type=F32)
    return jnp.dot((_silu(a) * g).astype(BF16), w2_ref[0, 0], preferred_element_type=F32)


def _moe_kernel(elo_ref, ehi_ref, valid_ref, xs_ref, w1a_ref, w3a_ref, w2a_ref, w1b_ref, w3b_ref, w2b_ref,
                ys_ref):
    j = pl.program_id(0)
    half = ys_ref.shape[1]

    @pl.when(valid_ref[j] != 0)
    def _():
        hi, lo = _unpack_bf16_pairs(xs_ref[:, :half])
        h = jnp.concatenate([hi.astype(BF16), lo.astype(BF16)], axis=1)
        gates = pltpu.bitcast(xs_ref[:, half:], F32)
        y = (gates[:, 0:1] * _ffn(h, w1a_ref, w3a_ref, w2a_ref)
             + gates[:, 1:2] * _ffn(h, w1b_ref, w3b_ref, w2b_ref))
        ys_ref[...] = _pack_bf16_pairs(y)

    @pl.when(valid_ref[j] == 0)
    def _():
        ys_ref[...] = jnp.zeros_like(ys_ref)


def _moe_experts(xs, tile_elo, tile_ehi, tile_valid, w1, w3, w2, layer):
    n_slots, width = xs.shape
    d = w1.shape[2]
    n_tiles = n_slots // MOE_TILE
    wspec = lambda shape, which: pl.BlockSpec(
        (1, 1) + shape, lambda j, elo, ehi, valid: (layer, (elo, ehi)[which][j], 0, 0))
    grid_spec = pltpu.PrefetchScalarGridSpec(
        num_scalar_prefetch=3,
        grid=(n_tiles,),
        in_specs=[pl.BlockSpec((MOE_TILE, width), lambda j, elo, ehi, valid: (j, 0)),
                  wspec((d, D_EXPERT), 0), wspec((d, D_EXPERT), 0), wspec((D_EXPERT, d), 0),
                  wspec((d, D_EXPERT), 1), wspec((d, D_EXPERT), 1), wspec((D_EXPERT, d), 1)],
        out_specs=pl.BlockSpec((MOE_TILE, d // 2), lambda j, elo, ehi, valid: (j, 0)),
    )
    return pl.pallas_call(
        _moe_kernel,
        grid_spec=grid_spec,
        out_shape=jax.ShapeDtypeStruct((n_slots, d // 2), jnp.int32),
        compiler_params=_cparams("arbitrary"),
        name="moe_experts",
    )(tile_elo, tile_ehi, tile_valid, xs, w1, w3, w2, w1, w3, w2)


def _moe_combine_kernel(x_ref, m_ref, y_ref, *rest, tm, n_ctx, row0):
    o_ref = rest[-1]
    row = row0 + pl.program_id(1) * tm + lax.broadcasted_iota(jnp.int32, (tm, 1), 0)
    hi, lo = _unpack_bf16_pairs(y_ref[0])
    x = x_ref[0] + _pick(m_ref[0], row < n_ctx, 5) * jnp.concatenate([hi, lo], axis=1)
    if len(rest) == 2:
        ms = jnp.mean(x * x, axis=-1, keepdims=True)
        x = x * lax.rsqrt(ms + EPS) * rest[0][...]
    o_ref[0] = x


def _moe_combine(x1, mvec, yt, n_ctx, final_w=None):
    b, n, d = x1.shape
    if final_w is None:
        tm, skip, n_out, extra, extra_specs = _token_tile(n, 1100), 0, n, (), []
    else:
        tm = _token_tile(n_ctx, 1024)
        skip, n_out = n_ctx // tm, n - n_ctx
        extra, extra_specs = (final_w.reshape(1, d),), [pl.BlockSpec((1, d), lambda bi, i: (0, 0))]
    kern = functools.partial(_moe_combine_kernel, tm=tm, n_ctx=n_ctx, row0=skip * tm)
    tok = lambda w: pl.BlockSpec((1, tm, w), lambda bi, i: (bi, i + skip, 0))
    return pl.pallas_call(
        kern,
        grid=(b, n_out // tm),
        in_specs=[tok(d), pl.BlockSpec((1, MOD_ROWS, d), lambda bi, i: (bi, 0, 0)), tok(d // 2)] + extra_specs,
        out_specs=pl.BlockSpec((1, tm, d), lambda bi, i: (bi, i, 0)),
        out_shape=jax.ShapeDtypeStruct((b, n_out, d), F32),
        compiler_params=_cparams("parallel", "parallel"),
        name="moe_combine",
    )(x1, mvec, yt, *extra)


def _moe(x1, mvec, hp, route, counts, w1, w3, w2, layer, n_ctx, final_w=None):
    b, n, d = x1.shape
    m = b * n
    n_slots = m + N_CLASSES * MOE_TILE
    cls = route[:, 0, :].reshape(m)
    rank = route[:, 1, :].reshape(m)
    cnt = counts[0, :N_CLASSES]
    padded = (cnt + MOE_TILE - 1) // MOE_TILE * MOE_TILE
    ends = jnp.cumsum(padded)
    pos = (ends - padded)[cls] + rank
    tile_start = jnp.arange(n_slots // MOE_TILE, dtype=jnp.int32) * MOE_TILE
    tile_cls = jnp.minimum(jnp.searchsorted(ends, tile_start, side="right"), N_CLASSES - 1).astype(jnp.int32)
    tile_valid = (tile_start < ends[-1]).astype(jnp.int32)
    pair_lo = jnp.asarray([0, 0, 0, 1, 1, 2], jnp.int32)
    pair_hi = jnp.asarray([1, 2, 3, 2, 3, 3], jnp.int32)
    group = tile_cls // N_PAIRS
    tile_elo = group * EXP_PER_GROUP + pair_lo[tile_cls % N_PAIRS]
    tile_ehi = group * EXP_PER_GROUP + pair_hi[tile_cls % N_PAIRS]

    xs = _row_move(hp.reshape(m, hp.shape[-1]), pos, n_out=n_slots)
    ys = _moe_experts(xs, tile_elo, tile_ehi, tile_valid, w1, w3, w2, layer)
    yt = _row_move(ys, pos)
    return _moe_combine(x1, mvec, yt.reshape(b, n, d // 2), n_ctx, final_w)


def _rope_tables(n_lat, n_ctx):
    rows = n_lat // GRID_W
    row = jnp.repeat(jnp.arange(rows), GRID_W).astype(F32)
    col = jnp.tile(jnp.arange(GRID_W), rows).astype(F32)
    nq = HEAD_DIM // 4
    inv = jnp.power(ROPE_BASE, -jnp.arange(nq, dtype=F32) / nq)
    ang = jnp.concatenate([row[:, None] * inv, col[:, None] * inv], -1)
    cos = jnp.cos(ang)
    sin = jnp.sin(ang)
    cos_h = jnp.concatenate([cos, cos], -1)
    sin_h = jnp.concatenate([-sin, sin], -1)
    cos_full = jnp.concatenate([jnp.ones((n_ctx, HEAD_DIM), F32), cos_h], 0)
    sin_full = jnp.concatenate([jnp.zeros((n_ctx, HEAD_DIM), F32), sin_h], 0)
    return jnp.tile(cos_full, (1, N_HEADS)), jnp.tile(sin_full, (1, N_HEADS))


def _block_diag_heads(w):
    eye = jnp.eye(N_HEADS, dtype=w.dtype)
    return jnp.einsum("hij,hg->higj", w, eye).reshape(MIX_W, MIX_W)


def kernel(x, c, ctx, c_ctx, w_mod, b_mod, w_in, a_conv_w, a_conv_b, a_gate_w, a_gate_b, a_lambda, b_theta,
           c_conv_w, c_conv_b, c_gate_b, d_lb, w_branch, w_out, moe_w_group, moe_b_group, moe_w_router,
           moe_b_router, moe_w1, moe_w3, moe_w2, final_norm_w):
    bsz, n_lat, d = x.shape
    n_ctx = ctx.shape[1]
    depth = w_mod.shape[0]
    assert n_ctx % CHUNK == 0 and n_lat % CHUNK == 0 and n_ctx % 256 == 0 and n_lat % 256 == 0

    xc = jnp.concatenate([ctx, x], axis=1)
    cos, sin = _rope_tables(n_lat, n_ctx)

    cc = jnp.zeros((8, d), F32).at[:bsz].set(c).at[bsz].set(c_ctx)
    mod = _mod_vectors(cc, w_mod, b_mod)
    mx = mod[:, :bsz].reshape(depth, bsz, N_MOD, d)
    mc = jnp.broadcast_to(mod[:, bsz].reshape(depth, 1, N_MOD, d), (depth, bsz, N_MOD, d))
    pad = jnp.zeros((depth, bsz, 8 - N_MOD, d), F32)
    mvec = jnp.concatenate([mx, pad, mc, pad], axis=2)

    lbs = jnp.cumsum(jax.nn.softmax(d_lb.astype(F32), axis=0), axis=0)
    lbs = lbs - lbs[0]

    n_mix_cols = 15 * MIX_W
    gate0 = 10 * MIX_W
    n_gate = 4 * N_HEADS
    w1b = moe_w1.astype(BF16)
    w3b = moe_w3.astype(BF16)
    w2b = moe_w2.astype(BF16)

    for l in range(depth):
        wl = w_in[l]
        w_mix = jnp.concatenate([wl[:, :gate0], wl[:, gate0 + n_gate:n_mix_cols + n_gate]], axis=1).astype(BF16)
        w_gate = jnp.pad(wl[:, gate0:gate0 + n_gate], ((0, 0), (0, LANES - n_gate))).astype(BF16)
        w_merge = wl[:, n_mix_cols + n_gate:].astype(BF16)
        p, gates_c = _in_proj(xc, mvec[l], w_mix, w_gate, cos, sin, n_ctx)

        gw = jnp.concatenate([_block_diag_heads(a_gate_w[l, dd, j]) for dd in range(2) for j in range(2)],
                             axis=1).astype(BF16)
        gb = a_gate_b[l].reshape(1, 4 * MIX_W)
        ya = _mixer_a(p, a_conv_w[l], a_conv_b[l].reshape(1, MIX_W), gw, gb, a_lambda[l], n_ctx)

        log_gamma = jnp.repeat(jax.nn.log_sigmoid(b_theta[l].astype(F32)), HEAD_DIM, axis=1)
        yb = _mixer_b(p, log_gamma, n_ctx)

        gate_b = jnp.pad(c_gate_b[l].reshape(1, n_gate), ((0, 0), (0, LANES - n_gate)))
        yc = _mixer_c(p, gates_c, c_conv_w[l], c_conv_b[l].reshape(1, 2 * MIX_W), gate_b, n_ctx)

        yd = _mixer_d(p, lbs[l].reshape(1, MIX_W), n_ctx)

        w_route = jnp.pad(jnp.concatenate([moe_w_router[l], moe_w_group[l]], axis=1),
                          ((0, 0), (0, LANES - N_EXPERTS - N_GROUPS)))
        b_route = jnp.pad(jnp.concatenate([moe_b_router[l], moe_b_group[l]]),
                          (0, LANES - N_EXPERTS - N_GROUPS)).reshape(1, LANES)
        x1, hp, route, counts = _merge(xc, mvec[l], (ya, yb, yc, yd), w_merge, w_branch[l].astype(BF16),
                                       w_out[l].astype(BF16), w_route, b_route, n_ctx)
        xc = _moe(x1, mvec[l], hp, route, counts, w1b, w3b, w2b, l, n_ctx,
                  final_w=final_norm_w if l == depth - 1 else None)

    return xc
```

```python
import functools

import jax
import jax.numpy as jnp
import numpy as np
from jax import lax
from jax.experimental import pallas as pl
from jax.experimental.pallas import tpu as pltpu
from jax.experimental.pallas import tpu_sc as plsc

F32 = jnp.float32
BF16 = jnp.bfloat16

EPS = 1e-6
N_HEADS = 4
HEAD_DIM = 64
MIX_W = N_HEADS * HEAD_DIM
N_BRANCH = 4
CONV_W = 4
LRU_C = 8.0
GRID_W = 64
ROPE_BASE = 10000.0
N_GROUPS = 4
EXP_PER_GROUP = 4
N_EXPERTS = N_GROUPS * EXP_PER_GROUP
D_EXPERT = 512
N_MOD = 6
M_INIT = -1e30

CHUNK = 128
RET_CHUNK = 256
CONV_STEP = 128
OUT_ROWS = 256
SUB = 16
HGRN_CHUNK = 256
HALO = 16
LANES = 128
MOD_ROWS = 16
ROWS = 1
VMEM_LIMIT_BYTES = 56 * 1024 * 1024
N_PAIRS = 6
N_CLASSES = N_GROUPS * N_PAIRS
MOE_TILE = 256
SC_CORES = 2
SC_SUBCORES = 16
SC_WORKERS = SC_CORES * SC_SUBCORES
SC_MAX_CHUNK = 32
NEG_INF = float("-inf")
LOG2_E = 1.4426950408889634


def _cparams(*sem):
    return pltpu.CompilerParams(dimension_semantics=sem, vmem_limit_bytes=VMEM_LIMIT_BYTES)


def _token_tile(n, cap):
    best = 16
    for t in range(16, cap + 1, 16):
        if n % t == 0:
            best = t
    return best


def _modulate(x, shift, scale):
    ms = jnp.mean(x * x, axis=-1, keepdims=True)
    return x * lax.rsqrt(ms + EPS) * (1.0 + scale) + shift


def _pick(m, is_ctx, k):
    return jnp.where(is_ctx, m[8 + k:9 + k], m[k:k + 1])


def _sigmoid(x):
    return 0.5 * jnp.tanh(0.5 * x) + 0.5


def _silu(x):
    return x * _sigmoid(x)


def _log_sigmoid(x):
    return jnp.minimum(x, 0.0) - jnp.log(1.0 + jnp.exp(-jnp.abs(x)))


def _split3(x):
    hi = x.astype(BF16)
    r = x - hi.astype(F32)
    mid = r.astype(BF16)
    lo = (r - mid.astype(F32)).astype(BF16)
    return hi, mid, lo


def _split2(x):
    hi = x.astype(BF16)
    return hi, (x - hi.astype(F32)).astype(BF16)


def _sel_dot(sel, x):
    hi, lo = _split2(x)
    d = functools.partial(jnp.dot, preferred_element_type=F32)
    return d(sel, hi) + d(sel, lo)


def _dot_sel(x, sel):
    hi, lo = _split2(x)
    d = functools.partial(jnp.dot, preferred_element_type=F32)
    return d(hi, sel) + d(lo, sel)


def _dot_nt(a, b):
    return lax.dot_general(a, b, (((1,), (1,)), ((), ())), preferred_element_type=F32)


def _dot_tn(a, b):
    return lax.dot_general(a, b, (((0,), (0,)), ((), ())), preferred_element_type=F32)


def _head_of_lane(width):
    return lax.broadcasted_iota(jnp.int32, (1, width), 1) // HEAD_DIM


def _block_ones(width):
    r = lax.broadcasted_iota(jnp.int32, (width, width), 0) // HEAD_DIM
    c = lax.broadcasted_iota(jnp.int32, (width, width), 1) // HEAD_DIM
    return r == c


def _head_norm(o, center):
    ones = jnp.where(_block_ones(MIX_W), 1.0, 0.0).astype(BF16)
    inv = 1.0 / HEAD_DIM
    if center:
        o = o - _dot_sel(o, ones) * inv
    var = _dot_sel(o * o, ones) * inv
    return o * lax.rsqrt(var + EPS)


def _bwd_chunk(s, n_ctx_chunks, n_chunks):
    return jnp.where(s < n_ctx_chunks, n_ctx_chunks - 1 - s, n_chunks - 1 + n_ctx_chunks - s)


def _conv_chunk(c, operands, *, n, n_ctx):
    L = CONV_STEP
    wl = L + 2 * HALO
    r0 = c * L
    start = pl.multiple_of(jnp.clip(r0 - HALO, 0, n - wl), HALO)
    off = r0 - start
    shift = (lax.broadcasted_iota(jnp.int32, (L, wl), 1) - lax.broadcasted_iota(jnp.int32, (L, wl), 0)) - off
    g = r0 + lax.broadcasted_iota(jnp.int32, (L, 1), 0)
    seg_g = jnp.where(g < n_ctx, 0, 1)
    sels = []
    for k in range(CONV_W):
        src = g + (k - 2)
        row_ok = (jnp.where(src < n_ctx, 0, 1) == seg_g) & (src >= 0) & (src < n)
        sels.append(jnp.where((shift == k - 2) & row_ok, 1.0, 0.0).astype(BF16))
    outs = []
    for ref, w, b in operands:
        win = ref[0, pl.ds(start, wl), :]
        y = b
        for k in range(CONV_W):
            y = y + w[k:k + 1] * jnp.dot(sels[k], win, preferred_element_type=F32)
        outs.append(y)
    return outs


def _mod_kernel(cc_ref, w_ref, b_ref, o_ref):
    s = _silu(cc_ref[...])
    o_ref[0] = jnp.dot(s, w_ref[0], precision=lax.Precision.HIGHEST, preferred_element_type=F32) + b_ref[0]


def _mod_vectors(cc, w_mod, b_mod):
    depth, d, dm = w_mod.shape
    tn = 1536
    return pl.pallas_call(
        _mod_kernel,
        grid=(depth, dm // tn),
        in_specs=[pl.BlockSpec((8, d), lambda l, j: (0, 0)),
                  pl.BlockSpec((1, d, tn), lambda l, j: (l, 0, j)),
                  pl.BlockSpec((1, 1, tn), lambda l, j: (l, 0, j))],
        out_specs=pl.BlockSpec((1, 8, tn), lambda l, j: (l, 0, j)),
        out_shape=jax.ShapeDtypeStruct((depth, 8, dm), F32),
        compiler_params=_cparams("parallel", "parallel"),
        name="mod_vectors",
    )(cc, w_mod, b_mod.reshape(depth, 1, dm))


def _rope(t, cos, sin):
    lane = lax.broadcasted_iota(jnp.int32, (1, MIX_W), 1) % HEAD_DIM
    half = HEAD_DIM // 2
    swapped = jnp.where(lane < half, pltpu.roll(t, MIX_W - half, 1), pltpu.roll(t, half, 1))
    return t * cos + swapped * sin


RET_Q_COL = 2 * MIX_W
RET_K_COL = 3 * MIX_W


def _in_proj_kernel(x_ref, m_ref, w_ref, wg_ref, cos_ref, sin_ref, p_ref, g_ref, *, tm, n_ctx, col_chunk):
    i = pl.program_id(1)
    row = i * tm + lax.broadcasted_iota(jnp.int32, (tm, 1), 0)
    is_ctx = row < n_ctx
    m = m_ref[0]
    h = _modulate(x_ref[0], _pick(m, is_ctx, 0), _pick(m, is_ctx, 1)).astype(BF16)
    for j in range(w_ref.shape[1] // col_chunk):
        sl = slice(j * col_chunk, (j + 1) * col_chunk)
        t = jnp.dot(h, w_ref[:, sl], preferred_element_type=F32)
        if j * col_chunk == RET_Q_COL:
            t = _rope(t, cos_ref[...], sin_ref[...])
        elif j * col_chunk == RET_K_COL:
            t = _rope(t, cos_ref[...], sin_ref[...]) * HEAD_DIM ** -0.5
        p_ref[0, :, sl] = t.astype(BF16)
    g_ref[0] = jnp.dot(h, wg_ref[...], preferred_element_type=F32)


def _in_proj(xc, mvec, w_mix, w_gate, cos, sin, n_ctx):
    b, n, d = xc.shape
    tm = _token_tile(n, 640)
    wc = w_mix.shape[1]
    kern = functools.partial(_in_proj_kernel, tm=tm, n_ctx=n_ctx, col_chunk=MIX_W)
    return pl.pallas_call(
        kern,
        grid=(b, n // tm),
        in_specs=[pl.BlockSpec((1, tm, d), lambda bi, i: (bi, i, 0)),
                  pl.BlockSpec((1, MOD_ROWS, d), lambda bi, i: (bi, 0, 0)),
                  pl.BlockSpec((d, wc), lambda bi, i: (0, 0)),
                  pl.BlockSpec((d, LANES), lambda bi, i: (0, 0)),
                  pl.BlockSpec((tm, MIX_W), lambda bi, i: (i, 0)),
                  pl.BlockSpec((tm, MIX_W), lambda bi, i: (i, 0))],
        out_specs=[pl.BlockSpec((1, tm, wc), lambda bi, i: (bi, i, 0)),
                   pl.BlockSpec((1, tm, LANES), lambda bi, i: (bi, i, 0))],
        out_shape=[jax.ShapeDtypeStruct((b, n, wc), BF16),
                   jax.ShapeDtypeStruct((b, n, LANES), F32)],
        compiler_params=_cparams("parallel", "parallel"),
        name="in_proj",
    )(xc, mvec, w_mix, w_gate, cos, sin)


def _lin_scan(a, x, carry, reverse):
    L = a.shape[0]
    row = lax.broadcasted_iota(jnp.int32, (L, 1), 0)
    k = 1
    while k < L:
        if reverse:
            a_s = pltpu.roll(a, L - k, 0)
            x_s = pltpu.roll(x, L - k, 0)
            valid = row < L - k
        else:
            a_s = pltpu.roll(a, k, 0)
            x_s = pltpu.roll(x, k, 0)
            valid = row >= k
        x = jnp.where(valid, a * x_s + x, x)
        a = jnp.where(valid, a * a_s, a)
        k *= 2
    h = x + a * carry
    return h, (h[0:1] if reverse else h[L - 1:L])


def _mixer_a_kernel(ax_ref, ag_ref, cw_ref, cb_ref, gw_ref, gb_ref, lam_ref, out_ref, u_s, hf_s,
                    *, n, n_ctx):
    L = CHUNK
    n_chunks = n // L
    n_ctx_chunks = n_ctx // L
    cw = cw_ref[...]
    cb = cb_ref[...]
    log_lam = _log_sigmoid(lam_ref[...])

    def direction(d, c, carry):
        r0 = pl.multiple_of(c * L, L)
        u = u_s[pl.ds(r0, L), :]
        pre = jnp.dot(u.astype(BF16), gw_ref[:, d * 2 * MIX_W:(d + 1) * 2 * MIX_W],
                      preferred_element_type=F32) + gb_ref[:, d * 2 * MIX_W:(d + 1) * 2 * MIX_W]
        r = _sigmoid(pre[:, :MIX_W])
        gi = _sigmoid(pre[:, MIX_W:])
        log_a = LRU_C * r * log_lam[d:d + 1]
        a = jnp.exp(log_a)
        inp = jnp.sqrt(1.0 - jnp.exp(2.0 * log_a)) * (gi * u)
        h, new_carry = _lin_scan(a, inp, carry, reverse=(d == 1))
        return r0, h, new_carry

    def conv_body(c, _):
        r0 = pl.multiple_of(c * CONV_STEP, CONV_STEP)
        u_s[pl.ds(r0, CONV_STEP), :] = _conv_chunk(c, [(ax_ref, cw, cb)], n=n, n_ctx=n_ctx)[0]
        return 0

    lax.fori_loop(0, n // CONV_STEP, conv_body, 0)

    def walk_body(s, carries):
        r0, h, carry_f = direction(0, s, carries[0])
        hf_s[0, pl.ds(r0, L), :] = h
        r0, h, carry_b = direction(1, _bwd_chunk(s, n_ctx_chunks, n_chunks), carries[1])
        hf_s[1, pl.ds(r0, L), :] = h
        return carry_f, carry_b

    zero = jnp.zeros((1, MIX_W), F32)
    lax.fori_loop(0, n_chunks, walk_body, (zero, zero))

    def out_body(c, _):
        rows = pl.ds(pl.multiple_of(c * L, L), L)
        gate = jax.nn.gelu(ag_ref[0, rows, :].astype(F32), approximate=True)
        out_ref[0, rows, :] = (gate * (hf_s[0, rows, :] + hf_s[1, rows, :])).astype(out_ref.dtype)
        return 0

    lax.fori_loop(0, n_chunks, out_body, 0)


def _seq_spec(n, col, rows=1):
    if rows == 1:
        return pl.BlockSpec((1, n, MIX_W), lambda b: (b, 0, col))
    return pl.BlockSpec((rows, n, MIX_W), lambda b: (b, 0, col), pipeline_mode=pl.Buffered(1))


def _full_spec(shape):
    return pl.BlockSpec(shape, lambda b: (0,) * len(shape))


def _mixer_a(p, conv_w, conv_b, gate_w, gate_b, lam, n_ctx):
    b, n, _ = p.shape
    kern = functools.partial(_mixer_a_kernel, n=n, n_ctx=n_ctx)
    return pl.pallas_call(
        kern,
        grid=(b,),
        in_specs=[_seq_spec(n, 0), _seq_spec(n, 1),
                  _full_spec(conv_w.shape), _full_spec(conv_b.shape),
                  _full_spec(gate_w.shape), _full_spec(gate_b.shape), _full_spec(lam.shape)],
        out_specs=pl.BlockSpec((1, n, MIX_W), lambda bi: (bi, 0, 0)),
        out_shape=jax.ShapeDtypeStruct((b, n, MIX_W), BF16),
        scratch_shapes=[pltpu.VMEM((n, MIX_W), F32), pltpu.VMEM((2, n, MIX_W), F32)],
        compiler_params=_cparams("parallel"),
        name="mixer_rglru",
    )(p, p, conv_w, conv_b, gate_w, gate_b, lam)


def _mixer_b_kernel(q_ref, k_ref, v_ref, g_ref, lg_ref, out_ref, oi_s, dec_s, st_s, *, n, n_ctx):
    L = RET_CHUNK
    n_chunks = n // L
    n_ctx_chunks = n_ctx // L
    head = _head_of_lane(MIX_W)
    bd = _block_ones(MIX_W)
    lgf = lg_ref[0:1, :]
    lgb = lg_ref[1:2, :]
    pos = lax.broadcasted_iota(jnp.int32, (L, 1), 0).astype(F32)
    diff = (lax.broadcasted_iota(jnp.int32, (L, L), 0) - lax.broadcasted_iota(jnp.int32, (L, L), 1)).astype(F32)
    for h in range(N_HEADS):
        lf = lgf[:, h * HEAD_DIM:h * HEAD_DIM + 1]
        lb = lgb[:, h * HEAD_DIM:h * HEAD_DIM + 1]
        dec_s[h] = jnp.where(diff >= 0, jnp.exp(lf * jnp.maximum(diff, 0.0)), jnp.exp(lb * jnp.maximum(-diff, 0.0)))

    qw = (jnp.exp(lgf * (pos + 1.0)), jnp.exp(lgb * (float(L) - pos)))
    kw = (jnp.exp(lgf * (float(L) - 1.0 - pos)), jnp.exp(lgb * pos))
    g_chunk = (jnp.exp(lgf * float(L)), jnp.exp(lgb * float(L)))

    def state_step(d, c):
        rows = pl.ds(pl.multiple_of(c * L, L), L)
        q = q_ref[0, rows, :].astype(F32)
        k = k_ref[0, rows, :].astype(F32)
        st = st_s[d]
        oi_s[d, rows, :] = jnp.dot((q * qw[d]).astype(BF16), st.astype(BF16), preferred_element_type=F32)
        kv = _dot_tn((k * kw[d]).astype(BF16), v_ref[0, rows, :])
        st_s[d] = g_chunk[d] * st + jnp.where(bd, kv, 0.0)

    st_s[...] = jnp.zeros_like(st_s)

    def walk_body(s, _):
        state_step(0, s)
        state_step(1, _bwd_chunk(s, n_ctx_chunks, n_chunks))
        return 0

    lax.fori_loop(0, n_chunks, walk_body, 0)

    def out_body(c, _):
        rows = pl.ds(pl.multiple_of(c * L, L), L)
        q = q_ref[0, rows, :]
        k = k_ref[0, rows, :]
        v = v_ref[0, rows, :]
        scs, vals = [], []
        for h in range(N_HEADS):
            hm = head == h
            scs.append((_dot_nt(jnp.where(hm, q, jnp.zeros_like(q)), k) * dec_s[h]).astype(BF16))
            vals.append(jnp.where(hm, v, jnp.zeros_like(v)))
        o = (oi_s[0, rows, :] + oi_s[1, rows, :]
             + jnp.dot(jnp.concatenate(scs, axis=1), jnp.concatenate(vals, axis=0), preferred_element_type=F32))
        gate = _silu(g_ref[0, rows, :].astype(F32))
        out_ref[0, rows, :] = (_head_norm(o, True) * gate).astype(out_ref.dtype)
        return 0

    lax.fori_loop(0, n_chunks, out_body, 0, unroll=2)


def _mixer_b(p, log_gamma, n_ctx):
    b, n, _ = p.shape
    kern = functools.partial(_mixer_b_kernel, n=n, n_ctx=n_ctx)
    return pl.pallas_call(
        kern,
        grid=(b,),
        in_specs=[_seq_spec(n, 2), _seq_spec(n, 3), _seq_spec(n, 4), _seq_spec(n, 5),
                  _full_spec(log_gamma.shape)],
        out_specs=pl.BlockSpec((1, n, MIX_W), lambda bi: (bi, 0, 0)),
        out_shape=jax.ShapeDtypeStruct((b, n, MIX_W), BF16),
        scratch_shapes=[pltpu.VMEM((2, n, MIX_W), F32),
                        pltpu.VMEM((N_HEADS, RET_CHUNK, RET_CHUNK), F32),
                        pltpu.VMEM((2, MIX_W, MIX_W), F32)],
        compiler_params=_cparams("parallel"),
        name="mixer_retention",
    )(p, p, p, p, log_gamma)


def _mixer_c_kernel(q_ref, k_ref, v_ref, o_ref, g_ref, cw_ref, cb_ref, gb_ref, out_ref,
                    qs_s, ks_s, hf_s, c_s, n_s, m_s, *, n, n_ctx):
    L = CHUNK
    n_chunks = n // L
    n_ctx_chunks = n_ctx // L
    head = _head_of_lane(MIX_W)
    bd = _block_ones(MIX_W)
    ones_bd = jnp.where(bd, 1.0, 0.0).astype(BF16)
    cw = cw_ref[...]
    cb = cb_ref[...]
    ri = lax.broadcasted_iota(jnp.int32, (L, L), 0)
    ci = lax.broadcasted_iota(jnp.int32, (L, L), 1)
    gl = lax.broadcasted_iota(jnp.int32, (LANES, MIX_W), 0)
    hl = lax.broadcasted_iota(jnp.int32, (LANES, MIX_W), 1) // HEAD_DIM

    def expand(kind):
        return jnp.where(gl == kind * N_HEADS + hl, 1.0, 0.0).astype(BF16)

    def conv_body(c, _):
        rows = pl.ds(pl.multiple_of(c * CONV_STEP, CONV_STEP), CONV_STEP)
        cq, ck = _conv_chunk(c, [(q_ref, cw[:, :MIX_W], cb[:, :MIX_W]), (k_ref, cw[:, MIX_W:], cb[:, MIX_W:])],
                             n=n, n_ctx=n_ctx)
        qs_s[rows, :] = _silu(cq).astype(BF16)
        ks_s[rows, :] = (_silu(ck) * HEAD_DIM ** -0.5).astype(BF16)
        return 0

    lax.fori_loop(0, n // CONV_STEP, conv_body, 0)

    def chunk(d, c):
        rev = d == 1
        rows = pl.ds(pl.multiple_of(c * L, L), L)
        q = qs_s[rows, :]
        k = ks_s[rows, :]
        v = v_ref[0, rows, :]
        g = g_ref[0, rows, :] + gb_ref[...]
        log_i = _dot_sel(g, expand(2 * d))
        log_f = _dot_sel(_log_sigmoid(g), expand(2 * d + 1))
        causal = (ci >= ri) if rev else (ci <= ri)
        tri = jnp.where(causal, 1.0, 0.0).astype(BF16)
        cum = _sel_dot(tri, log_f)
        cum_end = cum[0:1] if rev else cum[L - 1:L]
        m_prev = m_s[d]
        c_prev = c_s[d]
        n_prev = n_s[d]
        row_src = (log_i - cum).T
        m_inter = cum + m_prev
        num_inter = jnp.dot(q, c_prev.astype(BF16), preferred_element_type=F32)
        qn = _dot_sel(q.astype(F32) * n_prev, ones_bd)
        m_q = jnp.zeros((L, MIX_W), F32)
        scs, rhs = [], []
        for h in range(N_HEADS):
            hm = head == h
            lane0 = h * HEAD_DIM
            log_d = jnp.where(causal, cum[:, lane0:lane0 + 1] + row_src[lane0:lane0 + 1, :], NEG_INF)
            m_h = jnp.maximum(jnp.max(log_d, axis=1, keepdims=True), m_inter[:, lane0:lane0 + 1])
            sc = _dot_nt(jnp.where(hm, q, jnp.zeros_like(q)), k) * jnp.exp(log_d - m_h)
            scs.append(sc.astype(BF16))
            rhs.append(jnp.concatenate([jnp.where(hm, v, jnp.zeros_like(v)),
                                        jnp.broadcast_to(jnp.where(hm, 1.0, 0.0).astype(BF16), (L, MIX_W))], axis=1))
            m_q = jnp.where(hm, m_h, m_q)
        intra = jnp.dot(jnp.concatenate(scs, axis=1), jnp.concatenate(rhs, axis=0), preferred_element_type=F32)
        s_inter = jnp.exp(m_inter - m_q)
        num = intra[:, :MIX_W] + s_inter * num_inter
        den = intra[:, MIX_W:] + s_inter * qn
        hid = num / jnp.maximum(jnp.abs(den), jnp.exp(-m_q))
        log_w = cum_end - cum + log_i
        m_loc = jnp.max(log_w, axis=0, keepdims=True)
        kw = k.astype(F32) * jnp.exp(log_w - m_loc)
        c_loc = jnp.where(bd, _dot_tn(kw.astype(BF16), v), 0.0)
        n_loc = jnp.sum(kw, axis=0, keepdims=True)
        m_new = jnp.maximum(cum_end + m_prev, m_loc)
        s_old = jnp.exp(cum_end + m_prev - m_new)
        s_loc = jnp.exp(m_loc - m_new)
        c_s[d] = s_old * c_prev + s_loc * c_loc
        n_s[d] = s_old * n_prev + s_loc * n_loc
        m_s[d] = m_new
        return rows, hid

    c_s[...] = jnp.zeros_like(c_s)
    n_s[...] = jnp.zeros_like(n_s)
    m_s[...] = jnp.full(m_s.shape, M_INIT, F32)

    def walk_body(s, _):
        rows, hid = chunk(0, s)
        hf_s[0, rows, :] = hid
        rows, hid = chunk(1, _bwd_chunk(s, n_ctx_chunks, n_chunks))
        hf_s[1, rows, :] = hid
        return 0

    lax.fori_loop(0, n_chunks, walk_body, 0)

    def out_body(c, _):
        rows = pl.ds(pl.multiple_of(c * OUT_ROWS, OUT_ROWS), OUT_ROWS)
        gate = _sigmoid(o_ref[0, rows, :].astype(F32))
        out_ref[0, rows, :] = (gate * _head_norm(hf_s[0, rows, :] + hf_s[1, rows, :], True)).astype(out_ref.dtype)
        return 0

    lax.fori_loop(0, n // OUT_ROWS, out_body, 0)


def _mixer_c(p, gates, conv_w, conv_b, gate_b, n_ctx):
    b, n, _ = p.shape
    kern = functools.partial(_mixer_c_kernel, n=n, n_ctx=n_ctx)
    return pl.pallas_call(
        kern,
        grid=(b,),
        in_specs=[_seq_spec(n, 6), _seq_spec(n, 7), _seq_spec(n, 8), _seq_spec(n, 9),
                  pl.BlockSpec((1, n, LANES), lambda bi: (bi, 0, 0)),
                  _full_spec(conv_w.shape), _full_spec(conv_b.shape), _full_spec(gate_b.shape)],
        out_specs=pl.BlockSpec((1, n, MIX_W), lambda bi: (bi, 0, 0)),
        out_shape=jax.ShapeDtypeStruct((b, n, MIX_W), BF16),
        scratch_shapes=[pltpu.VMEM((n, MIX_W), BF16), pltpu.VMEM((n, MIX_W), BF16),
                        pltpu.VMEM((2, n, MIX_W), F32),
                        pltpu.VMEM((2, MIX_W, MIX_W), F32),
                        pltpu.VMEM((2, 1, MIX_W), F32),
                        pltpu.VMEM((2, 1, MIX_W), F32)],
        compiler_params=_cparams("parallel"),
        name="mixer_mlstm",
    )(p, p, p, p, gates, conv_w, conv_b, gate_b)


def _hgrn2_tables():
    L, S = HGRN_CHUNK, SUB
    r = np.arange(L)
    same = (r[:, None] // S) == (r[None, :] // S)
    tri = np.stack([same & (r[None, :] <= r[:, None]), same & (r[None, :] >= r[:, None])])
    fold = []
    for rev in (False, True):
        pi = np.concatenate([np.full(len(_pair_rows(i, rev)), i) for i in range(S)])
        pj = np.concatenate([_pair_rows(i, rev) for i in range(S)])
        visible = (pj >= pi) if rev else (pj <= pi)
        fold.append((np.arange(S)[:, None] == pi[None, :]) & visible[None, :])
    as_bf16 = lambda a: jnp.asarray(a.astype(np.float32), dtype=BF16)
    return as_bf16(tri), as_bf16(same), as_bf16(np.stack(fold))


def _pair_rows(i, rev):
    half = SUB // 2
    if rev:
        return np.arange(0, SUB) if i < half else np.arange(half, SUB)
    return np.arange(0, half) if i < half else np.arange(0, SUB)


def _mixer_d_kernel(q_ref, ff_ref, fb_ref, v_ref, g_ref, lb_ref, tri_ref, same_ref, fold_ref, out_ref,
                    o_s, st_s, *, n, n_ctx):
    L = HGRN_CHUNK
    S = SUB
    nb = L // S
    n_chunks = n // L
    n_ctx_chunks = n_ctx // L
    lb = lb_ref[...]
    bd = _block_ones(MIX_W)
    ones_bd = jnp.where(bd, 1.0, 0.0).astype(BF16)

    def chunk(r, d, c):
        rev = d == 1
        rows = pl.ds(pl.multiple_of(c * L, L), L)
        q = _silu(q_ref[r, rows, :].astype(F32))
        raw = (fb_ref if rev else ff_ref)[r, rows, :].astype(F32)
        v = v_ref[r, rows, :]
        vf = v.astype(F32)
        f = lb + (1.0 - lb) * _sigmoid(raw)
        k = 1.0 - f
        log_f = jnp.log(f)
        cum = _sel_dot(tri_ref[d], log_f)
        tot = _sel_dot(same_ref[...], log_f)
        qt = (q * jnp.exp(cum)).astype(BF16)
        kt = (k * jnp.exp(tot - cum)).astype(BF16)
        g = jnp.exp(tot)
        cum2 = cum * LOG2_E
        prod, vj, scores, intra = {}, {}, {}, {}
        for a in range(nb + 2):
            if a < nb:
                sl = slice(a * S, (a + 1) * S)
                cb, qb, kb, vb = cum2[sl], q[sl], k[sl], vf[sl]
                prods, v_js = [], []
                for i in range(S):
                    js = _pair_rows(i, rev)
                    jsl = slice(int(js[0]), int(js[-1]) + 1)
                    dec = jnp.exp2(jnp.minimum(cb[i:i + 1] - cb[jsl], 0.0))
                    prods.append(qb[i:i + 1] * kb[jsl] * dec)
                    v_js.append(vb[jsl])
                prod[a] = jnp.concatenate(prods, axis=0).astype(BF16)
                vj[a] = jnp.concatenate(v_js, axis=0)
            if 1 <= a <= nb:
                scores[a - 1] = jnp.dot(prod.pop(a - 1), ones_bd, preferred_element_type=F32)
            if a >= 2:
                weighted = (scores.pop(a - 2) * vj.pop(a - 2)).astype(BF16)
                intra[a - 2] = jnp.dot(fold_ref[d], weighted, preferred_element_type=F32)
        o = jnp.concatenate([intra[a] for a in range(nb)], axis=0)
        blk = lax.broadcasted_iota(jnp.int32, (L, 1), 0) // S
        kt_blocks = jnp.concatenate([jnp.where(blk == a, kt, jnp.zeros_like(kt)) for a in range(nb)], axis=1)
        kv_all = _dot_tn(v, kt_blocks)
        st = st_s[r, d]
        inter = [None] * nb
        for a in (range(nb - 1, -1, -1) if rev else range(nb)):
            sl = slice(a * S, (a + 1) * S)
            inter[a] = _dot_nt(qt[sl], st.astype(BF16))
            st = st * g[a * S:a * S + 1] + jnp.where(bd, kv_all[:, a * MIX_W:(a + 1) * MIX_W], 0.0)
        st_s[r, d] = st
        o_s[r, d, rows, :] = o + jnp.concatenate(inter, axis=0)

    st_s[...] = jnp.zeros_like(st_s)

    def walk_body(s, _):
        for r in range(ROWS):
            chunk(r, 0, s)
            chunk(r, 1, _bwd_chunk(s, n_ctx_chunks, n_chunks))
        return 0

    lax.fori_loop(0, n_chunks, walk_body, 0)

    def out_body(c, _):
        rows = pl.ds(pl.multiple_of(c * L, L), L)
        for r in range(ROWS):
            gate = _silu(g_ref[r, rows, :].astype(F32))
            o = o_s[r, 0, rows, :] + o_s[r, 1, rows, :]
            out_ref[r, rows, :] = (_head_norm(o, False) * gate).astype(out_ref.dtype)
        return 0

    lax.fori_loop(0, n_chunks, out_body, 0)


def _mixer_d(p, lb, n_ctx):
    b, n, _ = p.shape
    tri, same, fold = _hgrn2_tables()
    kern = functools.partial(_mixer_d_kernel, n=n, n_ctx=n_ctx)
    return pl.pallas_call(
        kern,
        grid=(b // ROWS,),
        in_specs=[_seq_spec(n, 10, ROWS), _seq_spec(n, 11, ROWS), _seq_spec(n, 12, ROWS), _seq_spec(n, 13, ROWS),
                  _seq_spec(n, 14, ROWS), _full_spec(lb.shape), _full_spec(tri.shape), _full_spec(same.shape),
                  _full_spec(fold.shape)],
        out_specs=pl.BlockSpec((ROWS, n, MIX_W), lambda bi: (bi, 0, 0)),
        out_shape=jax.ShapeDtypeStruct((b, n, MIX_W), BF16),
        scratch_shapes=[pltpu.VMEM((ROWS, 2, n, MIX_W), F32), pltpu.VMEM((ROWS, 2, MIX_W, MIX_W), F32)],
        compiler_params=_cparams("parallel"),
        name="mixer_hgrn2",
    )(p, p, p, p, p, lb, tri, same, fold)


def _pack_bf16_pairs(t):
    w = t.shape[1] // 2
    hi = pltpu.bitcast(t[:, :w].astype(BF16).astype(F32), jnp.uint32)
    lo = pltpu.bitcast(t[:, w:].astype(BF16).astype(F32), jnp.uint32)
    return pltpu.bitcast(hi | (lo >> 16), jnp.int32)


def _unpack_bf16_pairs(p):
    u = pltpu.bitcast(p, jnp.uint32)
    hi = pltpu.bitcast(u & jnp.uint32(0xFFFF0000), F32)
    lo = pltpu.bitcast(u << 16, F32)
    return hi, lo


def _merge_kernel(x_ref, m_ref, ya_ref, yb_ref, yc_ref, yd_ref, wm_ref, wb_ref, wo_ref, wr_ref, br_ref,
                  earlier_ref, xo_ref, hp_ref, route_ref, counts_ref, carry_s, *, tm, n_ctx):
    i = pl.program_id(1)

    @pl.when((pl.program_id(0) == 0) & (i == 0))
    def _():
        carry_s[...] = jnp.zeros_like(carry_s)

    row = i * tm + lax.broadcasted_iota(jnp.int32, (tm, 1), 0)
    is_ctx = row < n_ctx
    m = m_ref[0]
    x = x_ref[0]
    d = x.shape[-1]
    h = _modulate(x, _pick(m, is_ctx, 0), _pick(m, is_ctx, 1)).astype(BF16)
    z = jnp.zeros((tm, d), F32)
    for nb, y_ref in enumerate((ya_ref, yb_ref, yc_ref, yd_ref)):
        gate = _sigmoid(jnp.dot(h, wm_ref[:, nb * d:(nb + 1) * d], preferred_element_type=F32))
        z = z + gate * jnp.dot(y_ref[0], wb_ref[nb], preferred_element_type=F32)
    mix = jnp.dot(z.astype(BF16), wo_ref[...], preferred_element_type=F32)
    x1 = x + _pick(m, is_ctx, 2) * mix
    xo_ref[0] = x1
    h2 = _modulate(x1, _pick(m, is_ctx, 3), _pick(m, is_ctx, 4))

    wr = wr_ref[...]
    wr_hi = wr.astype(BF16)
    wr_lo = (wr - wr_hi.astype(F32)).astype(BF16)
    h2_hi = h2.astype(BF16)
    h2_lo = (h2 - h2_hi.astype(F32)).astype(BF16)
    dot = functools.partial(jnp.dot, preferred_element_type=F32)
    logit = dot(h2_hi, wr_hi) + dot(h2_lo, wr_hi) + dot(h2_hi, wr_lo) + br_ref[...]
    lane = lax.broadcasted_iota(jnp.int32, (tm, LANES), 1)
    big = jnp.int32(LANES)
    is_group = (lane >= N_EXPERTS) & (lane < N_EXPERTS + N_GROUPS)
    gl = jnp.where(is_group, logit, NEG_INF)
    g_max = jnp.max(gl, axis=1, keepdims=True)
    g_idx = jnp.min(jnp.where(gl == g_max, lane, big), axis=1, keepdims=True) - N_EXPERTS
    g_prob = 1.0 / jnp.sum(jnp.where(is_group, jnp.exp(logit - g_max), 0.0), axis=1, keepdims=True)
    in_group = (lane < N_EXPERTS) & (lane // EXP_PER_GROUP == g_idx)
    el = jnp.where(in_group, logit, NEG_INF)
    v1 = jnp.max(el, axis=1, keepdims=True)
    i1 = jnp.min(jnp.where(el == v1, lane, big), axis=1, keepdims=True)
    el2 = jnp.where(lane == i1, NEG_INF, el)
    v2 = jnp.max(el2, axis=1, keepdims=True)
    i2 = jnp.min(jnp.where(el2 == v2, lane, big), axis=1, keepdims=True)
    e2 = jnp.exp(v2 - v1)
    w1 = g_prob / (1.0 + e2)
    w2 = g_prob * e2 / (1.0 + e2)
    first_low = i1 < i2
    lo = jnp.where(first_low, i1, i2) - g_idx * EXP_PER_GROUP
    hi = jnp.where(first_low, i2, i1) - g_idx * EXP_PER_GROUP
    pair = 3 * lo - ((lo * (lo - 1)) >> 1) + (hi - lo - 1)
    cls = g_idx * N_PAIRS + pair
    w_lo = jnp.where(first_low, w1, w2)
    w_hi = jnp.where(first_low, w2, w1)
    onehot = jnp.where(lane == cls, 1.0, 0.0)
    before = jnp.dot(earlier_ref[...], onehot.astype(BF16), preferred_element_type=F32) + carry_s[...]
    rank = jnp.sum(onehot * before, axis=1, keepdims=True)
    carry_s[...] += jnp.sum(onehot, axis=0, keepdims=True)
    counts_ref[...] = carry_s[...].astype(jnp.int32)
    cols = jnp.where(lane == 0, cls.astype(F32), jnp.where(lane == 1, rank, 0.0))
    pick = jnp.where(lax.broadcasted_iota(jnp.int32, (8, LANES), 0)
                     == lax.broadcasted_iota(jnp.int32, (8, LANES), 1), 1.0, 0.0).astype(BF16)
    route_ref[0] = sum(_dot_nt(pick, part) for part in _split3(cols)).astype(jnp.int32)
    gate_bits = pltpu.bitcast(jnp.where(lane == 0, w_lo, jnp.where(lane == 1, w_hi, 0.0)), jnp.int32)
    hp_ref[0] = jnp.concatenate([_pack_bf16_pairs(h2), gate_bits], axis=1)


def _merge(xc, mvec, ys, w_merge, w_branch, w_out, w_route, b_route, n_ctx):
    b, n, d = xc.shape
    tm = _token_tile(n, 640)
    kern = functools.partial(_merge_kernel, tm=tm, n_ctx=n_ctx)
    tok = lambda w: pl.BlockSpec((1, tm, w), lambda bi, i: (bi, i, 0))
    const = lambda shape: pl.BlockSpec(shape, lambda bi, i: (0,) * len(shape))
    earlier = jnp.asarray(np.tri(tm, k=-1, dtype=np.float32), dtype=BF16)
    return pl.pallas_call(
        kern,
        grid=(b, n // tm),
        in_specs=[tok(d), pl.BlockSpec((1, MOD_ROWS, d), lambda bi, i: (bi, 0, 0)),
                  tok(MIX_W), tok(MIX_W), tok(MIX_W), tok(MIX_W),
                  const(w_merge.shape), const(w_branch.shape), const(w_out.shape),
                  const(w_route.shape), const(b_route.shape), const(earlier.shape)],
        out_specs=[tok(d), tok(d // 2 + LANES),
                   pl.BlockSpec((1, 8, tm), lambda bi, i: (bi * (n // tm) + i, 0, 0)), const((1, LANES))],
        out_shape=[jax.ShapeDtypeStruct((b, n, d), F32),
                   jax.ShapeDtypeStruct((b, n, d // 2 + LANES), jnp.int32),
                   jax.ShapeDtypeStruct((b * (n // tm), 8, tm), jnp.int32),
                   jax.ShapeDtypeStruct((1, LANES), jnp.int32)],
        scratch_shapes=[pltpu.VMEM((1, LANES), F32)],
        compiler_params=_cparams("arbitrary", "arbitrary"),
        name="merge_route",
    )(xc, mvec, *ys, w_merge, w_branch, w_out, w_route, b_route, earlier)


def _row_move(table, idx, n_out=None):
    scatter = n_out is not None
    n_idx = idx.shape[0]
    width = table.shape[1]
    per_worker = n_idx // SC_WORKERS
    assert per_worker * SC_WORKERS == n_idx and per_worker % 8 == 0
    chunk = max(c for c in range(8, SC_MAX_CHUNK + 1, 8) if per_worker % c == 0)
    n_chunks = per_worker // chunk
    mesh = plsc.VectorSubcoreMesh(core_axis_name="c", subcore_axis_name="s",
                                  num_cores=SC_CORES, num_subcores=SC_SUBCORES)

    @functools.partial(
        pl.kernel, mesh=mesh,
        out_type=jax.ShapeDtypeStruct((n_out if scatter else n_idx, width), table.dtype),
        scratch_types=[pltpu.VMEM((chunk,), jnp.int32),
                       pltpu.VMEM((chunk, width), table.dtype),
                       pltpu.SemaphoreType.DMA],
        name="sc_row_scatter" if scatter else "sc_row_gather",
    )
    def move(table_hbm, idx_hbm, out_hbm, idx_v, rows_v, sem):
        worker = lax.axis_index("s") * SC_CORES + lax.axis_index("c")
        base = worker * per_worker

        @pl.loop(0, n_chunks)
        def _(j):
            off = pl.multiple_of(base + j * chunk, 8)
            pltpu.sync_copy(idx_hbm.at[pl.ds(off, chunk)], idx_v)
            if scatter:
                pltpu.sync_copy(table_hbm.at[pl.ds(off, chunk)], rows_v)
                pltpu.async_copy(rows_v, out_hbm.at[idx_v], sem).wait()
            else:
                pltpu.async_copy(table_hbm.at[idx_v], rows_v, sem).wait()
                pltpu.sync_copy(rows_v, out_hbm.at[pl.ds(off, chunk)])

    return move(table, idx)


def _ffn(h, w1_ref, w3_ref, w2_ref):
    a = jnp.dot(h, w1_ref[0, 0], preferred_element_type=F32)
    g = jnp.dot(h, w3_ref[0, 0], preferred_element_type=F32)
    return jnp.dot((_silu(a) * g).astype(BF16), w2_ref[0, 0], preferred_element_type=F32)


def _moe_kernel(elo_ref, ehi_ref, valid_ref, xs_ref, w1a_ref, w3a_ref, w2a_ref, w1b_ref, w3b_ref, w2b_ref,
                ys_ref):
    j = pl.program_id(0)
    half = ys_ref.shape[1]

    @pl.when(valid_ref[j] != 0)
    def _():
        hi, lo = _unpack_bf16_pairs(xs_ref[:, :half])
        h = jnp.concatenate([hi.astype(BF16), lo.astype(BF16)], axis=1)
        gates = pltpu.bitcast(xs_ref[:, half:], F32)
        y = (gates[:, 0:1] * _ffn(h, w1a_ref, w3a_ref, w2a_ref)
             + gates[:, 1:2] * _ffn(h, w1b_ref, w3b_ref, w2b_ref))
        ys_ref[...] = _pack_bf16_pairs(y)

    @pl.when(valid_ref[j] == 0)
    def _():
        ys_ref[...] = jnp.zeros_like(ys_ref)


def _moe_experts(xs, tile_elo, tile_ehi, tile_valid, w1, w3, w2, layer):
    n_slots, width = xs.shape
    d = w1.shape[2]
    n_tiles = n_slots // MOE_TILE
    wspec = lambda shape, which: pl.BlockSpec(
        (1, 1) + shape, lambda j, elo, ehi, valid: (layer, (elo, ehi)[which][j], 0, 0))
    grid_spec = pltpu.PrefetchScalarGridSpec(
        num_scalar_prefetch=3,
        grid=(n_tiles,),
        in_specs=[pl.BlockSpec((MOE_TILE, width), lambda j, elo, ehi, valid: (j, 0)),
                  wspec((d, D_EXPERT), 0), wspec((d, D_EXPERT), 0), wspec((D_EXPERT, d), 0),
                  wspec((d, D_EXPERT), 1), wspec((d, D_EXPERT), 1), wspec((D_EXPERT, d), 1)],
        out_specs=pl.BlockSpec((MOE_TILE, d // 2), lambda j, elo, ehi, valid: (j, 0)),
    )
    return pl.pallas_call(
        _moe_kernel,
        grid_spec=grid_spec,
        out_shape=jax.ShapeDtypeStruct((n_slots, d // 2), jnp.int32),
        compiler_params=_cparams("arbitrary"),
        name="moe_experts",
    )(tile_elo, tile_ehi, tile_valid, xs, w1, w3, w2, w1, w3, w2)


def _moe_combine_kernel(x_ref, m_ref, y_ref, *rest, tm, n_ctx, row0):
    o_ref = rest[-1]
    row = row0 + pl.program_id(1) * tm + lax.broadcasted_iota(jnp.int32, (tm, 1), 0)
    hi, lo = _unpack_bf16_pairs(y_ref[0])
    x = x_ref[0] + _pick(m_ref[0], row < n_ctx, 5) * jnp.concatenate([hi, lo], axis=1)
    if len(rest) == 2:
        ms = jnp.mean(x * x, axis=-1, keepdims=True)
        x = x * lax.rsqrt(ms + EPS) * rest[0][...]
    o_ref[0] = x


def _moe_combine(x1, mvec, yt, n_ctx, final_w=None):
    b, n, d = x1.shape
    if final_w is None:
        tm, skip, n_out, extra, extra_specs = _token_tile(n, 1100), 0, n, (), []
    else:
        tm = _token_tile(n_ctx, 1024)
        skip, n_out = n_ctx // tm, n - n_ctx
        extra, extra_specs = (final_w.reshape(1, d),), [pl.BlockSpec((1, d), lambda bi, i: (0, 0))]
    kern = functools.partial(_moe_combine_kernel, tm=tm, n_ctx=n_ctx, row0=skip * tm)
    tok = lambda w: pl.BlockSpec((1, tm, w), lambda bi, i: (bi, i + skip, 0))
    return pl.pallas_call(
        kern,
        grid=(b, n_out // tm),
        in_specs=[tok(d), pl.BlockSpec((1, MOD_ROWS, d), lambda bi, i: (bi, 0, 0)), tok(d // 2)] + extra_specs,
        out_specs=pl.BlockSpec((1, tm, d), lambda bi, i: (bi, i, 0)),
        out_shape=jax.ShapeDtypeStruct((b, n_out, d), F32),
        compiler_params=_cparams("parallel", "parallel"),
        name="moe_combine",
    )(x1, mvec, yt, *extra)


def _moe(x1, mvec, hp, route, counts, w1, w3, w2, layer, n_ctx, final_w=None):
    b, n, d = x1.shape
    m = b * n
    n_slots = m + N_CLASSES * MOE_TILE
    cls = route[:, 0, :].reshape(m)
    rank = route[:, 1, :].reshape(m)
    cnt = counts[0, :N_CLASSES]
    padded = (cnt + MOE_TILE - 1) // MOE_TILE * MOE_TILE
    ends = jnp.cumsum(padded)
    pos = (ends - padded)[cls] + rank
    tile_start = jnp.arange(n_slots // MOE_TILE, dtype=jnp.int32) * MOE_TILE
    tile_cls = jnp.minimum(jnp.searchsorted(ends, tile_start, side="right"), N_CLASSES - 1).astype(jnp.int32)
    tile_valid = (tile_start < ends[-1]).astype(jnp.int32)
    pair_lo = jnp.asarray([0, 0, 0, 1, 1, 2], jnp.int32)
    pair_hi = jnp.asarray([1, 2, 3, 2, 3, 3], jnp.int32)
    group = tile_cls // N_PAIRS
    tile_elo = group * EXP_PER_GROUP + pair_lo[tile_cls % N_PAIRS]
    tile_ehi = group * EXP_PER_GROUP + pair_hi[tile_cls % N_PAIRS]

    xs = _row_move(hp.reshape(m, hp.shape[-1]), pos, n_out=n_slots)
    ys = _moe_experts(xs, tile_elo, tile_ehi, tile_valid, w1, w3, w2, layer)
    yt = _row_move(ys, pos)
    return _moe_combine(x1, mvec, yt.reshape(b, n, d // 2), n_ctx, final_w)


def _rope_tables(n_lat, n_ctx):
    rows = n_lat // GRID_W
    row = jnp.repeat(jnp.arange(rows), GRID_W).astype(F32)
    col = jnp.tile(jnp.arange(GRID_W), rows).astype(F32)
    nq = HEAD_DIM // 4
    inv = jnp.power(ROPE_BASE, -jnp.arange(nq, dtype=F32) / nq)
    ang = jnp.concatenate([row[:, None] * inv, col[:, None] * inv], -1)
    cos = jnp.cos(ang)
    sin = jnp.sin(ang)
    cos_h = jnp.concatenate([cos, cos], -1)
    sin_h = jnp.concatenate([-sin, sin], -1)
    cos_full = jnp.concatenate([jnp.ones((n_ctx, HEAD_DIM), F32), cos_h], 0)
    sin_full = jnp.concatenate([jnp.zeros((n_ctx, HEAD_DIM), F32), sin_h], 0)
    return jnp.tile(cos_full, (1, N_HEADS)), jnp.tile(sin_full, (1, N_HEADS))


def _block_diag_heads(w):
    eye = jnp.eye(N_HEADS, dtype=w.dtype)
    return jnp.einsum("hij,hg->higj", w, eye).reshape(MIX_W, MIX_W)


def kernel(x, c, ctx, c_ctx, w_mod, b_mod, w_in, a_conv_w, a_conv_b, a_gate_w, a_gate_b, a_lambda, b_theta,
           c_conv_w, c_conv_b, c_gate_b, d_lb, w_branch, w_out, moe_w_group, moe_b_group, moe_w_router,
           moe_b_router, moe_w1, moe_w3, moe_w2, final_norm_w):
    bsz, n_lat, d = x.shape
    n_ctx = ctx.shape[1]
    depth = w_mod.shape[0]
    assert n_ctx % CHUNK == 0 and n_lat % CHUNK == 0 and n_ctx % 256 == 0 and n_lat % 256 == 0

    xc = jnp.concatenate([ctx, x], axis=1)
    cos, sin = _rope_tables(n_lat, n_ctx)

    cc = jnp.zeros((8, d), F32).at[:bsz].set(c).at[bsz].set(c_ctx)
    mod = _mod_vectors(cc, w_mod, b_mod)
    mx = mod[:, :bsz].reshape(depth, bsz, N_MOD, d)
    mc = jnp.broadcast_to(mod[:, bsz].reshape(depth, 1, N_MOD, d), (depth, bsz, N_MOD, d))
    pad = jnp.zeros((depth, bsz, 8 - N_MOD, d), F32)
    mvec = jnp.concatenate([mx, pad, mc, pad], axis=2)

    lbs = jnp.cumsum(jax.nn.softmax(d_lb.astype(F32), axis=0), axis=0)
    lbs = lbs - lbs[0]

    n_mix_cols = 15 * MIX_W
    gate0 = 10 * MIX_W
    n_gate = 4 * N_HEADS
    w1b = moe_w1.astype(BF16)
    w3b = moe_w3.astype(BF16)
    w2b = moe_w2.astype(BF16)

    for l in range(depth):
        wl = w_in[l]
        w_mix = jnp.concatenate([wl[:, :gate0], wl[:, gate0 + n_gate:n_mix_cols + n_gate]], axis=1).astype(BF16)
        w_gate = jnp.pad(wl[:, gate0:gate0 + n_gate], ((0, 0), (0, LANES - n_gate))).astype(BF16)
        w_merge = wl[:, n_mix_cols + n_gate:].astype(BF16)
        p, gates_c = _in_proj(xc, mvec[l], w_mix, w_gate, cos, sin, n_ctx)

        gw = jnp.concatenate([_block_diag_heads(a_gate_w[l, dd, j]) for dd in range(2) for j in range(2)],
                             axis=1).astype(BF16)
        gb = a_gate_b[l].reshape(1, 4 * MIX_W)
        ya = _mixer_a(p, a_conv_w[l], a_conv_b[l].reshape(1, MIX_W), gw, gb, a_lambda[l], n_ctx)

        log_gamma = jnp.repeat(jax.nn.log_sigmoid(b_theta[l].astype(F32)), HEAD_DIM, axis=1)
        yb = _mixer_b(p, log_gamma, n_ctx)

        gate_b = jnp.pad(c_gate_b[l].reshape(1, n_gate), ((0, 0), (0, LANES - n_gate)))
        yc = _mixer_c(p, gates_c, c_conv_w[l], c_conv_b[l].reshape(1, 2 * MIX_W), gate_b, n_ctx)

        yd = _mixer_d(p, lbs[l].reshape(1, MIX_W), n_ctx)

        w_route = jnp.pad(jnp.concatenate([moe_w_router[l], moe_w_group[l]], axis=1),
                          ((0, 0), (0, LANES - N_EXPERTS - N_GROUPS)))
        b_route = jnp.pad(jnp.concatenate([moe_b_router[l], moe_b_group[l]]),
                          (0, LANES - N_EXPERTS - N_GROUPS)).reshape(1, LANES)
        x1, hp, route, counts = _merge(xc, mvec[l], (ya, yb, yc, yd), w_merge, w_branch[l].astype(BF16),
                                       w_out[l].astype(BF16), w_route, b_route, n_ctx)
        xc = _moe(x1, mvec[l], hp, route, counts, w1b, w3b, w2b, l, n_ctx,
                  final_w=final_norm_w if l == depth - 1 else None)

    return xc
```

```python
import functools

import jax
import jax.numpy as jnp
import numpy as np
from jax import lax
from jax.experimental import pallas as pl
from jax.experimental.pallas import tpu as pltpu
from jax.experimental.pallas import tpu_sc as plsc

F32 = jnp.float32
BF16 = jnp.bfloat16

EPS = 1e-6
N_HEADS = 4
HEAD_DIM = 64
MIX_W = N_HEADS * HEAD_DIM
N_BRANCH = 4
CONV_W = 4
LRU_C = 8.0
GRID_W = 64
ROPE_BASE = 10000.0
N_GROUPS = 4
EXP_PER_GROUP = 4
N_EXPERTS = N_GROUPS * EXP_PER_GROUP
D_EXPERT = 512
N_MOD = 6
M_INIT = -1e30

CHUNK = 128
RET_CHUNK = 256
CONV_STEP = 128
OUT_ROWS = 256
SUB = 16
HGRN_CHUNK = 256
HALO = 16
LANES = 128
MOD_ROWS = 16
ROWS = 1
VMEM_LIMIT_BYTES = 56 * 1024 * 1024
N_PAIRS = 6
N_CLASSES = N_GROUPS * N_PAIRS
MOE_TILE = 256
SC_CORES = 2
SC_SUBCORES = 16
SC_WORKERS = SC_CORES * SC_SUBCORES
SC_MAX_CHUNK = 32
NEG_INF = float("-inf")
LOG2_E = 1.4426950408889634


def _cparams(*sem):
    return pltpu.CompilerParams(dimension_semantics=sem, vmem_limit_bytes=VMEM_LIMIT_BYTES)


def _token_tile(n, cap):
    best = 16
    for t in range(16, cap + 1, 16):
        if n % t == 0:
            best = t
    return best


def _modulate(x, shift, scale):
    ms = jnp.mean(x * x, axis=-1, keepdims=True)
    return x * lax.rsqrt(ms + EPS) * (1.0 + scale) + shift


def _pick(m, is_ctx, k):
    return jnp.where(is_ctx, m[8 + k:9 + k], m[k:k + 1])


def _sigmoid(x):
    return 0.5 * jnp.tanh(0.5 * x) + 0.5


def _silu(x):
    return x * _sigmoid(x)


def _log_sigmoid(x):
    return jnp.minimum(x, 0.0) - jnp.log(1.0 + jnp.exp(-jnp.abs(x)))


def _split3(x):
    hi = x.astype(BF16)
    r = x - hi.astype(F32)
    mid = r.astype(BF16)
    lo = (r - mid.astype(F32)).astype(BF16)
    return hi, mid, lo


def _split2(x):
    hi = x.astype(BF16)
    return hi, (x - hi.astype(F32)).astype(BF16)


def _sel_dot(sel, x):
    hi, lo = _split2(x)
    d = functools.partial(jnp.dot, preferred_element_type=F32)
    return d(sel, hi) + d(sel, lo)


def _dot_sel(x, sel):
    hi, lo = _split2(x)
    d = functools.partial(jnp.dot, preferred_element_type=F32)
    return d(hi, sel) + d(lo, sel)


def _dot_nt(a, b):
    return lax.dot_general(a, b, (((1,), (1,)), ((), ())), preferred_element_type=F32)


def _dot_tn(a, b):
    return lax.dot_general(a, b, (((0,), (0,)), ((), ())), preferred_element_type=F32)


def _head_of_lane(width):
    return lax.broadcasted_iota(jnp.int32, (1, width), 1) // HEAD_DIM


def _block_ones(width):
    r = lax.broadcasted_iota(jnp.int32, (width, width), 0) // HEAD_DIM
    c = lax.broadcasted_iota(jnp.int32, (width, width), 1) // HEAD_DIM
    return r == c


def _head_norm(o, center):
    ones = jnp.where(_block_ones(MIX_W), 1.0, 0.0).astype(BF16)
    inv = 1.0 / HEAD_DIM
    if center:
        o = o - _dot_sel(o, ones) * inv
    var = _dot_sel(o * o, ones) * inv
    return o * lax.rsqrt(var + EPS)


def _bwd_chunk(s, n_ctx_chunks, n_chunks):
    return jnp.where(s < n_ctx_chunks, n_ctx_chunks - 1 - s, n_chunks - 1 + n_ctx_chunks - s)


def _conv_chunk(c, operands, *, n, n_ctx):
    L = CONV_STEP
    wl = L + 2 * HALO
    r0 = c * L
    start = pl.multiple_of(jnp.clip(r0 - HALO, 0, n - wl), HALO)
    off = r0 - start
    shift = (lax.broadcasted_iota(jnp.int32, (L, wl), 1) - lax.broadcasted_iota(jnp.int32, (L, wl), 0)) - off
    g = r0 + lax.broadcasted_iota(jnp.int32, (L, 1), 0)
    seg_g = jnp.where(g < n_ctx, 0, 1)
    sels = []
    for k in range(CONV_W):
        src = g + (k - 2)
        row_ok = (jnp.where(src < n_ctx, 0, 1) == seg_g) & (src >= 0) & (src < n)
        sels.append(jnp.where((shift == k - 2) & row_ok, 1.0, 0.0).astype(BF16))
    outs = []
    for ref, w, b in operands:
        win = ref[0, pl.ds(start, wl), :]
        y = b
        for k in range(CONV_W):
            y = y + w[k:k + 1] * jnp.dot(sels[k], win, preferred_element_type=F32)
        outs.append(y)
    return outs


def _mod_kernel(cc_ref, w_ref, b_ref, o_ref):
    s = _silu(cc_ref[...])
    o_ref[0] = jnp.dot(s, w_ref[0], precision=lax.Precision.HIGHEST, preferred_element_type=F32) + b_ref[0]


def _mod_vectors(cc, w_mod, b_mod):
    depth, d, dm = w_mod.shape
    tn = 1536
    return pl.pallas_call(
        _mod_kernel,
        grid=(depth, dm // tn),
        in_specs=[pl.BlockSpec((8, d), lambda l, j: (0, 0)),
                  pl.BlockSpec((1, d, tn), lambda l, j: (l, 0, j)),
                  pl.BlockSpec((1, 1, tn), lambda l, j: (l, 0, j))],
        out_specs=pl.BlockSpec((1, 8, tn), lambda l, j: (l, 0, j)),
        out_shape=jax.ShapeDtypeStruct((depth, 8, dm), F32),
        compiler_params=_cparams("parallel", "parallel"),
        name="mod_vectors",
    )(cc, w_mod, b_mod.reshape(depth, 1, dm))


def _rope(t, cos, sin):
    lane = lax.broadcasted_iota(jnp.int32, (1, MIX_W), 1) % HEAD_DIM
    half = HEAD_DIM // 2
    swapped = jnp.where(lane < half, pltpu.roll(t, MIX_W - half, 1), pltpu.roll(t, half, 1))
    return t * cos + swapped * sin


RET_Q_COL = 2 * MIX_W
RET_K_COL = 3 * MIX_W


def _in_proj_kernel(x_ref, m_ref, w_ref, wg_ref, cos_ref, sin_ref, p_ref, g_ref, *, tm, n_ctx, col_chunk):
    i = pl.program_id(1)
    row = i * tm + lax.broadcasted_iota(jnp.int32, (tm, 1), 0)
    is_ctx = row < n_ctx
    m = m_ref[0]
    h = _modulate(x_ref[0], _pick(m, is_ctx, 0), _pick(m, is_ctx, 1)).astype(BF16)
    for j in range(w_ref.shape[1] // col_chunk):
        sl = slice(j * col_chunk, (j + 1) * col_chunk)
        t = jnp.dot(h, w_ref[:, sl], preferred_element_type=F32)
        if j * col_chunk == RET_Q_COL:
            t = _rope(t, cos_ref[...], sin_ref[...])
        elif j * col_chunk == RET_K_COL:
            t = _rope(t, cos_ref[...], sin_ref[...]) * HEAD_DIM ** -0.5
        p_ref[0, :, sl] = t.astype(BF16)
    g_ref[0] = jnp.dot(h, wg_ref[...], preferred_element_type=F32)


def _in_proj(xc, mvec, w_mix, w_gate, cos, sin, n_ctx):
    b, n, d = xc.shape
    tm = _token_tile(n, 640)
    wc = w_mix.shape[1]
    kern = functools.partial(_in_proj_kernel, tm=tm, n_ctx=n_ctx, col_chunk=MIX_W)
    return pl.pallas_call(
        kern,
        grid=(b, n // tm),
        in_specs=[pl.BlockSpec((1, tm, d), lambda bi, i: (bi, i, 0)),
                  pl.BlockSpec((1, MOD_ROWS, d), lambda bi, i: (bi, 0, 0)),
                  pl.BlockSpec((d, wc), lambda bi, i: (0, 0)),
                  pl.BlockSpec((d, LANES), lambda bi, i: (0, 0)),
                  pl.BlockSpec((tm, MIX_W), lambda bi, i: (i, 0)),
                  pl.BlockSpec((tm, MIX_W), lambda bi, i: (i, 0))],
        out_specs=[pl.BlockSpec((1, tm, wc), lambda bi, i: (bi, i, 0)),
                   pl.BlockSpec((1, tm, LANES), lambda bi, i: (bi, i, 0))],
        out_shape=[jax.ShapeDtypeStruct((b, n, wc), BF16),
                   jax.ShapeDtypeStruct((b, n, LANES), F32)],
        compiler_params=_cparams("parallel", "parallel"),
        name="in_proj",
    )(xc, mvec, w_mix, w_gate, cos, sin)


def _lin_scan(a, x, carry, reverse):
    L = a.shape[0]
    row = lax.broadcasted_iota(jnp.int32, (L, 1), 0)
    k = 1
    while k < L:
        if reverse:
            a_s = pltpu.roll(a, L - k, 0)
            x_s = pltpu.roll(x, L - k, 0)
            valid = row < L - k
        else:
            a_s = pltpu.roll(a, k, 0)
            x_s = pltpu.roll(x, k, 0)
            valid = row >= k
        x = jnp.where(valid, a * x_s + x, x)
        a = jnp.where(valid, a * a_s, a)
        k *= 2
    h = x + a * carry
    return h, (h[0:1] if reverse else h[L - 1:L])


def _mixer_a_kernel(ax_ref, ag_ref, cw_ref, cb_ref, gw_ref, gb_ref, lam_ref, out_ref, u_s, hf_s,
                    *, n, n_ctx):
    L = CHUNK
    n_chunks = n // L
    n_ctx_chunks = n_ctx // L
    cw = cw_ref[...]
    cb = cb_ref[...]
    log_lam = _log_sigmoid(lam_ref[...])

    def direction(d, c, carry):
        r0 = pl.multiple_of(c * L, L)
        u = u_s[pl.ds(r0, L), :]
        pre = jnp.dot(u.astype(BF16), gw_ref[:, d * 2 * MIX_W:(d + 1) * 2 * MIX_W],
                      preferred_element_type=F32) + gb_ref[:, d * 2 * MIX_W:(d + 1) * 2 * MIX_W]
        r = _sigmoid(pre[:, :MIX_W])
        gi = _sigmoid(pre[:, MIX_W:])
        log_a = LRU_C * r * log_lam[d:d + 1]
        a = jnp.exp(log_a)
        inp = jnp.sqrt(1.0 - jnp.exp(2.0 * log_a)) * (gi * u)
        h, new_carry = _lin_scan(a, inp, carry, reverse=(d == 1))
        return r0, h, new_carry

    def conv_body(c, _):
        r0 = pl.multiple_of(c * CONV_STEP, CONV_STEP)
        u_s[pl.ds(r0, CONV_STEP), :] = _conv_chunk(c, [(ax_ref, cw, cb)], n=n, n_ctx=n_ctx)[0]
        return 0

    lax.fori_loop(0, n // CONV_STEP, conv_body, 0)

    def walk_body(s, carries):
        r0, h, carry_f = direction(0, s, carries[0])
        hf_s[0, pl.ds(r0, L), :] = h
        r0, h, carry_b = direction(1, _bwd_chunk(s, n_ctx_chunks, n_chunks), carries[1])
        hf_s[1, pl.ds(r0, L), :] = h
        return carry_f, carry_b

    zero = jnp.zeros((1, MIX_W), F32)
    lax.fori_loop(0, n_chunks, walk_body, (zero, zero))

    def out_body(c, _):
        rows = pl.ds(pl.multiple_of(c * L, L), L)
        gate = jax.nn.gelu(ag_ref[0, rows, :].astype(F32), approximate=True)
        out_ref[0, rows, :] = (gate * (hf_s[0, rows, :] + hf_s[1, rows, :])).astype(out_ref.dtype)
        return 0

    lax.fori_loop(0, n_chunks, out_body, 0)


def _seq_spec(n, col, rows=1):
    if rows == 1:
        return pl.BlockSpec((1, n, MIX_W), lambda b: (b, 0, col))
    return pl.BlockSpec((rows, n, MIX_W), lambda b: (b, 0, col), pipeline_mode=pl.Buffered(1))


def _full_spec(shape):
    return pl.BlockSpec(shape, lambda b: (0,) * len(shape))


def _mixer_a(p, conv_w, conv_b, gate_w, gate_b, lam, n_ctx):
    b, n, _ = p.shape
    kern = functools.partial(_mixer_a_kernel, n=n, n_ctx=n_ctx)
    return pl.pallas_call(
        kern,
        grid=(b,),
        in_specs=[_seq_spec(n, 0), _seq_spec(n, 1),
                  _full_spec(conv_w.shape), _full_spec(conv_b.shape),
                  _full_spec(gate_w.shape), _full_spec(gate_b.shape), _full_spec(lam.shape)],
        out_specs=pl.BlockSpec((1, n, MIX_W), lambda bi: (bi, 0, 0)),
        out_shape=jax.ShapeDtypeStruct((b, n, MIX_W), BF16),
        scratch_shapes=[pltpu.VMEM((n, MIX_W), F32), pltpu.VMEM((2, n, MIX_W), F32)],
        compiler_params=_cparams("parallel"),
        name="mixer_rglru",
    )(p, p, conv_w, conv_b, gate_w, gate_b, lam)


def _mixer_b_kernel(q_ref, k_ref, v_ref, g_ref, lg_ref, out_ref, oi_s, dec_s, st_s, *, n, n_ctx):
    L = RET_CHUNK
    n_chunks = n // L
    n_ctx_chunks = n_ctx // L
    head = _head_of_lane(MIX_W)
    bd = _block_ones(MIX_W)
    lgf = lg_ref[0:1, :]
    lgb = lg_ref[1:2, :]
    pos = lax.broadcasted_iota(jnp.int32, (L, 1), 0).astype(F32)
    diff = (lax.broadcasted_iota(jnp.int32, (L, L), 0) - lax.broadcasted_iota(jnp.int32, (L, L), 1)).astype(F32)
    for h in range(N_HEADS):
        lf = lgf[:, h * HEAD_DIM:h * HEAD_DIM + 1]
        lb = lgb[:, h * HEAD_DIM:h * HEAD_DIM + 1]
        dec_s[h] = jnp.where(diff >= 0, jnp.exp(lf * jnp.maximum(diff, 0.0)), jnp.exp(lb * jnp.maximum(-diff, 0.0)))

    qw = (jnp.exp(lgf * (pos + 1.0)), jnp.exp(lgb * (float(L) - pos)))
    kw = (jnp.exp(lgf * (float(L) - 1.0 - pos)), jnp.exp(lgb * pos))
    g_chunk = (jnp.exp(lgf * float(L)), jnp.exp(lgb * float(L)))

    def state_step(d, c):
        rows = pl.ds(pl.multiple_of(c * L, L), L)
        q = q_ref[0, rows, :].astype(F32)
        k = k_ref[0, rows, :].astype(F32)
        st = st_s[d]
        oi_s[d, rows, :] = jnp.dot((q * qw[d]).astype(BF16), st.astype(BF16), preferred_element_type=F32)
        kv = _dot_tn((k * kw[d]).astype(BF16), v_ref[0, rows, :])
        st_s[d] = g_chunk[d] * st + jnp.where(bd, kv, 0.0)

    st_s[...] = jnp.zeros_like(st_s)

    def walk_body(s, _):
        state_step(0, s)
        state_step(1, _bwd_chunk(s, n_ctx_chunks, n_chunks))
        return 0

    lax.fori_loop(0, n_chunks, walk_body, 0)

    def out_body(c, _):
        rows = pl.ds(pl.multiple_of(c * L, L), L)
        q = q_ref[0, rows, :]
        k = k_ref[0, rows, :]
        v = v_ref[0, rows, :]
        scs, vals = [], []
        for h in range(N_HEADS):
            hm = head == h
            scs.append((_dot_nt(jnp.where(hm, q, jnp.zeros_like(q)), k) * dec_s[h]).astype(BF16))
            vals.append(jnp.where(hm, v, jnp.zeros_like(v)))
        o = (oi_s[0, rows, :] + oi_s[1, rows, :]
             + jnp.dot(jnp.concatenate(scs, axis=1), jnp.concatenate(vals, axis=0), preferred_element_type=F32))
        gate = _silu(g_ref[0, rows, :].astype(F32))
        out_ref[0, rows, :] = (_head_norm(o, True) * gate).astype(out_ref.dtype)
        return 0

    lax.fori_loop(0, n_chunks, out_body, 0, unroll=2)


def _mixer_b(p, log_gamma, n_ctx):
    b, n, _ = p.shape
    kern = functools.partial(_mixer_b_kernel, n=n, n_ctx=n_ctx)
    return pl.pallas_call(
        kern,
        grid=(b,),
        in_specs=[_seq_spec(n, 2), _seq_spec(n, 3), _seq_spec(n, 4), _seq_spec(n, 5),
                  _full_spec(log_gamma.shape)],
        out_specs=pl.BlockSpec((1, n, MIX_W), lambda bi: (bi, 0, 0)),
        out_shape=jax.ShapeDtypeStruct((b, n, MIX_W), BF16),
        scratch_shapes=[pltpu.VMEM((2, n, MIX_W), F32),
                        pltpu.VMEM((N_HEADS, RET_CHUNK, RET_CHUNK), F32),
                        pltpu.VMEM((2, MIX_W, MIX_W), F32)],
        compiler_params=_cparams("parallel"),
        name="mixer_retention",
    )(p, p, p, p, log_gamma)


def _mixer_c_kernel(q_ref, k_ref, v_ref, o_ref, g_ref, cw_ref, cb_ref, gb_ref, out_ref,
                    qs_s, ks_s, hf_s, c_s, n_s, m_s, *, n, n_ctx):
    L = CHUNK
    n_chunks = n // L
    n_ctx_chunks = n_ctx // L
    head = _head_of_lane(MIX_W)
    bd = _block_ones(MIX_W)
    ones_bd = jnp.where(bd, 1.0, 0.0).astype(BF16)
    cw = cw_ref[...]
    cb = cb_ref[...]
    ri = lax.broadcasted_iota(jnp.int32, (L, L), 0)
    ci = lax.broadcasted_iota(jnp.int32, (L, L), 1)
    gl = lax.broadcasted_iota(jnp.int32, (LANES, MIX_W), 0)
    hl = lax.broadcasted_iota(jnp.int32, (LANES, MIX_W), 1) // HEAD_DIM

    def expand(kind):
        return jnp.where(gl == kind * N_HEADS + hl, 1.0, 0.0).astype(BF16)

    def conv_body(c, _):
        rows = pl.ds(pl.multiple_of(c * CONV_STEP, CONV_STEP), CONV_STEP)
        cq, ck = _conv_chunk(c, [(q_ref, cw[:, :MIX_W], cb[:, :MIX_W]), (k_ref, cw[:, MIX_W:], cb[:, MIX_W:])],
                             n=n, n_ctx=n_ctx)
        qs_s[rows, :] = _silu(cq).astype(BF16)
        ks_s[rows, :] = (_silu(ck) * HEAD_DIM ** -0.5).astype(BF16)
        return 0

    lax.fori_loop(0, n // CONV_STEP, conv_body, 0)

    def chunk(d, c):
        rev = d == 1
        rows = pl.ds(pl.multiple_of(c * L, L), L)
        q = qs_s[rows, :]
        k = ks_s[rows, :]
        v = v_ref[0, rows, :]
        g = g_ref[0, rows, :] + gb_ref[...]
        log_i = _dot_sel(g, expand(2 * d))
        log_f = _dot_sel(_log_sigmoid(g), expand(2 * d + 1))
        causal = (ci >= ri) if rev else (ci <= ri)
        tri = jnp.where(causal, 1.0, 0.0).astype(BF16)
        cum = _sel_dot(tri, log_f)
        cum_end = cum[0:1] if rev else cum[L - 1:L]
        m_prev = m_s[d]
        c_prev = c_s[d]
        n_prev = n_s[d]
        row_src = (log_i - cum).T
        m_inter = cum + m_prev
        num_inter = jnp.dot(q, c_prev.astype(BF16), preferred_element_type=F32)
        qn = _dot_sel(q.astype(F32) * n_prev, ones_bd)
        m_q = jnp.zeros((L, MIX_W), F32)
        scs, rhs = [], []
        for h in range(N_HEADS):
            hm = head == h
            lane0 = h * HEAD_DIM
            log_d = jnp.where(causal, cum[:, lane0:lane0 + 1] + row_src[lane0:lane0 + 1, :], NEG_INF)
            m_h = jnp.maximum(jnp.max(log_d, axis=1, keepdims=True), m_inter[:, lane0:lane0 + 1])
            sc = _dot_nt(jnp.where(hm, q, jnp.zeros_like(q)), k) * jnp.exp(log_d - m_h)
            scs.append(sc.astype(BF16))
            rhs.append(jnp.concatenate([jnp.where(hm, v, jnp.zeros_like(v)),
                                        jnp.broadcast_to(jnp.where(hm, 1.0, 0.0).astype(BF16), (L, MIX_W))], axis=1))
            m_q = jnp.where(hm, m_h, m_q)
        intra = jnp.dot(jnp.concatenate(scs, axis=1), jnp.concatenate(rhs, axis=0), preferred_element_type=F32)
        s_inter = jnp.exp(m_inter - m_q)
        num = intra[:, :MIX_W] + s_inter * num_inter
        den = intra[:, MIX_W:] + s_inter * qn
        hid = num / jnp.maximum(jnp.abs(den), jnp.exp(-m_q))
        log_w = cum_end - cum + log_i
        m_loc = jnp.max(log_w, axis=0, keepdims=True)
        kw = k.astype(F32) * jnp.exp(log_w - m_loc)
        c_loc = jnp.where(bd, _dot_tn(kw.astype(BF16), v), 0.0)
        n_loc = jnp.sum(kw, axis=0, keepdims=True)
        m_new = jnp.maximum(cum_end + m_prev, m_loc)
        s_old = jnp.exp(cum_end + m_prev - m_new)
        s_loc = jnp.exp(m_loc - m_new)
        c_s[d] = s_old * c_prev + s_loc * c_loc
        n_s[d] = s_old * n_prev + s_loc * n_loc
        m_s[d] = m_new
        return rows, hid

    c_s[...] = jnp.zeros_like(c_s)
    n_s[...] = jnp.zeros_like(n_s)
    m_s[...] = jnp.full(m_s.shape, M_INIT, F32)

    def walk_body(s, _):
        rows, hid = chunk(0, s)
        hf_s[0, rows, :] = hid
        rows, hid = chunk(1, _bwd_chunk(s, n_ctx_chunks, n_chunks))
        hf_s[1, rows, :] = hid
        return 0

    lax.fori_loop(0, n_chunks, walk_body, 0)

    def out_body(c, _):
        rows = pl.ds(pl.multiple_of(c * OUT_ROWS, OUT_ROWS), OUT_ROWS)
        gate = _sigmoid(o_ref[0, rows, :].astype(F32))
        out_ref[0, rows, :] = (gate * _head_norm(hf_s[0, rows, :] + hf_s[1, rows, :], True)).astype(out_ref.dtype)
        return 0

    lax.fori_loop(0, n // OUT_ROWS, out_body, 0)


def _mixer_c(p, gates, conv_w, conv_b, gate_b, n_ctx):
    b, n, _ = p.shape
    kern = functools.partial(_mixer_c_kernel, n=n, n_ctx=n_ctx)
    return pl.pallas_call(
        kern,
        grid=(b,),
        in_specs=[_seq_spec(n, 6), _seq_spec(n, 7), _seq_spec(n, 8), _seq_spec(n, 9),
                  pl.BlockSpec((1, n, LANES), lambda bi: (bi, 0, 0)),
                  _full_spec(conv_w.shape), _full_spec(conv_b.shape), _full_spec(gate_b.shape)],
        out_specs=pl.BlockSpec((1, n, MIX_W), lambda bi: (bi, 0, 0)),
        out_shape=jax.ShapeDtypeStruct((b, n, MIX_W), BF16),
        scratch_shapes=[pltpu.VMEM((n, MIX_W), BF16), pltpu.VMEM((n, MIX_W), BF16),
                        pltpu.VMEM((2, n, MIX_W), F32),
                        pltpu.VMEM((2, MIX_W, MIX_W), F32),
                        pltpu.VMEM((2, 1, MIX_W), F32),
                        pltpu.VMEM((2, 1, MIX_W), F32)],
        compiler_params=_cparams("parallel"),
        name="mixer_mlstm",
    )(p, p, p, p, gates, conv_w, conv_b, gate_b)


def _hgrn2_tables():
    L, S = HGRN_CHUNK, SUB
    r = np.arange(L)
    same = (r[:, None] // S) == (r[None, :] // S)
    tri = np.stack([same & (r[None, :] <= r[:, None]), same & (r[None, :] >= r[:, None])])
    fold = []
    for rev in (False, True):
        pi = np.concatenate([np.full(len(_pair_rows(i, rev)), i) for i in range(S)])
        pj = np.concatenate([_pair_rows(i, rev) for i in range(S)])
        visible = (pj >= pi) if rev else (pj <= pi)
        fold.append((np.arange(S)[:, None] == pi[None, :]) & visible[None, :])
    as_bf16 = lambda a: jnp.asarray(a.astype(np.float32), dtype=BF16)
    return as_bf16(tri), as_bf16(same), as_bf16(np.stack(fold))


def _pair_rows(i, rev):
    half = SUB // 2
    if rev:
        return np.arange(0, SUB) if i < half else np.arange(half, SUB)
    return np.arange(0, half) if i < half else np.arange(0, SUB)


def _mixer_d_kernel(q_ref, ff_ref, fb_ref, v_ref, g_ref, lb_ref, tri_ref, same_ref, fold_ref, out_ref,
                    o_s, st_s, *, n, n_ctx):
    L = HGRN_CHUNK
    S = SUB
    nb = L // S
    n_chunks = n // L
    n_ctx_chunks = n_ctx // L
    lb = lb_ref[...]
    bd = _block_ones(MIX_W)
    ones_bd = jnp.where(bd, 1.0, 0.0).astype(BF16)

    def chunk(r, d, c):
        rev = d == 1
        rows = pl.ds(pl.multiple_of(c * L, L), L)
        q = _silu(q_ref[r, rows, :].astype(F32))
        raw = (fb_ref if rev else ff_ref)[r, rows, :].astype(F32)
        v = v_ref[r, rows, :]
        vf = v.astype(F32)
        f = lb + (1.0 - lb) * _sigmoid(raw)
        k = 1.0 - f
        log_f = jnp.log(f)
        cum = _sel_dot(tri_ref[d], log_f)
        tot = _sel_dot(same_ref[...], log_f)
        qt = (q * jnp.exp(cum)).astype(BF16)
        kt = (k * jnp.exp(tot - cum)).astype(BF16)
        g = jnp.exp(tot)
        cum2 = cum * LOG2_E
        prod, vj, scores, intra = {}, {}, {}, {}
        for a in range(nb + 2):
            if a < nb:
                sl = slice(a * S, (a + 1) * S)
                cb, qb, kb, vb = cum2[sl], q[sl], k[sl], vf[sl]
                prods, v_js = [], []
                for i in range(S):
                    js = _pair_rows(i, rev)
                    jsl = slice(int(js[0]), int(js[-1]) + 1)
                    dec = jnp.exp2(jnp.minimum(cb[i:i + 1] - cb[jsl], 0.0))
                    prods.append(qb[i:i + 1] * kb[jsl] * dec)
                    v_js.append(vb[jsl])
                prod[a] = jnp.concatenate(prods, axis=0).astype(BF16)
                vj[a] = jnp.concatenate(v_js, axis=0)
            if 1 <= a <= nb:
                scores[a - 1] = jnp.dot(prod.pop(a - 1), ones_bd, preferred_element_type=F32)
            if a >= 2:
                weighted = (scores.pop(a - 2) * vj.pop(a - 2)).astype(BF16)
                intra[a - 2] = jnp.dot(fold_ref[d], weighted, preferred_element_type=F32)
        o = jnp.concatenate([intra[a] for a in range(nb)], axis=0)
        blk = lax.broadcasted_iota(jnp.int32, (L, 1), 0) // S
        kt_blocks = jnp.concatenate([jnp.where(blk == a, kt, jnp.zeros_like(kt)) for a in range(nb)], axis=1)
        kv_all = _dot_tn(v, kt_blocks)
        st = st_s[r, d]
        inter = [None] * nb
        for a in (range(nb - 1, -1, -1) if rev else range(nb)):
            sl = slice(a * S, (a + 1) * S)
            inter[a] = _dot_nt(qt[sl], st.astype(BF16))
            st = st * g[a * S:a * S + 1] + jnp.where(bd, kv_all[:, a * MIX_W:(a + 1) * MIX_W], 0.0)
        st_s[r, d] = st
        o_s[r, d, rows, :] = o + jnp.concatenate(inter, axis=0)

    st_s[...] = jnp.zeros_like(st_s)

    def walk_body(s, _):
        for r in range(ROWS):
            chunk(r, 0, s)
            chunk(r, 1, _bwd_chunk(s, n_ctx_chunks, n_chunks))
        return 0

    lax.fori_loop(0, n_chunks, walk_body, 0)

    def out_body(c, _):
        rows = pl.ds(pl.multiple_of(c * L, L), L)
        for r in range(ROWS):
            gate = _silu(g_ref[r, rows, :].astype(F32))
            o = o_s[r, 0, rows, :] + o_s[r, 1, rows, :]
            out_ref[r, rows, :] = (_head_norm(o, False) * gate).astype(out_ref.dtype)
        return 0

    lax.fori_loop(0, n_chunks, out_body, 0)


def _mixer_d(p, lb, n_ctx):
    b, n, _ = p.shape
    tri, same, fold = _hgrn2_tables()
    kern = functools.partial(_mixer_d_kernel, n=n, n_ctx=n_ctx)
    return pl.pallas_call(
        kern,
        grid=(b // ROWS,),
        in_specs=[_seq_spec(n, 10, ROWS), _seq_spec(n, 11, ROWS), _seq_spec(n, 12, ROWS), _seq_spec(n, 13, ROWS),
                  _seq_spec(n, 14, ROWS), _full_spec(lb.shape), _full_spec(tri.shape), _full_spec(same.shape),
                  _full_spec(fold.shape)],
        out_specs=pl.BlockSpec((ROWS, n, MIX_W), lambda bi: (bi, 0, 0)),
        out_shape=jax.ShapeDtypeStruct((b, n, MIX_W), BF16),
        scratch_shapes=[pltpu.VMEM((ROWS, 2, n, MIX_W), F32), pltpu.VMEM((ROWS, 2, MIX_W, MIX_W), F32)],
        compiler_params=_cparams("parallel"),
        name="mixer_hgrn2",
    )(p, p, p, p, p, lb, tri, same, fold)


def _pack_bf16_pairs(t):
    w = t.shape[1] // 2
    hi = pltpu.bitcast(t[:, :w].astype(BF16).astype(F32), jnp.uint32)
    lo = pltpu.bitcast(t[:, w:].astype(BF16).astype(F32), jnp.uint32)
    return pltpu.bitcast(hi | (lo >> 16), jnp.int32)


def _unpack_bf16_pairs(p):
    u = pltpu.bitcast(p, jnp.uint32)
    hi = pltpu.bitcast(u & jnp.uint32(0xFFFF0000), F32)
    lo = pltpu.bitcast(u << 16, F32)
    return hi, lo


def _merge_kernel(x_ref, m_ref, ya_ref, yb_ref, yc_ref, yd_ref, wm_ref, wb_ref, wo_ref, wr_ref, br_ref,
                  earlier_ref, xo_ref, hp_ref, route_ref, counts_ref, carry_s, *, tm, n_ctx):
    i = pl.program_id(1)

    @pl.when((pl.program_id(0) == 0) & (i == 0))
    def _():
        carry_s[...] = jnp.zeros_like(carry_s)

    row = i * tm + lax.broadcasted_iota(jnp.int32, (tm, 1), 0)
    is_ctx = row < n_ctx
    m = m_ref[0]
    x = x_ref[0]
    d = x.shape[-1]
    h = _modulate(x, _pick(m, is_ctx, 0), _pick(m, is_ctx, 1)).astype(BF16)
    z = jnp.zeros((tm, d), F32)
    for nb, y_ref in enumerate((ya_ref, yb_ref, yc_ref, yd_ref)):
        gate = _sigmoid(jnp.dot(h, wm_ref[:, nb * d:(nb + 1) * d], preferred_element_type=F32))
        z = z + gate * jnp.dot(y_ref[0], wb_ref[nb], preferred_element_type=F32)
    mix = jnp.dot(z.astype(BF16), wo_ref[...], preferred_element_type=F32)
    x1 = x + _pick(m, is_ctx, 2) * mix
    xo_ref[0] = x1
    h2 = _modulate(x1, _pick(m, is_ctx, 3), _pick(m, is_ctx, 4))

    wr = wr_ref[...]
    wr_hi = wr.astype(BF16)
    wr_lo = (wr - wr_hi.astype(F32)).astype(BF16)
    h2_hi = h2.astype(BF16)
    h2_lo = (h2 - h2_hi.astype(F32)).astype(BF16)
    dot = functools.partial(jnp.dot, preferred_element_type=F32)
    logit = dot(h2_hi, wr_hi) + dot(h2_lo, wr_hi) + dot(h2_hi, wr_lo) + br_ref[...]
    lane = lax.broadcasted_iota(jnp.int32, (tm, LANES), 1)
    big = jnp.int32(LANES)
    is_group = (lane >= N_EXPERTS) & (lane < N_EXPERTS + N_GROUPS)
    gl = jnp.where(is_group, logit, NEG_INF)
    g_max = jnp.max(gl, axis=1, keepdims=True)
    g_idx = jnp.min(jnp.where(gl == g_max, lane, big), axis=1, keepdims=True) - N_EXPERTS
    g_prob = 1.0 / jnp.sum(jnp.where(is_group, jnp.exp(logit - g_max), 0.0), axis=1, keepdims=True)
    in_group = (lane < N_EXPERTS) & (lane // EXP_PER_GROUP == g_idx)
    el = jnp.where(in_group, logit, NEG_INF)
    v1 = jnp.max(el, axis=1, keepdims=True)
    i1 = jnp.min(jnp.where(el == v1, lane, big), axis=1, keepdims=True)
    el2 = jnp.where(lane == i1, NEG_INF, el)
    v2 = jnp.max(el2, axis=1, keepdims=True)
    i2 = jnp.min(jnp.where(el2 == v2, lane, big), axis=1, keepdims=True)
    e2 = jnp.exp(v2 - v1)
    w1 = g_prob / (1.0 + e2)
    w2 = g_prob * e2 / (1.0 + e2)
    first_low = i1 < i2
    lo = jnp.where(first_low, i1, i2) - g_idx * EXP_PER_GROUP
    hi = jnp.where(first_low, i2, i1) - g_idx * EXP_PER_GROUP
    pair = 3 * lo - ((lo * (lo - 1)) >> 1) + (hi - lo - 1)
    cls = g_idx * N_PAIRS + pair
    w_lo = jnp.where(first_low, w1, w2)
    w_hi = jnp.where(first_low, w2, w1)
    onehot = jnp.where(lane == cls, 1.0, 0.0)
    before = jnp.dot(earlier_ref[...], onehot.astype(BF16), preferred_element_type=F32) + carry_s[...]
    rank = jnp.sum(onehot * before, axis=1, keepdims=True)
    carry_s[...] += jnp.sum(onehot, axis=0, keepdims=True)
    counts_ref[...] = carry_s[...].astype(jnp.int32)
    cols = jnp.where(lane == 0, cls.astype(F32), jnp.where(lane == 1, rank, 0.0))
    pick = jnp.where(lax.broadcasted_iota(jnp.int32, (8, LANES), 0)
                     == lax.broadcasted_iota(jnp.int32, (8, LANES), 1), 1.0, 0.0).astype(BF16)
    route_ref[0] = sum(_dot_nt(pick, part) for part in _split3(cols)).astype(jnp.int32)
    gate_bits = pltpu.bitcast(jnp.where(lane == 0, w_lo, jnp.where(lane == 1, w_hi, 0.0)), jnp.int32)
    hp_ref[0] = jnp.concatenate([_pack_bf16_pairs(h2), gate_bits], axis=1)


def _merge(xc, mvec, ys, w_merge, w_branch, w_out, w_route, b_route, n_ctx):
    b, n, d = xc.shape
    tm = _token_tile(n, 640)
    kern = functools.partial(_merge_kernel, tm=tm, n_ctx=n_ctx)
    tok = lambda w: pl.BlockSpec((1, tm, w), lambda bi, i: (bi, i, 0))
    const = lambda shape: pl.BlockSpec(shape, lambda bi, i: (0,) * len(shape))
    earlier = jnp.asarray(np.tri(tm, k=-1, dtype=np.float32), dtype=BF16)
    return pl.pallas_call(
        kern,
        grid=(b, n // tm),
        in_specs=[tok(d), pl.BlockSpec((1, MOD_ROWS, d), lambda bi, i: (bi, 0, 0)),
                  tok(MIX_W), tok(MIX_W), tok(MIX_W), tok(MIX_W),
                  const(w_merge.shape), const(w_branch.shape), const(w_out.shape),
                  const(w_route.shape), const(b_route.shape), const(earlier.shape)],
        out_specs=[tok(d), tok(d // 2 + LANES),
                   pl.BlockSpec((1, 8, tm), lambda bi, i: (bi * (n // tm) + i, 0, 0)), const((1, LANES))],
        out_shape=[jax.ShapeDtypeStruct((b, n, d), F32),
                   jax.ShapeDtypeStruct((b, n, d // 2 + LANES), jnp.int32),
                   jax.ShapeDtypeStruct((b * (n // tm), 8, tm), jnp.int32),
                   jax.ShapeDtypeStruct((1, LANES), jnp.int32)],
        scratch_shapes=[pltpu.VMEM((1, LANES), F32)],
        compiler_params=_cparams("arbitrary", "arbitrary"),
        name="merge_route",
    )(xc, mvec, *ys, w_merge, w_branch, w_out, w_route, b_route, earlier)


def _row_move(table, idx, n_out=None):
    scatter = n_out is not None
    n_idx = idx.shape[0]
    width = table.shape[1]
    per_worker = n_idx // SC_WORKERS
    assert per_worker * SC_WORKERS == n_idx and per_worker % 8 == 0
    chunk = max(c for c in range(8, SC_MAX_CHUNK + 1, 8) if per_worker % c == 0)
    n_chunks = per_worker // chunk
    mesh = plsc.VectorSubcoreMesh(core_axis_name="c", subcore_axis_name="s",
                                  num_cores=SC_CORES, num_subcores=SC_SUBCORES)

    @functools.partial(
        pl.kernel, mesh=mesh,
        out_type=jax.ShapeDtypeStruct((n_out if scatter else n_idx, width), table.dtype),
        scratch_types=[pltpu.VMEM((chunk,), jnp.int32),
                       pltpu.VMEM((chunk, width), table.dtype),
                       pltpu.SemaphoreType.DMA],
        name="sc_row_scatter" if scatter else "sc_row_gather",
    )
    def move(table_hbm, idx_hbm, out_hbm, idx_v, rows_v, sem):
        worker = lax.axis_index("s") * SC_CORES + lax.axis_index("c")
        base = worker * per_worker

        @pl.loop(0, n_chunks)
        def _(j):
            off = pl.multiple_of(base + j * chunk, 8)
            pltpu.sync_copy(idx_hbm.at[pl.ds(off, chunk)], idx_v)
            if scatter:
                pltpu.sync_copy(table_hbm.at[pl.ds(off, chunk)], rows_v)
                pltpu.async_copy(rows_v, out_hbm.at[idx_v], sem).wait()
            else:
                pltpu.async_copy(table_hbm.at[idx_v], rows_v, sem).wait()
                pltpu.sync_copy(rows_v, out_hbm.at[pl.ds(off, chunk)])

    return move(table, idx)


def _ffn(h, w1_ref, w3_ref, w2_ref):
    a = jnp.dot(h, w1_ref[0, 0], preferred_element_type=F32)
    g = jnp.dot(h, w3_ref[0, 0], preferred_element_type=F32)
    return jnp.dot((_silu(a) * g).astype(BF16), w2_ref[0, 0], preferred_element_type=F32)


def _moe_kernel(elo_ref, ehi_ref, valid_ref, xs_ref, w1a_ref, w3a_ref, w2a_ref, w1b_ref, w3b_ref, w2b_ref,
                ys_ref):
    j = pl.program_id(0)
    half = ys_ref.shape[1]

    @pl.when(valid_ref[j] != 0)
    def _():
        hi, lo = _unpack_bf16_pairs(xs_ref[:, :half])
        h = jnp.concatenate([hi.astype(BF16), lo.astype(BF16)], axis=1)
        gates = pltpu.bitcast(xs_ref[:, half:], F32)
        y = (gates[:, 0:1] * _ffn(h, w1a_ref, w3a_ref, w2a_ref)
             + gates[:, 1:2] * _ffn(h, w1b_ref, w3b_ref, w2b_ref))
        ys_ref[...] = _pack_bf16_pairs(y)

    @pl.when(valid_ref[j] == 0)
    def _():
        ys_ref[...] = jnp.zeros_like(ys_ref)


def _moe_experts(xs, tile_elo, tile_ehi, tile_valid, w1, w3, w2, layer):
    n_slots, width = xs.shape
    d = w1.shape[2]
    n_tiles = n_slots // MOE_TILE
    wspec = lambda shape, which: pl.BlockSpec(
        (1, 1) + shape, lambda j, elo, ehi, valid: (layer, (elo, ehi)[which][j], 0, 0))
    grid_spec = pltpu.PrefetchScalarGridSpec(
        num_scalar_prefetch=3,
        grid=(n_tiles,),
        in_specs=[pl.BlockSpec((MOE_TILE, width), lambda j, elo, ehi, valid: (j, 0)),
                  wspec((d, D_EXPERT), 0), wspec((d, D_EXPERT), 0), wspec((D_EXPERT, d), 0),
                  wspec((d, D_EXPERT), 1), wspec((d, D_EXPERT), 1), wspec((D_EXPERT, d), 1)],
        out_specs=pl.BlockSpec((MOE_TILE, d // 2), lambda j, elo, ehi, valid: (j, 0)),
    )
    return pl.pallas_call(
        _moe_kernel,
        grid_spec=grid_spec,
        out_shape=jax.ShapeDtypeStruct((n_slots, d // 2), jnp.int32),
        compiler_params=_cparams("arbitrary"),
        name="moe_experts",
    )(tile_elo, tile_ehi, tile_valid, xs, w1, w3, w2, w1, w3, w2)


def _moe_combine_kernel(x_ref, m_ref, y_ref, *rest, tm, n_ctx, row0):
    o_ref = rest[-1]
    row = row0 + pl.program_id(1) * tm + lax.broadcasted_iota(jnp.int32, (tm, 1), 0)
    hi, lo = _unpack_bf16_pairs(y_ref[0])
    x = x_ref[0] + _pick(m_ref[0], row < n_ctx, 5) * jnp.concatenate([hi, lo], axis=1)
    if len(rest) == 2:
        ms = jnp.mean(x * x, axis=-1, keepdims=True)
        x = x * lax.rsqrt(ms + EPS) * rest[0][...]
    o_ref[0] = x


def _moe_combine(x1, mvec, yt, n_ctx, final_w=None):
    b, n, d = x1.shape
    if final_w is None:
        tm, skip, n_out, extra, extra_specs = _token_tile(n, 1100), 0, n, (), []
    else:
        tm = _token_tile(n_ctx, 1024)
        skip, n_out = n_ctx // tm, n - n_ctx
        extra, extra_specs = (final_w.reshape(1, d),), [pl.BlockSpec((1, d), lambda bi, i: (0, 0))]
    kern = functools.partial(_moe_combine_kernel, tm=tm, n_ctx=n_ctx, row0=skip * tm)
    tok = lambda w: pl.BlockSpec((1, tm, w), lambda bi, i: (bi, i + skip, 0))
    return pl.pallas_call(
        kern,
        grid=(b, n_out // tm),
        in_specs=[tok(d), pl.BlockSpec((1, MOD_ROWS, d), lambda bi, i: (bi, 0, 0)), tok(d // 2)] + extra_specs,
        out_specs=pl.BlockSpec((1, tm, d), lambda bi, i: (bi, i, 0)),
        out_shape=jax.ShapeDtypeStruct((b, n_out, d), F32),
        compiler_params=_cparams("parallel", "parallel"),
        name="moe_combine",
    )(x1, mvec, yt, *extra)


def _moe(x1, mvec, hp, route, counts, w1, w3, w2, layer, n_ctx, final_w=None):
    b, n, d = x1.shape
    m = b * n
    n_slots = m + N_CLASSES * MOE_TILE
    cls = route[:, 0, :].reshape(m)
    rank = route[:, 1, :].reshape(m)
    cnt = counts[0, :N_CLASSES]
    padded = (cnt + MOE_TILE - 1) // MOE_TILE * MOE_TILE
    ends = jnp.cumsum(padded)
    pos = (ends - padded)[cls] + rank
    tile_start = jnp.arange(n_slots // MOE_TILE, dtype=jnp.int32) * MOE_TILE
    tile_cls = jnp.minimum(jnp.searchsorted(ends, tile_start, side="right"), N_CLASSES - 1).astype(jnp.int32)
    tile_valid = (tile_start < ends[-1]).astype(jnp.int32)
    pair_lo = jnp.asarray([0, 0, 0, 1, 1, 2], jnp.int32)
    pair_hi = jnp.asarray([1, 2, 3, 2, 3, 3], jnp.int32)
    group = tile_cls // N_PAIRS
    tile_elo = group * EXP_PER_GROUP + pair_lo[tile_cls % N_PAIRS]
    tile_ehi = group * EXP_PER_GROUP + pair_hi[tile_cls % N_PAIRS]

    xs = _row_move(hp.reshape(m, hp.shape[-1]), pos, n_out=n_slots)
    ys = _moe_experts(xs, tile_elo, tile_ehi, tile_valid, w1, w3, w2, layer)
    yt = _row_move(ys, pos)
    return _moe_combine(x1, mvec, yt.reshape(b, n, d // 2), n_ctx, final_w)


def _rope_tables(n_lat, n_ctx):
    rows = n_lat // GRID_W
    row = jnp.repeat(jnp.arange(rows), GRID_W).astype(F32)
    col = jnp.tile(jnp.arange(GRID_W), rows).astype(F32)
    nq = HEAD_DIM // 4
    inv = jnp.power(ROPE_BASE, -jnp.arange(nq, dtype=F32) / nq)
    ang = jnp.concatenate([row[:, None] * inv, col[:, None] * inv], -1)
    cos = jnp.cos(ang)
    sin = jnp.sin(ang)
    cos_h = jnp.concatenate([cos, cos], -1)
    sin_h = jnp.concatenate([-sin, sin], -1)
    cos_full = jnp.concatenate([jnp.ones((n_ctx, HEAD_DIM), F32), cos_h], 0)
    sin_full = jnp.concatenate([jnp.zeros((n_ctx, HEAD_DIM), F32), sin_h], 0)
    return jnp.tile(cos_full, (1, N_HEADS)), jnp.tile(sin_full, (1, N_HEADS))


def _block_diag_heads(w):
    eye = jnp.eye(N_HEADS, dtype=w.dtype)
    return jnp.einsum("hij,hg->higj", w, eye).reshape(MIX_W, MIX_W)


def kernel(x, c, ctx, c_ctx, w_mod, b_mod, w_in, a_conv_w, a_conv_b, a_gate_w, a_gate_b, a_lambda, b_theta,
           c_conv_w, c_conv_b, c_gate_b, d_lb, w_branch, w_out, moe_w_group, moe_b_group, moe_w_router,
           moe_b_router, moe_w1, moe_w3, moe_w2, final_norm_w):
    bsz, n_lat, d = x.shape
    n_ctx = ctx.shape[1]
    depth = w_mod.shape[0]
    assert n_ctx % CHUNK == 0 and n_lat % CHUNK == 0 and n_ctx % 256 == 0 and n_lat % 256 == 0

    xc = jnp.concatenate([ctx, x], axis=1)
    cos, sin = _rope_tables(n_lat, n_ctx)

    cc = jnp.zeros((8, d), F32).at[:bsz].set(c).at[bsz].set(c_ctx)
    mod = _mod_vectors(cc, w_mod, b_mod)
    mx = mod[:, :bsz].reshape(depth, bsz, N_MOD, d)
    mc = jnp.broadcast_to(mod[:, bsz].reshape(depth, 1, N_MOD, d), (depth, bsz, N_MOD, d))
    pad = jnp.zeros((depth, bsz, 8 - N_MOD, d), F32)
    mvec = jnp.concatenate([mx, pad, mc, pad], axis=2)

    lbs = jnp.cumsum(jax.nn.softmax(d_lb.astype(F32), axis=0), axis=0)
    lbs = lbs - lbs[0]

    n_mix_cols = 15 * MIX_W
    gate0 = 10 * MIX_W
    n_gate = 4 * N_HEADS
    for l in range(depth):
        wl = w_in[l]
        w_mix = jnp.concatenate([wl[:, :gate0], wl[:, gate0 + n_gate:n_mix_cols + n_gate]], axis=1).astype(BF16)
        w_gate = jnp.pad(wl[:, gate0:gate0 + n_gate], ((0, 0), (0, LANES - n_gate))).astype(BF16)
        w_merge = wl[:, n_mix_cols + n_gate:].astype(BF16)
        p, gates_c = _in_proj(xc, mvec[l], w_mix, w_gate, cos, sin, n_ctx)

        gw = jnp.concatenate([_block_diag_heads(a_gate_w[l, dd, j]) for dd in range(2) for j in range(2)],
                             axis=1).astype(BF16)
        gb = a_gate_b[l].reshape(1, 4 * MIX_W)
        ya = _mixer_a(p, a_conv_w[l], a_conv_b[l].reshape(1, MIX_W), gw, gb, a_lambda[l], n_ctx)

        log_gamma = jnp.repeat(jax.nn.log_sigmoid(b_theta[l].astype(F32)), HEAD_DIM, axis=1)
        yb = _mixer_b(p, log_gamma, n_ctx)

        gate_b = jnp.pad(c_gate_b[l].reshape(1, n_gate), ((0, 0), (0, LANES - n_gate)))
        yc = _mixer_c(p, gates_c, c_conv_w[l], c_conv_b[l].reshape(1, 2 * MIX_W), gate_b, n_ctx)

        yd = _mixer_d(p, lbs[l].reshape(1, MIX_W), n_ctx)

        w_route = jnp.pad(jnp.concatenate([moe_w_router[l], moe_w_group[l]], axis=1),
                          ((0, 0), (0, LANES - N_EXPERTS - N_GROUPS)))
        b_route = jnp.pad(jnp.concatenate([moe_b_router[l], moe_b_group[l]]),
                          (0, LANES - N_EXPERTS - N_GROUPS)).reshape(1, LANES)
        x1, hp, route, counts = _merge(xc, mvec[l], (ya, yb, yc, yd), w_merge, w_branch[l].astype(BF16),
                                       w_out[l].astype(BF16), w_route, b_route, n_ctx)
        w1b, w3b, w2b = (w[l:l + 1].astype(BF16) for w in (moe_w1, moe_w3, moe_w2))
        xc = _moe(x1, mvec[l], hp, route, counts, w1b, w3b, w2b, 0, n_ctx,
                  final_w=final_norm_w if l == depth - 1 else None)

    return xc
```

```python
import functools

import jax
import jax.numpy as jnp
import numpy as np
from jax import lax
from jax.experimental import pallas as pl
from jax.experimental.pallas import tpu as pltpu
from jax.experimental.pallas import tpu_sc as plsc

F32 = jnp.float32
BF16 = jnp.bfloat16

EPS = 1e-6
N_HEADS = 4
HEAD_DIM = 64
MIX_W = N_HEADS * HEAD_DIM
N_BRANCH = 4
CONV_W = 4
LRU_C = 8.0
GRID_W = 64
ROPE_BASE = 10000.0
N_GROUPS = 4
EXP_PER_GROUP = 4
N_EXPERTS = N_GROUPS * EXP_PER_GROUP
D_EXPERT = 512
N_MOD = 6
M_INIT = -1e30

CHUNK = 128
RET_CHUNK = 256
CONV_STEP = 128
OUT_ROWS = 256
SUB = 16
HGRN_CHUNK = 256
PAIR_GROUP = 4
HALO = 16
LANES = 128
MOD_ROWS = 16
ROWS = 1
VMEM_LIMIT_BYTES = 56 * 1024 * 1024
N_PAIRS = 6
N_CLASSES = N_GROUPS * N_PAIRS
MOE_TILE = 256
SC_CORES = 2
SC_SUBCORES = 16
SC_WORKERS = SC_CORES * SC_SUBCORES
SC_MAX_CHUNK = 32
NEG_INF = float("-inf")
LOG2_E = 1.4426950408889634


def _cparams(*sem):
    return pltpu.CompilerParams(dimension_semantics=sem, vmem_limit_bytes=VMEM_LIMIT_BYTES)


def _token_tile(n, cap):
    best = 16
    for t in range(16, cap + 1, 16):
        if n % t == 0:
            best = t
    return best


def _modulate(x, shift, scale):
    ms = jnp.mean(x * x, axis=-1, keepdims=True)
    return x * lax.rsqrt(ms + EPS) * (1.0 + scale) + shift


def _pick(m, is_ctx, k):
    return jnp.where(is_ctx, m[8 + k:9 + k], m[k:k + 1])


def _sigmoid(x):
    return 0.5 * jnp.tanh(0.5 * x) + 0.5


def _silu(x):
    return x * _sigmoid(x)


def _log_sigmoid(x):
    return jnp.minimum(x, 0.0) - jnp.log(1.0 + jnp.exp(-jnp.abs(x)))


def _split3(x):
    hi = x.astype(BF16)
    r = x - hi.astype(F32)
    mid = r.astype(BF16)
    lo = (r - mid.astype(F32)).astype(BF16)
    return hi, mid, lo


def _split2(x):
    hi = x.astype(BF16)
    return hi, (x - hi.astype(F32)).astype(BF16)


def _sel_dot(sel, x):
    hi, lo = _split2(x)
    d = functools.partial(jnp.dot, preferred_element_type=F32)
    return d(sel, hi) + d(sel, lo)


def _dot_sel(x, sel):
    hi, lo = _split2(x)
    d = functools.partial(jnp.dot, preferred_element_type=F32)
    return d(hi, sel) + d(lo, sel)


def _dot_nt(a, b):
    return lax.dot_general(a, b, (((1,), (1,)), ((), ())), preferred_element_type=F32)


def _dot_tn(a, b):
    return lax.dot_general(a, b, (((0,), (0,)), ((), ())), preferred_element_type=F32)


def _head_of_lane(width):
    return lax.broadcasted_iota(jnp.int32, (1, width), 1) // HEAD_DIM


def _block_ones(width):
    r = lax.broadcasted_iota(jnp.int32, (width, width), 0) // HEAD_DIM
    c = lax.broadcasted_iota(jnp.int32, (width, width), 1) // HEAD_DIM
    return r == c


def _head_norm(o, center):
    ones = jnp.where(_block_ones(MIX_W), 1.0, 0.0).astype(BF16)
    inv = 1.0 / HEAD_DIM
    if center:
        o = o - _dot_sel(o, ones) * inv
    var = _dot_sel(o * o, ones) * inv
    return o * lax.rsqrt(var + EPS)


def _bwd_chunk(s, n_ctx_chunks, n_chunks):
    return jnp.where(s < n_ctx_chunks, n_ctx_chunks - 1 - s, n_chunks - 1 + n_ctx_chunks - s)


def _conv_chunk(c, operands, *, n, n_ctx):
    L = CONV_STEP
    wl = L + 2 * HALO
    r0 = c * L
    start = pl.multiple_of(jnp.clip(r0 - HALO, 0, n - wl), HALO)
    off = r0 - start
    shift = (lax.broadcasted_iota(jnp.int32, (L, wl), 1) - lax.broadcasted_iota(jnp.int32, (L, wl), 0)) - off
    g = r0 + lax.broadcasted_iota(jnp.int32, (L, 1), 0)
    seg_g = jnp.where(g < n_ctx, 0, 1)
    sels = []
    for k in range(CONV_W):
        src = g + (k - 2)
        row_ok = (jnp.where(src < n_ctx, 0, 1) == seg_g) & (src >= 0) & (src < n)
        sels.append(jnp.where((shift == k - 2) & row_ok, 1.0, 0.0).astype(BF16))
    outs = []
    for ref, w, b in operands:
        win = ref[0, pl.ds(start, wl), :]
        y = b
        for k in range(CONV_W):
            y = y + w[k:k + 1] * jnp.dot(sels[k], win, preferred_element_type=F32)
        outs.append(y)
    return outs


def _mod_kernel(cc_ref, w_ref, b_ref, o_ref):
    s = _silu(cc_ref[...])
    o_ref[0] = jnp.dot(s, w_ref[0], precision=lax.Precision.HIGHEST, preferred_element_type=F32) + b_ref[0]


def _mod_vectors(cc, w_mod, b_mod):
    depth, d, dm = w_mod.shape
    tn = 1536
    return pl.pallas_call(
        _mod_kernel,
        grid=(depth, dm // tn),
        in_specs=[pl.BlockSpec((8, d), lambda l, j: (0, 0)),
                  pl.BlockSpec((1, d, tn), lambda l, j: (l, 0, j)),
                  pl.BlockSpec((1, 1, tn), lambda l, j: (l, 0, j))],
        out_specs=pl.BlockSpec((1, 8, tn), lambda l, j: (l, 0, j)),
        out_shape=jax.ShapeDtypeStruct((depth, 8, dm), F32),
        compiler_params=_cparams("parallel", "parallel"),
        name="mod_vectors",
    )(cc, w_mod, b_mod.reshape(depth, 1, dm))


def _rope(t, cos, sin):
    lane = lax.broadcasted_iota(jnp.int32, (1, MIX_W), 1) % HEAD_DIM
    half = HEAD_DIM // 2
    swapped = jnp.where(lane < half, pltpu.roll(t, MIX_W - half, 1), pltpu.roll(t, half, 1))
    return t * cos + swapped * sin


RET_Q_COL = 2 * MIX_W
RET_K_COL = 3 * MIX_W


def _in_proj_kernel(x_ref, m_ref, w_ref, wg_ref, cos_ref, sin_ref, p_ref, g_ref, *, tm, n_ctx, col_chunk):
    i = pl.program_id(1)
    row = i * tm + lax.broadcasted_iota(jnp.int32, (tm, 1), 0)
    is_ctx = row < n_ctx
    m = m_ref[0]
    h = _modulate(x_ref[0], _pick(m, is_ctx, 0), _pick(m, is_ctx, 1)).astype(BF16)
    for j in range(w_ref.shape[1] // col_chunk):
        sl = slice(j * col_chunk, (j + 1) * col_chunk)
        t = jnp.dot(h, w_ref[:, sl], preferred_element_type=F32)
        if j * col_chunk == RET_Q_COL:
            t = _rope(t, cos_ref[...], sin_ref[...])
        elif j * col_chunk == RET_K_COL:
            t = _rope(t, cos_ref[...], sin_ref[...]) * HEAD_DIM ** -0.5
        p_ref[0, :, sl] = t.astype(BF16)
    g_ref[0] = jnp.dot(h, wg_ref[...], preferred_element_type=F32)


def _in_proj(xc, mvec, w_mix, w_gate, cos, sin, n_ctx):
    b, n, d = xc.shape
    tm = _token_tile(n, 640)
    wc = w_mix.shape[1]
    kern = functools.partial(_in_proj_kernel, tm=tm, n_ctx=n_ctx, col_chunk=MIX_W)
    return pl.pallas_call(
        kern,
        grid=(b, n // tm),
        in_specs=[pl.BlockSpec((1, tm, d), lambda bi, i: (bi, i, 0)),
                  pl.BlockSpec((1, MOD_ROWS, d), lambda bi, i: (bi, 0, 0)),
                  pl.BlockSpec((d, wc), lambda bi, i: (0, 0)),
                  pl.BlockSpec((d, LANES), lambda bi, i: (0, 0)),
                  pl.BlockSpec((tm, MIX_W), lambda bi, i: (i, 0)),
                  pl.BlockSpec((tm, MIX_W), lambda bi, i: (i, 0))],
        out_specs=[pl.BlockSpec((1, tm, wc), lambda bi, i: (bi, i, 0)),
                   pl.BlockSpec((1, tm, LANES), lambda bi, i: (bi, i, 0))],
        out_shape=[jax.ShapeDtypeStruct((b, n, wc), BF16),
                   jax.ShapeDtypeStruct((b, n, LANES), F32)],
        compiler_params=_cparams("parallel", "parallel"),
        name="in_proj",
    )(xc, mvec, w_mix, w_gate, cos, sin)


def _lin_scan(a, x, carry, reverse):
    L = a.shape[0]
    row = lax.broadcasted_iota(jnp.int32, (L, 1), 0)
    k = 1
    while k < L:
        if reverse:
            a_s = pltpu.roll(a, L - k, 0)
            x_s = pltpu.roll(x, L - k, 0)
            valid = row < L - k
        else:
            a_s = pltpu.roll(a, k, 0)
            x_s = pltpu.roll(x, k, 0)
            valid = row >= k
        x = jnp.where(valid, a * x_s + x, x)
        a = jnp.where(valid, a * a_s, a)
        k *= 2
    h = x + a * carry
    return h, (h[0:1] if reverse else h[L - 1:L])


def _mixer_a_kernel(ax_ref, ag_ref, cw_ref, cb_ref, gw_ref, gb_ref, lam_ref, out_ref, u_s, hf_s,
                    *, n, n_ctx):
    L = CHUNK
    n_chunks = n // L
    n_ctx_chunks = n_ctx // L
    cw = cw_ref[...]
    cb = cb_ref[...]
    log_lam = _log_sigmoid(lam_ref[...])

    def direction(d, c, carry):
        r0 = pl.multiple_of(c * L, L)
        u = u_s[pl.ds(r0, L), :]
        pre = jnp.dot(u.astype(BF16), gw_ref[:, d * 2 * MIX_W:(d + 1) * 2 * MIX_W],
                      preferred_element_type=F32) + gb_ref[:, d * 2 * MIX_W:(d + 1) * 2 * MIX_W]
        r = _sigmoid(pre[:, :MIX_W])
        gi = _sigmoid(pre[:, MIX_W:])
        log_a = LRU_C * r * log_lam[d:d + 1]
        a = jnp.exp(log_a)
        inp = jnp.sqrt(1.0 - jnp.exp(2.0 * log_a)) * (gi * u)
        h, new_carry = _lin_scan(a, inp, carry, reverse=(d == 1))
        return r0, h, new_carry

    def conv_body(c, _):
        r0 = pl.multiple_of(c * CONV_STEP, CONV_STEP)
        u_s[pl.ds(r0, CONV_STEP), :] = _conv_chunk(c, [(ax_ref, cw, cb)], n=n, n_ctx=n_ctx)[0]
        return 0

    lax.fori_loop(0, n // CONV_STEP, conv_body, 0)

    def walk_body(s, carries):
        r0, h, carry_f = direction(0, s, carries[0])
        hf_s[0, pl.ds(r0, L), :] = h
        r0, h, carry_b = direction(1, _bwd_chunk(s, n_ctx_chunks, n_chunks), carries[1])
        hf_s[1, pl.ds(r0, L), :] = h
        return carry_f, carry_b

    zero = jnp.zeros((1, MIX_W), F32)
    lax.fori_loop(0, n_chunks, walk_body, (zero, zero))

    def out_body(c, _):
        rows = pl.ds(pl.multiple_of(c * L, L), L)
        gate = jax.nn.gelu(ag_ref[0, rows, :].astype(F32), approximate=True)
        out_ref[0, rows, :] = (gate * (hf_s[0, rows, :] + hf_s[1, rows, :])).astype(out_ref.dtype)
        return 0

    lax.fori_loop(0, n_chunks, out_body, 0)


def _seq_spec(n, col, rows=1):
    if rows == 1:
        return pl.BlockSpec((1, n, MIX_W), lambda b: (b, 0, col))
    return pl.BlockSpec((rows, n, MIX_W), lambda b: (b, 0, col), pipeline_mode=pl.Buffered(1))


def _full_spec(shape):
    return pl.BlockSpec(shape, lambda b: (0,) * len(shape))


def _mixer_a(p, conv_w, conv_b, gate_w, gate_b, lam, n_ctx):
    b, n, _ = p.shape
    kern = functools.partial(_mixer_a_kernel, n=n, n_ctx=n_ctx)
    return pl.pallas_call(
        kern,
        grid=(b,),
        in_specs=[_seq_spec(n, 0), _seq_spec(n, 1),
                  _full_spec(conv_w.shape), _full_spec(conv_b.shape),
                  _full_spec(gate_w.shape), _full_spec(gate_b.shape), _full_spec(lam.shape)],
        out_specs=pl.BlockSpec((1, n, MIX_W), lambda bi: (bi, 0, 0)),
        out_shape=jax.ShapeDtypeStruct((b, n, MIX_W), BF16),
        scratch_shapes=[pltpu.VMEM((n, MIX_W), F32), pltpu.VMEM((2, n, MIX_W), F32)],
        compiler_params=_cparams("parallel"),
        name="mixer_rglru",
    )(p, p, conv_w, conv_b, gate_w, gate_b, lam)


def _mixer_b_kernel(q_ref, k_ref, v_ref, g_ref, lg_ref, out_ref, oi_s, dec_s, st_s, *, n, n_ctx):
    L = RET_CHUNK
    n_chunks = n // L
    n_ctx_chunks = n_ctx // L
    head = _head_of_lane(MIX_W)
    bd = _block_ones(MIX_W)
    lgf = lg_ref[0:1, :]
    lgb = lg_ref[1:2, :]
    pos = lax.broadcasted_iota(jnp.int32, (L, 1), 0).astype(F32)
    diff = (lax.broadcasted_iota(jnp.int32, (L, L), 0) - lax.broadcasted_iota(jnp.int32, (L, L), 1)).astype(F32)
    for h in range(N_HEADS):
        lf = lgf[:, h * HEAD_DIM:h * HEAD_DIM + 1]
        lb = lgb[:, h * HEAD_DIM:h * HEAD_DIM + 1]
        dec_s[h] = jnp.where(diff >= 0, jnp.exp(lf * jnp.maximum(diff, 0.0)), jnp.exp(lb * jnp.maximum(-diff, 0.0)))

    qw = (jnp.exp(lgf * (pos + 1.0)), jnp.exp(lgb * (float(L) - pos)))
    kw = (jnp.exp(lgf * (float(L) - 1.0 - pos)), jnp.exp(lgb * pos))
    g_chunk = (jnp.exp(lgf * float(L)), jnp.exp(lgb * float(L)))

    def state_step(d, c):
        rows = pl.ds(pl.multiple_of(c * L, L), L)
        q = q_ref[0, rows, :].astype(F32)
        k = k_ref[0, rows, :].astype(F32)
        st = st_s[d]
        oi_s[d, rows, :] = jnp.dot((q * qw[d]).astype(BF16), st.astype(BF16), preferred_element_type=F32)
        kv = _dot_tn((k * kw[d]).astype(BF16), v_ref[0, rows, :])
        st_s[d] = g_chunk[d] * st + jnp.where(bd, kv, 0.0)

    st_s[...] = jnp.zeros_like(st_s)

    def walk_body(s, _):
        state_step(0, s)
        state_step(1, _bwd_chunk(s, n_ctx_chunks, n_chunks))
        return 0

    lax.fori_loop(0, n_chunks, walk_body, 0)

    def out_body(c, _):
        rows = pl.ds(pl.multiple_of(c * L, L), L)
        q = q_ref[0, rows, :]
        k = k_ref[0, rows, :]
        v = v_ref[0, rows, :]
        scs, vals = [], []
        for h in range(N_HEADS):
            hm = head == h
            scs.append((_dot_nt(jnp.where(hm, q, jnp.zeros_like(q)), k) * dec_s[h]).astype(BF16))
            vals.append(jnp.where(hm, v, jnp.zeros_like(v)))
        o = (oi_s[0, rows, :] + oi_s[1, rows, :]
             + jnp.dot(jnp.concatenate(scs, axis=1), jnp.concatenate(vals, axis=0), preferred_element_type=F32))
        gate = _silu(g_ref[0, rows, :].astype(F32))
        out_ref[0, rows, :] = (_head_norm(o, True) * gate).astype(out_ref.dtype)
        return 0

    lax.fori_loop(0, n_chunks, out_body, 0, unroll=2)


def _mixer_b(p, log_gamma, n_ctx):
    b, n, _ = p.shape
    kern = functools.partial(_mixer_b_kernel, n=n, n_ctx=n_ctx)
    return pl.pallas_call(
        kern,
        grid=(b,),
        in_specs=[_seq_spec(n, 2), _seq_spec(n, 3), _seq_spec(n, 4), _seq_spec(n, 5),
                  _full_spec(log_gamma.shape)],
        out_specs=pl.BlockSpec((1, n, MIX_W), lambda bi: (bi, 0, 0)),
        out_shape=jax.ShapeDtypeStruct((b, n, MIX_W), BF16),
        scratch_shapes=[pltpu.VMEM((2, n, MIX_W), F32),
                        pltpu.VMEM((N_HEADS, RET_CHUNK, RET_CHUNK), F32),
                        pltpu.VMEM((2, MIX_W, MIX_W), F32)],
        compiler_params=_cparams("parallel"),
        name="mixer_retention",
    )(p, p, p, p, log_gamma)


def _mixer_c_kernel(q_ref, k_ref, v_ref, o_ref, g_ref, cw_ref, cb_ref, gb_ref, out_ref,
                    qs_s, ks_s, hf_s, c_s, m_s, *, n, n_ctx):
    L = CHUNK
    n_chunks = n // L
    n_ctx_chunks = n_ctx // L
    head = _head_of_lane(MIX_W)
    bd = _block_ones(MIX_W)
    bd2 = jnp.concatenate([bd, bd], axis=1)
    cw = cw_ref[...]
    cb = cb_ref[...]
    ri = lax.broadcasted_iota(jnp.int32, (L, L), 0)
    ci = lax.broadcasted_iota(jnp.int32, (L, L), 1)
    gl = lax.broadcasted_iota(jnp.int32, (LANES, MIX_W), 0)
    hl = lax.broadcasted_iota(jnp.int32, (LANES, MIX_W), 1) // HEAD_DIM

    def expand(kind):
        return jnp.where(gl == kind * N_HEADS + hl, 1.0, 0.0).astype(BF16)

    def conv_body(c, _):
        rows = pl.ds(pl.multiple_of(c * CONV_STEP, CONV_STEP), CONV_STEP)
        cq, ck = _conv_chunk(c, [(q_ref, cw[:, :MIX_W], cb[:, :MIX_W]), (k_ref, cw[:, MIX_W:], cb[:, MIX_W:])],
                             n=n, n_ctx=n_ctx)
        qs_s[rows, :] = _silu(cq).astype(BF16)
        ks_s[rows, :] = (_silu(ck) * HEAD_DIM ** -0.5).astype(BF16)
        return 0

    lax.fori_loop(0, n // CONV_STEP, conv_body, 0)

    def chunk(d, c):
        rev = d == 1
        rows = pl.ds(pl.multiple_of(c * L, L), L)
        q = qs_s[rows, :]
        k = ks_s[rows, :]
        v = v_ref[0, rows, :]
        g = g_ref[0, rows, :] + gb_ref[...]
        log_i = _dot_sel(g, expand(2 * d))
        log_f = _dot_sel(_log_sigmoid(g), expand(2 * d + 1))
        causal = (ci >= ri) if rev else (ci <= ri)
        tri = jnp.where(causal, 1.0, 0.0).astype(BF16)
        cum = _sel_dot(tri, log_f)
        cum_end = cum[0:1] if rev else cum[L - 1:L]
        m_prev = m_s[d]
        c_prev = c_s[d]
        row_src = (log_i - cum).T
        m_inter = cum + m_prev
        inter = jnp.dot(q, c_prev.astype(BF16), preferred_element_type=F32)
        num_inter = inter[:, :MIX_W]
        qn = inter[:, MIX_W:]
        m_q = jnp.zeros((L, MIX_W), F32)
        scs, rhs = [], []
        for h in range(N_HEADS):
            hm = head == h
            lane0 = h * HEAD_DIM
            log_d = jnp.where(causal, cum[:, lane0:lane0 + 1] + row_src[lane0:lane0 + 1, :], NEG_INF)
            m_h = jnp.maximum(jnp.max(log_d, axis=1, keepdims=True), m_inter[:, lane0:lane0 + 1])
            sc = _dot_nt(jnp.where(hm, q, jnp.zeros_like(q)), k) * jnp.exp(log_d - m_h)
            scs.append(sc.astype(BF16))
            rhs.append(jnp.concatenate([jnp.where(hm, v, jnp.zeros_like(v)),
                                        jnp.broadcast_to(jnp.where(hm, 1.0, 0.0).astype(BF16), (L, MIX_W))], axis=1))
            m_q = jnp.where(hm, m_h, m_q)
        intra = jnp.dot(jnp.concatenate(scs, axis=1), jnp.concatenate(rhs, axis=0), preferred_element_type=F32)
        s_inter = jnp.exp(m_inter - m_q)
        num = intra[:, :MIX_W] + s_inter * num_inter
        den = intra[:, MIX_W:] + s_inter * qn
        hid = num / jnp.maximum(jnp.abs(den), jnp.exp(-m_q))
        log_w = cum_end - cum + log_i
        m_loc = jnp.max(log_w, axis=0, keepdims=True)
        kw = k.astype(F32) * jnp.exp(log_w - m_loc)
        c_loc = jnp.where(bd2, _dot_tn(kw.astype(BF16), jnp.concatenate([v, jnp.ones_like(v)], axis=1)), 0.0)
        m_new = jnp.maximum(cum_end + m_prev, m_loc)
        s_old = jnp.exp(cum_end + m_prev - m_new)
        s_loc = jnp.exp(m_loc - m_new)
        twice = lambda t: jnp.concatenate([t, t], axis=1)
        c_s[d] = twice(s_old) * c_prev + twice(s_loc) * c_loc
        m_s[d] = m_new
        return rows, hid

    c_s[...] = jnp.zeros_like(c_s)
    m_s[...] = jnp.full(m_s.shape, M_INIT, F32)

    def walk_body(s, _):
        rows, hid = chunk(0, s)
        hf_s[0, rows, :] = hid
        rows, hid = chunk(1, _bwd_chunk(s, n_ctx_chunks, n_chunks))
        hf_s[1, rows, :] = hid
        return 0

    lax.fori_loop(0, n_chunks, walk_body, 0)

    def out_body(c, _):
        rows = pl.ds(pl.multiple_of(c * OUT_ROWS, OUT_ROWS), OUT_ROWS)
        gate = _sigmoid(o_ref[0, rows, :].astype(F32))
        out_ref[0, rows, :] = (gate * _head_norm(hf_s[0, rows, :] + hf_s[1, rows, :], True)).astype(out_ref.dtype)
        return 0

    lax.fori_loop(0, n // OUT_ROWS, out_body, 0)


def _mixer_c(p, gates, conv_w, conv_b, gate_b, n_ctx):
    b, n, _ = p.shape
    kern = functools.partial(_mixer_c_kernel, n=n, n_ctx=n_ctx)
    return pl.pallas_call(
        kern,
        grid=(b,),
        in_specs=[_seq_spec(n, 6), _seq_spec(n, 7), _seq_spec(n, 8), _seq_spec(n, 9),
                  pl.BlockSpec((1, n, LANES), lambda bi: (bi, 0, 0)),
                  _full_spec(conv_w.shape), _full_spec(conv_b.shape), _full_spec(gate_b.shape)],
        out_specs=pl.BlockSpec((1, n, MIX_W), lambda bi: (bi, 0, 0)),
        out_shape=jax.ShapeDtypeStruct((b, n, MIX_W), BF16),
        scratch_shapes=[pltpu.VMEM((n, MIX_W), BF16), pltpu.VMEM((n, MIX_W), BF16),
                        pltpu.VMEM((2, n, MIX_W), F32),
                        pltpu.VMEM((2, MIX_W, 2 * MIX_W), F32),
                        pltpu.VMEM((2, 1, MIX_W), F32)],
        compiler_params=_cparams("parallel"),
        name="mixer_mlstm",
    )(p, p, p, p, gates, conv_w, conv_b, gate_b)


def _hgrn2_tables():
    L, S = HGRN_CHUNK, SUB
    r = np.arange(L)
    same = (r[:, None] // S) == (r[None, :] // S)
    tri = np.stack([same & (r[None, :] <= r[:, None]), same & (r[None, :] >= r[:, None])])
    fold = []
    for rev in (False, True):
        pi = np.concatenate([np.full(len(_pair_rows(i, rev)), i) for i in range(S)])
        pj = np.concatenate([_pair_rows(i, rev) for i in range(S)])
        visible = (pj >= pi) if rev else (pj <= pi)
        block = (np.arange(S)[:, None] == pi[None, :]) & visible[None, :]
        fold.append(np.kron(np.eye(PAIR_GROUP, dtype=bool), block))
    as_bf16 = lambda a: jnp.asarray(a.astype(np.float32), dtype=BF16)
    return as_bf16(tri), as_bf16(same), as_bf16(np.stack(fold))


def _pair_rows(i, rev):
    half = SUB // 2
    if rev:
        return np.arange(0, SUB) if i < half else np.arange(half, SUB)
    return np.arange(0, half) if i < half else np.arange(0, SUB)


def _mixer_d_kernel(q_ref, ff_ref, fb_ref, v_ref, g_ref, lb_ref, tri_ref, same_ref, fold_ref, out_ref,
                    o_s, st_s, *, n, n_ctx):
    L = HGRN_CHUNK
    S = SUB
    nb = L // S
    n_chunks = n // L
    n_ctx_chunks = n_ctx // L
    lb = lb_ref[...]
    bd = _block_ones(MIX_W)
    ones_bd = jnp.where(bd, 1.0, 0.0).astype(BF16)

    def chunk(r, d, c):
        rev = d == 1
        rows = pl.ds(pl.multiple_of(c * L, L), L)
        q = _silu(q_ref[r, rows, :].astype(F32))
        raw = (fb_ref if rev else ff_ref)[r, rows, :].astype(F32)
        v = v_ref[r, rows, :]
        vf = v.astype(F32)
        f = lb + (1.0 - lb) * _sigmoid(raw)
        k = 1.0 - f
        log_f = jnp.log(f)
        cum = _sel_dot(tri_ref[d], log_f)
        tot = _sel_dot(same_ref[...], log_f)
        qt = (q * jnp.exp(cum)).astype(BF16)
        kt = (k * jnp.exp(tot - cum)).astype(BF16)
        g = jnp.exp(tot)
        cum2 = cum * LOG2_E
        prod, vj, scores, intra = {}, {}, {}, {}
        ng = nb // PAIR_GROUP
        for gi in range(ng + 2):
            if gi < ng:
                prods, v_js = [], []
                for a in range(gi * PAIR_GROUP, (gi + 1) * PAIR_GROUP):
                    sl = slice(a * S, (a + 1) * S)
                    cb, qb, kb, vb = cum2[sl], q[sl], k[sl], vf[sl]
                    for i in range(S):
                        js = _pair_rows(i, rev)
                        jsl = slice(int(js[0]), int(js[-1]) + 1)
                        dec = jnp.exp2(jnp.minimum(cb[i:i + 1] - cb[jsl], 0.0))
                        prods.append(qb[i:i + 1] * kb[jsl] * dec)
                        v_js.append(vb[jsl])
                prod[gi] = jnp.concatenate(prods, axis=0).astype(BF16)
                vj[gi] = jnp.concatenate(v_js, axis=0)
            if 1 <= gi <= ng:
                scores[gi - 1] = jnp.dot(prod.pop(gi - 1), ones_bd, preferred_element_type=F32)
            if gi >= 2:
                weighted = (scores.pop(gi - 2) * vj.pop(gi - 2)).astype(BF16)
                intra[gi - 2] = jnp.dot(fold_ref[d], weighted, preferred_element_type=F32)
        o = jnp.concatenate([intra[gi] for gi in range(ng)], axis=0)
        blk = lax.broadcasted_iota(jnp.int32, (L, 1), 0) // S
        kt_blocks = jnp.concatenate([jnp.where(blk == a, kt, jnp.zeros_like(kt)) for a in range(nb)], axis=1)
        kv_all = _dot_tn(v, kt_blocks)
        st = st_s[r, d]
        inter = [None] * nb
        for a in (range(nb - 1, -1, -1) if rev else range(nb)):
            sl = slice(a * S, (a + 1) * S)
            inter[a] = _dot_nt(qt[sl], st.astype(BF16))
            st = st * g[a * S:a * S + 1] + jnp.where(bd, kv_all[:, a * MIX_W:(a + 1) * MIX_W], 0.0)
        st_s[r, d] = st
        o_s[r, d, rows, :] = o + jnp.concatenate(inter, axis=0)

    st_s[...] = jnp.zeros_like(st_s)

    def walk_body(s, _):
        for r in range(ROWS):
            chunk(r, 0, s)
            chunk(r, 1, _bwd_chunk(s, n_ctx_chunks, n_chunks))
        return 0

    lax.fori_loop(0, n_chunks, walk_body, 0)

    def out_body(c, _):
        rows = pl.ds(pl.multiple_of(c * L, L), L)
        for r in range(ROWS):
            gate = _silu(g_ref[r, rows, :].astype(F32))
            o = o_s[r, 0, rows, :] + o_s[r, 1, rows, :]
            out_ref[r, rows, :] = (_head_norm(o, False) * gate).astype(out_ref.dtype)
        return 0

    lax.fori_loop(0, n_chunks, out_body, 0)


def _mixer_d(p, lb, n_ctx):
    b, n, _ = p.shape
    tri, same, fold = _hgrn2_tables()
    kern = functools.partial(_mixer_d_kernel, n=n, n_ctx=n_ctx)
    return pl.pallas_call(
        kern,
        grid=(b // ROWS,),
        in_specs=[_seq_spec(n, 10, ROWS), _seq_spec(n, 11, ROWS), _seq_spec(n, 12, ROWS), _seq_spec(n, 13, ROWS),
                  _seq_spec(n, 14, ROWS), _full_spec(lb.shape), _full_spec(tri.shape), _full_spec(same.shape),
                  _full_spec(fold.shape)],
        out_specs=pl.BlockSpec((ROWS, n, MIX_W), lambda bi: (bi, 0, 0)),
        out_shape=jax.ShapeDtypeStruct((b, n, MIX_W), BF16),
        scratch_shapes=[pltpu.VMEM((ROWS, 2, n, MIX_W), F32), pltpu.VMEM((ROWS, 2, MIX_W, MIX_W), F32)],
        compiler_params=_cparams("parallel"),
        name="mixer_hgrn2",
    )(p, p, p, p, p, lb, tri, same, fold)


def _pack_bf16_pairs(t):
    w = t.shape[1] // 2
    hi = pltpu.bitcast(t[:, :w].astype(BF16).astype(F32), jnp.uint32)
    lo = pltpu.bitcast(t[:, w:].astype(BF16).astype(F32), jnp.uint32)
    return pltpu.bitcast(hi | (lo >> 16), jnp.int32)


def _unpack_bf16_pairs(p):
    u = pltpu.bitcast(p, jnp.uint32)
    hi = pltpu.bitcast(u & jnp.uint32(0xFFFF0000), F32)
    lo = pltpu.bitcast(u << 16, F32)
    return hi, lo


def _merge_kernel(x_ref, m_ref, ya_ref, yb_ref, yc_ref, yd_ref, wm_ref, wb_ref, wo_ref, wr_ref, br_ref,
                  earlier_ref, xo_ref, hp_ref, route_ref, counts_ref, carry_s, *, tm, n_ctx):
    i = pl.program_id(1)

    @pl.when((pl.program_id(0) == 0) & (i == 0))
    def _():
        carry_s[...] = jnp.zeros_like(carry_s)

    row = i * tm + lax.broadcasted_iota(jnp.int32, (tm, 1), 0)
    is_ctx = row < n_ctx
    m = m_ref[0]
    x = x_ref[0]
    d = x.shape[-1]
    h = _modulate(x, _pick(m, is_ctx, 0), _pick(m, is_ctx, 1)).astype(BF16)
    z = jnp.zeros((tm, d), F32)
    for nb, y_ref in enumerate((ya_ref, yb_ref, yc_ref, yd_ref)):
        gate = _sigmoid(jnp.dot(h, wm_ref[:, nb * d:(nb + 1) * d], preferred_element_type=F32))
        z = z + gate * jnp.dot(y_ref[0], wb_ref[nb], preferred_element_type=F32)
    mix = jnp.dot(z.astype(BF16), wo_ref[...], preferred_element_type=F32)
    x1 = x + _pick(m, is_ctx, 2) * mix
    xo_ref[0] = x1
    h2 = _modulate(x1, _pick(m, is_ctx, 3), _pick(m, is_ctx, 4))

    wr = wr_ref[...]
    wr_hi = wr.astype(BF16)
    wr_lo = (wr - wr_hi.astype(F32)).astype(BF16)
    h2_hi = h2.astype(BF16)
    h2_lo = (h2 - h2_hi.astype(F32)).astype(BF16)
    dot = functools.partial(jnp.dot, preferred_element_type=F32)
    logit = dot(h2_hi, wr_hi) + dot(h2_lo, wr_hi) + dot(h2_hi, wr_lo) + br_ref[...]
    lane = lax.broadcasted_iota(jnp.int32, (tm, LANES), 1)
    big = jnp.int32(LANES)
    is_group = (lane >= N_EXPERTS) & (lane < N_EXPERTS + N_GROUPS)
    gl = jnp.where(is_group, logit, NEG_INF)
    g_max = jnp.max(gl, axis=1, keepdims=True)
    g_idx = jnp.min(jnp.where(gl == g_max, lane, big), axis=1, keepdims=True) - N_EXPERTS
    g_prob = 1.0 / jnp.sum(jnp.where(is_group, jnp.exp(logit - g_max), 0.0), axis=1, keepdims=True)
    in_group = (lane < N_EXPERTS) & (lane // EXP_PER_GROUP == g_idx)
    el = jnp.where(in_group, logit, NEG_INF)
    v1 = jnp.max(el, axis=1, keepdims=True)
    i1 = jnp.min(jnp.where(el == v1, lane, big), axis=1, keepdims=True)
    el2 = jnp.where(lane == i1, NEG_INF, el)
    v2 = jnp.max(el2, axis=1, keepdims=True)
    i2 = jnp.min(jnp.where(el2 == v2, lane, big), axis=1, keepdims=True)
    e2 = jnp.exp(v2 - v1)
    w1 = g_prob / (1.0 + e2)
    w2 = g_prob * e2 / (1.0 + e2)
    first_low = i1 < i2
    lo = jnp.where(first_low, i1, i2) - g_idx * EXP_PER_GROUP
    hi = jnp.where(first_low, i2, i1) - g_idx * EXP_PER_GROUP
    pair = 3 * lo - ((lo * (lo - 1)) >> 1) + (hi - lo - 1)
    cls = g_idx * N_PAIRS + pair
    w_lo = jnp.where(first_low, w1, w2)
    w_hi = jnp.where(first_low, w2, w1)
    onehot = jnp.where(lane == cls, 1.0, 0.0)
    before = jnp.dot(earlier_ref[...], onehot.astype(BF16), preferred_element_type=F32) + carry_s[...]
    rank = jnp.sum(onehot * before, axis=1, keepdims=True)
    carry_s[...] += jnp.sum(onehot, axis=0, keepdims=True)
    counts_ref[...] = carry_s[...].astype(jnp.int32)
    cols = jnp.where(lane == 0, cls.astype(F32), jnp.where(lane == 1, rank, 0.0))
    pick = jnp.where(lax.broadcasted_iota(jnp.int32, (8, LANES), 0)
                     == lax.broadcasted_iota(jnp.int32, (8, LANES), 1), 1.0, 0.0).astype(BF16)
    route_ref[0] = sum(_dot_nt(pick, part) for part in _split3(cols)).astype(jnp.int32)
    gate_bits = pltpu.bitcast(jnp.where(lane == 0, w_lo, jnp.where(lane == 1, w_hi, 0.0)), jnp.int32)
    hp_ref[0] = jnp.concatenate([_pack_bf16_pairs(h2), gate_bits], axis=1)


def _merge(xc, mvec, ys, w_merge, w_branch, w_out, w_route, b_route, n_ctx):
    b, n, d = xc.shape
    tm = _token_tile(n, 640)
    kern = functools.partial(_merge_kernel, tm=tm, n_ctx=n_ctx)
    tok = lambda w: pl.BlockSpec((1, tm, w), lambda bi, i: (bi, i, 0))
    const = lambda shape: pl.BlockSpec(shape, lambda bi, i: (0,) * len(shape))
    earlier = jnp.asarray(np.tri(tm, k=-1, dtype=np.float32), dtype=BF16)
    return pl.pallas_call(
        kern,
        grid=(b, n // tm),
        in_specs=[tok(d), pl.BlockSpec((1, MOD_ROWS, d), lambda bi, i: (bi, 0, 0)),
                  tok(MIX_W), tok(MIX_W), tok(MIX_W), tok(MIX_W),
                  const(w_merge.shape), const(w_branch.shape), const(w_out.shape),
                  const(w_route.shape), const(b_route.shape), const(earlier.shape)],
        out_specs=[tok(d), tok(d // 2 + LANES),
                   pl.BlockSpec((1, 8, tm), lambda bi, i: (bi * (n // tm) + i, 0, 0)), const((1, LANES))],
        out_shape=[jax.ShapeDtypeStruct((b, n, d), F32),
                   jax.ShapeDtypeStruct((b, n, d // 2 + LANES), jnp.int32),
                   jax.ShapeDtypeStruct((b * (n // tm), 8, tm), jnp.int32),
                   jax.ShapeDtypeStruct((1, LANES), jnp.int32)],
        scratch_shapes=[pltpu.VMEM((1, LANES), F32)],
        compiler_params=_cparams("arbitrary", "arbitrary"),
        name="merge_route",
    )(xc, mvec, *ys, w_merge, w_branch, w_out, w_route, b_route, earlier)


def _row_move(table, idx, n_out=None):
    scatter = n_out is not None
    n_idx = idx.shape[0]
    width = table.shape[1]
    per_worker = n_idx // SC_WORKERS
    assert per_worker * SC_WORKERS == n_idx and per_worker % 8 == 0
    chunk = max(c for c in range(8, SC_MAX_CHUNK + 1, 8) if per_worker % c == 0)
    n_chunks = per_worker // chunk
    mesh = plsc.VectorSubcoreMesh(core_axis_name="c", subcore_axis_name="s",
                                  num_cores=SC_CORES, num_subcores=SC_SUBCORES)

    @functools.partial(
        pl.kernel, mesh=mesh,
        out_type=jax.ShapeDtypeStruct((n_out if scatter else n_idx, width), table.dtype),
        scratch_types=[pltpu.VMEM((chunk,), jnp.int32),
                       pltpu.VMEM((chunk, width), table.dtype),
                       pltpu.SemaphoreType.DMA],
        name="sc_row_scatter" if scatter else "sc_row_gather",
    )
    def move(table_hbm, idx_hbm, out_hbm, idx_v, rows_v, sem):
        worker = lax.axis_index("s") * SC_CORES + lax.axis_index("c")
        base = worker * per_worker

        @pl.loop(0, n_chunks)
        def _(j):
            off = pl.multiple_of(base + j * chunk, 8)
            pltpu.sync_copy(idx_hbm.at[pl.ds(off, chunk)], idx_v)
            if scatter:
                pltpu.sync_copy(table_hbm.at[pl.ds(off, chunk)], rows_v)
                pltpu.async_copy(rows_v, out_hbm.at[idx_v], sem).wait()
            else:
                pltpu.async_copy(table_hbm.at[idx_v], rows_v, sem).wait()
                pltpu.sync_copy(rows_v, out_hbm.at[pl.ds(off, chunk)])

    return move(table, idx)


def _ffn(h, w1_ref, w3_ref, w2_ref):
    a = jnp.dot(h, w1_ref[0, 0], preferred_element_type=F32)
    g = jnp.dot(h, w3_ref[0, 0], preferred_element_type=F32)
    return jnp.dot((_silu(a) * g).astype(BF16), w2_ref[0, 0], preferred_element_type=F32)


def _moe_kernel(elo_ref, ehi_ref, valid_ref, xs_ref, w1a_ref, w3a_ref, w2a_ref, w1b_ref, w3b_ref, w2b_ref,
                ys_ref):
    j = pl.program_id(0)
    half = ys_ref.shape[1]

    @pl.when(valid_ref[j] != 0)
    def _():
        hi, lo = _unpack_bf16_pairs(xs_ref[:, :half])
        h = jnp.concatenate([hi.astype(BF16), lo.astype(BF16)], axis=1)
        gates = pltpu.bitcast(xs_ref[:, half:], F32)
        y = (gates[:, 0:1] * _ffn(h, w1a_ref, w3a_ref, w2a_ref)
             + gates[:, 1:2] * _ffn(h, w1b_ref, w3b_ref, w2b_ref))
        ys_ref[...] = _pack_bf16_pairs(y)

    @pl.when(valid_ref[j] == 0)
    def _():
        ys_ref[...] = jnp.zeros_like(ys_ref)


def _moe_experts(xs, tile_elo, tile_ehi, tile_valid, w1, w3, w2, layer):
    n_slots, width = xs.shape
    d = w1.shape[2]
    n_tiles = n_slots // MOE_TILE
    wspec = lambda shape, which: pl.BlockSpec(
        (1, 1) + shape, lambda j, elo, ehi, valid: (layer, (elo, ehi)[which][j], 0, 0))
    grid_spec = pltpu.PrefetchScalarGridSpec(
        num_scalar_prefetch=3,
        grid=(n_tiles,),
        in_specs=[pl.BlockSpec((MOE_TILE, width), lambda j, elo, ehi, valid: (j, 0)),
                  wspec((d, D_EXPERT), 0), wspec((d, D_EXPERT), 0), wspec((D_EXPERT, d), 0),
                  wspec((d, D_EXPERT), 1), wspec((d, D_EXPERT), 1), wspec((D_EXPERT, d), 1)],
        out_specs=pl.BlockSpec((MOE_TILE, d // 2), lambda j, elo, ehi, valid: (j, 0)),
    )
    return pl.pallas_call(
        _moe_kernel,
        grid_spec=grid_spec,
        out_shape=jax.ShapeDtypeStruct((n_slots, d // 2), jnp.int32),
        compiler_params=_cparams("arbitrary"),
        name="moe_experts",
    )(tile_elo, tile_ehi, tile_valid, xs, w1, w3, w2, w1, w3, w2)


def _moe_combine_kernel(x_ref, m_ref, y_ref, *rest, tm, n_ctx, row0):
    o_ref = rest[-1]
    row = row0 + pl.program_id(1) * tm + lax.broadcasted_iota(jnp.int32, (tm, 1), 0)
    hi, lo = _unpack_bf16_pairs(y_ref[0])
    x = x_ref[0] + _pick(m_ref[0], row < n_ctx, 5) * jnp.concatenate([hi, lo], axis=1)
    if len(rest) == 2:
        ms = jnp.mean(x * x, axis=-1, keepdims=True)
        x = x * lax.rsqrt(ms + EPS) * rest[0][...]
    o_ref[0] = x


def _moe_combine(x1, mvec, yt, n_ctx, final_w=None):
    b, n, d = x1.shape
    if final_w is None:
        tm, skip, n_out, extra, extra_specs = _token_tile(n, 1100), 0, n, (), []
    else:
        tm = _token_tile(n_ctx, 1024)
        skip, n_out = n_ctx // tm, n - n_ctx
        extra, extra_specs = (final_w.reshape(1, d),), [pl.BlockSpec((1, d), lambda bi, i: (0, 0))]
    kern = functools.partial(_moe_combine_kernel, tm=tm, n_ctx=n_ctx, row0=skip * tm)
    tok = lambda w: pl.BlockSpec((1, tm, w), lambda bi, i: (bi, i + skip, 0))
    return pl.pallas_call(
        kern,
        grid=(b, n_out // tm),
        in_specs=[tok(d), pl.BlockSpec((1, MOD_ROWS, d), lambda bi, i: (bi, 0, 0)), tok(d // 2)] + extra_specs,
        out_specs=pl.BlockSpec((1, tm, d), lambda bi, i: (bi, i, 0)),
        out_shape=jax.ShapeDtypeStruct((b, n_out, d), F32),
        compiler_params=_cparams("parallel", "parallel"),
        name="moe_combine",
    )(x1, mvec, yt, *extra)


def _moe(x1, mvec, hp, route, counts, w1, w3, w2, layer, n_ctx, final_w=None):
    b, n, d = x1.shape
    m = b * n
    n_slots = m + N_CLASSES * MOE_TILE
    cls = route[:, 0, :].reshape(m)
    rank = route[:, 1, :].reshape(m)
    cnt = counts[0, :N_CLASSES]
    padded = (cnt + MOE_TILE - 1) // MOE_TILE * MOE_TILE
    ends = jnp.cumsum(padded)
    pos = (ends - padded)[cls] + rank
    tile_start = jnp.arange(n_slots // MOE_TILE, dtype=jnp.int32) * MOE_TILE
    tile_cls = jnp.minimum(jnp.searchsorted(ends, tile_start, side="right"), N_CLASSES - 1).astype(jnp.int32)
    tile_valid = (tile_start < ends[-1]).astype(jnp.int32)
    pair_lo = jnp.asarray([0, 0, 0, 1, 1, 2], jnp.int32)
    pair_hi = jnp.asarray([1, 2, 3, 2, 3, 3], jnp.int32)
    group = tile_cls // N_PAIRS
    tile_elo = group * EXP_PER_GROUP + pair_lo[tile_cls % N_PAIRS]
    tile_ehi = group * EXP_PER_GROUP + pair_hi[tile_cls % N_PAIRS]

    xs = _row_move(hp.reshape(m, hp.shape[-1]), pos, n_out=n_slots)
    ys = _moe_experts(xs, tile_elo, tile_ehi, tile_valid, w1, w3, w2, layer)
    yt = _row_move(ys, pos)
    return _moe_combine(x1, mvec, yt.reshape(b, n, d // 2), n_ctx, final_w)


def _rope_tables(n_lat, n_ctx):
    rows = n_lat // GRID_W
    row = jnp.repeat(jnp.arange(rows), GRID_W).astype(F32)
    col = jnp.tile(jnp.arange(GRID_W), rows).astype(F32)
    nq = HEAD_DIM // 4
    inv = jnp.power(ROPE_BASE, -jnp.arange(nq, dtype=F32) / nq)
    ang = jnp.concatenate([row[:, None] * inv, col[:, None] * inv], -1)
    cos = jnp.cos(ang)
    sin = jnp.sin(ang)
    cos_h = jnp.concatenate([cos, cos], -1)
    sin_h = jnp.concatenate([-sin, sin], -1)
    cos_full = jnp.concatenate([jnp.ones((n_ctx, HEAD_DIM), F32), cos_h], 0)
    sin_full = jnp.concatenate([jnp.zeros((n_ctx, HEAD_DIM), F32), sin_h], 0)
    return jnp.tile(cos_full, (1, N_HEADS)), jnp.tile(sin_full, (1, N_HEADS))


def _block_diag_heads(w):
    eye = jnp.eye(N_HEADS, dtype=w.dtype)
    return jnp.einsum("hij,hg->higj", w, eye).reshape(MIX_W, MIX_W)


def kernel(x, c, ctx, c_ctx, w_mod, b_mod, w_in, a_conv_w, a_conv_b, a_gate_w, a_gate_b, a_lambda, b_theta,
           c_conv_w, c_conv_b, c_gate_b, d_lb, w_branch, w_out, moe_w_group, moe_b_group, moe_w_router,
           moe_b_router, moe_w1, moe_w3, moe_w2, final_norm_w):
    bsz, n_lat, d = x.shape
    n_ctx = ctx.shape[1]
    depth = w_mod.shape[0]
    assert n_ctx % CHUNK == 0 and n_lat % CHUNK == 0 and n_ctx % 256 == 0 and n_lat % 256 == 0

    xc = jnp.concatenate([ctx, x], axis=1)
    cos, sin = _rope_tables(n_lat, n_ctx)

    cc = jnp.zeros((8, d), F32).at[:bsz].set(c).at[bsz].set(c_ctx)
    mod = _mod_vectors(cc, w_mod, b_mod)
    mx = mod[:, :bsz].reshape(depth, bsz, N_MOD, d)
    mc = jnp.broadcast_to(mod[:, bsz].reshape(depth, 1, N_MOD, d), (depth, bsz, N_MOD, d))
    pad = jnp.zeros((depth, bsz, 8 - N_MOD, d), F32)
    mvec = jnp.concatenate([mx, pad, mc, pad], axis=2)

    lbs = jnp.cumsum(jax.nn.softmax(d_lb.astype(F32), axis=0), axis=0)
    lbs = lbs - lbs[0]

    n_mix_cols = 15 * MIX_W
    gate0 = 10 * MIX_W
    n_gate = 4 * N_HEADS
    w1b = moe_w1.astype(BF16)
    w3b = moe_w3.astype(BF16)
    w2b = moe_w2.astype(BF16)

    for l in range(depth):
        wl = w_in[l]
        w_mix = jnp.concatenate([wl[:, :gate0], wl[:, gate0 + n_gate:n_mix_cols + n_gate]], axis=1).astype(BF16)
        w_gate = jnp.pad(wl[:, gate0:gate0 + n_gate], ((0, 0), (0, LANES - n_gate))).astype(BF16)
        w_merge = wl[:, n_mix_cols + n_gate:].astype(BF16)
        p, gates_c = _in_proj(xc, mvec[l], w_mix, w_gate, cos, sin, n_ctx)

        gw = jnp.concatenate([_block_diag_heads(a_gate_w[l, dd, j]) for dd in range(2) for j in range(2)],
                             axis=1).astype(BF16)
        gb = a_gate_b[l].reshape(1, 4 * MIX_W)
        ya = _mixer_a(p, a_conv_w[l], a_conv_b[l].reshape(1, MIX_W), gw, gb, a_lambda[l], n_ctx)

        log_gamma = jnp.repeat(jax.nn.log_sigmoid(b_theta[l].astype(F32)), HEAD_DIM, axis=1)
        yb = _mixer_b(p, log_gamma, n_ctx)

        gate_b = jnp.pad(c_gate_b[l].reshape(1, n_gate), ((0, 0), (0, LANES - n_gate)))
        yc = _mixer_c(p, gates_c, c_conv_w[l], c_conv_b[l].reshape(1, 2 * MIX_W), gate_b, n_ctx)

        yd = _mixer_d(p, lbs[l].reshape(1, MIX_W), n_ctx)

        w_route = jnp.pad(jnp.concatenate([moe_w_router[l], moe_w_group[l]], axis=1),
                          ((0, 0), (0, LANES - N_EXPERTS - N_GROUPS)))
        b_route = jnp.pad(jnp.concatenate([moe_b_router[l], moe_b_group[l]]),
                          (0, LANES - N_EXPERTS - N_GROUPS)).reshape(1, LANES)
        x1, hp, route, counts = _merge(xc, mvec[l], (ya, yb, yc, yd), w_merge, w_branch[l].astype(BF16),
                                       w_out[l].astype(BF16), w_route, b_route, n_ctx)
        xc = _moe(x1, mvec[l], hp, route, counts, w1b, w3b, w2b, l, n_ctx,
                  final_w=final_norm_w if l == depth - 1 else None)

    return xc
```

```python
import functools

import jax
import jax.numpy as jnp
import numpy as np
from jax import lax
from jax.experimental import pallas as pl
from jax.experimental.pallas import tpu as pltpu
from jax.experimental.pallas import tpu_sc as plsc

F32 = jnp.float32
BF16 = jnp.bfloat16

EPS = 1e-6
N_HEADS = 4
HEAD_DIM = 64
MIX_W = N_HEADS * HEAD_DIM
N_BRANCH = 4
CONV_W = 4
LRU_C = 8.0
GRID_W = 64
ROPE_BASE = 10000.0
N_GROUPS = 4
EXP_PER_GROUP = 4
N_EXPERTS = N_GROUPS * EXP_PER_GROUP
D_EXPERT = 512
N_MOD = 6
M_INIT = -1e30

CHUNK = 128
RET_CHUNK = 256
CONV_STEP = 128
OUT_ROWS = 256
SUB = 16
HGRN_CHUNK = 256
PAIR_GROUP = 4
HALO = 16
LANES = 128
MOD_ROWS = 16
ROWS = 1
VMEM_LIMIT_BYTES = 56 * 1024 * 1024
N_PAIRS = 6
N_CLASSES = N_GROUPS * N_PAIRS
MOE_TILE = 256
SC_CORES = 2
SC_SUBCORES = 16
SC_WORKERS = SC_CORES * SC_SUBCORES
SC_MAX_CHUNK = 32
NEG_INF = float("-inf")
LOG2_E = 1.4426950408889634


def _cparams(*sem):
    return pltpu.CompilerParams(dimension_semantics=sem, vmem_limit_bytes=VMEM_LIMIT_BYTES)


def _token_tile(n, cap):
    best = 16
    for t in range(16, cap + 1, 16):
        if n % t == 0:
            best = t
    return best


def _modulate(x, shift, scale):
    ms = jnp.mean(x * x, axis=-1, keepdims=True)
    return x * lax.rsqrt(ms + EPS) * (1.0 + scale) + shift


def _pick(m, is_ctx, k):
    return jnp.where(is_ctx, m[8 + k:9 + k], m[k:k + 1])


def _sigmoid(x):
    return 0.5 * jnp.tanh(0.5 * x) + 0.5


def _silu(x):
    return x * _sigmoid(x)


def _log_sigmoid(x):
    return jnp.minimum(x, 0.0) - jnp.log(1.0 + jnp.exp(-jnp.abs(x)))


def _split3(x):
    hi = x.astype(BF16)
    r = x - hi.astype(F32)
    mid = r.astype(BF16)
    lo = (r - mid.astype(F32)).astype(BF16)
    return hi, mid, lo


def _split2(x):
    hi = x.astype(BF16)
    return hi, (x - hi.astype(F32)).astype(BF16)


def _sel_dot(sel, x):
    hi, lo = _split2(x)
    d = functools.partial(jnp.dot, preferred_element_type=F32)
    return d(sel, hi) + d(sel, lo)


def _dot_sel(x, sel):
    hi, lo = _split2(x)
    d = functools.partial(jnp.dot, preferred_element_type=F32)
    return d(hi, sel) + d(lo, sel)


def _dot_nt(a, b):
    return lax.dot_general(a, b, (((1,), (1,)), ((), ())), preferred_element_type=F32)


def _dot_tn(a, b):
    return lax.dot_general(a, b, (((0,), (0,)), ((), ())), preferred_element_type=F32)


def _head_of_lane(width):
    return lax.broadcasted_iota(jnp.int32, (1, width), 1) // HEAD_DIM


def _block_ones(width):
    r = lax.broadcasted_iota(jnp.int32, (width, width), 0) // HEAD_DIM
    c = lax.broadcasted_iota(jnp.int32, (width, width), 1) // HEAD_DIM
    return r == c


def _head_norm(o, center):
    ones = jnp.where(_block_ones(MIX_W), 1.0, 0.0).astype(BF16)
    inv = 1.0 / HEAD_DIM
    if center:
        o = o - _dot_sel(o, ones) * inv
    var = _dot_sel(o * o, ones) * inv
    return o * lax.rsqrt(var + EPS)


def _bwd_chunk(s, n_ctx_chunks, n_chunks):
    return jnp.where(s < n_ctx_chunks, n_ctx_chunks - 1 - s, n_chunks - 1 + n_ctx_chunks - s)


def _conv_chunk(c, operands, *, n, n_ctx):
    L = CONV_STEP
    wl = L + 2 * HALO
    r0 = c * L
    start = pl.multiple_of(jnp.clip(r0 - HALO, 0, n - wl), HALO)
    off = r0 - start
    shift = (lax.broadcasted_iota(jnp.int32, (L, wl), 1) - lax.broadcasted_iota(jnp.int32, (L, wl), 0)) - off
    g = r0 + lax.broadcasted_iota(jnp.int32, (L, 1), 0)
    seg_g = jnp.where(g < n_ctx, 0, 1)
    sels = []
    for k in range(CONV_W):
        src = g + (k - 2)
        row_ok = (jnp.where(src < n_ctx, 0, 1) == seg_g) & (src >= 0) & (src < n)
        sels.append(jnp.where((shift == k - 2) & row_ok, 1.0, 0.0).astype(BF16))
    outs = []
    for ref, w, b in operands:
        win = ref[0, pl.ds(start, wl), :]
        y = b
        for k in range(CONV_W):
            y = y + w[k:k + 1] * jnp.dot(sels[k], win, preferred_element_type=F32)
        outs.append(y)
    return outs


def _mod_kernel(cc_ref, w_ref, b_ref, o_ref):
    s = _silu(cc_ref[...])
    o_ref[0] = jnp.dot(s, w_ref[0], precision=lax.Precision.HIGHEST, preferred_element_type=F32) + b_ref[0]


def _mod_vectors(cc, w_mod, b_mod):
    depth, d, dm = w_mod.shape
    tn = 1536
    return pl.pallas_call(
        _mod_kernel,
        grid=(depth, dm // tn),
        in_specs=[pl.BlockSpec((8, d), lambda l, j: (0, 0)),
                  pl.BlockSpec((1, d, tn), lambda l, j: (l, 0, j)),
                  pl.BlockSpec((1, 1, tn), lambda l, j: (l, 0, j))],
        out_specs=pl.BlockSpec((1, 8, tn), lambda l, j: (l, 0, j)),
        out_shape=jax.ShapeDtypeStruct((depth, 8, dm), F32),
        compiler_params=_cparams("parallel", "parallel"),
        name="mod_vectors",
    )(cc, w_mod, b_mod.reshape(depth, 1, dm))


def _rope(t, cos, sin):
    lane = lax.broadcasted_iota(jnp.int32, (1, MIX_W), 1) % HEAD_DIM
    half = HEAD_DIM // 2
    swapped = jnp.where(lane < half, pltpu.roll(t, MIX_W - half, 1), pltpu.roll(t, half, 1))
    return t * cos + swapped * sin


RET_Q_COL = 2 * MIX_W
RET_K_COL = 3 * MIX_W


def _in_proj_kernel(x_ref, m_ref, w_ref, wg_ref, cos_ref, sin_ref, p_ref, g_ref, *, tm, n_ctx, col_chunk):
    i = pl.program_id(1)
    row = i * tm + lax.broadcasted_iota(jnp.int32, (tm, 1), 0)
    is_ctx = row < n_ctx
    m = m_ref[0]
    h = _modulate(x_ref[0], _pick(m, is_ctx, 0), _pick(m, is_ctx, 1)).astype(BF16)
    for j in range(w_ref.shape[1] // col_chunk):
        sl = slice(j * col_chunk, (j + 1) * col_chunk)
        t = jnp.dot(h, w_ref[:, sl], preferred_element_type=F32)
        if j * col_chunk == RET_Q_COL:
            t = _rope(t, cos_ref[...], sin_ref[...])
        elif j * col_chunk == RET_K_COL:
            t = _rope(t, cos_ref[...], sin_ref[...]) * HEAD_DIM ** -0.5
        p_ref[0, :, sl] = t.astype(BF16)
    g_ref[0] = jnp.dot(h, wg_ref[...], preferred_element_type=F32)


def _in_proj(xc, mvec, w_mix, w_gate, cos, sin, n_ctx):
    b, n, d = xc.shape
    tm = _token_tile(n, 640)
    wc = w_mix.shape[1]
    kern = functools.partial(_in_proj_kernel, tm=tm, n_ctx=n_ctx, col_chunk=MIX_W)
    return pl.pallas_call(
        kern,
        grid=(b, n // tm),
        in_specs=[pl.BlockSpec((1, tm, d), lambda bi, i: (bi, i, 0)),
                  pl.BlockSpec((1, MOD_ROWS, d), lambda bi, i: (bi, 0, 0)),
                  pl.BlockSpec((d, wc), lambda bi, i: (0, 0)),
                  pl.BlockSpec((d, LANES), lambda bi, i: (0, 0)),
                  pl.BlockSpec((tm, MIX_W), lambda bi, i: (i, 0)),
                  pl.BlockSpec((tm, MIX_W), lambda bi, i: (i, 0))],
        out_specs=[pl.BlockSpec((1, tm, wc), lambda bi, i: (bi, i, 0)),
                   pl.BlockSpec((1, tm, LANES), lambda bi, i: (bi, i, 0))],
        out_shape=[jax.ShapeDtypeStruct((b, n, wc), BF16),
                   jax.ShapeDtypeStruct((b, n, LANES), F32)],
        compiler_params=_cparams("parallel", "parallel"),
        name="in_proj",
    )(xc, mvec, w_mix, w_gate, cos, sin)


def _lin_scan(a, x, carry, reverse):
    L = a.shape[0]
    row = lax.broadcasted_iota(jnp.int32, (L, 1), 0)
    k = 1
    while k < L:
        if reverse:
            a_s = pltpu.roll(a, L - k, 0)
            x_s = pltpu.roll(x, L - k, 0)
            valid = row < L - k
        else:
            a_s = pltpu.roll(a, k, 0)
            x_s = pltpu.roll(x, k, 0)
            valid = row >= k
        x = jnp.where(valid, a * x_s + x, x)
        a = jnp.where(valid, a * a_s, a)
        k *= 2
    h = x + a * carry
    return h, (h[0:1] if reverse else h[L - 1:L])


def _mixer_a_kernel(ax_ref, ag_ref, cw_ref, cb_ref, gw_ref, gb_ref, lam_ref, out_ref, u_s, hf_s,
                    *, n, n_ctx):
    L = CHUNK
    n_chunks = n // L
    n_ctx_chunks = n_ctx // L
    cw = cw_ref[...]
    cb = cb_ref[...]
    log_lam = _log_sigmoid(lam_ref[...])

    def direction(d, c, carry):
        r0 = pl.multiple_of(c * L, L)
        u = u_s[pl.ds(r0, L), :]
        pre = jnp.dot(u.astype(BF16), gw_ref[:, d * 2 * MIX_W:(d + 1) * 2 * MIX_W],
                      preferred_element_type=F32) + gb_ref[:, d * 2 * MIX_W:(d + 1) * 2 * MIX_W]
        r = _sigmoid(pre[:, :MIX_W])
        gi = _sigmoid(pre[:, MIX_W:])
        log_a = LRU_C * r * log_lam[d:d + 1]
        a = jnp.exp(log_a)
        inp = jnp.sqrt(1.0 - jnp.exp(2.0 * log_a)) * (gi * u)
        h, new_carry = _lin_scan(a, inp, carry, reverse=(d == 1))
        return r0, h, new_carry

    def conv_body(c, _):
        r0 = pl.multiple_of(c * CONV_STEP, CONV_STEP)
        u_s[pl.ds(r0, CONV_STEP), :] = _conv_chunk(c, [(ax_ref, cw, cb)], n=n, n_ctx=n_ctx)[0]
        return 0

    lax.fori_loop(0, n // CONV_STEP, conv_body, 0)

    def walk_body(s, carries):
        r0, h, carry_f = direction(0, s, carries[0])
        hf_s[0, pl.ds(r0, L), :] = h
        r0, h, carry_b = direction(1, _bwd_chunk(s, n_ctx_chunks, n_chunks), carries[1])
        hf_s[1, pl.ds(r0, L), :] = h
        return carry_f, carry_b

    zero = jnp.zeros((1, MIX_W), F32)
    lax.fori_loop(0, n_chunks, walk_body, (zero, zero))

    def out_body(c, _):
        rows = pl.ds(pl.multiple_of(c * L, L), L)
        gate = jax.nn.gelu(ag_ref[0, rows, :].astype(F32), approximate=True)
        out_ref[0, rows, :] = (gate * (hf_s[0, rows, :] + hf_s[1, rows, :])).astype(out_ref.dtype)
        return 0

    lax.fori_loop(0, n_chunks, out_body, 0)


def _seq_spec(n, col, rows=1):
    if rows == 1:
        return pl.BlockSpec((1, n, MIX_W), lambda b: (b, 0, col))
    return pl.BlockSpec((rows, n, MIX_W), lambda b: (b, 0, col), pipeline_mode=pl.Buffered(1))


def _full_spec(shape):
    return pl.BlockSpec(shape, lambda b: (0,) * len(shape))


def _mixer_a(p, conv_w, conv_b, gate_w, gate_b, lam, n_ctx):
    b, n, _ = p.shape
    kern = functools.partial(_mixer_a_kernel, n=n, n_ctx=n_ctx)
    return pl.pallas_call(
        kern,
        grid=(b,),
        in_specs=[_seq_spec(n, 0), _seq_spec(n, 1),
                  _full_spec(conv_w.shape), _full_spec(conv_b.shape),
                  _full_spec(gate_w.shape), _full_spec(gate_b.shape), _full_spec(lam.shape)],
        out_specs=pl.BlockSpec((1, n, MIX_W), lambda bi: (bi, 0, 0)),
        out_shape=jax.ShapeDtypeStruct((b, n, MIX_W), BF16),
        scratch_shapes=[pltpu.VMEM((n, MIX_W), F32), pltpu.VMEM((2, n, MIX_W), F32)],
        compiler_params=_cparams("parallel"),
        name="mixer_rglru",
    )(p, p, conv_w, conv_b, gate_w, gate_b, lam)


def _mixer_b_kernel(q_ref, k_ref, v_ref, g_ref, lg_ref, out_ref, oi_s, dec_s, st_s, *, n, n_ctx):
    L = RET_CHUNK
    n_chunks = n // L
    n_ctx_chunks = n_ctx // L
    head = _head_of_lane(MIX_W)
    bd = _block_ones(MIX_W)
    lgf = lg_ref[0:1, :]
    lgb = lg_ref[1:2, :]
    pos = lax.broadcasted_iota(jnp.int32, (L, 1), 0).astype(F32)
    diff = (lax.broadcasted_iota(jnp.int32, (L, L), 0) - lax.broadcasted_iota(jnp.int32, (L, L), 1)).astype(F32)
    for h in range(N_HEADS):
        lf = lgf[:, h * HEAD_DIM:h * HEAD_DIM + 1]
        lb = lgb[:, h * HEAD_DIM:h * HEAD_DIM + 1]
        dec_s[h] = jnp.where(diff >= 0, jnp.exp(lf * jnp.maximum(diff, 0.0)), jnp.exp(lb * jnp.maximum(-diff, 0.0)))

    qw = (jnp.exp(lgf * (pos + 1.0)), jnp.exp(lgb * (float(L) - pos)))
    kw = (jnp.exp(lgf * (float(L) - 1.0 - pos)), jnp.exp(lgb * pos))
    g_chunk = (jnp.exp(lgf * float(L)), jnp.exp(lgb * float(L)))

    def state_step(d, c):
        rows = pl.ds(pl.multiple_of(c * L, L), L)
        q = q_ref[0, rows, :].astype(F32)
        k = k_ref[0, rows, :].astype(F32)
        st = st_s[d]
        oi_s[d, rows, :] = jnp.dot((q * qw[d]).astype(BF16), st.astype(BF16), preferred_element_type=F32)
        kv = _dot_tn((k * kw[d]).astype(BF16), v_ref[0, rows, :])
        st_s[d] = g_chunk[d] * st + jnp.where(bd, kv, 0.0)

    st_s[...] = jnp.zeros_like(st_s)

    def walk_body(s, _):
        state_step(0, s)
        state_step(1, _bwd_chunk(s, n_ctx_chunks, n_chunks))
        return 0

    lax.fori_loop(0, n_chunks, walk_body, 0)

    def out_body(c, _):
        rows = pl.ds(pl.multiple_of(c * L, L), L)
        q = q_ref[0, rows, :]
        k = k_ref[0, rows, :]
        v = v_ref[0, rows, :]
        scs, vals = [], []
        qk = _dot_nt(jnp.concatenate([jnp.where(head == h, q, jnp.zeros_like(q)) for h in range(N_HEADS)], axis=0), k)
        for h in range(N_HEADS):
            hm = head == h
            scs.append((qk[h * L:(h + 1) * L] * dec_s[h]).astype(BF16))
            vals.append(jnp.where(hm, v, jnp.zeros_like(v)))
        o = (oi_s[0, rows, :] + oi_s[1, rows, :]
             + jnp.dot(jnp.concatenate(scs, axis=1), jnp.concatenate(vals, axis=0), preferred_element_type=F32))
        gate = _silu(g_ref[0, rows, :].astype(F32))
        out_ref[0, rows, :] = (_head_norm(o, True) * gate).astype(out_ref.dtype)
        return 0

    lax.fori_loop(0, n_chunks, out_body, 0, unroll=2)


def _mixer_b(p, log_gamma, n_ctx):
    b, n, _ = p.shape
    kern = functools.partial(_mixer_b_kernel, n=n, n_ctx=n_ctx)
    return pl.pallas_call(
        kern,
        grid=(b,),
        in_specs=[_seq_spec(n, 2), _seq_spec(n, 3), _seq_spec(n, 4), _seq_spec(n, 5),
                  _full_spec(log_gamma.shape)],
        out_specs=pl.BlockSpec((1, n, MIX_W), lambda bi: (bi, 0, 0)),
        out_shape=jax.ShapeDtypeStruct((b, n, MIX_W), BF16),
        scratch_shapes=[pltpu.VMEM((2, n, MIX_W), F32),
                        pltpu.VMEM((N_HEADS, RET_CHUNK, RET_CHUNK), F32),
                        pltpu.VMEM((2, MIX_W, MIX_W), F32)],
        compiler_params=_cparams("parallel"),
        name="mixer_retention",
    )(p, p, p, p, log_gamma)


def _mixer_c_kernel(q_ref, k_ref, v_ref, o_ref, g_ref, cw_ref, cb_ref, gb_ref, out_ref,
                    qs_s, ks_s, hf_s, c_s, n_s, m_s, *, n, n_ctx):
    L = CHUNK
    n_chunks = n // L
    n_ctx_chunks = n_ctx // L
    head = _head_of_lane(MIX_W)
    bd = _block_ones(MIX_W)
    ones_bd = jnp.where(bd, 1.0, 0.0).astype(BF16)
    cw = cw_ref[...]
    cb = cb_ref[...]
    ri = lax.broadcasted_iota(jnp.int32, (L, L), 0)
    ci = lax.broadcasted_iota(jnp.int32, (L, L), 1)
    gl = lax.broadcasted_iota(jnp.int32, (LANES, MIX_W), 0)
    hl = lax.broadcasted_iota(jnp.int32, (LANES, MIX_W), 1) // HEAD_DIM

    def expand(kind):
        return jnp.where(gl == kind * N_HEADS + hl, 1.0, 0.0).astype(BF16)

    def conv_body(c, _):
        rows = pl.ds(pl.multiple_of(c * CONV_STEP, CONV_STEP), CONV_STEP)
        cq, ck = _conv_chunk(c, [(q_ref, cw[:, :MIX_W], cb[:, :MIX_W]), (k_ref, cw[:, MIX_W:], cb[:, MIX_W:])],
                             n=n, n_ctx=n_ctx)
        qs_s[rows, :] = _silu(cq).astype(BF16)
        ks_s[rows, :] = (_silu(ck) * HEAD_DIM ** -0.5).astype(BF16)
        return 0

    lax.fori_loop(0, n // CONV_STEP, conv_body, 0)

    def chunk(d, c):
        rev = d == 1
        rows = pl.ds(pl.multiple_of(c * L, L), L)
        q = qs_s[rows, :]
        k = ks_s[rows, :]
        v = v_ref[0, rows, :]
        g = g_ref[0, rows, :] + gb_ref[...]
        log_i = _dot_sel(g, expand(2 * d))
        log_f = _dot_sel(_log_sigmoid(g), expand(2 * d + 1))
        causal = (ci >= ri) if rev else (ci <= ri)
        tri = jnp.where(causal, 1.0, 0.0).astype(BF16)
        cum = _sel_dot(tri, log_f)
        cum_end = cum[0:1] if rev else cum[L - 1:L]
        m_prev = m_s[d]
        c_prev = c_s[d]
        n_prev = n_s[d]
        row_src = (log_i - cum).T
        m_inter = cum + m_prev
        num_inter = jnp.dot(q, c_prev.astype(BF16), preferred_element_type=F32)
        qn = _dot_sel(q.astype(F32) * n_prev, ones_bd)
        m_q = jnp.zeros((L, MIX_W), F32)
        scs, rhs = [], []
        for h in range(N_HEADS):
            hm = head == h
            lane0 = h * HEAD_DIM
            log_d = jnp.where(causal, cum[:, lane0:lane0 + 1] + row_src[lane0:lane0 + 1, :], NEG_INF)
            m_h = jnp.maximum(jnp.max(log_d, axis=1, keepdims=True), m_inter[:, lane0:lane0 + 1])
            sc = _dot_nt(jnp.where(hm, q, jnp.zeros_like(q)), k) * jnp.exp(log_d - m_h)
            scs.append(sc.astype(BF16))
            rhs.append(jnp.concatenate([jnp.where(hm, v, jnp.zeros_like(v)),
                                        jnp.broadcast_to(jnp.where(hm, 1.0, 0.0).astype(BF16), (L, MIX_W))], axis=1))
            m_q = jnp.where(hm, m_h, m_q)
        intra = jnp.dot(jnp.concatenate(scs, axis=1), jnp.concatenate(rhs, axis=0), preferred_element_type=F32)
        s_inter = jnp.exp(m_inter - m_q)
        num = intra[:, :MIX_W] + s_inter * num_inter
        den = intra[:, MIX_W:] + s_inter * qn
        hid = num / jnp.maximum(jnp.abs(den), jnp.exp(-m_q))
        log_w = cum_end - cum + log_i
        m_loc = jnp.max(log_w, axis=0, keepdims=True)
        kw = k.astype(F32) * jnp.exp(log_w - m_loc)
        c_loc = jnp.where(bd, _dot_tn(kw.astype(BF16), v), 0.0)
        n_loc = jnp.sum(kw, axis=0, keepdims=True)
        m_new = jnp.maximum(cum_end + m_prev, m_loc)
        s_old = jnp.exp(cum_end + m_prev - m_new)
        s_loc = jnp.exp(m_loc - m_new)
        c_s[d] = s_old * c_prev + s_loc * c_loc
        n_s[d] = s_old * n_prev + s_loc * n_loc
        m_s[d] = m_new
        return rows, hid

    c_s[...] = jnp.zeros_like(c_s)
    n_s[...] = jnp.zeros_like(n_s)
    m_s[...] = jnp.full(m_s.shape, M_INIT, F32)

    def walk_body(s, _):
        rows, hid = chunk(0, s)
        hf_s[0, rows, :] = hid
        rows, hid = chunk(1, _bwd_chunk(s, n_ctx_chunks, n_chunks))
        hf_s[1, rows, :] = hid
        return 0

    lax.fori_loop(0, n_chunks, walk_body, 0)

    def out_body(c, _):
        rows = pl.ds(pl.multiple_of(c * OUT_ROWS, OUT_ROWS), OUT_ROWS)
        gate = _sigmoid(o_ref[0, rows, :].astype(F32))
        out_ref[0, rows, :] = (gate * _head_norm(hf_s[0, rows, :] + hf_s[1, rows, :], True)).astype(out_ref.dtype)
        return 0

    lax.fori_loop(0, n // OUT_ROWS, out_body, 0)


def _mixer_c(p, gates, conv_w, conv_b, gate_b, n_ctx):
    b, n, _ = p.shape
    kern = functools.partial(_mixer_c_kernel, n=n, n_ctx=n_ctx)
    return pl.pallas_call(
        kern,
        grid=(b,),
        in_specs=[_seq_spec(n, 6), _seq_spec(n, 7), _seq_spec(n, 8), _seq_spec(n, 9),
                  pl.BlockSpec((1, n, LANES), lambda bi: (bi, 0, 0)),
                  _full_spec(conv_w.shape), _full_spec(conv_b.shape), _full_spec(gate_b.shape)],
        out_specs=pl.BlockSpec((1, n, MIX_W), lambda bi: (bi, 0, 0)),
        out_shape=jax.ShapeDtypeStruct((b, n, MIX_W), BF16),
        scratch_shapes=[pltpu.VMEM((n, MIX_W), BF16), pltpu.VMEM((n, MIX_W), BF16),
                        pltpu.VMEM((2, n, MIX_W), F32),
                        pltpu.VMEM((2, MIX_W, MIX_W), F32),
                        pltpu.VMEM((2, 1, MIX_W), F32),
                        pltpu.VMEM((2, 1, MIX_W), F32)],
        compiler_params=_cparams("parallel"),
        name="mixer_mlstm",
    )(p, p, p, p, gates, conv_w, conv_b, gate_b)


def _hgrn2_tables():
    L, S = HGRN_CHUNK, SUB
    r = np.arange(L)
    same = (r[:, None] // S) == (r[None, :] // S)
    tri = np.stack([same & (r[None, :] <= r[:, None]), same & (r[None, :] >= r[:, None])])
    fold = []
    for rev in (False, True):
        pi = np.concatenate([np.full(len(_pair_rows(i, rev)), i) for i in range(S)])
        pj = np.concatenate([_pair_rows(i, rev) for i in range(S)])
        visible = (pj >= pi) if rev else (pj <= pi)
        block = (np.arange(S)[:, None] == pi[None, :]) & visible[None, :]
        fold.append(np.kron(np.eye(PAIR_GROUP, dtype=bool), block))
    as_bf16 = lambda a: jnp.asarray(a.astype(np.float32), dtype=BF16)
    return as_bf16(tri), as_bf16(same), as_bf16(np.stack(fold))


def _pair_rows(i, rev):
    half = SUB // 2
    if rev:
        return np.arange(0, SUB) if i < half else np.arange(half, SUB)
    return np.arange(0, half) if i < half else np.arange(0, SUB)


def _mixer_d_kernel(q_ref, ff_ref, fb_ref, v_ref, g_ref, lb_ref, tri_ref, same_ref, fold_ref, out_ref,
                    o_s, st_s, *, n, n_ctx):
    L = HGRN_CHUNK
    S = SUB
    nb = L // S
    n_chunks = n // L
    n_ctx_chunks = n_ctx // L
    lb = lb_ref[...]
    bd = _block_ones(MIX_W)
    ones_bd = jnp.where(bd, 1.0, 0.0).astype(BF16)

    def chunk(r, d, c):
        rev = d == 1
        rows = pl.ds(pl.multiple_of(c * L, L), L)
        q = _silu(q_ref[r, rows, :].astype(F32))
        raw = (fb_ref if rev else ff_ref)[r, rows, :].astype(F32)
        v = v_ref[r, rows, :]
        vf = v.astype(F32)
        f = lb + (1.0 - lb) * _sigmoid(raw)
        k = 1.0 - f
        log_f = jnp.log(f)
        cum = _sel_dot(tri_ref[d], log_f)
        tot = _sel_dot(same_ref[...], log_f)
        qt = (q * jnp.exp(cum)).astype(BF16)
        kt = (k * jnp.exp(tot - cum)).astype(BF16)
        g = jnp.exp(tot)
        cum2 = cum * LOG2_E
        prod, vj, scores, intra = {}, {}, {}, {}
        ng = nb // PAIR_GROUP
        for gi in range(ng + 2):
            if gi < ng:
                prods, v_js = [], []
                for a in range(gi * PAIR_GROUP, (gi + 1) * PAIR_GROUP):
                    sl = slice(a * S, (a + 1) * S)
                    cb, qb, kb, vb = cum2[sl], q[sl], k[sl], vf[sl]
                    for i in range(S):
                        js = _pair_rows(i, rev)
                        jsl = slice(int(js[0]), int(js[-1]) + 1)
                        dec = jnp.exp2(jnp.minimum(cb[i:i + 1] - cb[jsl], 0.0))
                        prods.append(qb[i:i + 1] * kb[jsl] * dec)
                        v_js.append(vb[jsl])
                prod[gi] = jnp.concatenate(prods, axis=0).astype(BF16)
                vj[gi] = jnp.concatenate(v_js, axis=0)
            if 1 <= gi <= ng:
                scores[gi - 1] = jnp.dot(prod.pop(gi - 1), ones_bd, preferred_element_type=F32)
            if gi >= 2:
                weighted = (scores.pop(gi - 2) * vj.pop(gi - 2)).astype(BF16)
                intra[gi - 2] = jnp.dot(fold_ref[d], weighted, preferred_element_type=F32)
        o = jnp.concatenate([intra[gi] for gi in range(ng)], axis=0)
        blk = lax.broadcasted_iota(jnp.int32, (L, 1), 0) // S
        kt_blocks = jnp.concatenate([jnp.where(blk == a, kt, jnp.zeros_like(kt)) for a in range(nb)], axis=1)
        kv_all = _dot_tn(v, kt_blocks)
        st = st_s[r, d]
        inter = [None] * nb
        for a in (range(nb - 1, -1, -1) if rev else range(nb)):
            sl = slice(a * S, (a + 1) * S)
            inter[a] = _dot_nt(qt[sl], st.astype(BF16))
            st = st * g[a * S:a * S + 1] + jnp.where(bd, kv_all[:, a * MIX_W:(a + 1) * MIX_W], 0.0)
        st_s[r, d] = st
        o_s[r, d, rows, :] = o + jnp.concatenate(inter, axis=0)

    st_s[...] = jnp.zeros_like(st_s)

    def walk_body(s, _):
        for r in range(ROWS):
            chunk(r, 0, s)
            chunk(r, 1, _bwd_chunk(s, n_ctx_chunks, n_chunks))
        return 0

    lax.fori_loop(0, n_chunks, walk_body, 0)

    def out_body(c, _):
        rows = pl.ds(pl.multiple_of(c * L, L), L)
        for r in range(ROWS):
            gate = _silu(g_ref[r, rows, :].astype(F32))
            o = o_s[r, 0, rows, :] + o_s[r, 1, rows, :]
            out_ref[r, rows, :] = (_head_norm(o, False) * gate).astype(out_ref.dtype)
        return 0

    lax.fori_loop(0, n_chunks, out_body, 0)


def _mixer_d(p, lb, n_ctx):
    b, n, _ = p.shape
    tri, same, fold = _hgrn2_tables()
    kern = functools.partial(_mixer_d_kernel, n=n, n_ctx=n_ctx)
    return pl.pallas_call(
        kern,
        grid=(b // ROWS,),
        in_specs=[_seq_spec(n, 10, ROWS), _seq_spec(n, 11, ROWS), _seq_spec(n, 12, ROWS), _seq_spec(n, 13, ROWS),
                  _seq_spec(n, 14, ROWS), _full_spec(lb.shape), _full_spec(tri.shape), _full_spec(same.shape),
                  _full_spec(fold.shape)],
        out_specs=pl.BlockSpec((ROWS, n, MIX_W), lambda bi: (bi, 0, 0)),
        out_shape=jax.ShapeDtypeStruct((b, n, MIX_W), BF16),
        scratch_shapes=[pltpu.VMEM((ROWS, 2, n, MIX_W), F32), pltpu.VMEM((ROWS, 2, MIX_W, MIX_W), F32)],
        compiler_params=_cparams("parallel"),
        name="mixer_hgrn2",
    )(p, p, p, p, p, lb, tri, same, fold)


def _pack_bf16_pairs(t):
    w = t.shape[1] // 2
    hi = pltpu.bitcast(t[:, :w].astype(BF16).astype(F32), jnp.uint32)
    lo = pltpu.bitcast(t[:, w:].astype(BF16).astype(F32), jnp.uint32)
    return pltpu.bitcast(hi | (lo >> 16), jnp.int32)


def _unpack_bf16_pairs(p):
    u = pltpu.bitcast(p, jnp.uint32)
    hi = pltpu.bitcast(u & jnp.uint32(0xFFFF0000), F32)
    lo = pltpu.bitcast(u << 16, F32)
    return hi, lo


def _merge_kernel(x_ref, m_ref, ya_ref, yb_ref, yc_ref, yd_ref, wm_ref, wb_ref, wo_ref, wr_ref, br_ref,
                  earlier_ref, xo_ref, hp_ref, route_ref, counts_ref, carry_s, *, tm, n_ctx):
    i = pl.program_id(1)

    @pl.when((pl.program_id(0) == 0) & (i == 0))
    def _():
        carry_s[...] = jnp.zeros_like(carry_s)

    row = i * tm + lax.broadcasted_iota(jnp.int32, (tm, 1), 0)
    is_ctx = row < n_ctx
    m = m_ref[0]
    x = x_ref[0]
    d = x.shape[-1]
    h = _modulate(x, _pick(m, is_ctx, 0), _pick(m, is_ctx, 1)).astype(BF16)
    z = jnp.zeros((tm, d), F32)
    for nb, y_ref in enumerate((ya_ref, yb_ref, yc_ref, yd_ref)):
        gate = _sigmoid(jnp.dot(h, wm_ref[:, nb * d:(nb + 1) * d], preferred_element_type=F32))
        z = z + gate * jnp.dot(y_ref[0], wb_ref[nb], preferred_element_type=F32)
    mix = jnp.dot(z.astype(BF16), wo_ref[...], preferred_element_type=F32)
    x1 = x + _pick(m, is_ctx, 2) * mix
    xo_ref[0] = x1
    h2 = _modulate(x1, _pick(m, is_ctx, 3), _pick(m, is_ctx, 4))

    wr = wr_ref[...]
    wr_hi = wr.astype(BF16)
    wr_lo = (wr - wr_hi.astype(F32)).astype(BF16)
    h2_hi = h2.astype(BF16)
    h2_lo = (h2 - h2_hi.astype(F32)).astype(BF16)
    dot = functools.partial(jnp.dot, preferred_element_type=F32)
    logit = dot(h2_hi, wr_hi) + dot(h2_lo, wr_hi) + dot(h2_hi, wr_lo) + br_ref[...]
    lane = lax.broadcasted_iota(jnp.int32, (tm, LANES), 1)
    big = jnp.int32(LANES)
    is_group = (lane >= N_EXPERTS) & (lane < N_EXPERTS + N_GROUPS)
    gl = jnp.where(is_group, logit, NEG_INF)
    g_max = jnp.max(gl, axis=1, keepdims=True)
    g_idx = jnp.min(jnp.where(gl == g_max, lane, big), axis=1, keepdims=True) - N_EXPERTS
    g_prob = 1.0 / jnp.sum(jnp.where(is_group, jnp.exp(logit - g_max), 0.0), axis=1, keepdims=True)
    in_group = (lane < N_EXPERTS) & (lane // EXP_PER_GROUP == g_idx)
    el = jnp.where(in_group, logit, NEG_INF)
    v1 = jnp.max(el, axis=1, keepdims=True)
    i1 = jnp.min(jnp.where(el == v1, lane, big), axis=1, keepdims=True)
    el2 = jnp.where(lane == i1, NEG_INF, el)
    v2 = jnp.max(el2, axis=1, keepdims=True)
    i2 = jnp.min(jnp.where(el2 == v2, lane, big), axis=1, keepdims=True)
    e2 = jnp.exp(v2 - v1)
    w1 = g_prob / (1.0 + e2)
    w2 = g_prob * e2 / (1.0 + e2)
    first_low = i1 < i2
    lo = jnp.where(first_low, i1, i2) - g_idx * EXP_PER_GROUP
    hi = jnp.where(first_low, i2, i1) - g_idx * EXP_PER_GROUP
    pair = 3 * lo - ((lo * (lo - 1)) >> 1) + (hi - lo - 1)
    cls = g_idx * N_PAIRS + pair
    w_lo = jnp.where(first_low, w1, w2)
    w_hi = jnp.where(first_low, w2, w1)
    onehot = jnp.where(lane == cls, 1.0, 0.0)
    before = jnp.dot(earlier_ref[...], onehot.astype(BF16), preferred_element_type=F32) + carry_s[...]
    rank = jnp.sum(onehot * before, axis=1, keepdims=True)
    carry_s[...] += jnp.sum(onehot, axis=0, keepdims=True)
    counts_ref[...] = carry_s[...].astype(jnp.int32)
    cols = jnp.where(lane == 0, cls.astype(F32), jnp.where(lane == 1, rank, 0.0))
    pick = jnp.where(lax.broadcasted_iota(jnp.int32, (8, LANES), 0)
                     == lax.broadcasted_iota(jnp.int32, (8, LANES), 1), 1.0, 0.0).astype(BF16)
    route_ref[0] = sum(_dot_nt(pick, part) for part in _split3(cols)).astype(jnp.int32)
    gate_bits = pltpu.bitcast(jnp.where(lane == 0, w_lo, jnp.where(lane == 1, w_hi, 0.0)), jnp.int32)
    hp_ref[0] = jnp.concatenate([_pack_bf16_pairs(h2), gate_bits], axis=1)


def _merge(xc, mvec, ys, w_merge, w_branch, w_out, w_route, b_route, n_ctx):
    b, n, d = xc.shape
    tm = _token_tile(n, 640)
    kern = functools.partial(_merge_kernel, tm=tm, n_ctx=n_ctx)
    tok = lambda w: pl.BlockSpec((1, tm, w), lambda bi, i: (bi, i, 0))
    const = lambda shape: pl.BlockSpec(shape, lambda bi, i: (0,) * len(shape))
    earlier = jnp.asarray(np.tri(tm, k=-1, dtype=np.float32), dtype=BF16)
    return pl.pallas_call(
        kern,
        grid=(b, n // tm),
        in_specs=[tok(d), pl.BlockSpec((1, MOD_ROWS, d), lambda bi, i: (bi, 0, 0)),
                  tok(MIX_W), tok(MIX_W), tok(MIX_W), tok(MIX_W),
                  const(w_merge.shape), const(w_branch.shape), const(w_out.shape),
                  const(w_route.shape), const(b_route.shape), const(earlier.shape)],
        out_specs=[tok(d), tok(d // 2 + LANES),
                   pl.BlockSpec((1, 8, tm), lambda bi, i: (bi * (n // tm) + i, 0, 0)), const((1, LANES))],
        out_shape=[jax.ShapeDtypeStruct((b, n, d), F32),
                   jax.ShapeDtypeStruct((b, n, d // 2 + LANES), jnp.int32),
                   jax.ShapeDtypeStruct((b * (n // tm), 8, tm), jnp.int32),
                   jax.ShapeDtypeStruct((1, LANES), jnp.int32)],
        scratch_shapes=[pltpu.VMEM((1, LANES), F32)],
        compiler_params=_cparams("arbitrary", "arbitrary"),
        name="merge_route",
    )(xc, mvec, *ys, w_merge, w_branch, w_out, w_route, b_route, earlier)


def _row_move(table, idx, n_out=None):
    scatter = n_out is not None
    n_idx = idx.shape[0]
    width = table.shape[1]
    per_worker = n_idx // SC_WORKERS
    assert per_worker * SC_WORKERS == n_idx and per_worker % 8 == 0
    chunk = max(c for c in range(8, SC_MAX_CHUNK + 1, 8) if per_worker % c == 0)
    n_chunks = per_worker // chunk
    mesh = plsc.VectorSubcoreMesh(core_axis_name="c", subcore_axis_name="s",
                                  num_cores=SC_CORES, num_subcores=SC_SUBCORES)

    @functools.partial(
        pl.kernel, mesh=mesh,
        out_type=jax.ShapeDtypeStruct((n_out if scatter else n_idx, width), table.dtype),
        scratch_types=[pltpu.VMEM((chunk,), jnp.int32),
                       pltpu.VMEM((chunk, width), table.dtype),
                       pltpu.SemaphoreType.DMA],
        name="sc_row_scatter" if scatter else "sc_row_gather",
    )
    def move(table_hbm, idx_hbm, out_hbm, idx_v, rows_v, sem):
        worker = lax.axis_index("s") * SC_CORES + lax.axis_index("c")
        base = worker * per_worker

        @pl.loop(0, n_chunks)
        def _(j):
            off = pl.multiple_of(base + j * chunk, 8)
            pltpu.sync_copy(idx_hbm.at[pl.ds(off, chunk)], idx_v)
            if scatter:
                pltpu.sync_copy(table_hbm.at[pl.ds(off, chunk)], rows_v)
                pltpu.async_copy(rows_v, out_hbm.at[idx_v], sem).wait()
            else:
                pltpu.async_copy(table_hbm.at[idx_v], rows_v, sem).wait()
                pltpu.sync_copy(rows_v, out_hbm.at[pl.ds(off, chunk)])

    return move(table, idx)


def _ffn(h, w1_ref, w3_ref, w2_ref):
    a = jnp.dot(h, w1_ref[0, 0], preferred_element_type=F32)
    g = jnp.dot(h, w3_ref[0, 0], preferred_element_type=F32)
    return jnp.dot((_silu(a) * g).astype(BF16), w2_ref[0, 0], preferred_element_type=F32)


def _moe_kernel(elo_ref, ehi_ref, valid_ref, xs_ref, w1a_ref, w3a_ref, w2a_ref, w1b_ref, w3b_ref, w2b_ref,
                ys_ref):
    j = pl.program_id(0)
    half = ys_ref.shape[1]

    @pl.when(valid_ref[j] != 0)
    def _():
        hi, lo = _unpack_bf16_pairs(xs_ref[:, :half])
        h = jnp.concatenate([hi.astype(BF16), lo.astype(BF16)], axis=1)
        gates = pltpu.bitcast(xs_ref[:, half:], F32)
        y = (gates[:, 0:1] * _ffn(h, w1a_ref, w3a_ref, w2a_ref)
             + gates[:, 1:2] * _ffn(h, w1b_ref, w3b_ref, w2b_ref))
        ys_ref[...] = _pack_bf16_pairs(y)

    @pl.when(valid_ref[j] == 0)
    def _():
        ys_ref[...] = jnp.zeros_like(ys_ref)


def _moe_experts(xs, tile_elo, tile_ehi, tile_valid, w1, w3, w2, layer):
    n_slots, width = xs.shape
    d = w1.shape[2]
    n_tiles = n_slots // MOE_TILE
    wspec = lambda shape, which: pl.BlockSpec(
        (1, 1) + shape, lambda j, elo, ehi, valid: (layer, (elo, ehi)[which][j], 0, 0))
    grid_spec = pltpu.PrefetchScalarGridSpec(
        num_scalar_prefetch=3,
        grid=(n_tiles,),
        in_specs=[pl.BlockSpec((MOE_TILE, width), lambda j, elo, ehi, valid: (j, 0)),
                  wspec((d, D_EXPERT), 0), wspec((d, D_EXPERT), 0), wspec((D_EXPERT, d), 0),
                  wspec((d, D_EXPERT), 1), wspec((d, D_EXPERT), 1), wspec((D_EXPERT, d), 1)],
        out_specs=pl.BlockSpec((MOE_TILE, d // 2), lambda j, elo, ehi, valid: (j, 0)),
    )
    return pl.pallas_call(
        _moe_kernel,
        grid_spec=grid_spec,
        out_shape=jax.ShapeDtypeStruct((n_slots, d // 2), jnp.int32),
        compiler_params=_cparams("arbitrary"),
        name="moe_experts",
    )(tile_elo, tile_ehi, tile_valid, xs, w1, w3, w2, w1, w3, w2)


def _moe_combine_kernel(x_ref, m_ref, y_ref, *rest, tm, n_ctx, row0):
    o_ref = rest[-1]
    row = row0 + pl.program_id(1) * tm + lax.broadcasted_iota(jnp.int32, (tm, 1), 0)
    hi, lo = _unpack_bf16_pairs(y_ref[0])
    x = x_ref[0] + _pick(m_ref[0], row < n_ctx, 5) * jnp.concatenate([hi, lo], axis=1)
    if len(rest) == 2:
        ms = jnp.mean(x * x, axis=-1, keepdims=True)
        x = x * lax.rsqrt(ms + EPS) * rest[0][...]
    o_ref[0] = x


def _moe_combine(x1, mvec, yt, n_ctx, final_w=None):
    b, n, d = x1.shape
    if final_w is None:
        tm, skip, n_out, extra, extra_specs = _token_tile(n, 1100), 0, n, (), []
    else:
        tm = _token_tile(n_ctx, 1024)
        skip, n_out = n_ctx // tm, n - n_ctx
        extra, extra_specs = (final_w.reshape(1, d),), [pl.BlockSpec((1, d), lambda bi, i: (0, 0))]
    kern = functools.partial(_moe_combine_kernel, tm=tm, n_ctx=n_ctx, row0=skip * tm)
    tok = lambda w: pl.BlockSpec((1, tm, w), lambda bi, i: (bi, i + skip, 0))
    return pl.pallas_call(
        kern,
        grid=(b, n_out // tm),
        in_specs=[tok(d), pl.BlockSpec((1, MOD_ROWS, d), lambda bi, i: (bi, 0, 0)), tok(d // 2)] + extra_specs,
        out_specs=pl.BlockSpec((1, tm, d), lambda bi, i: (bi, i, 0)),
        out_shape=jax.ShapeDtypeStruct((b, n_out, d), F32),
        compiler_params=_cparams("parallel", "parallel"),
        name="moe_combine",
    )(x1, mvec, yt, *extra)


def _moe(x1, mvec, hp, route, counts, w1, w3, w2, layer, n_ctx, final_w=None):
    b, n, d = x1.shape
    m = b * n
    n_slots = m + N_CLASSES * MOE_TILE
    cls = route[:, 0, :].reshape(m)
    rank = route[:, 1, :].reshape(m)
    cnt = counts[0, :N_CLASSES]
    padded = (cnt + MOE_TILE - 1) // MOE_TILE * MOE_TILE
    ends = jnp.cumsum(padded)
    pos = (ends - padded)[cls] + rank
    tile_start = jnp.arange(n_slots // MOE_TILE, dtype=jnp.int32) * MOE_TILE
    tile_cls = jnp.minimum(jnp.searchsorted(ends, tile_start, side="right"), N_CLASSES - 1).astype(jnp.int32)
    tile_valid = (tile_start < ends[-1]).astype(jnp.int32)
    pair_lo = jnp.asarray([0, 0, 0, 1, 1, 2], jnp.int32)
    pair_hi = jnp.asarray([1, 2, 3, 2, 3, 3], jnp.int32)
    group = tile_cls // N_PAIRS
    tile_elo = group * EXP_PER_GROUP + pair_lo[tile_cls % N_PAIRS]
    tile_ehi = group * EXP_PER_GROUP + pair_hi[tile_cls % N_PAIRS]

    xs = _row_move(hp.reshape(m, hp.shape[-1]), pos, n_out=n_slots)
    ys = _moe_experts(xs, tile_elo, tile_ehi, tile_valid, w1, w3, w2, layer)
    yt = _row_move(ys, pos)
    return _moe_combine(x1, mvec, yt.reshape(b, n, d // 2), n_ctx, final_w)


def _rope_tables(n_lat, n_ctx):
    rows = n_lat // GRID_W
    row = jnp.repeat(jnp.arange(rows), GRID_W).astype(F32)
    col = jnp.tile(jnp.arange(GRID_W), rows).astype(F32)
    nq = HEAD_DIM // 4
    inv = jnp.power(ROPE_BASE, -jnp.arange(nq, dtype=F32) / nq)
    ang = jnp.concatenate([row[:, None] * inv, col[:, None] * inv], -1)
    cos = jnp.cos(ang)
    sin = jnp.sin(ang)
    cos_h = jnp.concatenate([cos, cos], -1)
    sin_h = jnp.concatenate([-sin, sin], -1)
    cos_full = jnp.concatenate([jnp.ones((n_ctx, HEAD_DIM), F32), cos_h], 0)
    sin_full = jnp.concatenate([jnp.zeros((n_ctx, HEAD_DIM), F32), sin_h], 0)
    return jnp.tile(cos_full, (1, N_HEADS)), jnp.tile(sin_full, (1, N_HEADS))


def _block_diag_heads(w):
    eye = jnp.eye(N_HEADS, dtype=w.dtype)
    return jnp.einsum("hij,hg->higj", w, eye).reshape(MIX_W, MIX_W)


def kernel(x, c, ctx, c_ctx, w_mod, b_mod, w_in, a_conv_w, a_conv_b, a_gate_w, a_gate_b, a_lambda, b_theta,
           c_conv_w, c_conv_b, c_gate_b, d_lb, w_branch, w_out, moe_w_group, moe_b_group, moe_w_router,
           moe_b_router, moe_w1, moe_w3, moe_w2, final_norm_w):
    bsz, n_lat, d = x.shape
    n_ctx = ctx.shape[1]
    depth = w_mod.shape[0]
    assert n_ctx % CHUNK == 0 and n_lat % CHUNK == 0 and n_ctx % 256 == 0 and n_lat % 256 == 0

    xc = jnp.concatenate([ctx, x], axis=1)
    cos, sin = _rope_tables(n_lat, n_ctx)

    cc = jnp.zeros((8, d), F32).at[:bsz].set(c).at[bsz].set(c_ctx)
    mod = _mod_vectors(cc, w_mod, b_mod)
    mx = mod[:, :bsz].reshape(depth, bsz, N_MOD, d)
    mc = jnp.broadcast_to(mod[:, bsz].reshape(depth, 1, N_MOD, d), (depth, bsz, N_MOD, d))
    pad = jnp.zeros((depth, bsz, 8 - N_MOD, d), F32)
    mvec = jnp.concatenate([mx, pad, mc, pad], axis=2)

    lbs = jnp.cumsum(jax.nn.softmax(d_lb.astype(F32), axis=0), axis=0)
    lbs = lbs - lbs[0]

    n_mix_cols = 15 * MIX_W
    gate0 = 10 * MIX_W
    n_gate = 4 * N_HEADS
    w1b = moe_w1.astype(BF16)
    w3b = moe_w3.astype(BF16)
    w2b = moe_w2.astype(BF16)

    for l in range(depth):
        wl = w_in[l]
        w_mix = jnp.concatenate([wl[:, :gate0], wl[:, gate0 + n_gate:n_mix_cols + n_gate]], axis=1).astype(BF16)
        w_gate = jnp.pad(wl[:, gate0:gate0 + n_gate], ((0, 0), (0, LANES - n_gate))).astype(BF16)
        w_merge = wl[:, n_mix_cols + n_gate:].astype(BF16)
        p, gates_c = _in_proj(xc, mvec[l], w_mix, w_gate, cos, sin, n_ctx)

        gw = jnp.concatenate([_block_diag_heads(a_gate_w[l, dd, j]) for dd in range(2) for j in range(2)],
                             axis=1).astype(BF16)
        gb = a_gate_b[l].reshape(1, 4 * MIX_W)
        ya = _mixer_a(p, a_conv_w[l], a_conv_b[l].reshape(1, MIX_W), gw, gb, a_lambda[l], n_ctx)

        log_gamma = jnp.repeat(jax.nn.log_sigmoid(b_theta[l].astype(F32)), HEAD_DIM, axis=1)
        yb = _mixer_b(p, log_gamma, n_ctx)

        gate_b = jnp.pad(c_gate_b[l].reshape(1, n_gate), ((0, 0), (0, LANES - n_gate)))
        yc = _mixer_c(p, gates_c, c_conv_w[l], c_conv_b[l].reshape(1, 2 * MIX_W), gate_b, n_ctx)

        yd = _mixer_d(p, lbs[l].reshape(1, MIX_W), n_ctx)

        w_route = jnp.pad(jnp.concatenate([moe_w_router[l], moe_w_group[l]], axis=1),
                          ((0, 0), (0, LANES - N_EXPERTS - N_GROUPS)))
        b_route = jnp.pad(jnp.concatenate([moe_b_router[l], moe_b_group[l]]),
                          (0, LANES - N_EXPERTS - N_GROUPS)).reshape(1, LANES)
        x1, hp, route, counts = _merge(xc, mvec[l], (ya, yb, yc, yd), w_merge, w_branch[l].astype(BF16),
                                       w_out[l].astype(BF16), w_route, b_route, n_ctx)
        xc = _moe(x1, mvec[l], hp, route, counts, w1b, w3b, w2b, l, n_ctx,
                  final_w=final_norm_w if l == depth - 1 else None)

    return xc
```

```python
import functools

import jax
import jax.numpy as jnp
import numpy as np
from jax import lax
from jax.experimental import pallas as pl
from jax.experimental.pallas import tpu as pltpu
from jax.experimental.pallas import tpu_sc as plsc

F32 = jnp.float32
BF16 = jnp.bfloat16

EPS = 1e-6
N_HEADS = 4
HEAD_DIM = 64
MIX_W = N_HEADS * HEAD_DIM
N_BRANCH = 4
CONV_W = 4
LRU_C = 8.0
GRID_W = 64
ROPE_BASE = 10000.0
N_GROUPS = 4
EXP_PER_GROUP = 4
N_EXPERTS = N_GROUPS * EXP_PER_GROUP
D_EXPERT = 512
N_MOD = 6
M_INIT = -1e30

CHUNK = 128
RET_CHUNK = 256
CONV_STEP = 128
OUT_ROWS = 256
SUB = 16
HGRN_CHUNK = 256
PAIR_GROUP = 4
HALO = 16
LANES = 128
MOD_ROWS = 16
ROWS = 1
VMEM_LIMIT_BYTES = 56 * 1024 * 1024
N_PAIRS = 6
N_CLASSES = N_GROUPS * N_PAIRS
MOE_TILE = 256
SC_CORES = 2
SC_SUBCORES = 16
SC_WORKERS = SC_CORES * SC_SUBCORES
SC_MAX_CHUNK = 32
NEG_INF = float("-inf")
LOG2_E = 1.4426950408889634


def _cparams(*sem):
    return pltpu.CompilerParams(dimension_semantics=sem, vmem_limit_bytes=VMEM_LIMIT_BYTES)


def _token_tile(n, cap):
    best = 16
    for t in range(16, cap + 1, 16):
        if n % t == 0:
            best = t
    return best


def _modulate(x, shift, scale):
    ms = jnp.mean(x * x, axis=-1, keepdims=True)
    return x * lax.rsqrt(ms + EPS) * (1.0 + scale) + shift


def _pick(m, is_ctx, k):
    return jnp.where(is_ctx, m[8 + k:9 + k], m[k:k + 1])


def _sigmoid(x):
    return 0.5 * jnp.tanh(0.5 * x) + 0.5


def _silu(x):
    return x * _sigmoid(x)


def _log_sigmoid(x):
    return jnp.minimum(x, 0.0) - jnp.log(1.0 + jnp.exp(-jnp.abs(x)))


def _split3(x):
    hi = x.astype(BF16)
    r = x - hi.astype(F32)
    mid = r.astype(BF16)
    lo = (r - mid.astype(F32)).astype(BF16)
    return hi, mid, lo


def _split2(x):
    hi = x.astype(BF16)
    return hi, (x - hi.astype(F32)).astype(BF16)


def _sel_dot(sel, x):
    hi, lo = _split2(x)
    d = functools.partial(jnp.dot, preferred_element_type=F32)
    return d(sel, hi) + d(sel, lo)


def _dot_sel(x, sel):
    hi, lo = _split2(x)
    d = functools.partial(jnp.dot, preferred_element_type=F32)
    return d(hi, sel) + d(lo, sel)


def _dot_nt(a, b):
    return lax.dot_general(a, b, (((1,), (1,)), ((), ())), preferred_element_type=F32)


def _dot_tn(a, b):
    return lax.dot_general(a, b, (((0,), (0,)), ((), ())), preferred_element_type=F32)


def _head_of_lane(width):
    return lax.broadcasted_iota(jnp.int32, (1, width), 1) // HEAD_DIM


def _block_ones(width):
    r = lax.broadcasted_iota(jnp.int32, (width, width), 0) // HEAD_DIM
    c = lax.broadcasted_iota(jnp.int32, (width, width), 1) // HEAD_DIM
    return r == c


def _head_norm(o, center):
    ones = jnp.where(_block_ones(MIX_W), 1.0, 0.0).astype(BF16)
    inv = 1.0 / HEAD_DIM
    if center:
        o = o - _dot_sel(o, ones) * inv
    var = _dot_sel(o * o, ones) * inv
    return o * lax.rsqrt(var + EPS)


def _bwd_chunk(s, n_ctx_chunks, n_chunks):
    return jnp.where(s < n_ctx_chunks, n_ctx_chunks - 1 - s, n_chunks - 1 + n_ctx_chunks - s)


def _conv_chunk(c, operands, *, n, n_ctx):
    L = CONV_STEP
    wl = L + 2 * HALO
    r0 = c * L
    start = pl.multiple_of(jnp.clip(r0 - HALO, 0, n - wl), HALO)
    off = r0 - start
    shift = (lax.broadcasted_iota(jnp.int32, (L, wl), 1) - lax.broadcasted_iota(jnp.int32, (L, wl), 0)) - off
    g = r0 + lax.broadcasted_iota(jnp.int32, (L, 1), 0)
    seg_g = jnp.where(g < n_ctx, 0, 1)
    sels = []
    for k in range(CONV_W):
        src = g + (k - 2)
        row_ok = (jnp.where(src < n_ctx, 0, 1) == seg_g) & (src >= 0) & (src < n)
        sels.append(jnp.where((shift == k - 2) & row_ok, 1.0, 0.0).astype(BF16))
    outs = []
    for ref, w, b in operands:
        win = ref[0, pl.ds(start, wl), :]
        y = b
        for k in range(CONV_W):
            y = y + w[k:k + 1] * jnp.dot(sels[k], win, preferred_element_type=F32)
        outs.append(y)
    return outs


def _mod_kernel(cc_ref, w_ref, b_ref, o_ref):
    s = _silu(cc_ref[...])
    o_ref[0] = jnp.dot(s, w_ref[0], precision=lax.Precision.HIGHEST, preferred_element_type=F32) + b_ref[0]


def _mod_vectors(cc, w_mod, b_mod):
    depth, d, dm = w_mod.shape
    tn = 1536
    return pl.pallas_call(
        _mod_kernel,
        grid=(depth, dm // tn),
        in_specs=[pl.BlockSpec((8, d), lambda l, j: (0, 0)),
                  pl.BlockSpec((1, d, tn), lambda l, j: (l, 0, j)),
                  pl.BlockSpec((1, 1, tn), lambda l, j: (l, 0, j))],
        out_specs=pl.BlockSpec((1, 8, tn), lambda l, j: (l, 0, j)),
        out_shape=jax.ShapeDtypeStruct((depth, 8, dm), F32),
        compiler_params=_cparams("parallel", "parallel"),
        name="mod_vectors",
    )(cc, w_mod, b_mod.reshape(depth, 1, dm))


def _rope(t, cos, sin):
    lane = lax.broadcasted_iota(jnp.int32, (1, MIX_W), 1) % HEAD_DIM
    half = HEAD_DIM // 2
    swapped = jnp.where(lane < half, pltpu.roll(t, MIX_W - half, 1), pltpu.roll(t, half, 1))
    return t * cos + swapped * sin


RET_Q_COL = 2 * MIX_W
RET_K_COL = 3 * MIX_W


def _in_proj_kernel(x_ref, m_ref, w_ref, wg_ref, cos_ref, sin_ref, p_ref, g_ref, *, tm, n_ctx, col_chunk):
    i = pl.program_id(1)
    row = i * tm + lax.broadcasted_iota(jnp.int32, (tm, 1), 0)
    is_ctx = row < n_ctx
    m = m_ref[0]
    h = _modulate(x_ref[0], _pick(m, is_ctx, 0), _pick(m, is_ctx, 1)).astype(BF16)
    for j in range(w_ref.shape[1] // col_chunk):
        sl = slice(j * col_chunk, (j + 1) * col_chunk)
        t = jnp.dot(h, w_ref[:, sl], preferred_element_type=F32)
        if j * col_chunk == RET_Q_COL:
            t = _rope(t, cos_ref[...], sin_ref[...])
        elif j * col_chunk == RET_K_COL:
            t = _rope(t, cos_ref[...], sin_ref[...]) * HEAD_DIM ** -0.5
        p_ref[0, :, sl] = t.astype(BF16)
    g_ref[0] = jnp.dot(h, wg_ref[...], preferred_element_type=F32)


def _in_proj(xc, mvec, w_mix, w_gate, cos, sin, n_ctx):
    b, n, d = xc.shape
    tm = _token_tile(n, 640)
    wc = w_mix.shape[1]
    kern = functools.partial(_in_proj_kernel, tm=tm, n_ctx=n_ctx, col_chunk=MIX_W)
    return pl.pallas_call(
        kern,
        grid=(b, n // tm),
        in_specs=[pl.BlockSpec((1, tm, d), lambda bi, i: (bi, i, 0)),
                  pl.BlockSpec((1, MOD_ROWS, d), lambda bi, i: (bi, 0, 0)),
                  pl.BlockSpec((d, wc), lambda bi, i: (0, 0)),
                  pl.BlockSpec((d, LANES), lambda bi, i: (0, 0)),
                  pl.BlockSpec((tm, MIX_W), lambda bi, i: (i, 0)),
                  pl.BlockSpec((tm, MIX_W), lambda bi, i: (i, 0))],
        out_specs=[pl.BlockSpec((1, tm, wc), lambda bi, i: (bi, i, 0)),
                   pl.BlockSpec((1, tm, LANES), lambda bi, i: (bi, i, 0))],
        out_shape=[jax.ShapeDtypeStruct((b, n, wc), BF16),
                   jax.ShapeDtypeStruct((b, n, LANES), F32)],
        compiler_params=_cparams("parallel", "parallel"),
        name="in_proj",
    )(xc, mvec, w_mix, w_gate, cos, sin)


def _lin_scan(a, x, carry, reverse):
    L = a.shape[0]
    row = lax.broadcasted_iota(jnp.int32, (L, 1), 0)
    k = 1
    while k < L:
        if reverse:
            a_s = pltpu.roll(a, L - k, 0)
            x_s = pltpu.roll(x, L - k, 0)
            valid = row < L - k
        else:
            a_s = pltpu.roll(a, k, 0)
            x_s = pltpu.roll(x, k, 0)
            valid = row >= k
        x = jnp.where(valid, a * x_s + x, x)
        a = jnp.where(valid, a * a_s, a)
        k *= 2
    h = x + a * carry
    return h, (h[0:1] if reverse else h[L - 1:L])


def _mixer_a_kernel(ax_ref, ag_ref, cw_ref, cb_ref, gw_ref, gb_ref, lam_ref, out_ref, u_s, hf_s,
                    *, n, n_ctx):
    L = CHUNK
    n_chunks = n // L
    n_ctx_chunks = n_ctx // L
    cw = cw_ref[...]
    cb = cb_ref[...]
    log_lam = _log_sigmoid(lam_ref[...])

    def direction(d, c, carry):
        r0 = pl.multiple_of(c * L, L)
        u = u_s[pl.ds(r0, L), :]
        pre = jnp.dot(u.astype(BF16), gw_ref[:, d * 2 * MIX_W:(d + 1) * 2 * MIX_W],
                      preferred_element_type=F32) + gb_ref[:, d * 2 * MIX_W:(d + 1) * 2 * MIX_W]
        r = _sigmoid(pre[:, :MIX_W])
        gi = _sigmoid(pre[:, MIX_W:])
        log_a = LRU_C * r * log_lam[d:d + 1]
        a = jnp.exp(log_a)
        inp = jnp.sqrt(1.0 - jnp.exp(2.0 * log_a)) * (gi * u)
        h, new_carry = _lin_scan(a, inp, carry, reverse=(d == 1))
        return r0, h, new_carry

    def conv_body(c, _):
        r0 = pl.multiple_of(c * CONV_STEP, CONV_STEP)
        u_s[pl.ds(r0, CONV_STEP), :] = _conv_chunk(c, [(ax_ref, cw, cb)], n=n, n_ctx=n_ctx)[0]
        return 0

    lax.fori_loop(0, n // CONV_STEP, conv_body, 0, unroll=2)

    def walk_body(s, carries):
        r0, h, carry_f = direction(0, s, carries[0])
        hf_s[0, pl.ds(r0, L), :] = h
        r0, h, carry_b = direction(1, _bwd_chunk(s, n_ctx_chunks, n_chunks), carries[1])
        hf_s[1, pl.ds(r0, L), :] = h
        return carry_f, carry_b

    zero = jnp.zeros((1, MIX_W), F32)
    lax.fori_loop(0, n_chunks, walk_body, (zero, zero))

    def out_body(c, _):
        rows = pl.ds(pl.multiple_of(c * L, L), L)
        gate = jax.nn.gelu(ag_ref[0, rows, :].astype(F32), approximate=True)
        out_ref[0, rows, :] = (gate * (hf_s[0, rows, :] + hf_s[1, rows, :])).astype(out_ref.dtype)
        return 0

    lax.fori_loop(0, n_chunks, out_body, 0, unroll=2)


def _seq_spec(n, col, rows=1):
    if rows == 1:
        return pl.BlockSpec((1, n, MIX_W), lambda b: (b, 0, col))
    return pl.BlockSpec((rows, n, MIX_W), lambda b: (b, 0, col), pipeline_mode=pl.Buffered(1))


def _full_spec(shape):
    return pl.BlockSpec(shape, lambda b: (0,) * len(shape))


def _mixer_a(p, conv_w, conv_b, gate_w, gate_b, lam, n_ctx):
    b, n, _ = p.shape
    kern = functools.partial(_mixer_a_kernel, n=n, n_ctx=n_ctx)
    return pl.pallas_call(
        kern,
        grid=(b,),
        in_specs=[_seq_spec(n, 0), _seq_spec(n, 1),
                  _full_spec(conv_w.shape), _full_spec(conv_b.shape),
                  _full_spec(gate_w.shape), _full_spec(gate_b.shape), _full_spec(lam.shape)],
        out_specs=pl.BlockSpec((1, n, MIX_W), lambda bi: (bi, 0, 0)),
        out_shape=jax.ShapeDtypeStruct((b, n, MIX_W), BF16),
        scratch_shapes=[pltpu.VMEM((n, MIX_W), F32), pltpu.VMEM((2, n, MIX_W), F32)],
        compiler_params=_cparams("parallel"),
        name="mixer_rglru",
    )(p, p, conv_w, conv_b, gate_w, gate_b, lam)


def _mixer_b_kernel(q_ref, k_ref, v_ref, g_ref, lg_ref, out_ref, oi_s, dec_s, st_s, *, n, n_ctx):
    L = RET_CHUNK
    n_chunks = n // L
    n_ctx_chunks = n_ctx // L
    head = _head_of_lane(MIX_W)
    bd = _block_ones(MIX_W)
    lgf = lg_ref[0:1, :]
    lgb = lg_ref[1:2, :]
    pos = lax.broadcasted_iota(jnp.int32, (L, 1), 0).astype(F32)
    diff = (lax.broadcasted_iota(jnp.int32, (L, L), 0) - lax.broadcasted_iota(jnp.int32, (L, L), 1)).astype(F32)
    for h in range(N_HEADS):
        lf = lgf[:, h * HEAD_DIM:h * HEAD_DIM + 1]
        lb = lgb[:, h * HEAD_DIM:h * HEAD_DIM + 1]
        dec_s[h] = jnp.where(diff >= 0, jnp.exp(lf * jnp.maximum(diff, 0.0)), jnp.exp(lb * jnp.maximum(-diff, 0.0)))

    qw = (jnp.exp(lgf * (pos + 1.0)), jnp.exp(lgb * (float(L) - pos)))
    kw = (jnp.exp(lgf * (float(L) - 1.0 - pos)), jnp.exp(lgb * pos))
    g_chunk = (jnp.exp(lgf * float(L)), jnp.exp(lgb * float(L)))

    def state_step(d, c):
        rows = pl.ds(pl.multiple_of(c * L, L), L)
        q = q_ref[0, rows, :].astype(F32)
        k = k_ref[0, rows, :].astype(F32)
        st = st_s[d]
        oi_s[d, rows, :] = jnp.dot((q * qw[d]).astype(BF16), st.astype(BF16), preferred_element_type=F32)
        kv = _dot_tn((k * kw[d]).astype(BF16), v_ref[0, rows, :])
        st_s[d] = g_chunk[d] * st + jnp.where(bd, kv, 0.0)

    st_s[...] = jnp.zeros_like(st_s)

    def walk_body(s, _):
        state_step(0, s)
        state_step(1, _bwd_chunk(s, n_ctx_chunks, n_chunks))
        return 0

    lax.fori_loop(0, n_chunks, walk_body, 0)

    def out_body(c, _):
        rows = pl.ds(pl.multiple_of(c * L, L), L)
        q = q_ref[0, rows, :]
        k = k_ref[0, rows, :]
        v = v_ref[0, rows, :]
        scs, vals = [], []
        qk = _dot_nt(jnp.concatenate([jnp.where(head == h, q, jnp.zeros_like(q)) for h in range(N_HEADS)], axis=0), k)
        for h in range(N_HEADS):
            hm = head == h
            scs.append((qk[h * L:(h + 1) * L] * dec_s[h]).astype(BF16))
            vals.append(jnp.where(hm, v, jnp.zeros_like(v)))
        o = (oi_s[0, rows, :] + oi_s[1, rows, :]
             + jnp.dot(jnp.concatenate(scs, axis=1), jnp.concatenate(vals, axis=0), preferred_element_type=F32))
        gate = _silu(g_ref[0, rows, :].astype(F32))
        out_ref[0, rows, :] = (_head_norm(o, True) * gate).astype(out_ref.dtype)
        return 0

    lax.fori_loop(0, n_chunks, out_body, 0, unroll=2)


def _mixer_b(p, log_gamma, n_ctx):
    b, n, _ = p.shape
    kern = functools.partial(_mixer_b_kernel, n=n, n_ctx=n_ctx)
    return pl.pallas_call(
        kern,
        grid=(b,),
        in_specs=[_seq_spec(n, 2), _seq_spec(n, 3), _seq_spec(n, 4), _seq_spec(n, 5),
                  _full_spec(log_gamma.shape)],
        out_specs=pl.BlockSpec((1, n, MIX_W), lambda bi: (bi, 0, 0)),
        out_shape=jax.ShapeDtypeStruct((b, n, MIX_W), BF16),
        scratch_shapes=[pltpu.VMEM((2, n, MIX_W), F32),
                        pltpu.VMEM((N_HEADS, RET_CHUNK, RET_CHUNK), F32),
                        pltpu.VMEM((2, MIX_W, MIX_W), F32)],
        compiler_params=_cparams("parallel"),
        name="mixer_retention",
    )(p, p, p, p, log_gamma)


def _mixer_c_kernel(q_ref, k_ref, v_ref, o_ref, g_ref, cw_ref, cb_ref, gb_ref, out_ref,
                    qs_s, ks_s, hf_s, c_s, n_s, m_s, *, n, n_ctx):
    L = CHUNK
    n_chunks = n // L
    n_ctx_chunks = n_ctx // L
    head = _head_of_lane(MIX_W)
    bd = _block_ones(MIX_W)
    ones_bd = jnp.where(bd, 1.0, 0.0).astype(BF16)
    cw = cw_ref[...]
    cb = cb_ref[...]
    ri = lax.broadcasted_iota(jnp.int32, (L, L), 0)
    ci = lax.broadcasted_iota(jnp.int32, (L, L), 1)
    gl = lax.broadcasted_iota(jnp.int32, (LANES, MIX_W), 0)
    hl = lax.broadcasted_iota(jnp.int32, (LANES, MIX_W), 1) // HEAD_DIM

    def expand(kind):
        return jnp.where(gl == kind * N_HEADS + hl, 1.0, 0.0).astype(BF16)

    def conv_body(c, _):
        rows = pl.ds(pl.multiple_of(c * CONV_STEP, CONV_STEP), CONV_STEP)
        cq, ck = _conv_chunk(c, [(q_ref, cw[:, :MIX_W], cb[:, :MIX_W]), (k_ref, cw[:, MIX_W:], cb[:, MIX_W:])],
                             n=n, n_ctx=n_ctx)
        qs_s[rows, :] = _silu(cq).astype(BF16)
        ks_s[rows, :] = (_silu(ck) * HEAD_DIM ** -0.5).astype(BF16)
        return 0

    lax.fori_loop(0, n // CONV_STEP, conv_body, 0, unroll=2)

    def chunk(d, c):
        rev = d == 1
        rows = pl.ds(pl.multiple_of(c * L, L), L)
        q = qs_s[rows, :]
        k = ks_s[rows, :]
        v = v_ref[0, rows, :]
        g = g_ref[0, rows, :] + gb_ref[...]
        log_i = _dot_sel(g, expand(2 * d))
        log_f = _dot_sel(_log_sigmoid(g), expand(2 * d + 1))
        causal = (ci >= ri) if rev else (ci <= ri)
        tri = jnp.where(causal, 1.0, 0.0).astype(BF16)
        cum = _sel_dot(tri, log_f)
        cum_end = cum[0:1] if rev else cum[L - 1:L]
        m_prev = m_s[d]
        c_prev = c_s[d]
        n_prev = n_s[d]
        row_src = (log_i - cum).T
        m_inter = cum + m_prev
        num_inter = jnp.dot(q, c_prev.astype(BF16), preferred_element_type=F32)
        qn = _dot_sel(q.astype(F32) * n_prev, ones_bd)
        m_q = jnp.zeros((L, MIX_W), F32)
        scs, rhs = [], []
        for h in range(N_HEADS):
            hm = head == h
            lane0 = h * HEAD_DIM
            log_d = jnp.where(causal, cum[:, lane0:lane0 + 1] + row_src[lane0:lane0 + 1, :], NEG_INF)
            m_h = jnp.maximum(jnp.max(log_d, axis=1, keepdims=True), m_inter[:, lane0:lane0 + 1])
            sc = _dot_nt(jnp.where(hm, q, jnp.zeros_like(q)), k) * jnp.exp(log_d - m_h)
            scs.append(sc.astype(BF16))
            rhs.append(jnp.concatenate([jnp.where(hm, v, jnp.zeros_like(v)),
                                        jnp.broadcast_to(jnp.where(hm, 1.0, 0.0).astype(BF16), (L, MIX_W))], axis=1))
            m_q = jnp.where(hm, m_h, m_q)
        intra = jnp.dot(jnp.concatenate(scs, axis=1), jnp.concatenate(rhs, axis=0), preferred_element_type=F32)
        s_inter = jnp.exp(m_inter - m_q)
        num = intra[:, :MIX_W] + s_inter * num_inter
        den = intra[:, MIX_W:] + s_inter * qn
        hid = num / jnp.maximum(jnp.abs(den), jnp.exp(-m_q))
        log_w = cum_end - cum + log_i
        m_loc = jnp.max(log_w, axis=0, keepdims=True)
        kw = k.astype(F32) * jnp.exp(log_w - m_loc)
        c_loc = jnp.where(bd, _dot_tn(kw.astype(BF16), v), 0.0)
        n_loc = jnp.sum(kw, axis=0, keepdims=True)
        m_new = jnp.maximum(cum_end + m_prev, m_loc)
        s_old = jnp.exp(cum_end + m_prev - m_new)
        s_loc = jnp.exp(m_loc - m_new)
        c_s[d] = s_old * c_prev + s_loc * c_loc
        n_s[d] = s_old * n_prev + s_loc * n_loc
        m_s[d] = m_new
        return rows, hid

    c_s[...] = jnp.zeros_like(c_s)
    n_s[...] = jnp.zeros_like(n_s)
    m_s[...] = jnp.full(m_s.shape, M_INIT, F32)

    def walk_body(s, _):
        rows, hid = chunk(0, s)
        hf_s[0, rows, :] = hid
        rows, hid = chunk(1, _bwd_chunk(s, n_ctx_chunks, n_chunks))
        hf_s[1, rows, :] = hid
        return 0

    lax.fori_loop(0, n_chunks, walk_body, 0)

    def out_body(c, _):
        rows = pl.ds(pl.multiple_of(c * OUT_ROWS, OUT_ROWS), OUT_ROWS)
        gate = _sigmoid(o_ref[0, rows, :].astype(F32))
        out_ref[0, rows, :] = (gate * _head_norm(hf_s[0, rows, :] + hf_s[1, rows, :], True)).astype(out_ref.dtype)
        return 0

    lax.fori_loop(0, n // OUT_ROWS, out_body, 0, unroll=2)


def _mixer_c(p, gates, conv_w, conv_b, gate_b, n_ctx):
    b, n, _ = p.shape
    kern = functools.partial(_mixer_c_kernel, n=n, n_ctx=n_ctx)
    return pl.pallas_call(
        kern,
        grid=(b,),
        in_specs=[_seq_spec(n, 6), _seq_spec(n, 7), _seq_spec(n, 8), _seq_spec(n, 9),
                  pl.BlockSpec((1, n, LANES), lambda bi: (bi, 0, 0)),
                  _full_spec(conv_w.shape), _full_spec(conv_b.shape), _full_spec(gate_b.shape)],
        out_specs=pl.BlockSpec((1, n, MIX_W), lambda bi: (bi, 0, 0)),
        out_shape=jax.ShapeDtypeStruct((b, n, MIX_W), BF16),
        scratch_shapes=[pltpu.VMEM((n, MIX_W), BF16), pltpu.VMEM((n, MIX_W), BF16),
                        pltpu.VMEM((2, n, MIX_W), F32),
                        pltpu.VMEM((2, MIX_W, MIX_W), F32),
                        pltpu.VMEM((2, 1, MIX_W), F32),
                        pltpu.VMEM((2, 1, MIX_W), F32)],
        compiler_params=_cparams("parallel"),
        name="mixer_mlstm",
    )(p, p, p, p, gates, conv_w, conv_b, gate_b)


def _hgrn2_tables():
    L, S = HGRN_CHUNK, SUB
    r = np.arange(L)
    same = (r[:, None] // S) == (r[None, :] // S)
    tri = np.stack([same & (r[None, :] <= r[:, None]), same & (r[None, :] >= r[:, None])])
    fold = []
    for rev in (False, True):
        pi = np.concatenate([np.full(len(_pair_rows(i, rev)), i) for i in range(S)])
        pj = np.concatenate([_pair_rows(i, rev) for i in range(S)])
        visible = (pj >= pi) if rev else (pj <= pi)
        block = (np.arange(S)[:, None] == pi[None, :]) & visible[None, :]
        fold.append(np.kron(np.eye(PAIR_GROUP, dtype=bool), block))
    as_bf16 = lambda a: jnp.asarray(a.astype(np.float32), dtype=BF16)
    return as_bf16(tri), as_bf16(same), as_bf16(np.stack(fold))


def _pair_rows(i, rev):
    half = SUB // 2
    if rev:
        return np.arange(0, SUB) if i < half else np.arange(half, SUB)
    return np.arange(0, half) if i < half else np.arange(0, SUB)


def _mixer_d_kernel(q_ref, ff_ref, fb_ref, v_ref, g_ref, lb_ref, tri_ref, same_ref, fold_ref, out_ref,
                    o_s, st_s, *, n, n_ctx):
    L = HGRN_CHUNK
    S = SUB
    nb = L // S
    n_chunks = n // L
    n_ctx_chunks = n_ctx // L
    lb = lb_ref[...]
    bd = _block_ones(MIX_W)
    ones_bd = jnp.where(bd, 1.0, 0.0).astype(BF16)

    def chunk(r, d, c):
        rev = d == 1
        rows = pl.ds(pl.multiple_of(c * L, L), L)
        q = _silu(q_ref[r, rows, :].astype(F32))
        raw = (fb_ref if rev else ff_ref)[r, rows, :].astype(F32)
        v = v_ref[r, rows, :]
        vf = v.astype(F32)
        f = lb + (1.0 - lb) * _sigmoid(raw)
        k = 1.0 - f
        log_f = jnp.log(f)
        cum = _sel_dot(tri_ref[d], log_f)
        tot = _sel_dot(same_ref[...], log_f)
        qt = (q * jnp.exp(cum)).astype(BF16)
        kt = (k * jnp.exp(tot - cum)).astype(BF16)
        g = jnp.exp(tot)
        cum2 = cum * LOG2_E
        prod, vj, scores, intra = {}, {}, {}, {}
        ng = nb // PAIR_GROUP
        for gi in range(ng + 2):
            if gi < ng:
                prods, v_js = [], []
                for a in range(gi * PAIR_GROUP, (gi + 1) * PAIR_GROUP):
                    sl = slice(a * S, (a + 1) * S)
                    cb, qb, kb, vb = cum2[sl], q[sl], k[sl], vf[sl]
                    for i in range(S):
                        js = _pair_rows(i, rev)
                        jsl = slice(int(js[0]), int(js[-1]) + 1)
                        dec = jnp.exp2(jnp.minimum(cb[i:i + 1] - cb[jsl], 0.0))
                        prods.append(qb[i:i + 1] * kb[jsl] * dec)
                        v_js.append(vb[jsl])
                prod[gi] = jnp.concatenate(prods, axis=0).astype(BF16)
                vj[gi] = jnp.concatenate(v_js, axis=0)
            if 1 <= gi <= ng:
                scores[gi - 1] = jnp.dot(prod.pop(gi - 1), ones_bd, preferred_element_type=F32)
            if gi >= 2:
                weighted = (scores.pop(gi - 2) * vj.pop(gi - 2)).astype(BF16)
                intra[gi - 2] = jnp.dot(fold_ref[d], weighted, preferred_element_type=F32)
        o = jnp.concatenate([intra[gi] for gi in range(ng)], axis=0)
        blk = lax.broadcasted_iota(jnp.int32, (L, 1), 0) // S
        kt_blocks = jnp.concatenate([jnp.where(blk == a, kt, jnp.zeros_like(kt)) for a in range(nb)], axis=1)
        kv_all = _dot_tn(v, kt_blocks)
        st = st_s[r, d]
        inter = [None] * nb
        for a in (range(nb - 1, -1, -1) if rev else range(nb)):
            sl = slice(a * S, (a + 1) * S)
            inter[a] = _dot_nt(qt[sl], st.astype(BF16))
            st = st * g[a * S:a * S + 1] + jnp.where(bd, kv_all[:, a * MIX_W:(a + 1) * MIX_W], 0.0)
        st_s[r, d] = st
        o_s[r, d, rows, :] = o + jnp.concatenate(inter, axis=0)

    st_s[...] = jnp.zeros_like(st_s)

    def walk_body(s, _):
        for r in range(ROWS):
            chunk(r, 0, s)
            chunk(r, 1, _bwd_chunk(s, n_ctx_chunks, n_chunks))
        return 0

    lax.fori_loop(0, n_chunks, walk_body, 0)

    def out_body(c, _):
        rows = pl.ds(pl.multiple_of(c * L, L), L)
        for r in range(ROWS):
            gate = _silu(g_ref[r, rows, :].astype(F32))
            o = o_s[r, 0, rows, :] + o_s[r, 1, rows, :]
            out_ref[r, rows, :] = (_head_norm(o, False) * gate).astype(out_ref.dtype)
        return 0

    lax.fori_loop(0, n_chunks, out_body, 0, unroll=2)


def _mixer_d(p, lb, n_ctx):
    b, n, _ = p.shape
    tri, same, fold = _hgrn2_tables()
    kern = functools.partial(_mixer_d_kernel, n=n, n_ctx=n_ctx)
    return pl.pallas_call(
        kern,
        grid=(b // ROWS,),
        in_specs=[_seq_spec(n, 10, ROWS), _seq_spec(n, 11, ROWS), _seq_spec(n, 12, ROWS), _seq_spec(n, 13, ROWS),
                  _seq_spec(n, 14, ROWS), _full_spec(lb.shape), _full_spec(tri.shape), _full_spec(same.shape),
                  _full_spec(fold.shape)],
        out_specs=pl.BlockSpec((ROWS, n, MIX_W), lambda bi: (bi, 0, 0)),
        out_shape=jax.ShapeDtypeStruct((b, n, MIX_W), BF16),
        scratch_shapes=[pltpu.VMEM((ROWS, 2, n, MIX_W), F32), pltpu.VMEM((ROWS, 2, MIX_W, MIX_W), F32)],
        compiler_params=_cparams("parallel"),
        name="mixer_hgrn2",
    )(p, p, p, p, p, lb, tri, same, fold)


def _pack_bf16_pairs(t):
    w = t.shape[1] // 2
    hi = pltpu.bitcast(t[:, :w].astype(BF16).astype(F32), jnp.uint32)
    lo = pltpu.bitcast(t[:, w:].astype(BF16).astype(F32), jnp.uint32)
    return pltpu.bitcast(hi | (lo >> 16), jnp.int32)


def _unpack_bf16_pairs(p):
    u = pltpu.bitcast(p, jnp.uint32)
    hi = pltpu.bitcast(u & jnp.uint32(0xFFFF0000), F32)
    lo = pltpu.bitcast(u << 16, F32)
    return hi, lo


def _merge_kernel(x_ref, m_ref, ya_ref, yb_ref, yc_ref, yd_ref, wm_ref, wb_ref, wo_ref, wr_ref, br_ref,
                  earlier_ref, xo_ref, hp_ref, route_ref, counts_ref, carry_s, *, tm, n_ctx):
    i = pl.program_id(1)

    @pl.when((pl.program_id(0) == 0) & (i == 0))
    def _():
        carry_s[...] = jnp.zeros_like(carry_s)

    row = i * tm + lax.broadcasted_iota(jnp.int32, (tm, 1), 0)
    is_ctx = row < n_ctx
    m = m_ref[0]
    x = x_ref[0]
    d = x.shape[-1]
    h = _modulate(x, _pick(m, is_ctx, 0), _pick(m, is_ctx, 1)).astype(BF16)
    z = jnp.zeros((tm, d), F32)
    for nb, y_ref in enumerate((ya_ref, yb_ref, yc_ref, yd_ref)):
        gate = _sigmoid(jnp.dot(h, wm_ref[:, nb * d:(nb + 1) * d], preferred_element_type=F32))
        z = z + gate * jnp.dot(y_ref[0], wb_ref[nb], preferred_element_type=F32)
    mix = jnp.dot(z.astype(BF16), wo_ref[...], preferred_element_type=F32)
    x1 = x + _pick(m, is_ctx, 2) * mix
    xo_ref[0] = x1
    h2 = _modulate(x1, _pick(m, is_ctx, 3), _pick(m, is_ctx, 4))

    wr = wr_ref[...]
    wr_hi = wr.astype(BF16)
    wr_lo = (wr - wr_hi.astype(F32)).astype(BF16)
    h2_hi = h2.astype(BF16)
    h2_lo = (h2 - h2_hi.astype(F32)).astype(BF16)
    dot = functools.partial(jnp.dot, preferred_element_type=F32)
    logit = dot(h2_hi, wr_hi) + dot(h2_lo, wr_hi) + dot(h2_hi, wr_lo) + br_ref[...]
    lane = lax.broadcasted_iota(jnp.int32, (tm, LANES), 1)
    big = jnp.int32(LANES)
    is_group = (lane >= N_EXPERTS) & (lane < N_EXPERTS + N_GROUPS)
    gl = jnp.where(is_group, logit, NEG_INF)
    g_max = jnp.max(gl, axis=1, keepdims=True)
    g_idx = jnp.min(jnp.where(gl == g_max, lane, big), axis=1, keepdims=True) - N_EXPERTS
    g_prob = 1.0 / jnp.sum(jnp.where(is_group, jnp.exp(logit - g_max), 0.0), axis=1, keepdims=True)
    in_group = (lane < N_EXPERTS) & (lane // EXP_PER_GROUP == g_idx)
    el = jnp.where(in_group, logit, NEG_INF)
    v1 = jnp.max(el, axis=1, keepdims=True)
    i1 = jnp.min(jnp.where(el == v1, lane, big), axis=1, keepdims=True)
    el2 = jnp.where(lane == i1, NEG_INF, el)
    v2 = jnp.max(el2, axis=1, keepdims=True)
    i2 = jnp.min(jnp.where(el2 == v2, lane, big), axis=1, keepdims=True)
    e2 = jnp.exp(v2 - v1)
    w1 = g_prob / (1.0 + e2)
    w2 = g_prob * e2 / (1.0 + e2)
    first_low = i1 < i2
    lo = jnp.where(first_low, i1, i2) - g_idx * EXP_PER_GROUP
    hi = jnp.where(first_low, i2, i1) - g_idx * EXP_PER_GROUP
    pair = 3 * lo - ((lo * (lo - 1)) >> 1) + (hi - lo - 1)
    cls = g_idx * N_PAIRS + pair
    w_lo = jnp.where(first_low, w1, w2)
    w_hi = jnp.where(first_low, w2, w1)
    onehot = jnp.where(lane == cls, 1.0, 0.0)
    before = jnp.dot(earlier_ref[...], onehot.astype(BF16), preferred_element_type=F32) + carry_s[...]
    rank = jnp.sum(onehot * before, axis=1, keepdims=True)
    carry_s[...] += jnp.sum(onehot, axis=0, keepdims=True)
    counts_ref[...] = carry_s[...].astype(jnp.int32)
    cols = jnp.where(lane == 0, cls.astype(F32), jnp.where(lane == 1, rank, 0.0))
    pick = jnp.where(lax.broadcasted_iota(jnp.int32, (8, LANES), 0)
                     == lax.broadcasted_iota(jnp.int32, (8, LANES), 1), 1.0, 0.0).astype(BF16)
    route_ref[0] = sum(_dot_nt(pick, part) for part in _split3(cols)).astype(jnp.int32)
    gate_bits = pltpu.bitcast(jnp.where(lane == 0, w_lo, jnp.where(lane == 1, w_hi, 0.0)), jnp.int32)
    hp_ref[0] = jnp.concatenate([_pack_bf16_pairs(h2), gate_bits], axis=1)


def _merge(xc, mvec, ys, w_merge, w_branch, w_out, w_route, b_route, n_ctx):
    b, n, d = xc.shape
    tm = _token_tile(n, 640)
    kern = functools.partial(_merge_kernel, tm=tm, n_ctx=n_ctx)
    tok = lambda w: pl.BlockSpec((1, tm, w), lambda bi, i: (bi, i, 0))
    const = lambda shape: pl.BlockSpec(shape, lambda bi, i: (0,) * len(shape))
    earlier = jnp.asarray(np.tri(tm, k=-1, dtype=np.float32), dtype=BF16)
    return pl.pallas_call(
        kern,
        grid=(b, n // tm),
        in_specs=[tok(d), pl.BlockSpec((1, MOD_ROWS, d), lambda bi, i: (bi, 0, 0)),
                  tok(MIX_W), tok(MIX_W), tok(MIX_W), tok(MIX_W),
                  const(w_merge.shape), const(w_branch.shape), const(w_out.shape),
                  const(w_route.shape), const(b_route.shape), const(earlier.shape)],
        out_specs=[tok(d), tok(d // 2 + LANES),
                   pl.BlockSpec((1, 8, tm), lambda bi, i: (bi * (n // tm) + i, 0, 0)), const((1, LANES))],
        out_shape=[jax.ShapeDtypeStruct((b, n, d), F32),
                   jax.ShapeDtypeStruct((b, n, d // 2 + LANES), jnp.int32),
                   jax.ShapeDtypeStruct((b * (n // tm), 8, tm), jnp.int32),
                   jax.ShapeDtypeStruct((1, LANES), jnp.int32)],
        scratch_shapes=[pltpu.VMEM((1, LANES), F32)],
        compiler_params=_cparams("arbitrary", "arbitrary"),
        name="merge_route",
    )(xc, mvec, *ys, w_merge, w_branch, w_out, w_route, b_route, earlier)


def _row_move(table, idx, n_out=None):
    scatter = n_out is not None
    n_idx = idx.shape[0]
    width = table.shape[1]
    per_worker = n_idx // SC_WORKERS
    assert per_worker * SC_WORKERS == n_idx and per_worker % 8 == 0
    chunk = max(c for c in range(8, SC_MAX_CHUNK + 1, 8) if per_worker % c == 0)
    n_chunks = per_worker // chunk
    mesh = plsc.VectorSubcoreMesh(core_axis_name="c", subcore_axis_name="s",
                                  num_cores=SC_CORES, num_subcores=SC_SUBCORES)

    @functools.partial(
        pl.kernel, mesh=mesh,
        out_type=jax.ShapeDtypeStruct((n_out if scatter else n_idx, width), table.dtype),
        scratch_types=[pltpu.VMEM((chunk,), jnp.int32),
                       pltpu.VMEM((chunk, width), table.dtype),
                       pltpu.SemaphoreType.DMA],
        name="sc_row_scatter" if scatter else "sc_row_gather",
    )
    def move(table_hbm, idx_hbm, out_hbm, idx_v, rows_v, sem):
        worker = lax.axis_index("s") * SC_CORES + lax.axis_index("c")
        base = worker * per_worker

        @pl.loop(0, n_chunks)
        def _(j):
            off = pl.multiple_of(base + j * chunk, 8)
            pltpu.sync_copy(idx_hbm.at[pl.ds(off, chunk)], idx_v)
            if scatter:
                pltpu.sync_copy(table_hbm.at[pl.ds(off, chunk)], rows_v)
                pltpu.async_copy(rows_v, out_hbm.at[idx_v], sem).wait()
            else:
                pltpu.async_copy(table_hbm.at[idx_v], rows_v, sem).wait()
                pltpu.sync_copy(rows_v, out_hbm.at[pl.ds(off, chunk)])

    return move(table, idx)


def _ffn(h, w1_ref, w3_ref, w2_ref):
    a = jnp.dot(h, w1_ref[0, 0], preferred_element_type=F32)
    g = jnp.dot(h, w3_ref[0, 0], preferred_element_type=F32)
    return jnp.dot((_silu(a) * g).astype(BF16), w2_ref[0, 0], preferred_element_type=F32)


def _moe_kernel(elo_ref, ehi_ref, valid_ref, xs_ref, w1a_ref, w3a_ref, w2a_ref, w1b_ref, w3b_ref, w2b_ref,
                ys_ref):
    j = pl.program_id(0)
    half = ys_ref.shape[1]

    @pl.when(valid_ref[j] != 0)
    def _():
        hi, lo = _unpack_bf16_pairs(xs_ref[:, :half])
        h = jnp.concatenate([hi.astype(BF16), lo.astype(BF16)], axis=1)
        gates = pltpu.bitcast(xs_ref[:, half:], F32)
        y = (gates[:, 0:1] * _ffn(h, w1a_ref, w3a_ref, w2a_ref)
             + gates[:, 1:2] * _ffn(h, w1b_ref, w3b_ref, w2b_ref))
        ys_ref[...] = _pack_bf16_pairs(y)

    @pl.when(valid_ref[j] == 0)
    def _():
        ys_ref[...] = jnp.zeros_like(ys_ref)


def _moe_experts(xs, tile_elo, tile_ehi, tile_valid, w1, w3, w2, layer):
    n_slots, width = xs.shape
    d = w1.shape[2]
    n_tiles = n_slots // MOE_TILE
    wspec = lambda shape, which: pl.BlockSpec(
        (1, 1) + shape, lambda j, elo, ehi, valid: (layer, (elo, ehi)[which][j], 0, 0))
    grid_spec = pltpu.PrefetchScalarGridSpec(
        num_scalar_prefetch=3,
        grid=(n_tiles,),
        in_specs=[pl.BlockSpec((MOE_TILE, width), lambda j, elo, ehi, valid: (j, 0)),
                  wspec((d, D_EXPERT), 0), wspec((d, D_EXPERT), 0), wspec((D_EXPERT, d), 0),
                  wspec((d, D_EXPERT), 1), wspec((d, D_EXPERT), 1), wspec((D_EXPERT, d), 1)],
        out_specs=pl.BlockSpec((MOE_TILE, d // 2), lambda j, elo, ehi, valid: (j, 0)),
    )
    return pl.pallas_call(
        _moe_kernel,
        grid_spec=grid_spec,
        out_shape=jax.ShapeDtypeStruct((n_slots, d // 2), jnp.int32),
        compiler_params=_cparams("arbitrary"),
        name="moe_experts",
    )(tile_elo, tile_ehi, tile_valid, xs, w1, w3, w2, w1, w3, w2)


def _moe_combine_kernel(x_ref, m_ref, y_ref, *rest, tm, n_ctx, row0):
    o_ref = rest[-1]
    row = row0 + pl.program_id(1) * tm + lax.broadcasted_iota(jnp.int32, (tm, 1), 0)
    hi, lo = _unpack_bf16_pairs(y_ref[0])
    x = x_ref[0] + _pick(m_ref[0], row < n_ctx, 5) * jnp.concatenate([hi, lo], axis=1)
    if len(rest) == 2:
        ms = jnp.mean(x * x, axis=-1, keepdims=True)
        x = x * lax.rsqrt(ms + EPS) * rest[0][...]
    o_ref[0] = x


def _moe_combine(x1, mvec, yt, n_ctx, final_w=None):
    b, n, d = x1.shape
    if final_w is None:
        tm, skip, n_out, extra, extra_specs = _token_tile(n, 1100), 0, n, (), []
    else:
        tm = _token_tile(n_ctx, 1024)
        skip, n_out = n_ctx // tm, n - n_ctx
        extra, extra_specs = (final_w.reshape(1, d),), [pl.BlockSpec((1, d), lambda bi, i: (0, 0))]
    kern = functools.partial(_moe_combine_kernel, tm=tm, n_ctx=n_ctx, row0=skip * tm)
    tok = lambda w: pl.BlockSpec((1, tm, w), lambda bi, i: (bi, i + skip, 0))
    return pl.pallas_call(
        kern,
        grid=(b, n_out // tm),
        in_specs=[tok(d), pl.BlockSpec((1, MOD_ROWS, d), lambda bi, i: (bi, 0, 0)), tok(d // 2)] + extra_specs,
        out_specs=pl.BlockSpec((1, tm, d), lambda bi, i: (bi, i, 0)),
        out_shape=jax.ShapeDtypeStruct((b, n_out, d), F32),
        compiler_params=_cparams("parallel", "parallel"),
        name="moe_combine",
    )(x1, mvec, yt, *extra)


def _moe(x1, mvec, hp, route, counts, w1, w3, w2, layer, n_ctx, final_w=None):
    b, n, d = x1.shape
    m = b * n
    n_slots = m + N_CLASSES * MOE_TILE
    cls = route[:, 0, :].reshape(m)
    rank = route[:, 1, :].reshape(m)
    cnt = counts[0, :N_CLASSES]
    padded = (cnt + MOE_TILE - 1) // MOE_TILE * MOE_TILE
    ends = jnp.cumsum(padded)
    pos = (ends - padded)[cls] + rank
    tile_start = jnp.arange(n_slots // MOE_TILE, dtype=jnp.int32) * MOE_TILE
    tile_cls = jnp.minimum(jnp.searchsorted(ends, tile_start, side="right"), N_CLASSES - 1).astype(jnp.int32)
    tile_valid = (tile_start < ends[-1]).astype(jnp.int32)
    pair_lo = jnp.asarray([0, 0, 0, 1, 1, 2], jnp.int32)
    pair_hi = jnp.asarray([1, 2, 3, 2, 3, 3], jnp.int32)
    group = tile_cls // N_PAIRS
    tile_elo = group * EXP_PER_GROUP + pair_lo[tile_cls % N_PAIRS]
    tile_ehi = group * EXP_PER_GROUP + pair_hi[tile_cls % N_PAIRS]

    xs = _row_move(hp.reshape(m, hp.shape[-1]), pos, n_out=n_slots)
    ys = _moe_experts(xs, tile_elo, tile_ehi, tile_valid, w1, w3, w2, layer)
    yt = _row_move(ys, pos)
    return _moe_combine(x1, mvec, yt.reshape(b, n, d // 2), n_ctx, final_w)


def _rope_tables(n_lat, n_ctx):
    rows = n_lat // GRID_W
    row = jnp.repeat(jnp.arange(rows), GRID_W).astype(F32)
    col = jnp.tile(jnp.arange(GRID_W), rows).astype(F32)
    nq = HEAD_DIM // 4
    inv = jnp.power(ROPE_BASE, -jnp.arange(nq, dtype=F32) / nq)
    ang = jnp.concatenate([row[:, None] * inv, col[:, None] * inv], -1)
    cos = jnp.cos(ang)
    sin = jnp.sin(ang)
    cos_h = jnp.concatenate([cos, cos], -1)
    sin_h = jnp.concatenate([-sin, sin], -1)
    cos_full = jnp.concatenate([jnp.ones((n_ctx, HEAD_DIM), F32), cos_h], 0)
    sin_full = jnp.concatenate([jnp.zeros((n_ctx, HEAD_DIM), F32), sin_h], 0)
    return jnp.tile(cos_full, (1, N_HEADS)), jnp.tile(sin_full, (1, N_HEADS))


def _block_diag_heads(w):
    eye = jnp.eye(N_HEADS, dtype=w.dtype)
    return jnp.einsum("hij,hg->higj", w, eye).reshape(MIX_W, MIX_W)


def kernel(x, c, ctx, c_ctx, w_mod, b_mod, w_in, a_conv_w, a_conv_b, a_gate_w, a_gate_b, a_lambda, b_theta,
           c_conv_w, c_conv_b, c_gate_b, d_lb, w_branch, w_out, moe_w_group, moe_b_group, moe_w_router,
           moe_b_router, moe_w1, moe_w3, moe_w2, final_norm_w):
    bsz, n_lat, d = x.shape
    n_ctx = ctx.shape[1]
    depth = w_mod.shape[0]
    assert n_ctx % CHUNK == 0 and n_lat % CHUNK == 0 and n_ctx % 256 == 0 and n_lat % 256 == 0

    xc = jnp.concatenate([ctx, x], axis=1)
    cos, sin = _rope_tables(n_lat, n_ctx)

    cc = jnp.zeros((8, d), F32).at[:bsz].set(c).at[bsz].set(c_ctx)
    mod = _mod_vectors(cc, w_mod, b_mod)
    mx = mod[:, :bsz].reshape(depth, bsz, N_MOD, d)
    mc = jnp.broadcast_to(mod[:, bsz].reshape(depth, 1, N_MOD, d), (depth, bsz, N_MOD, d))
    pad = jnp.zeros((depth, bsz, 8 - N_MOD, d), F32)
    mvec = jnp.concatenate([mx, pad, mc, pad], axis=2)

    lbs = jnp.cumsum(jax.nn.softmax(d_lb.astype(F32), axis=0), axis=0)
    lbs = lbs - lbs[0]

    n_mix_cols = 15 * MIX_W
    gate0 = 10 * MIX_W
    n_gate = 4 * N_HEADS
    w1b = moe_w1.astype(BF16)
    w3b = moe_w3.astype(BF16)
    w2b = moe_w2.astype(BF16)

    for l in range(depth):
        wl = w_in[l]
        w_mix = jnp.concatenate([wl[:, :gate0], wl[:, gate0 + n_gate:n_mix_cols + n_gate]], axis=1).astype(BF16)
        w_gate = jnp.pad(wl[:, gate0:gate0 + n_gate], ((0, 0), (0, LANES - n_gate))).astype(BF16)
        w_merge = wl[:, n_mix_cols + n_gate:].astype(BF16)
        p, gates_c = _in_proj(xc, mvec[l], w_mix, w_gate, cos, sin, n_ctx)

        gw = jnp.concatenate([_block_diag_heads(a_gate_w[l, dd, j]) for dd in range(2) for j in range(2)],
                             axis=1).astype(BF16)
        gb = a_gate_b[l].reshape(1, 4 * MIX_W)
        ya = _mixer_a(p, a_conv_w[l], a_conv_b[l].reshape(1, MIX_W), gw, gb, a_lambda[l], n_ctx)

        log_gamma = jnp.repeat(jax.nn.log_sigmoid(b_theta[l].astype(F32)), HEAD_DIM, axis=1)
        yb = _mixer_b(p, log_gamma, n_ctx)

        gate_b = jnp.pad(c_gate_b[l].reshape(1, n_gate), ((0, 0), (0, LANES - n_gate)))
        yc = _mixer_c(p, gates_c, c_conv_w[l], c_conv_b[l].reshape(1, 2 * MIX_W), gate_b, n_ctx)

        yd = _mixer_d(p, lbs[l].reshape(1, MIX_W), n_ctx)

        w_route = jnp.pad(jnp.concatenate([moe_w_router[l], moe_w_group[l]], axis=1),
                          ((0, 0), (0, LANES - N_EXPERTS - N_GROUPS)))
        b_route = jnp.pad(jnp.concatenate([moe_b_router[l], moe_b_group[l]]),
                          (0, LANES - N_EXPERTS - N_GROUPS)).reshape(1, LANES)
        x1, hp, route, counts = _merge(xc, mvec[l], (ya, yb, yc, yd), w_merge, w_branch[l].astype(BF16),
                                       w_out[l].astype(BF16), w_route, b_route, n_ctx)
        xc = _moe(x1, mvec[l], hp, route, counts, w1b, w3b, w2b, l, n_ctx,
                  final_w=final_norm_w if l == depth - 1 else None)

    return xc
```

```python
import functools

import jax
import jax.numpy as jnp
import numpy as np
from jax import lax
from jax.experimental import pallas as pl
from jax.experimental.pallas import tpu as pltpu
from jax.experimental.pallas import tpu_sc as plsc

F32 = jnp.float32
BF16 = jnp.bfloat16

EPS = 1e-6
N_HEADS = 4
HEAD_DIM = 64
MIX_W = N_HEADS * HEAD_DIM
N_BRANCH = 4
CONV_W = 4
LRU_C = 8.0
GRID_W = 64
ROPE_BASE = 10000.0
N_GROUPS = 4
EXP_PER_GROUP = 4
N_EXPERTS = N_GROUPS * EXP_PER_GROUP
D_EXPERT = 512
N_MOD = 6
M_INIT = -1e30

CHUNK = 128
RET_CHUNK = 256
CONV_STEP = 128
OUT_ROWS = 256
SUB = 16
HGRN_CHUNK = 256
PAIR_GROUP = 4
HALO = 16
LANES = 128
MOD_ROWS = 16
ROWS = 1
VMEM_LIMIT_BYTES = 56 * 1024 * 1024
N_PAIRS = 6
N_CLASSES = N_GROUPS * N_PAIRS
MOE_TILE = 256
SC_CORES = 2
SC_SUBCORES = 16
SC_WORKERS = SC_CORES * SC_SUBCORES
SC_MAX_CHUNK = 32
NEG_INF = float("-inf")
LOG2_E = 1.4426950408889634


def _cparams(*sem):
    return pltpu.CompilerParams(dimension_semantics=sem, vmem_limit_bytes=VMEM_LIMIT_BYTES)


def _token_tile(n, cap):
    best = 16
    for t in range(16, cap + 1, 16):
        if n % t == 0:
            best = t
    return best


def _modulate(x, shift, scale):
    ms = jnp.mean(x * x, axis=-1, keepdims=True)
    return x * lax.rsqrt(ms + EPS) * (1.0 + scale) + shift


def _pick(m, is_ctx, k):
    return jnp.where(is_ctx, m[8 + k:9 + k], m[k:k + 1])


def _sigmoid(x):
    return 0.5 * jnp.tanh(0.5 * x) + 0.5


def _silu(x):
    return x * _sigmoid(x)


def _log_sigmoid(x):
    return jnp.minimum(x, 0.0) - jnp.log(1.0 + jnp.exp(-jnp.abs(x)))


def _split3(x):
    hi = x.astype(BF16)
    r = x - hi.astype(F32)
    mid = r.astype(BF16)
    lo = (r - mid.astype(F32)).astype(BF16)
    return hi, mid, lo


def _split2(x):
    hi = x.astype(BF16)
    return hi, (x - hi.astype(F32)).astype(BF16)


def _sel_dot(sel, x):
    hi, lo = _split2(x)
    d = functools.partial(jnp.dot, preferred_element_type=F32)
    return d(sel, hi) + d(sel, lo)


def _dot_sel(x, sel):
    hi, lo = _split2(x)
    d = functools.partial(jnp.dot, preferred_element_type=F32)
    return d(hi, sel) + d(lo, sel)


def _dot_nt(a, b):
    return lax.dot_general(a, b, (((1,), (1,)), ((), ())), preferred_element_type=F32)


def _dot_tn(a, b):
    return lax.dot_general(a, b, (((0,), (0,)), ((), ())), preferred_element_type=F32)


def _head_of_lane(width):
    return lax.broadcasted_iota(jnp.int32, (1, width), 1) // HEAD_DIM


def _block_ones(width):
    r = lax.broadcasted_iota(jnp.int32, (width, width), 0) // HEAD_DIM
    c = lax.broadcasted_iota(jnp.int32, (width, width), 1) // HEAD_DIM
    return r == c


def _head_norm(o, center):
    ones = jnp.where(_block_ones(MIX_W), 1.0, 0.0).astype(BF16)
    inv = 1.0 / HEAD_DIM
    if center:
        o = o - _dot_sel(o, ones) * inv
    var = _dot_sel(o * o, ones) * inv
    return o * lax.rsqrt(var + EPS)


def _bwd_chunk(s, n_ctx_chunks, n_chunks):
    return jnp.where(s < n_ctx_chunks, n_ctx_chunks - 1 - s, n_chunks - 1 + n_ctx_chunks - s)


def _conv_chunk(c, operands, *, n, n_ctx):
    L = CONV_STEP
    wl = L + 2 * HALO
    r0 = c * L
    start = pl.multiple_of(jnp.clip(r0 - HALO, 0, n - wl), HALO)
    off = r0 - start
    shift = (lax.broadcasted_iota(jnp.int32, (L, wl), 1) - lax.broadcasted_iota(jnp.int32, (L, wl), 0)) - off
    g = r0 + lax.broadcasted_iota(jnp.int32, (L, 1), 0)
    seg_g = jnp.where(g < n_ctx, 0, 1)
    sels = []
    for k in range(CONV_W):
        src = g + (k - 2)
        row_ok = (jnp.where(src < n_ctx, 0, 1) == seg_g) & (src >= 0) & (src < n)
        sels.append(jnp.where((shift == k - 2) & row_ok, 1.0, 0.0).astype(BF16))
    outs = []
    for ref, w, b in operands:
        win = ref[0, pl.ds(start, wl), :]
        y = b
        for k in range(CONV_W):
            y = y + w[k:k + 1] * jnp.dot(sels[k], win, preferred_element_type=F32)
        outs.append(y)
    return outs


def _mod_kernel(cc_ref, w_ref, b_ref, o_ref):
    s = _silu(cc_ref[...])
    o_ref[0] = jnp.dot(s, w_ref[0], precision=lax.Precision.HIGHEST, preferred_element_type=F32) + b_ref[0]


def _mod_vectors(cc, w_mod, b_mod):
    depth, d, dm = w_mod.shape
    tn = 1536
    return pl.pallas_call(
        _mod_kernel,
        grid=(depth, dm // tn),
        in_specs=[pl.BlockSpec((8, d), lambda l, j: (0, 0)),
                  pl.BlockSpec((1, d, tn), lambda l, j: (l, 0, j)),
                  pl.BlockSpec((1, 1, tn), lambda l, j: (l, 0, j))],
        out_specs=pl.BlockSpec((1, 8, tn), lambda l, j: (l, 0, j)),
        out_shape=jax.ShapeDtypeStruct((depth, 8, dm), F32),
        compiler_params=_cparams("parallel", "parallel"),
        name="mod_vectors",
    )(cc, w_mod, b_mod.reshape(depth, 1, dm))


def _rope(t, cos, sin):
    lane = lax.broadcasted_iota(jnp.int32, (1, MIX_W), 1) % HEAD_DIM
    half = HEAD_DIM // 2
    swapped = jnp.where(lane < half, pltpu.roll(t, MIX_W - half, 1), pltpu.roll(t, half, 1))
    return t * cos + swapped * sin


RET_Q_COL = 2 * MIX_W
RET_K_COL = 3 * MIX_W


def _in_proj_kernel(x_ref, m_ref, w_ref, wg_ref, cos_ref, sin_ref, p_ref, g_ref, *, tm, n_ctx, col_chunk):
    i = pl.program_id(1)
    row = i * tm + lax.broadcasted_iota(jnp.int32, (tm, 1), 0)
    is_ctx = row < n_ctx
    m = m_ref[0]
    h = _modulate(x_ref[0], _pick(m, is_ctx, 0), _pick(m, is_ctx, 1)).astype(BF16)
    for j in range(w_ref.shape[1] // col_chunk):
        sl = slice(j * col_chunk, (j + 1) * col_chunk)
        t = jnp.dot(h, w_ref[:, sl], preferred_element_type=F32)
        if j * col_chunk == RET_Q_COL:
            t = _rope(t, cos_ref[...], sin_ref[...])
        elif j * col_chunk == RET_K_COL:
            t = _rope(t, cos_ref[...], sin_ref[...]) * HEAD_DIM ** -0.5
        p_ref[0, :, sl] = t.astype(BF16)
    g_ref[0] = jnp.dot(h, wg_ref[...], preferred_element_type=F32)


def _in_proj(xc, mvec, w_mix, w_gate, cos, sin, n_ctx):
    b, n, d = xc.shape
    tm = _token_tile(n, 640)
    wc = w_mix.shape[1]
    kern = functools.partial(_in_proj_kernel, tm=tm, n_ctx=n_ctx, col_chunk=MIX_W)
    return pl.pallas_call(
        kern,
        grid=(b, n // tm),
        in_specs=[pl.BlockSpec((1, tm, d), lambda bi, i: (bi, i, 0)),
                  pl.BlockSpec((1, MOD_ROWS, d), lambda bi, i: (bi, 0, 0)),
                  pl.BlockSpec((d, wc), lambda bi, i: (0, 0)),
                  pl.BlockSpec((d, LANES), lambda bi, i: (0, 0)),
                  pl.BlockSpec((tm, MIX_W), lambda bi, i: (i, 0)),
                  pl.BlockSpec((tm, MIX_W), lambda bi, i: (i, 0))],
        out_specs=[pl.BlockSpec((1, tm, wc), lambda bi, i: (bi, i, 0)),
                   pl.BlockSpec((1, tm, LANES), lambda bi, i: (bi, i, 0))],
        out_shape=[jax.ShapeDtypeStruct((b, n, wc), BF16),
                   jax.ShapeDtypeStruct((b, n, LANES), F32)],
        compiler_params=_cparams("parallel", "parallel"),
        name="in_proj",
    )(xc, mvec, w_mix, w_gate, cos, sin)


def _lin_scan(a, x, carry, reverse):
    L = a.shape[0]
    row = lax.broadcasted_iota(jnp.int32, (L, 1), 0)
    k = 1
    while k < L:
        if reverse:
            a_s = pltpu.roll(a, L - k, 0)
            x_s = pltpu.roll(x, L - k, 0)
            valid = row < L - k
        else:
            a_s = pltpu.roll(a, k, 0)
            x_s = pltpu.roll(x, k, 0)
            valid = row >= k
        x = jnp.where(valid, a * x_s + x, x)
        a = jnp.where(valid, a * a_s, a)
        k *= 2
    h = x + a * carry
    return h, (h[0:1] if reverse else h[L - 1:L])


def _mixer_a_kernel(ax_ref, ag_ref, cw_ref, cb_ref, gw_ref, gb_ref, lam_ref, out_ref, u_s, hf_s,
                    *, n, n_ctx):
    L = CHUNK
    n_chunks = n // L
    n_ctx_chunks = n_ctx // L
    cw = cw_ref[...]
    cb = cb_ref[...]
    log_lam = _log_sigmoid(lam_ref[...])

    def direction(d, c, carry):
        r0 = pl.multiple_of(c * L, L)
        u = u_s[pl.ds(r0, L), :]
        pre = jnp.dot(u.astype(BF16), gw_ref[:, d * 2 * MIX_W:(d + 1) * 2 * MIX_W],
                      preferred_element_type=F32) + gb_ref[:, d * 2 * MIX_W:(d + 1) * 2 * MIX_W]
        r = _sigmoid(pre[:, :MIX_W])
        gi = _sigmoid(pre[:, MIX_W:])
        log_a = LRU_C * r * log_lam[d:d + 1]
        a = jnp.exp(log_a)
        inp = jnp.sqrt(1.0 - jnp.exp(2.0 * log_a)) * (gi * u)
        h, new_carry = _lin_scan(a, inp, carry, reverse=(d == 1))
        return r0, h, new_carry

    def conv_body(c, _):
        r0 = pl.multiple_of(c * CONV_STEP, CONV_STEP)
        u_s[pl.ds(r0, CONV_STEP), :] = _conv_chunk(c, [(ax_ref, cw, cb)], n=n, n_ctx=n_ctx)[0]
        return 0

    lax.fori_loop(0, n // CONV_STEP, conv_body, 0, unroll=4)

    def walk_body(s, carries):
        r0, h, carry_f = direction(0, s, carries[0])
        hf_s[0, pl.ds(r0, L), :] = h
        r0, h, carry_b = direction(1, _bwd_chunk(s, n_ctx_chunks, n_chunks), carries[1])
        hf_s[1, pl.ds(r0, L), :] = h
        return carry_f, carry_b

    zero = jnp.zeros((1, MIX_W), F32)
    lax.fori_loop(0, n_chunks, walk_body, (zero, zero))

    def out_body(c, _):
        rows = pl.ds(pl.multiple_of(c * L, L), L)
        gate = jax.nn.gelu(ag_ref[0, rows, :].astype(F32), approximate=True)
        out_ref[0, rows, :] = (gate * (hf_s[0, rows, :] + hf_s[1, rows, :])).astype(out_ref.dtype)
        return 0

    lax.fori_loop(0, n_chunks, out_body, 0, unroll=2)


def _seq_spec(n, col, rows=1):
    if rows == 1:
        return pl.BlockSpec((1, n, MIX_W), lambda b: (b, 0, col))
    return pl.BlockSpec((rows, n, MIX_W), lambda b: (b, 0, col), pipeline_mode=pl.Buffered(1))


def _full_spec(shape):
    return pl.BlockSpec(shape, lambda b: (0,) * len(shape))


def _mixer_a(p, conv_w, conv_b, gate_w, gate_b, lam, n_ctx):
    b, n, _ = p.shape
    kern = functools.partial(_mixer_a_kernel, n=n, n_ctx=n_ctx)
    return pl.pallas_call(
        kern,
        grid=(b,),
        in_specs=[_seq_spec(n, 0), _seq_spec(n, 1),
                  _full_spec(conv_w.shape), _full_spec(conv_b.shape),
                  _full_spec(gate_w.shape), _full_spec(gate_b.shape), _full_spec(lam.shape)],
        out_specs=pl.BlockSpec((1, n, MIX_W), lambda bi: (bi, 0, 0)),
        out_shape=jax.ShapeDtypeStruct((b, n, MIX_W), BF16),
        scratch_shapes=[pltpu.VMEM((n, MIX_W), F32), pltpu.VMEM((2, n, MIX_W), F32)],
        compiler_params=_cparams("parallel"),
        name="mixer_rglru",
    )(p, p, conv_w, conv_b, gate_w, gate_b, lam)


def _mixer_b_kernel(q_ref, k_ref, v_ref, g_ref, lg_ref, out_ref, oi_s, dec_s, st_s, *, n, n_ctx):
    L = RET_CHUNK
    n_chunks = n // L
    n_ctx_chunks = n_ctx // L
    head = _head_of_lane(MIX_W)
    bd = _block_ones(MIX_W)
    lgf = lg_ref[0:1, :]
    lgb = lg_ref[1:2, :]
    pos = lax.broadcasted_iota(jnp.int32, (L, 1), 0).astype(F32)
    diff = (lax.broadcasted_iota(jnp.int32, (L, L), 0) - lax.broadcasted_iota(jnp.int32, (L, L), 1)).astype(F32)
    for h in range(N_HEADS):
        lf = lgf[:, h * HEAD_DIM:h * HEAD_DIM + 1]
        lb = lgb[:, h * HEAD_DIM:h * HEAD_DIM + 1]
        dec_s[h] = jnp.where(diff >= 0, jnp.exp(lf * jnp.maximum(diff, 0.0)), jnp.exp(lb * jnp.maximum(-diff, 0.0)))

    qw = (jnp.exp(lgf * (pos + 1.0)), jnp.exp(lgb * (float(L) - pos)))
    kw = (jnp.exp(lgf * (float(L) - 1.0 - pos)), jnp.exp(lgb * pos))
    g_chunk = (jnp.exp(lgf * float(L)), jnp.exp(lgb * float(L)))

    def state_step(d, c):
        rows = pl.ds(pl.multiple_of(c * L, L), L)
        q = q_ref[0, rows, :].astype(F32)
        k = k_ref[0, rows, :].astype(F32)
        st = st_s[d]
        oi_s[d, rows, :] = jnp.dot((q * qw[d]).astype(BF16), st.astype(BF16), preferred_element_type=F32)
        kv = _dot_tn((k * kw[d]).astype(BF16), v_ref[0, rows, :])
        st_s[d] = g_chunk[d] * st + jnp.where(bd, kv, 0.0)

    st_s[...] = jnp.zeros_like(st_s)

    def walk_body(s, _):
        state_step(0, s)
        state_step(1, _bwd_chunk(s, n_ctx_chunks, n_chunks))
        return 0

    lax.fori_loop(0, n_chunks, walk_body, 0)

    def out_body(c, _):
        rows = pl.ds(pl.multiple_of(c * L, L), L)
        q = q_ref[0, rows, :]
        k = k_ref[0, rows, :]
        v = v_ref[0, rows, :]
        scs, vals = [], []
        qk = _dot_nt(jnp.concatenate([jnp.where(head == h, q, jnp.zeros_like(q)) for h in range(N_HEADS)], axis=0), k)
        for h in range(N_HEADS):
            hm = head == h
            scs.append((qk[h * L:(h + 1) * L] * dec_s[h]).astype(BF16))
            vals.append(jnp.where(hm, v, jnp.zeros_like(v)))
        o = (oi_s[0, rows, :] + oi_s[1, rows, :]
             + jnp.dot(jnp.concatenate(scs, axis=1), jnp.concatenate(vals, axis=0), preferred_element_type=F32))
        gate = _silu(g_ref[0, rows, :].astype(F32))
        out_ref[0, rows, :] = (_head_norm(o, True) * gate).astype(out_ref.dtype)
        return 0

    lax.fori_loop(0, n_chunks, out_body, 0, unroll=2)


def _mixer_b(p, log_gamma, n_ctx):
    b, n, _ = p.shape
    kern = functools.partial(_mixer_b_kernel, n=n, n_ctx=n_ctx)
    return pl.pallas_call(
        kern,
        grid=(b,),
        in_specs=[_seq_spec(n, 2), _seq_spec(n, 3), _seq_spec(n, 4), _seq_spec(n, 5),
                  _full_spec(log_gamma.shape)],
        out_specs=pl.BlockSpec((1, n, MIX_W), lambda bi: (bi, 0, 0)),
        out_shape=jax.ShapeDtypeStruct((b, n, MIX_W), BF16),
        scratch_shapes=[pltpu.VMEM((2, n, MIX_W), F32),
                        pltpu.VMEM((N_HEADS, RET_CHUNK, RET_CHUNK), F32),
                        pltpu.VMEM((2, MIX_W, MIX_W), F32)],
        compiler_params=_cparams("parallel"),
        name="mixer_retention",
    )(p, p, p, p, log_gamma)


def _mixer_c_kernel(q_ref, k_ref, v_ref, o_ref, g_ref, cw_ref, cb_ref, gb_ref, out_ref,
                    qs_s, ks_s, hf_s, c_s, n_s, m_s, *, n, n_ctx):
    L = CHUNK
    n_chunks = n // L
    n_ctx_chunks = n_ctx // L
    head = _head_of_lane(MIX_W)
    bd = _block_ones(MIX_W)
    ones_bd = jnp.where(bd, 1.0, 0.0).astype(BF16)
    cw = cw_ref[...]
    cb = cb_ref[...]
    ri = lax.broadcasted_iota(jnp.int32, (L, L), 0)
    ci = lax.broadcasted_iota(jnp.int32, (L, L), 1)
    gl = lax.broadcasted_iota(jnp.int32, (LANES, MIX_W), 0)
    hl = lax.broadcasted_iota(jnp.int32, (LANES, MIX_W), 1) // HEAD_DIM

    def expand(kind):
        return jnp.where(gl == kind * N_HEADS + hl, 1.0, 0.0).astype(BF16)

    def conv_body(c, _):
        rows = pl.ds(pl.multiple_of(c * CONV_STEP, CONV_STEP), CONV_STEP)
        cq, ck = _conv_chunk(c, [(q_ref, cw[:, :MIX_W], cb[:, :MIX_W]), (k_ref, cw[:, MIX_W:], cb[:, MIX_W:])],
                             n=n, n_ctx=n_ctx)
        qs_s[rows, :] = _silu(cq).astype(BF16)
        ks_s[rows, :] = (_silu(ck) * HEAD_DIM ** -0.5).astype(BF16)
        return 0

    lax.fori_loop(0, n // CONV_STEP, conv_body, 0, unroll=4)

    def chunk(d, c):
        rev = d == 1
        rows = pl.ds(pl.multiple_of(c * L, L), L)
        q = qs_s[rows, :]
        k = ks_s[rows, :]
        v = v_ref[0, rows, :]
        g = g_ref[0, rows, :] + gb_ref[...]
        log_i = _dot_sel(g, expand(2 * d))
        log_f = _dot_sel(_log_sigmoid(g), expand(2 * d + 1))
        causal = (ci >= ri) if rev else (ci <= ri)
        tri = jnp.where(causal, 1.0, 0.0).astype(BF16)
        cum = _sel_dot(tri, log_f)
        cum_end = cum[0:1] if rev else cum[L - 1:L]
        m_prev = m_s[d]
        c_prev = c_s[d]
        n_prev = n_s[d]
        row_src = (log_i - cum).T
        m_inter = cum + m_prev
        num_inter = jnp.dot(q, c_prev.astype(BF16), preferred_element_type=F32)
        qn = _dot_sel(q.astype(F32) * n_prev, ones_bd)
        m_q = jnp.zeros((L, MIX_W), F32)
        scs, rhs = [], []
        for h in range(N_HEADS):
            hm = head == h
            lane0 = h * HEAD_DIM
            log_d = jnp.where(causal, cum[:, lane0:lane0 + 1] + row_src[lane0:lane0 + 1, :], NEG_INF)
            m_h = jnp.maximum(jnp.max(log_d, axis=1, keepdims=True), m_inter[:, lane0:lane0 + 1])
            sc = _dot_nt(jnp.where(hm, q, jnp.zeros_like(q)), k) * jnp.exp(log_d - m_h)
            scs.append(sc.astype(BF16))
            rhs.append(jnp.concatenate([jnp.where(hm, v, jnp.zeros_like(v)),
                                        jnp.broadcast_to(jnp.where(hm, 1.0, 0.0).astype(BF16), (L, MIX_W))], axis=1))
            m_q = jnp.where(hm, m_h, m_q)
        intra = jnp.dot(jnp.concatenate(scs, axis=1), jnp.concatenate(rhs, axis=0), preferred_element_type=F32)
        s_inter = jnp.exp(m_inter - m_q)
        num = intra[:, :MIX_W] + s_inter * num_inter
        den = intra[:, MIX_W:] + s_inter * qn
        hid = num / jnp.maximum(jnp.abs(den), jnp.exp(-m_q))
        log_w = cum_end - cum + log_i
        m_loc = jnp.max(log_w, axis=0, keepdims=True)
        kw = k.astype(F32) * jnp.exp(log_w - m_loc)
        c_loc = jnp.where(bd, _dot_tn(kw.astype(BF16), v), 0.0)
        n_loc = jnp.sum(kw, axis=0, keepdims=True)
        m_new = jnp.maximum(cum_end + m_prev, m_loc)
        s_old = jnp.exp(cum_end + m_prev - m_new)
        s_loc = jnp.exp(m_loc - m_new)
        c_s[d] = s_old * c_prev + s_loc * c_loc
        n_s[d] = s_old * n_prev + s_loc * n_loc
        m_s[d] = m_new
        return rows, hid

    c_s[...] = jnp.zeros_like(c_s)
    n_s[...] = jnp.zeros_like(n_s)
    m_s[...] = jnp.full(m_s.shape, M_INIT, F32)

    def walk_body(s, _):
        rows, hid = chunk(0, s)
        hf_s[0, rows, :] = hid
        rows, hid = chunk(1, _bwd_chunk(s, n_ctx_chunks, n_chunks))
        hf_s[1, rows, :] = hid
        return 0

    lax.fori_loop(0, n_chunks, walk_body, 0)

    def out_body(c, _):
        rows = pl.ds(pl.multiple_of(c * OUT_ROWS, OUT_ROWS), OUT_ROWS)
        gate = _sigmoid(o_ref[0, rows, :].astype(F32))
        out_ref[0, rows, :] = (gate * _head_norm(hf_s[0, rows, :] + hf_s[1, rows, :], True)).astype(out_ref.dtype)
        return 0

    lax.fori_loop(0, n // OUT_ROWS, out_body, 0, unroll=4)


def _mixer_c(p, gates, conv_w, conv_b, gate_b, n_ctx):
    b, n, _ = p.shape
    kern = functools.partial(_mixer_c_kernel, n=n, n_ctx=n_ctx)
    return pl.pallas_call(
        kern,
        grid=(b,),
        in_specs=[_seq_spec(n, 6), _seq_spec(n, 7), _seq_spec(n, 8), _seq_spec(n, 9),
                  pl.BlockSpec((1, n, LANES), lambda bi: (bi, 0, 0)),
                  _full_spec(conv_w.shape), _full_spec(conv_b.shape), _full_spec(gate_b.shape)],
        out_specs=pl.BlockSpec((1, n, MIX_W), lambda bi: (bi, 0, 0)),
        out_shape=jax.ShapeDtypeStruct((b, n, MIX_W), BF16),
        scratch_shapes=[pltpu.VMEM((n, MIX_W), BF16), pltpu.VMEM((n, MIX_W), BF16),
                        pltpu.VMEM((2, n, MIX_W), F32),
                        pltpu.VMEM((2, MIX_W, MIX_W), F32),
                        pltpu.VMEM((2, 1, MIX_W), F32),
                        pltpu.VMEM((2, 1, MIX_W), F32)],
        compiler_params=_cparams("parallel"),
        name="mixer_mlstm",
    )(p, p, p, p, gates, conv_w, conv_b, gate_b)


def _hgrn2_tables():
    L, S = HGRN_CHUNK, SUB
    r = np.arange(L)
    same = (r[:, None] // S) == (r[None, :] // S)
    tri = np.stack([same & (r[None, :] <= r[:, None]), same & (r[None, :] >= r[:, None])])
    fold = []
    for rev in (False, True):
        pi = np.concatenate([np.full(len(_pair_rows(i, rev)), i) for i in range(S)])
        pj = np.concatenate([_pair_rows(i, rev) for i in range(S)])
        visible = (pj >= pi) if rev else (pj <= pi)
        block = (np.arange(S)[:, None] == pi[None, :]) & visible[None, :]
        fold.append(np.kron(np.eye(PAIR_GROUP, dtype=bool), block))
    as_bf16 = lambda a: jnp.asarray(a.astype(np.float32), dtype=BF16)
    return as_bf16(tri), as_bf16(same), as_bf16(np.stack(fold))


def _pair_rows(i, rev):
    half = SUB // 2
    if rev:
        return np.arange(0, SUB) if i < half else np.arange(half, SUB)
    return np.arange(0, half) if i < half else np.arange(0, SUB)


def _mixer_d_kernel(q_ref, ff_ref, fb_ref, v_ref, g_ref, lb_ref, tri_ref, same_ref, fold_ref, out_ref,
                    o_s, st_s, *, n, n_ctx):
    L = HGRN_CHUNK
    S = SUB
    nb = L // S
    n_chunks = n // L
    n_ctx_chunks = n_ctx // L
    lb = lb_ref[...]
    bd = _block_ones(MIX_W)
    ones_bd = jnp.where(bd, 1.0, 0.0).astype(BF16)

    def chunk(r, d, c):
        rev = d == 1
        rows = pl.ds(pl.multiple_of(c * L, L), L)
        q = _silu(q_ref[r, rows, :].astype(F32))
        raw = (fb_ref if rev else ff_ref)[r, rows, :].astype(F32)
        v = v_ref[r, rows, :]
        vf = v.astype(F32)
        f = lb + (1.0 - lb) * _sigmoid(raw)
        k = 1.0 - f
        log_f = jnp.log(f)
        cum = _sel_dot(tri_ref[d], log_f)
        tot = _sel_dot(same_ref[...], log_f)
        qt = (q * jnp.exp(cum)).astype(BF16)
        kt = (k * jnp.exp(tot - cum)).astype(BF16)
        g = jnp.exp(tot)
        cum2 = cum * LOG2_E
        prod, vj, scores, intra = {}, {}, {}, {}
        ng = nb // PAIR_GROUP
        for gi in range(ng + 2):
            if gi < ng:
                prods, v_js = [], []
                for a in range(gi * PAIR_GROUP, (gi + 1) * PAIR_GROUP):
                    sl = slice(a * S, (a + 1) * S)
                    cb, qb, kb, vb = cum2[sl], q[sl], k[sl], vf[sl]
                    for i in range(S):
                        js = _pair_rows(i, rev)
                        jsl = slice(int(js[0]), int(js[-1]) + 1)
                        dec = jnp.exp2(jnp.minimum(cb[i:i + 1] - cb[jsl], 0.0))
                        prods.append(qb[i:i + 1] * kb[jsl] * dec)
                        v_js.append(vb[jsl])
                prod[gi] = jnp.concatenate(prods, axis=0).astype(BF16)
                vj[gi] = jnp.concatenate(v_js, axis=0)
            if 1 <= gi <= ng:
                scores[gi - 1] = jnp.dot(prod.pop(gi - 1), ones_bd, preferred_element_type=F32)
            if gi >= 2:
                weighted = (scores.pop(gi - 2) * vj.pop(gi - 2)).astype(BF16)
                intra[gi - 2] = jnp.dot(fold_ref[d], weighted, preferred_element_type=F32)
        o = jnp.concatenate([intra[gi] for gi in range(ng)], axis=0)
        blk = lax.broadcasted_iota(jnp.int32, (L, 1), 0) // S
        kt_blocks = jnp.concatenate([jnp.where(blk == a, kt, jnp.zeros_like(kt)) for a in range(nb)], axis=1)
        kv_all = _dot_tn(v, kt_blocks)
        st = st_s[r, d]
        inter = [None] * nb
        for a in (range(nb - 1, -1, -1) if rev else range(nb)):
            sl = slice(a * S, (a + 1) * S)
            inter[a] = _dot_nt(qt[sl], st.astype(BF16))
            st = st * g[a * S:a * S + 1] + jnp.where(bd, kv_all[:, a * MIX_W:(a + 1) * MIX_W], 0.0)
        st_s[r, d] = st
        o_s[r, d, rows, :] = o + jnp.concatenate(inter, axis=0)

    st_s[...] = jnp.zeros_like(st_s)

    def walk_body(s, _):
        for r in range(ROWS):
            chunk(r, 0, s)
            chunk(r, 1, _bwd_chunk(s, n_ctx_chunks, n_chunks))
        return 0

    lax.fori_loop(0, n_chunks, walk_body, 0)

    def out_body(c, _):
        rows = pl.ds(pl.multiple_of(c * L, L), L)
        for r in range(ROWS):
            gate = _silu(g_ref[r, rows, :].astype(F32))
            o = o_s[r, 0, rows, :] + o_s[r, 1, rows, :]
            out_ref[r, rows, :] = (_head_norm(o, False) * gate).astype(out_ref.dtype)
        return 0

    lax.fori_loop(0, n_chunks, out_body, 0, unroll=2)


def _mixer_d(p, lb, n_ctx):
    b, n, _ = p.shape
    tri, same, fold = _hgrn2_tables()
    kern = functools.partial(_mixer_d_kernel, n=n, n_ctx=n_ctx)
    return pl.pallas_call(
        kern,
        grid=(b // ROWS,),
        in_specs=[_seq_spec(n, 10, ROWS), _seq_spec(n, 11, ROWS), _seq_spec(n, 12, ROWS), _seq_spec(n, 13, ROWS),
                  _seq_spec(n, 14, ROWS), _full_spec(lb.shape), _full_spec(tri.shape), _full_spec(same.shape),
                  _full_spec(fold.shape)],
        out_specs=pl.BlockSpec((ROWS, n, MIX_W), lambda bi: (bi, 0, 0)),
        out_shape=jax.ShapeDtypeStruct((b, n, MIX_W), BF16),
        scratch_shapes=[pltpu.VMEM((ROWS, 2, n, MIX_W), F32), pltpu.VMEM((ROWS, 2, MIX_W, MIX_W), F32)],
        compiler_params=_cparams("parallel"),
        name="mixer_hgrn2",
    )(p, p, p, p, p, lb, tri, same, fold)


def _pack_bf16_pairs(t):
    w = t.shape[1] // 2
    hi = pltpu.bitcast(t[:, :w].astype(BF16).astype(F32), jnp.uint32)
    lo = pltpu.bitcast(t[:, w:].astype(BF16).astype(F32), jnp.uint32)
    return pltpu.bitcast(hi | (lo >> 16), jnp.int32)


def _unpack_bf16_pairs(p):
    u = pltpu.bitcast(p, jnp.uint32)
    hi = pltpu.bitcast(u & jnp.uint32(0xFFFF0000), F32)
    lo = pltpu.bitcast(u << 16, F32)
    return hi, lo


def _merge_kernel(x_ref, m_ref, ya_ref, yb_ref, yc_ref, yd_ref, wm_ref, wb_ref, wo_ref, wr_ref, br_ref,
                  earlier_ref, xo_ref, hp_ref, route_ref, counts_ref, carry_s, *, tm, n_ctx):
    i = pl.program_id(1)

    @pl.when((pl.program_id(0) == 0) & (i == 0))
    def _():
        carry_s[...] = jnp.zeros_like(carry_s)

    row = i * tm + lax.broadcasted_iota(jnp.int32, (tm, 1), 0)
    is_ctx = row < n_ctx
    m = m_ref[0]
    x = x_ref[0]
    d = x.shape[-1]
    h = _modulate(x, _pick(m, is_ctx, 0), _pick(m, is_ctx, 1)).astype(BF16)
    z = jnp.zeros((tm, d), F32)
    for nb, y_ref in enumerate((ya_ref, yb_ref, yc_ref, yd_ref)):
        gate = _sigmoid(jnp.dot(h, wm_ref[:, nb * d:(nb + 1) * d], preferred_element_type=F32))
        z = z + gate * jnp.dot(y_ref[0], wb_ref[nb], preferred_element_type=F32)
    mix = jnp.dot(z.astype(BF16), wo_ref[...], preferred_element_type=F32)
    x1 = x + _pick(m, is_ctx, 2) * mix
    xo_ref[0] = x1
    h2 = _modulate(x1, _pick(m, is_ctx, 3), _pick(m, is_ctx, 4))

    wr = wr_ref[...]
    wr_hi = wr.astype(BF16)
    wr_lo = (wr - wr_hi.astype(F32)).astype(BF16)
    h2_hi = h2.astype(BF16)
    h2_lo = (h2 - h2_hi.astype(F32)).astype(BF16)
    dot = functools.partial(jnp.dot, preferred_element_type=F32)
    logit = dot(h2_hi, wr_hi) + dot(h2_lo, wr_hi) + dot(h2_hi, wr_lo) + br_ref[...]
    lane = lax.broadcasted_iota(jnp.int32, (tm, LANES), 1)
    big = jnp.int32(LANES)
    is_group = (lane >= N_EXPERTS) & (lane < N_EXPERTS + N_GROUPS)
    gl = jnp.where(is_group, logit, NEG_INF)
    g_max = jnp.max(gl, axis=1, keepdims=True)
    g_idx = jnp.min(jnp.where(gl == g_max, lane, big), axis=1, keepdims=True) - N_EXPERTS
    g_prob = 1.0 / jnp.sum(jnp.where(is_group, jnp.exp(logit - g_max), 0.0), axis=1, keepdims=True)
    in_group = (lane < N_EXPERTS) & (lane // EXP_PER_GROUP == g_idx)
    el = jnp.where(in_group, logit, NEG_INF)
    v1 = jnp.max(el, axis=1, keepdims=True)
    i1 = jnp.min(jnp.where(el == v1, lane, big), axis=1, keepdims=True)
    el2 = jnp.where(lane == i1, NEG_INF, el)
    v2 = jnp.max(el2, axis=1, keepdims=True)
    i2 = jnp.min(jnp.where(el2 == v2, lane, big), axis=1, keepdims=True)
    e2 = jnp.exp(v2 - v1)
    w1 = g_prob / (1.0 + e2)
    w2 = g_prob * e2 / (1.0 + e2)
    first_low = i1 < i2
    lo = jnp.where(first_low, i1, i2) - g_idx * EXP_PER_GROUP
    hi = jnp.where(first_low, i2, i1) - g_idx * EXP_PER_GROUP
    pair = 3 * lo - ((lo * (lo - 1)) >> 1) + (hi - lo - 1)
    cls = g_idx * N_PAIRS + pair
    w_lo = jnp.where(first_low, w1, w2)
    w_hi = jnp.where(first_low, w2, w1)
    onehot = jnp.where(lane == cls, 1.0, 0.0)
    before = jnp.dot(earlier_ref[...], onehot.astype(BF16), preferred_element_type=F32) + carry_s[...]
    rank = jnp.sum(onehot * before, axis=1, keepdims=True)
    carry_s[...] += jnp.sum(onehot, axis=0, keepdims=True)
    counts_ref[...] = carry_s[...].astype(jnp.int32)
    cols = jnp.where(lane == 0, cls.astype(F32), jnp.where(lane == 1, rank, 0.0))
    pick = jnp.where(lax.broadcasted_iota(jnp.int32, (8, LANES), 0)
                     == lax.broadcasted_iota(jnp.int32, (8, LANES), 1), 1.0, 0.0).astype(BF16)
    route_ref[0] = sum(_dot_nt(pick, part) for part in _split3(cols)).astype(jnp.int32)
    gate_bits = pltpu.bitcast(jnp.where(lane == 0, w_lo, jnp.where(lane == 1, w_hi, 0.0)), jnp.int32)
    hp_ref[0] = jnp.concatenate([_pack_bf16_pairs(h2), gate_bits], axis=1)


def _merge(xc, mvec, ys, w_merge, w_branch, w_out, w_route, b_route, n_ctx):
    b, n, d = xc.shape
    tm = _token_tile(n, 640)
    kern = functools.partial(_merge_kernel, tm=tm, n_ctx=n_ctx)
    tok = lambda w: pl.BlockSpec((1, tm, w), lambda bi, i: (bi, i, 0))
    const = lambda shape: pl.BlockSpec(shape, lambda bi, i: (0,) * len(shape))
    earlier = jnp.asarray(np.tri(tm, k=-1, dtype=np.float32), dtype=BF16)
    return pl.pallas_call(
        kern,
        grid=(b, n // tm),
        in_specs=[tok(d), pl.BlockSpec((1, MOD_ROWS, d), lambda bi, i: (bi, 0, 0)),
                  tok(MIX_W), tok(MIX_W), tok(MIX_W), tok(MIX_W),
                  const(w_merge.shape), const(w_branch.shape), const(w_out.shape),
                  const(w_route.shape), const(b_route.shape), const(earlier.shape)],
        out_specs=[tok(d), tok(d // 2 + LANES),
                   pl.BlockSpec((1, 8, tm), lambda bi, i: (bi * (n // tm) + i, 0, 0)), const((1, LANES))],
        out_shape=[jax.ShapeDtypeStruct((b, n, d), F32),
                   jax.ShapeDtypeStruct((b, n, d // 2 + LANES), jnp.int32),
                   jax.ShapeDtypeStruct((b * (n // tm), 8, tm), jnp.int32),
                   jax.ShapeDtypeStruct((1, LANES), jnp.int32)],
        scratch_shapes=[pltpu.VMEM((1, LANES), F32)],
        compiler_params=_cparams("arbitrary", "arbitrary"),
        name="merge_route",
    )(xc, mvec, *ys, w_merge, w_branch, w_out, w_route, b_route, earlier)


def _row_move(table, idx, n_out=None):
    scatter = n_out is not None
    n_idx = idx.shape[0]
    width = table.shape[1]
    per_worker = n_idx // SC_WORKERS
    assert per_worker * SC_WORKERS == n_idx and per_worker % 8 == 0
    chunk = max(c for c in range(8, SC_MAX_CHUNK + 1, 8) if per_worker % c == 0)
    n_chunks = per_worker // chunk
    mesh = plsc.VectorSubcoreMesh(core_axis_name="c", subcore_axis_name="s",
                                  num_cores=SC_CORES, num_subcores=SC_SUBCORES)

    @functools.partial(
        pl.kernel, mesh=mesh,
        out_type=jax.ShapeDtypeStruct((n_out if scatter else n_idx, width), table.dtype),
        scratch_types=[pltpu.VMEM((chunk,), jnp.int32),
                       pltpu.VMEM((chunk, width), table.dtype),
                       pltpu.SemaphoreType.DMA],
        name="sc_row_scatter" if scatter else "sc_row_gather",
    )
    def move(table_hbm, idx_hbm, out_hbm, idx_v, rows_v, sem):
        worker = lax.axis_index("s") * SC_CORES + lax.axis_index("c")
        base = worker * per_worker

        @pl.loop(0, n_chunks)
        def _(j):
            off = pl.multiple_of(base + j * chunk, 8)
            pltpu.sync_copy(idx_hbm.at[pl.ds(off, chunk)], idx_v)
            if scatter:
                pltpu.sync_copy(table_hbm.at[pl.ds(off, chunk)], rows_v)
                pltpu.async_copy(rows_v, out_hbm.at[idx_v], sem).wait()
            else:
                pltpu.async_copy(table_hbm.at[idx_v], rows_v, sem).wait()
                pltpu.sync_copy(rows_v, out_hbm.at[pl.ds(off, chunk)])

    return move(table, idx)


def _ffn(h, w1_ref, w3_ref, w2_ref):
    a = jnp.dot(h, w1_ref[0, 0], preferred_element_type=F32)
    g = jnp.dot(h, w3_ref[0, 0], preferred_element_type=F32)
    return jnp.dot((_silu(a) * g).astype(BF16), w2_ref[0, 0], preferred_element_type=F32)


def _moe_kernel(elo_ref, ehi_ref, valid_ref, xs_ref, w1a_ref, w3a_ref, w2a_ref, w1b_ref, w3b_ref, w2b_ref,
                ys_ref):
    j = pl.program_id(0)
    half = ys_ref.shape[1]

    @pl.when(valid_ref[j] != 0)
    def _():
        hi, lo = _unpack_bf16_pairs(xs_ref[:, :half])
        h = jnp.concatenate([hi.astype(BF16), lo.astype(BF16)], axis=1)
        gates = pltpu.bitcast(xs_ref[:, half:], F32)
        y = (gates[:, 0:1] * _ffn(h, w1a_ref, w3a_ref, w2a_ref)
             + gates[:, 1:2] * _ffn(h, w1b_ref, w3b_ref, w2b_ref))
        ys_ref[...] = _pack_bf16_pairs(y)

    @pl.when(valid_ref[j] == 0)
    def _():
        ys_ref[...] = jnp.zeros_like(ys_ref)


def _moe_experts(xs, tile_elo, tile_ehi, tile_valid, w1, w3, w2, layer):
    n_slots, width = xs.shape
    d = w1.shape[2]
    n_tiles = n_slots // MOE_TILE
    wspec = lambda shape, which: pl.BlockSpec(
        (1, 1) + shape, lambda j, elo, ehi, valid: (layer, (elo, ehi)[which][j], 0, 0))
    grid_spec = pltpu.PrefetchScalarGridSpec(
        num_scalar_prefetch=3,
        grid=(n_tiles,),
        in_specs=[pl.BlockSpec((MOE_TILE, width), lambda j, elo, ehi, valid: (j, 0)),
                  wspec((d, D_EXPERT), 0), wspec((d, D_EXPERT), 0), wspec((D_EXPERT, d), 0),
                  wspec((d, D_EXPERT), 1), wspec((d, D_EXPERT), 1), wspec((D_EXPERT, d), 1)],
        out_specs=pl.BlockSpec((MOE_TILE, d // 2), lambda j, elo, ehi, valid: (j, 0)),
    )
    return pl.pallas_call(
        _moe_kernel,
        grid_spec=grid_spec,
        out_shape=jax.ShapeDtypeStruct((n_slots, d // 2), jnp.int32),
        compiler_params=_cparams("arbitrary"),
        name="moe_experts",
    )(tile_elo, tile_ehi, tile_valid, xs, w1, w3, w2, w1, w3, w2)


def _moe_combine_kernel(x_ref, m_ref, y_ref, *rest, tm, n_ctx, row0):
    o_ref = rest[-1]
    row = row0 + pl.program_id(1) * tm + lax.broadcasted_iota(jnp.int32, (tm, 1), 0)
    hi, lo = _unpack_bf16_pairs(y_ref[0])
    x = x_ref[0] + _pick(m_ref[0], row < n_ctx, 5) * jnp.concatenate([hi, lo], axis=1)
    if len(rest) == 2:
        ms = jnp.mean(x * x, axis=-1, keepdims=True)
        x = x * lax.rsqrt(ms + EPS) * rest[0][...]
    o_ref[0] = x


def _moe_combine(x1, mvec, yt, n_ctx, final_w=None):
    b, n, d = x1.shape
    if final_w is None:
        tm, skip, n_out, extra, extra_specs = _token_tile(n, 1100), 0, n, (), []
    else:
        tm = _token_tile(n_ctx, 1024)
        skip, n_out = n_ctx // tm, n - n_ctx
        extra, extra_specs = (final_w.reshape(1, d),), [pl.BlockSpec((1, d), lambda bi, i: (0, 0))]
    kern = functools.partial(_moe_combine_kernel, tm=tm, n_ctx=n_ctx, row0=skip * tm)
    tok = lambda w: pl.BlockSpec((1, tm, w), lambda bi, i: (bi, i + skip, 0))
    return pl.pallas_call(
        kern,
        grid=(b, n_out // tm),
        in_specs=[tok(d), pl.BlockSpec((1, MOD_ROWS, d), lambda bi, i: (bi, 0, 0)), tok(d // 2)] + extra_specs,
        out_specs=pl.BlockSpec((1, tm, d), lambda bi, i: (bi, i, 0)),
        out_shape=jax.ShapeDtypeStruct((b, n_out, d), F32),
        compiler_params=_cparams("parallel", "parallel"),
        name="moe_combine",
    )(x1, mvec, yt, *extra)


def _moe(x1, mvec, hp, route, counts, w1, w3, w2, layer, n_ctx, final_w=None):
    b, n, d = x1.shape
    m = b * n
    n_slots = m + N_CLASSES * MOE_TILE
    cls = route[:, 0, :].reshape(m)
    rank = route[:, 1, :].reshape(m)
    cnt = counts[0, :N_CLASSES]
    padded = (cnt + MOE_TILE - 1) // MOE_TILE * MOE_TILE
    ends = jnp.cumsum(padded)
    pos = (ends - padded)[cls] + rank
    tile_start = jnp.arange(n_slots // MOE_TILE, dtype=jnp.int32) * MOE_TILE
    tile_cls = jnp.minimum(jnp.searchsorted(ends, tile_start, side="right"), N_CLASSES - 1).astype(jnp.int32)
    tile_valid = (tile_start < ends[-1]).astype(jnp.int32)
    pair_lo = jnp.asarray([0, 0, 0, 1, 1, 2], jnp.int32)
    pair_hi = jnp.asarray([1, 2, 3, 2, 3, 3], jnp.int32)
    group = tile_cls // N_PAIRS
    tile_elo = group * EXP_PER_GROUP + pair_lo[tile_cls % N_PAIRS]
    tile_ehi = group * EXP_PER_GROUP + pair_hi[tile_cls % N_PAIRS]

    xs = _row_move(hp.reshape(m, hp.shape[-1]), pos, n_out=n_slots)
    ys = _moe_experts(xs, tile_elo, tile_ehi, tile_valid, w1, w3, w2, layer)
    yt = _row_move(ys, pos)
    return _moe_combine(x1, mvec, yt.reshape(b, n, d // 2), n_ctx, final_w)


def _rope_tables(n_lat, n_ctx):
    rows = n_lat // GRID_W
    row = jnp.repeat(jnp.arange(rows), GRID_W).astype(F32)
    col = jnp.tile(jnp.arange(GRID_W), rows).astype(F32)
    nq = HEAD_DIM // 4
    inv = jnp.power(ROPE_BASE, -jnp.arange(nq, dtype=F32) / nq)
    ang = jnp.concatenate([row[:, None] * inv, col[:, None] * inv], -1)
    cos = jnp.cos(ang)
    sin = jnp.sin(ang)
    cos_h = jnp.concatenate([cos, cos], -1)
    sin_h = jnp.concatenate([-sin, sin], -1)
    cos_full = jnp.concatenate([jnp.ones((n_ctx, HEAD_DIM), F32), cos_h], 0)
    sin_full = jnp.concatenate([jnp.zeros((n_ctx, HEAD_DIM), F32), sin_h], 0)
    return jnp.tile(cos_full, (1, N_HEADS)), jnp.tile(sin_full, (1, N_HEADS))


def _block_diag_heads(w):
    eye = jnp.eye(N_HEADS, dtype=w.dtype)
    return jnp.einsum("hij,hg->higj", w, eye).reshape(MIX_W, MIX_W)


def kernel(x, c, ctx, c_ctx, w_mod, b_mod, w_in, a_conv_w, a_conv_b, a_gate_w, a_gate_b, a_lambda, b_theta,
           c_conv_w, c_conv_b, c_gate_b, d_lb, w_branch, w_out, moe_w_group, moe_b_group, moe_w_router,
           moe_b_router, moe_w1, moe_w3, moe_w2, final_norm_w):
    bsz, n_lat, d = x.shape
    n_ctx = ctx.shape[1]
    depth = w_mod.shape[0]
    assert n_ctx % CHUNK == 0 and n_lat % CHUNK == 0 and n_ctx % 256 == 0 and n_lat % 256 == 0

    xc = jnp.concatenate([ctx, x], axis=1)
    cos, sin = _rope_tables(n_lat, n_ctx)

    cc = jnp.zeros((8, d), F32).at[:bsz].set(c).at[bsz].set(c_ctx)
    mod = _mod_vectors(cc, w_mod, b_mod)
    mx = mod[:, :bsz].reshape(depth, bsz, N_MOD, d)
    mc = jnp.broadcast_to(mod[:, bsz].reshape(depth, 1, N_MOD, d), (depth, bsz, N_MOD, d))
    pad = jnp.zeros((depth, bsz, 8 - N_MOD, d), F32)
    mvec = jnp.concatenate([mx, pad, mc, pad], axis=2)

    lbs = jnp.cumsum(jax.nn.softmax(d_lb.astype(F32), axis=0), axis=0)
    lbs = lbs - lbs[0]

    n_mix_cols = 15 * MIX_W
    gate0 = 10 * MIX_W
    n_gate = 4 * N_HEADS
    w1b = moe_w1.astype(BF16)
    w3b = moe_w3.astype(BF16)
    w2b = moe_w2.astype(BF16)

    for l in range(depth):
        wl = w_in[l]
        w_mix = jnp.concatenate([wl[:, :gate0], wl[:, gate0 + n_gate:n_mix_cols + n_gate]], axis=1).astype(BF16)
        w_gate = jnp.pad(wl[:, gate0:gate0 + n_gate], ((0, 0), (0, LANES - n_gate))).astype(BF16)
        w_merge = wl[:, n_mix_cols + n_gate:].astype(BF16)
        p, gates_c = _in_proj(xc, mvec[l], w_mix, w_gate, cos, sin, n_ctx)

        gw = jnp.concatenate([_block_diag_heads(a_gate_w[l, dd, j]) for dd in range(2) for j in range(2)],
                             axis=1).astype(BF16)
        gb = a_gate_b[l].reshape(1, 4 * MIX_W)
        ya = _mixer_a(p, a_conv_w[l], a_conv_b[l].reshape(1, MIX_W), gw, gb, a_lambda[l], n_ctx)

        log_gamma = jnp.repeat(jax.nn.log_sigmoid(b_theta[l].astype(F32)), HEAD_DIM, axis=1)
        yb = _mixer_b(p, log_gamma, n_ctx)

        gate_b = jnp.pad(c_gate_b[l].reshape(1, n_gate), ((0, 0), (0, LANES - n_gate)))
        yc = _mixer_c(p, gates_c, c_conv_w[l], c_conv_b[l].reshape(1, 2 * MIX_W), gate_b, n_ctx)

        yd = _mixer_d(p, lbs[l].reshape(1, MIX_W), n_ctx)

        w_route = jnp.pad(jnp.concatenate([moe_w_router[l], moe_w_group[l]], axis=1),
                          ((0, 0), (0, LANES - N_EXPERTS - N_GROUPS)))
        b_route = jnp.pad(jnp.concatenate([moe_b_router[l], moe_b_group[l]]),
                          (0, LANES - N_EXPERTS - N_GROUPS)).reshape(1, LANES)
        x1, hp, route, counts = _merge(xc, mvec[l], (ya, yb, yc, yd), w_merge, w_branch[l].astype(BF16),
                                       w_out[l].astype(BF16), w_route, b_route, n_ctx)
        xc = _moe(x1, mvec[l], hp, route, counts, w1b, w3b, w2b, l, n_ctx,
                  final_w=final_norm_w if l == depth - 1 else None)

    return xc
```
